```python
import math
import jax, jax.numpy as jnp
from jax import lax
import numpy as np

D_MODEL = 1024
BATCH = 16
SEQ = 2048
DEPTH = 4

N_MIXERS = 4
EPS = 1e-6
POOL_WINDOWS = (2, 4, 8, 16)
POOL_GROUP = D_MODEL // len(POOL_WINDOWS)
S5_GROUP = 16
S5_GROUPS = D_MODEL // S5_GROUP
S5_STATE = 64
S5_DT_MIN = 1e-3
S5_DT_MAX = 1e-1
LRU_WIDTH = D_MODEL
LRU_BLOCKS = 4
LRU_BLOCK = LRU_WIDTH // LRU_BLOCKS
LRU_CONV = 4
LRU_C = 8.0
SB_HEADS = 16
SB_HEAD_DIM = D_MODEL // SB_HEADS
SB_Q_BLOCK = 128
FFN_HIDDEN = 2816
FFN_CONV = 3

kernel_name = "interleaved_pool_s5_rglru_stickbreak_trunk"


def n_layers_of(m):
    return len(range(m, DEPTH, N_MIXERS))


def rms_norm(x, g):
    xf = x.astype(jnp.float32)
    y = xf * lax.rsqrt(jnp.mean(xf * xf, axis=-1, keepdims=True) + EPS)
    return (y * g.astype(jnp.float32)).astype(x.dtype)


def causal_depthwise_conv(x, w, b):
    k_width = w.shape[0]
    seq = x.shape[1]
    xp = jnp.pad(x, ((0, 0), (k_width - 1, 0), (0, 0)))
    y = b
    for k in range(k_width):
        y = y + w[k] * xp[:, k:k + seq]
    return y


def linear_scan_combine(left, right):
    a_l, b_l = left
    a_r, b_r = right
    return a_r * a_l, a_r * b_l + b_r


def pool_mixer(x, w, b, scale):
    bsz, seq, _ = x.shape
    xf = x.astype(jnp.float32)
    cs = jnp.pad(jnp.cumsum(xf, axis=1), ((0, 0), (1, 0), (0, 0)))
    pos = jnp.arange(seq)
    groups = []
    for gi, w_len in enumerate(POOL_WINDOWS):
        c = cs[..., gi * POOL_GROUP:(gi + 1) * POOL_GROUP]
        lo = jnp.maximum(pos + 1 - w_len, 0)
        window_sum = c[:, 1:] - jnp.take(c, lo, axis=1)
        count = (pos + 1 - lo).astype(jnp.float32)[:, None]
        groups.append(window_sum / count - xf[..., gi * POOL_GROUP:(gi + 1) * POOL_GROUP])
    d = jnp.stack(groups, axis=2)
    y = jnp.einsum('bsgc,gcd->bsgd', d, w.astype(jnp.float32)).reshape(bsz, seq, D_MODEL) + b
    return (scale * y).astype(x.dtype)


def s5_mixer(x, lam_re, lam_im, log_dt, b_re, b_im, c_re, c_im, d_skip, w_out, b_out):
    f32 = jnp.float32
    bsz, seq, _ = x.shape
    xf = x.astype(f32)
    u = xf.reshape(bsz, seq, S5_GROUPS, S5_GROUP)
    lam = lax.complex(jnp.minimum(lam_re.astype(f32), -1e-4), lam_im.astype(f32))
    dt = jnp.exp(log_dt.astype(f32))[:, None]
    lam_bar = jnp.exp(lam * dt)
    b_bar = ((lam_bar - 1.0) / lam)[..., None] * lax.complex(b_re.astype(f32), b_im.astype(f32))
    bu = lax.complex(jnp.einsum('bsgh,gph->bsgp', u, jnp.real(b_bar)),
                     jnp.einsum('bsgh,gph->bsgp', u, jnp.imag(b_bar)))
    a = jnp.broadcast_to(lam_bar, (seq,) + lam_bar.shape)[None]
    _, states = lax.associative_scan(linear_scan_combine, (a, bu), axis=1)
    y = (jnp.einsum('bsgp,ghp->bsgh', jnp.real(states), c_re.astype(f32))
         - jnp.einsum('bsgp,ghp->bsgh', jnp.imag(states), c_im.astype(f32)))
    y = y.reshape(bsz, seq, D_MODEL) + d_skip.astype(f32) * xf
    y = jax.nn.gelu(y).astype(x.dtype)
    val, gate = jnp.split(y @ w_out + b_out, 2, axis=-1)
    return val * jax.nn.sigmoid(gate)


def rglru_mixer(x, w_in, conv_w, conv_b, w_a, b_a, w_x, b_x, lam, w_out):
    f32 = jnp.float32
    bsz, seq, _ = x.shape
    gate_branch, rec = jnp.split(x @ w_in, 2, axis=-1)
    rec = causal_depthwise_conv(rec, conv_w, conv_b).astype(f32)
    rb = rec.reshape(bsz, seq, LRU_BLOCKS, LRU_BLOCK)
    r = jax.nn.sigmoid(jnp.einsum('bsnc,ncd->bsnd', rb, w_a.astype(f32)).reshape(bsz, seq, LRU_WIDTH) + b_a)
    i = jax.nn.sigmoid(jnp.einsum('bsnc,ncd->bsnd', rb, w_x.astype(f32)).reshape(bsz, seq, LRU_WIDTH) + b_x)
    log_a = -LRU_C * r * jax.nn.softplus(-lam.astype(f32))
    a = jnp.exp(log_a)
    mult = jnp.sqrt(-jnp.expm1(2.0 * log_a))
    _, h = lax.associative_scan(linear_scan_combine, (a, mult * (i * rec)), axis=1)
    y = jax.nn.gelu(gate_branch.astype(f32)) * h
    return y.astype(x.dtype) @ w_out


def stick_breaking_mixer(x, w_qkv, q_g, k_g, w_o):
    bsz, seq, _ = x.shape
    q, k, v = jnp.split(x @ w_qkv, 3, axis=-1)
    to_heads = lambda t: t.reshape(bsz, seq, SB_HEADS, SB_HEAD_DIM).transpose(0, 2, 1, 3)
    q = rms_norm(to_heads(q), q_g)
    k = rms_norm(to_heads(k), k_g)
    v = to_heads(v)
    scale = 1.0 / math.sqrt(SB_HEAD_DIM)
    outs = []
    for start in range(0, seq, SB_Q_BLOCK):
        end = start + SB_Q_BLOCK
        kb, vb = k[:, :, :end], v[:, :, :end]
        z = jnp.einsum('bhtd,bhsd->bhts', q[:, :, start:end], kb).astype(jnp.float32) * scale
        t_pos = start + jnp.arange(SB_Q_BLOCK)[:, None]
        s_pos = jnp.arange(end)[None, :]
        mask = s_pos < t_pos
        log_1m_beta = jnp.where(mask, jax.nn.log_sigmoid(-z), 0.0)
        rest = lax.cumsum(log_1m_beta, axis=3, reverse=True) - log_1m_beta
        att = jnp.where(mask, jnp.exp(jax.nn.log_sigmoid(z) + rest), 0.0)
        outs.append(jnp.einsum('bhts,bhsd->bhtd', att.astype(v.dtype), vb))
    o = jnp.concatenate(outs, axis=2).transpose(0, 2, 1, 3).reshape(bsz, seq, D_MODEL)
    return o @ w_o


def conv_ffn(x, w_in, conv_w, conv_b, w_out):
    h = causal_depthwise_conv(x @ w_in, conv_w, conv_b)
    val, gate = jnp.split(h, 2, axis=-1)
    return (jax.nn.silu(gate) * val) @ w_out


def _fwd_setup_inputs(seed: int = 0) -> dict:
    key = jax.random.key(seed)
    ks = iter(jax.random.split(key, 40))
    nrm = lambda shape, std: jax.random.normal(next(ks), shape, jnp.float32) * std
    gain = lambda shape: 1.0 + nrm(shape, 0.02)
    nA, nB, nC, nD = (n_layers_of(m) for m in range(N_MIXERS))
    G, P, H = S5_GROUPS, S5_STATE, S5_GROUP
    a0 = jax.random.uniform(next(ks), (nC, LRU_WIDTH), jnp.float32, 0.9, 0.999)
    return {
        "x": nrm((BATCH, SEQ, D_MODEL), 1.0),
        "norm_mix_g": gain((DEPTH, D_MODEL)),
        "norm_ffn_g": gain((DEPTH, D_MODEL)),
        "pool_w": nrm((nA, len(POOL_WINDOWS), POOL_GROUP, POOL_GROUP), POOL_GROUP ** -0.5),
        "pool_b": nrm((nA, D_MODEL), 0.01),
        "pool_scale": 1.0 + nrm((nA, D_MODEL), 0.1),
        "s5_lam_re": -0.5 + nrm((nA * 0 + nB, G, P), 0.01),
        "s5_lam_im": math.pi * jnp.arange(P, dtype=jnp.float32) + nrm((nB, G, P), 0.01),
        "s5_log_dt": jax.random.uniform(next(ks), (nB, G), jnp.float32, math.log(S5_DT_MIN), math.log(S5_DT_MAX)),
        "s5_b_re": nrm((nB, G, P, H), (2 * H) ** -0.5),
        "s5_b_im": nrm((nB, G, P, H), (2 * H) ** -0.5),
        "s5_c_re": nrm((nB, G, H, P), P ** -0.5),
        "s5_c_im": nrm((nB, G, H, P), P ** -0.5),
        "s5_d": nrm((nB, D_MODEL), 1.0),
        "s5_w_out": nrm((nB, D_MODEL, 2 * D_MODEL), D_MODEL ** -0.5),
        "s5_b_out": nrm((nB, 2 * D_MODEL), 0.01),
        "lru_w_in": nrm((nC, D_MODEL, 2 * LRU_WIDTH), D_MODEL ** -0.5),
        "lru_conv_w": nrm((nC, LRU_CONV, LRU_WIDTH), LRU_CONV ** -0.5),
        "lru_conv_b": nrm((nC, LRU_WIDTH), 0.01),
        "lru_w_a": nrm((nC, LRU_BLOCKS, LRU_BLOCK, LRU_BLOCK), LRU_BLOCK ** -0.5),
        "lru_b_a": nrm((nC, LRU_WIDTH), 0.01),
        "lru_w_x": nrm((nC, LRU_BLOCKS, LRU_BLOCK, LRU_BLOCK), LRU_BLOCK ** -0.5),
        "lru_b_x": nrm((nC, LRU_WIDTH), 0.01),
        "lru_lam": jnp.log(a0) - jnp.log1p(-a0),
        "lru_w_out": nrm((nC, LRU_WIDTH, D_MODEL), LRU_WIDTH ** -0.5),
        "sb_w_qkv": nrm((nD, D_MODEL, 3 * D_MODEL), D_MODEL ** -0.5),
        "sb_q_g": gain((nD, SB_HEAD_DIM)),
        "sb_k_g": gain((nD, SB_HEAD_DIM)),
        "sb_w_o": nrm((nD, D_MODEL, D_MODEL), D_MODEL ** -0.5),
        "ffn_w_in": nrm((DEPTH, D_MODEL, 2 * FFN_HIDDEN), D_MODEL ** -0.5),
        "ffn_conv_w": nrm((DEPTH, FFN_CONV, 2 * FFN_HIDDEN), FFN_CONV ** -0.5),
        "ffn_conv_b": nrm((DEPTH, 2 * FFN_HIDDEN), 0.01),
        "ffn_w_out": nrm((DEPTH, FFN_HIDDEN, D_MODEL), FFN_HIDDEN ** -0.5),
    }


def _fwd_reference(x, norm_mix_g, norm_ffn_g,
              pool_w, pool_b, pool_scale,
              s5_lam_re, s5_lam_im, s5_log_dt, s5_b_re, s5_b_im, s5_c_re, s5_c_im, s5_d, s5_w_out, s5_b_out,
              lru_w_in, lru_conv_w, lru_conv_b, lru_w_a, lru_b_a, lru_w_x, lru_b_x, lru_lam, lru_w_out,
              sb_w_qkv, sb_q_g, sb_k_g, sb_w_o,
              ffn_w_in, ffn_conv_w, ffn_conv_b, ffn_w_out):
    for layer in range(DEPTH):
        m, j = layer % N_MIXERS, layer // N_MIXERS
        h = rms_norm(x, norm_mix_g[layer])
        if m == 0:
            y = pool_mixer(h, pool_w[j], pool_b[j], pool_scale[j])
        elif m == 1:
            y = s5_mixer(h, s5_lam_re[j], s5_lam_im[j], s5_log_dt[j], s5_b_re[j], s5_b_im[j],
                         s5_c_re[j], s5_c_im[j], s5_d[j], s5_w_out[j], s5_b_out[j])
        elif m == 2:
            y = rglru_mixer(h, lru_w_in[j], lru_conv_w[j], lru_conv_b[j], lru_w_a[j], lru_b_a[j],
                            lru_w_x[j], lru_b_x[j], lru_lam[j], lru_w_out[j])
        else:
            y = stick_breaking_mixer(h, sb_w_qkv[j], sb_q_g[j], sb_k_g[j], sb_w_o[j])
        x = x + y.astype(x.dtype)
        f = conv_ffn(rms_norm(x, norm_ffn_g[layer]), ffn_w_in[layer], ffn_conv_w[layer], ffn_conv_b[layer], ffn_w_out[layer])
        x = x + f.astype(x.dtype)
    return x


import jax as _jax
import jax.numpy as _jnp

TWIN_FORMAT = 'train_step'
FWD_PARAMS = ['x', 'norm_mix_g', 'norm_ffn_g', 'pool_w', 'pool_b', 'pool_scale', 's5_lam_re', 's5_lam_im', 's5_log_dt', 's5_b_re', 's5_b_im', 's5_c_re', 's5_c_im', 's5_d', 's5_w_out', 's5_b_out', 'lru_w_in', 'lru_conv_w', 'lru_conv_b', 'lru_w_a', 'lru_b_a', 'lru_w_x', 'lru_b_x', 'lru_lam', 'lru_w_out', 'sb_w_qkv', 'sb_q_g', 'sb_k_g', 'sb_w_o', 'ffn_w_in', 'ffn_conv_w', 'ffn_conv_b', 'ffn_w_out']
TWIN_WEIGHTS = ['norm_mix_g', 'norm_ffn_g', 'pool_w', 'pool_b', 'pool_scale', 's5_lam_re', 's5_lam_im', 's5_log_dt', 's5_b_re', 's5_b_im', 's5_c_re', 's5_c_im', 's5_d', 's5_w_out', 's5_b_out', 'lru_w_in', 'lru_conv_w', 'lru_conv_b', 'lru_w_a', 'lru_b_a', 'lru_w_x', 'lru_b_x', 'lru_lam', 'lru_w_out', 'sb_w_qkv', 'sb_q_g', 'sb_k_g', 'sb_w_o', 'ffn_w_in', 'ffn_conv_w', 'ffn_conv_b', 'ffn_w_out']
TWIN_DIFF_INPUT = 'x'
TWIN_INPUTS = ['x', 'norm_mix_g', 'norm_ffn_g', 'pool_w', 'pool_b', 'pool_scale', 's5_lam_re', 's5_lam_im', 's5_log_dt', 's5_b_re', 's5_b_im', 's5_c_re', 's5_c_im', 's5_d', 's5_w_out', 's5_b_out', 'lru_w_in', 'lru_conv_w', 'lru_conv_b', 'lru_w_a', 'lru_b_a', 'lru_w_x', 'lru_b_x', 'lru_lam', 'lru_w_out', 'sb_w_qkv', 'sb_q_g', 'sb_k_g', 'sb_w_o', 'ffn_w_in', 'ffn_conv_w', 'ffn_conv_b', 'ffn_w_out', 'loss_target', 'm_norm_mix_g', 'm_norm_ffn_g', 'm_pool_w', 'm_pool_b', 'm_pool_scale', 'm_s5_lam_re', 'm_s5_lam_im', 'm_s5_log_dt', 'm_s5_b_re', 'm_s5_b_im', 'm_s5_c_re', 'm_s5_c_im', 'm_s5_d', 'm_s5_w_out', 'm_s5_b_out', 'm_lru_w_in', 'm_lru_conv_w', 'm_lru_conv_b', 'm_lru_w_a', 'm_lru_b_a', 'm_lru_w_x', 'm_lru_b_x', 'm_lru_lam', 'm_lru_w_out', 'm_sb_w_qkv', 'm_sb_q_g', 'm_sb_k_g', 'm_sb_w_o', 'm_ffn_w_in', 'm_ffn_conv_w', 'm_ffn_conv_b', 'm_ffn_w_out', 'v_norm_mix_g', 'v_norm_ffn_g', 'v_pool_w', 'v_pool_b', 'v_pool_scale', 'v_s5_lam_re', 'v_s5_lam_im', 'v_s5_log_dt', 'v_s5_b_re', 'v_s5_b_im', 'v_s5_c_re', 'v_s5_c_im', 'v_s5_d', 'v_s5_w_out', 'v_s5_b_out', 'v_lru_w_in', 'v_lru_conv_w', 'v_lru_conv_b', 'v_lru_w_a', 'v_lru_b_a', 'v_lru_w_x', 'v_lru_b_x', 'v_lru_lam', 'v_lru_w_out', 'v_sb_w_qkv', 'v_sb_q_g', 'v_sb_k_g', 'v_sb_w_o', 'v_ffn_w_in', 'v_ffn_conv_w', 'v_ffn_conv_b', 'v_ffn_w_out']
TWIN_OUTPUTS = ['loss', 'grad_x', 'grad_norm_mix_g', 'grad_norm_ffn_g', 'grad_pool_w', 'grad_pool_b', 'grad_pool_scale', 'grad_s5_lam_re', 'grad_s5_lam_im', 'grad_s5_log_dt', 'grad_s5_b_re', 'grad_s5_b_im', 'grad_s5_c_re', 'grad_s5_c_im', 'grad_s5_d', 'grad_s5_w_out', 'grad_s5_b_out', 'grad_lru_w_in', 'grad_lru_conv_w', 'grad_lru_conv_b', 'grad_lru_w_a', 'grad_lru_b_a', 'grad_lru_w_x', 'grad_lru_b_x', 'grad_lru_lam', 'grad_lru_w_out', 'grad_sb_w_qkv', 'grad_sb_q_g', 'grad_sb_k_g', 'grad_sb_w_o', 'grad_ffn_w_in', 'grad_ffn_conv_w', 'grad_ffn_conv_b', 'grad_ffn_w_out', 'delta_norm_mix_g', 'delta_norm_ffn_g', 'delta_pool_w', 'delta_pool_b', 'delta_pool_scale', 'delta_s5_lam_re', 'delta_s5_lam_im', 'delta_s5_log_dt', 'delta_s5_b_re', 'delta_s5_b_im', 'delta_s5_c_re', 'delta_s5_c_im', 'delta_s5_d', 'delta_s5_w_out', 'delta_s5_b_out', 'delta_lru_w_in', 'delta_lru_conv_w', 'delta_lru_conv_b', 'delta_lru_w_a', 'delta_lru_b_a', 'delta_lru_w_x', 'delta_lru_b_x', 'delta_lru_lam', 'delta_lru_w_out', 'delta_sb_w_qkv', 'delta_sb_q_g', 'delta_sb_k_g', 'delta_sb_w_o', 'delta_ffn_w_in', 'delta_ffn_conv_w', 'delta_ffn_conv_b', 'delta_ffn_w_out', 'new_m_norm_mix_g', 'new_m_norm_ffn_g', 'new_m_pool_w', 'new_m_pool_b', 'new_m_pool_scale', 'new_m_s5_lam_re', 'new_m_s5_lam_im', 'new_m_s5_log_dt', 'new_m_s5_b_re', 'new_m_s5_b_im', 'new_m_s5_c_re', 'new_m_s5_c_im', 'new_m_s5_d', 'new_m_s5_w_out', 'new_m_s5_b_out', 'new_m_lru_w_in', 'new_m_lru_conv_w', 'new_m_lru_conv_b', 'new_m_lru_w_a', 'new_m_lru_b_a', 'new_m_lru_w_x', 'new_m_lru_b_x', 'new_m_lru_lam', 'new_m_lru_w_out', 'new_m_sb_w_qkv', 'new_m_sb_q_g', 'new_m_sb_k_g', 'new_m_sb_w_o', 'new_m_ffn_w_in', 'new_m_ffn_conv_w', 'new_m_ffn_conv_b', 'new_m_ffn_w_out', 'new_v_norm_mix_g', 'new_v_norm_ffn_g', 'new_v_pool_w', 'new_v_pool_b', 'new_v_pool_scale', 'new_v_s5_lam_re', 'new_v_s5_lam_im', 'new_v_s5_log_dt', 'new_v_s5_b_re', 'new_v_s5_b_im', 'new_v_s5_c_re', 'new_v_s5_c_im', 'new_v_s5_d', 'new_v_s5_w_out', 'new_v_s5_b_out', 'new_v_lru_w_in', 'new_v_lru_conv_w', 'new_v_lru_conv_b', 'new_v_lru_w_a', 'new_v_lru_b_a', 'new_v_lru_w_x', 'new_v_lru_b_x', 'new_v_lru_lam', 'new_v_lru_w_out', 'new_v_sb_w_qkv', 'new_v_sb_q_g', 'new_v_sb_k_g', 'new_v_sb_w_o', 'new_v_ffn_w_in', 'new_v_ffn_conv_w', 'new_v_ffn_conv_b', 'new_v_ffn_w_out']
TWIN_LEAF_KINDS = {'loss': 'loss', 'grad_x': 'grad_x', 'grad_norm_mix_g': 'grad_w', 'grad_norm_ffn_g': 'grad_w', 'grad_pool_w': 'grad_w', 'grad_pool_b': 'grad_w', 'grad_pool_scale': 'grad_w', 'grad_s5_lam_re': 'grad_w', 'grad_s5_lam_im': 'grad_w', 'grad_s5_log_dt': 'grad_w', 'grad_s5_b_re': 'grad_w', 'grad_s5_b_im': 'grad_w', 'grad_s5_c_re': 'grad_w', 'grad_s5_c_im': 'grad_w', 'grad_s5_d': 'grad_w', 'grad_s5_w_out': 'grad_w', 'grad_s5_b_out': 'grad_w', 'grad_lru_w_in': 'grad_w', 'grad_lru_conv_w': 'grad_w', 'grad_lru_conv_b': 'grad_w', 'grad_lru_w_a': 'grad_w', 'grad_lru_b_a': 'grad_w', 'grad_lru_w_x': 'grad_w', 'grad_lru_b_x': 'grad_w', 'grad_lru_lam': 'grad_w', 'grad_lru_w_out': 'grad_w', 'grad_sb_w_qkv': 'grad_w', 'grad_sb_q_g': 'grad_w', 'grad_sb_k_g': 'grad_w', 'grad_sb_w_o': 'grad_w', 'grad_ffn_w_in': 'grad_w', 'grad_ffn_conv_w': 'grad_w', 'grad_ffn_conv_b': 'grad_w', 'grad_ffn_w_out': 'grad_w', 'delta_norm_mix_g': 'delta_w', 'delta_norm_ffn_g': 'delta_w', 'delta_pool_w': 'delta_w', 'delta_pool_b': 'delta_w', 'delta_pool_scale': 'delta_w', 'delta_s5_lam_re': 'delta_w', 'delta_s5_lam_im': 'delta_w', 'delta_s5_log_dt': 'delta_w', 'delta_s5_b_re': 'delta_w', 'delta_s5_b_im': 'delta_w', 'delta_s5_c_re': 'delta_w', 'delta_s5_c_im': 'delta_w', 'delta_s5_d': 'delta_w', 'delta_s5_w_out': 'delta_w', 'delta_s5_b_out': 'delta_w', 'delta_lru_w_in': 'delta_w', 'delta_lru_conv_w': 'delta_w', 'delta_lru_conv_b': 'delta_w', 'delta_lru_w_a': 'delta_w', 'delta_lru_b_a': 'delta_w', 'delta_lru_w_x': 'delta_w', 'delta_lru_b_x': 'delta_w', 'delta_lru_lam': 'delta_w', 'delta_lru_w_out': 'delta_w', 'delta_sb_w_qkv': 'delta_w', 'delta_sb_q_g': 'delta_w', 'delta_sb_k_g': 'delta_w', 'delta_sb_w_o': 'delta_w', 'delta_ffn_w_in': 'delta_w', 'delta_ffn_conv_w': 'delta_w', 'delta_ffn_conv_b': 'delta_w', 'delta_ffn_w_out': 'delta_w', 'new_m_norm_mix_g': 'new_m', 'new_m_norm_ffn_g': 'new_m', 'new_m_pool_w': 'new_m', 'new_m_pool_b': 'new_m', 'new_m_pool_scale': 'new_m', 'new_m_s5_lam_re': 'new_m', 'new_m_s5_lam_im': 'new_m', 'new_m_s5_log_dt': 'new_m', 'new_m_s5_b_re': 'new_m', 'new_m_s5_b_im': 'new_m', 'new_m_s5_c_re': 'new_m', 'new_m_s5_c_im': 'new_m', 'new_m_s5_d': 'new_m', 'new_m_s5_w_out': 'new_m', 'new_m_s5_b_out': 'new_m', 'new_m_lru_w_in': 'new_m', 'new_m_lru_conv_w': 'new_m', 'new_m_lru_conv_b': 'new_m', 'new_m_lru_w_a': 'new_m', 'new_m_lru_b_a': 'new_m', 'new_m_lru_w_x': 'new_m', 'new_m_lru_b_x': 'new_m', 'new_m_lru_lam': 'new_m', 'new_m_lru_w_out': 'new_m', 'new_m_sb_w_qkv': 'new_m', 'new_m_sb_q_g': 'new_m', 'new_m_sb_k_g': 'new_m', 'new_m_sb_w_o': 'new_m', 'new_m_ffn_w_in': 'new_m', 'new_m_ffn_conv_w': 'new_m', 'new_m_ffn_conv_b': 'new_m', 'new_m_ffn_w_out': 'new_m', 'new_v_norm_mix_g': 'new_v', 'new_v_norm_ffn_g': 'new_v', 'new_v_pool_w': 'new_v', 'new_v_pool_b': 'new_v', 'new_v_pool_scale': 'new_v', 'new_v_s5_lam_re': 'new_v', 'new_v_s5_lam_im': 'new_v', 'new_v_s5_log_dt': 'new_v', 'new_v_s5_b_re': 'new_v', 'new_v_s5_b_im': 'new_v', 'new_v_s5_c_re': 'new_v', 'new_v_s5_c_im': 'new_v', 'new_v_s5_d': 'new_v', 'new_v_s5_w_out': 'new_v', 'new_v_s5_b_out': 'new_v', 'new_v_lru_w_in': 'new_v', 'new_v_lru_conv_w': 'new_v', 'new_v_lru_conv_b': 'new_v', 'new_v_lru_w_a': 'new_v', 'new_v_lru_b_a': 'new_v', 'new_v_lru_w_x': 'new_v', 'new_v_lru_b_x': 'new_v', 'new_v_lru_lam': 'new_v', 'new_v_lru_w_out': 'new_v', 'new_v_sb_w_qkv': 'new_v', 'new_v_sb_q_g': 'new_v', 'new_v_sb_k_g': 'new_v', 'new_v_sb_w_o': 'new_v', 'new_v_ffn_w_in': 'new_v', 'new_v_ffn_conv_w': 'new_v', 'new_v_ffn_conv_b': 'new_v', 'new_v_ffn_w_out': 'new_v'}


def _forward(args):
    return _fwd_reference(*[args[k] for k in FWD_PARAMS])


def _output_shape():
    out = _jax.eval_shape(lambda: _forward(_fwd_setup_inputs(0)))
    return out.shape, out.dtype

N_MICROBATCH = 1
ADAM_LR = 0.001
ADAM_B1 = 0.9
ADAM_B2 = 0.999
ADAM_EPS = 1e-08
ADAM_WD = 0.01
ADAM_STEP = 10
PER_EXAMPLE_BATCH_AXIS = {'x': 0, 'loss_target': 0}
SHARED_INPUTS = []
_WEIGHT_DTYPES = {'norm_mix_g': _jnp.float32, 'norm_ffn_g': _jnp.float32, 'pool_w': _jnp.float32, 'pool_b': _jnp.float32, 'pool_scale': _jnp.float32, 's5_lam_re': _jnp.float32, 's5_lam_im': _jnp.float32, 's5_log_dt': _jnp.float32, 's5_b_re': _jnp.float32, 's5_b_im': _jnp.float32, 's5_c_re': _jnp.float32, 's5_c_im': _jnp.float32, 's5_d': _jnp.float32, 's5_w_out': _jnp.float32, 's5_b_out': _jnp.float32, 'lru_w_in': _jnp.float32, 'lru_conv_w': _jnp.float32, 'lru_conv_b': _jnp.float32, 'lru_w_a': _jnp.float32, 'lru_b_a': _jnp.float32, 'lru_w_x': _jnp.float32, 'lru_b_x': _jnp.float32, 'lru_lam': _jnp.float32, 'lru_w_out': _jnp.float32, 'sb_w_qkv': _jnp.float32, 'sb_q_g': _jnp.float32, 'sb_k_g': _jnp.float32, 'sb_w_o': _jnp.float32, 'ffn_w_in': _jnp.float32, 'ffn_conv_w': _jnp.float32, 'ffn_conv_b': _jnp.float32, 'ffn_w_out': _jnp.float32}
MOMENT_SCALE = {'norm_mix_g': 1.601752e+01, 'norm_ffn_g': 2.620197e+01, 'pool_w': 1.700833e+00, 'pool_b': 2.554175e+01, 'pool_scale': 2.428938e+01, 's5_lam_re': 3.393967e-02, 's5_lam_im': 2.812418e-02, 's5_log_dt': 1.717771e+01, 's5_b_re': 2.058777e-02, 's5_b_im': 2.134838e-02, 's5_c_re': 2.997562e-02, 's5_c_im': 2.946688e-02, 's5_d': 7.271618e+00, 's5_w_out': 2.878395e+00, 's5_b_out': 8.309195e+00, 'lru_w_in': 2.124143e+00, 'lru_conv_w': 3.899812e+00, 'lru_conv_b': 1.441949e+01, 'lru_w_a': 4.590500e-01, 'lru_b_a': 5.198330e-01, 'lru_w_x': 9.539512e-01, 'lru_b_x': 2.362602e+00, 'lru_lam': 1.175763e+00, 'lru_w_out': 8.796994e-01, 'sb_w_qkv': 5.360641e-01, 'sb_q_g': 3.200669e+01, 'sb_k_g': 3.204207e+01, 'sb_w_o': 8.854679e-01, 'ffn_w_in': 4.546013e-01, 'ffn_conv_w': 3.561149e+00, 'ffn_conv_b': 3.898646e+00, 'ffn_w_out': 5.555724e-01}


def _to_microbatches(a, axis):
    t = _jnp.moveaxis(a, axis, 0)
    t = t.reshape((N_MICROBATCH, t.shape[0] // N_MICROBATCH) + t.shape[1:])
    return _jnp.moveaxis(t, 1, axis + 1)


def setup_inputs(seed: int = 0) -> dict:
    inp = _fwd_setup_inputs(seed)
    key = _jax.random.fold_in(_jax.random.key(seed), 7919)
    shape, _ = _output_shape()
    out = dict(inp)
    out["loss_target"] = _jax.random.normal(_jax.random.fold_in(key, 0), shape, _jnp.float32)
    for i, name in enumerate(TWIN_WEIGHTS):
        w = inp[name].astype(_jnp.float32)
        if MOMENT_SCALE is None:
            s = _jnp.sqrt(_jnp.mean(_jnp.square(w)) + 1e-30)
        else:
            s = MOMENT_SCALE[name]
        km, kv = _jax.random.split(_jax.random.fold_in(key, i + 1))
        out[name] = w
        out["m_" + name] = s * _jax.random.normal(km, w.shape, _jnp.float32)
        out["v_" + name] = (s * s) * _jax.random.uniform(kv, w.shape, _jnp.float32, 0.5, 1.5)
    if N_MICROBATCH > 1:
        for name, axis in PER_EXAMPLE_BATCH_AXIS.items():
            out[name] = _to_microbatches(out[name], axis)
    return {'x': out['x'], 'norm_mix_g': out['norm_mix_g'], 'norm_ffn_g': out['norm_ffn_g'], 'pool_w': out['pool_w'], 'pool_b': out['pool_b'], 'pool_scale': out['pool_scale'], 's5_lam_re': out['s5_lam_re'], 's5_lam_im': out['s5_lam_im'], 's5_log_dt': out['s5_log_dt'], 's5_b_re': out['s5_b_re'], 's5_b_im': out['s5_b_im'], 's5_c_re': out['s5_c_re'], 's5_c_im': out['s5_c_im'], 's5_d': out['s5_d'], 's5_w_out': out['s5_w_out'], 's5_b_out': out['s5_b_out'], 'lru_w_in': out['lru_w_in'], 'lru_conv_w': out['lru_conv_w'], 'lru_conv_b': out['lru_conv_b'], 'lru_w_a': out['lru_w_a'], 'lru_b_a': out['lru_b_a'], 'lru_w_x': out['lru_w_x'], 'lru_b_x': out['lru_b_x'], 'lru_lam': out['lru_lam'], 'lru_w_out': out['lru_w_out'], 'sb_w_qkv': out['sb_w_qkv'], 'sb_q_g': out['sb_q_g'], 'sb_k_g': out['sb_k_g'], 'sb_w_o': out['sb_w_o'], 'ffn_w_in': out['ffn_w_in'], 'ffn_conv_w': out['ffn_conv_w'], 'ffn_conv_b': out['ffn_conv_b'], 'ffn_w_out': out['ffn_w_out'], 'loss_target': out['loss_target'], 'm_norm_mix_g': out['m_norm_mix_g'], 'm_norm_ffn_g': out['m_norm_ffn_g'], 'm_pool_w': out['m_pool_w'], 'm_pool_b': out['m_pool_b'], 'm_pool_scale': out['m_pool_scale'], 'm_s5_lam_re': out['m_s5_lam_re'], 'm_s5_lam_im': out['m_s5_lam_im'], 'm_s5_log_dt': out['m_s5_log_dt'], 'm_s5_b_re': out['m_s5_b_re'], 'm_s5_b_im': out['m_s5_b_im'], 'm_s5_c_re': out['m_s5_c_re'], 'm_s5_c_im': out['m_s5_c_im'], 'm_s5_d': out['m_s5_d'], 'm_s5_w_out': out['m_s5_w_out'], 'm_s5_b_out': out['m_s5_b_out'], 'm_lru_w_in': out['m_lru_w_in'], 'm_lru_conv_w': out['m_lru_conv_w'], 'm_lru_conv_b': out['m_lru_conv_b'], 'm_lru_w_a': out['m_lru_w_a'], 'm_lru_b_a': out['m_lru_b_a'], 'm_lru_w_x': out['m_lru_w_x'], 'm_lru_b_x': out['m_lru_b_x'], 'm_lru_lam': out['m_lru_lam'], 'm_lru_w_out': out['m_lru_w_out'], 'm_sb_w_qkv': out['m_sb_w_qkv'], 'm_sb_q_g': out['m_sb_q_g'], 'm_sb_k_g': out['m_sb_k_g'], 'm_sb_w_o': out['m_sb_w_o'], 'm_ffn_w_in': out['m_ffn_w_in'], 'm_ffn_conv_w': out['m_ffn_conv_w'], 'm_ffn_conv_b': out['m_ffn_conv_b'], 'm_ffn_w_out': out['m_ffn_w_out'], 'v_norm_mix_g': out['v_norm_mix_g'], 'v_norm_ffn_g': out['v_norm_ffn_g'], 'v_pool_w': out['v_pool_w'], 'v_pool_b': out['v_pool_b'], 'v_pool_scale': out['v_pool_scale'], 'v_s5_lam_re': out['v_s5_lam_re'], 'v_s5_lam_im': out['v_s5_lam_im'], 'v_s5_log_dt': out['v_s5_log_dt'], 'v_s5_b_re': out['v_s5_b_re'], 'v_s5_b_im': out['v_s5_b_im'], 'v_s5_c_re': out['v_s5_c_re'], 'v_s5_c_im': out['v_s5_c_im'], 'v_s5_d': out['v_s5_d'], 'v_s5_w_out': out['v_s5_w_out'], 'v_s5_b_out': out['v_s5_b_out'], 'v_lru_w_in': out['v_lru_w_in'], 'v_lru_conv_w': out['v_lru_conv_w'], 'v_lru_conv_b': out['v_lru_conv_b'], 'v_lru_w_a': out['v_lru_w_a'], 'v_lru_b_a': out['v_lru_b_a'], 'v_lru_w_x': out['v_lru_w_x'], 'v_lru_b_x': out['v_lru_b_x'], 'v_lru_lam': out['v_lru_lam'], 'v_lru_w_out': out['v_lru_w_out'], 'v_sb_w_qkv': out['v_sb_w_qkv'], 'v_sb_q_g': out['v_sb_q_g'], 'v_sb_k_g': out['v_sb_k_g'], 'v_sb_w_o': out['v_sb_w_o'], 'v_ffn_w_in': out['v_ffn_w_in'], 'v_ffn_conv_w': out['v_ffn_conv_w'], 'v_ffn_conv_b': out['v_ffn_conv_b'], 'v_ffn_w_out': out['v_ffn_w_out']}


def _loss(weights, diff, rest, loss_target):
    with _jax.named_scope("forward"):
        args = {**rest, TWIN_DIFF_INPUT: diff, **{k: w.astype(_WEIGHT_DTYPES[k]) for k, w in weights.items()}}
        y = _forward(args)
    with _jax.named_scope("loss_head"):
        err = _jnp.square(y.astype(_jnp.float32) - loss_target)
        return 0.5 * _jnp.sum(_jnp.mean(err, axis=-1)) if err.ndim else 0.5 * err


def _adamw(w, g, m, v):
    m = ADAM_B1 * m + (1.0 - ADAM_B1) * g
    v = ADAM_B2 * v + (1.0 - ADAM_B2) * _jnp.square(g)
    m_hat = m / (1.0 - ADAM_B1 ** ADAM_STEP)
    v_hat = v / (1.0 - ADAM_B2 ** ADAM_STEP)
    delta = -ADAM_LR * (m_hat / (_jnp.sqrt(v_hat) + ADAM_EPS) + ADAM_WD * w)
    return delta, m, v


def reference(x, norm_mix_g, norm_ffn_g, pool_w, pool_b, pool_scale, s5_lam_re, s5_lam_im, s5_log_dt, s5_b_re, s5_b_im, s5_c_re, s5_c_im, s5_d, s5_w_out, s5_b_out, lru_w_in, lru_conv_w, lru_conv_b, lru_w_a, lru_b_a, lru_w_x, lru_b_x, lru_lam, lru_w_out, sb_w_qkv, sb_q_g, sb_k_g, sb_w_o, ffn_w_in, ffn_conv_w, ffn_conv_b, ffn_w_out, loss_target, m_norm_mix_g, m_norm_ffn_g, m_pool_w, m_pool_b, m_pool_scale, m_s5_lam_re, m_s5_lam_im, m_s5_log_dt, m_s5_b_re, m_s5_b_im, m_s5_c_re, m_s5_c_im, m_s5_d, m_s5_w_out, m_s5_b_out, m_lru_w_in, m_lru_conv_w, m_lru_conv_b, m_lru_w_a, m_lru_b_a, m_lru_w_x, m_lru_b_x, m_lru_lam, m_lru_w_out, m_sb_w_qkv, m_sb_q_g, m_sb_k_g, m_sb_w_o, m_ffn_w_in, m_ffn_conv_w, m_ffn_conv_b, m_ffn_w_out, v_norm_mix_g, v_norm_ffn_g, v_pool_w, v_pool_b, v_pool_scale, v_s5_lam_re, v_s5_lam_im, v_s5_log_dt, v_s5_b_re, v_s5_b_im, v_s5_c_re, v_s5_c_im, v_s5_d, v_s5_w_out, v_s5_b_out, v_lru_w_in, v_lru_conv_w, v_lru_conv_b, v_lru_w_a, v_lru_b_a, v_lru_w_x, v_lru_b_x, v_lru_lam, v_lru_w_out, v_sb_w_qkv, v_sb_q_g, v_sb_k_g, v_sb_w_o, v_ffn_w_in, v_ffn_conv_w, v_ffn_conv_b, v_ffn_w_out):
    given = dict(x=x, norm_mix_g=norm_mix_g, norm_ffn_g=norm_ffn_g, pool_w=pool_w, pool_b=pool_b, pool_scale=pool_scale, s5_lam_re=s5_lam_re, s5_lam_im=s5_lam_im, s5_log_dt=s5_log_dt, s5_b_re=s5_b_re, s5_b_im=s5_b_im, s5_c_re=s5_c_re, s5_c_im=s5_c_im, s5_d=s5_d, s5_w_out=s5_w_out, s5_b_out=s5_b_out, lru_w_in=lru_w_in, lru_conv_w=lru_conv_w, lru_conv_b=lru_conv_b, lru_w_a=lru_w_a, lru_b_a=lru_b_a, lru_w_x=lru_w_x, lru_b_x=lru_b_x, lru_lam=lru_lam, lru_w_out=lru_w_out, sb_w_qkv=sb_w_qkv, sb_q_g=sb_q_g, sb_k_g=sb_k_g, sb_w_o=sb_w_o, ffn_w_in=ffn_w_in, ffn_conv_w=ffn_conv_w, ffn_conv_b=ffn_conv_b, ffn_w_out=ffn_w_out, loss_target=loss_target, m_norm_mix_g=m_norm_mix_g, m_norm_ffn_g=m_norm_ffn_g, m_pool_w=m_pool_w, m_pool_b=m_pool_b, m_pool_scale=m_pool_scale, m_s5_lam_re=m_s5_lam_re, m_s5_lam_im=m_s5_lam_im, m_s5_log_dt=m_s5_log_dt, m_s5_b_re=m_s5_b_re, m_s5_b_im=m_s5_b_im, m_s5_c_re=m_s5_c_re, m_s5_c_im=m_s5_c_im, m_s5_d=m_s5_d, m_s5_w_out=m_s5_w_out, m_s5_b_out=m_s5_b_out, m_lru_w_in=m_lru_w_in, m_lru_conv_w=m_lru_conv_w, m_lru_conv_b=m_lru_conv_b, m_lru_w_a=m_lru_w_a, m_lru_b_a=m_lru_b_a, m_lru_w_x=m_lru_w_x, m_lru_b_x=m_lru_b_x, m_lru_lam=m_lru_lam, m_lru_w_out=m_lru_w_out, m_sb_w_qkv=m_sb_w_qkv, m_sb_q_g=m_sb_q_g, m_sb_k_g=m_sb_k_g, m_sb_w_o=m_sb_w_o, m_ffn_w_in=m_ffn_w_in, m_ffn_conv_w=m_ffn_conv_w, m_ffn_conv_b=m_ffn_conv_b, m_ffn_w_out=m_ffn_w_out, v_norm_mix_g=v_norm_mix_g, v_norm_ffn_g=v_norm_ffn_g, v_pool_w=v_pool_w, v_pool_b=v_pool_b, v_pool_scale=v_pool_scale, v_s5_lam_re=v_s5_lam_re, v_s5_lam_im=v_s5_lam_im, v_s5_log_dt=v_s5_log_dt, v_s5_b_re=v_s5_b_re, v_s5_b_im=v_s5_b_im, v_s5_c_re=v_s5_c_re, v_s5_c_im=v_s5_c_im, v_s5_d=v_s5_d, v_s5_w_out=v_s5_w_out, v_s5_b_out=v_s5_b_out, v_lru_w_in=v_lru_w_in, v_lru_conv_w=v_lru_conv_w, v_lru_conv_b=v_lru_conv_b, v_lru_w_a=v_lru_w_a, v_lru_b_a=v_lru_b_a, v_lru_w_x=v_lru_w_x, v_lru_b_x=v_lru_b_x, v_lru_lam=v_lru_lam, v_lru_w_out=v_lru_w_out, v_sb_w_qkv=v_sb_w_qkv, v_sb_q_g=v_sb_q_g, v_sb_k_g=v_sb_k_g, v_sb_w_o=v_sb_w_o, v_ffn_w_in=v_ffn_w_in, v_ffn_conv_w=v_ffn_conv_w, v_ffn_conv_b=v_ffn_conv_b, v_ffn_w_out=v_ffn_w_out)
    weights = {n: given[n] for n in TWIN_WEIGHTS}
    shared = {n: given[n] for n in SHARED_INPUTS}
    per_example = {n: given[n] for n in ['x']}
    grad_fn = _jax.value_and_grad(_loss, argnums=(0, 1))

    def one_microbatch(ex, loss_target):
        ex = dict(ex)
        diff = ex.pop(TWIN_DIFF_INPUT)
        return grad_fn(weights, diff, {**shared, **ex}, loss_target)

    if N_MICROBATCH == 1:
        loss, (grad_w, grad_x) = one_microbatch(per_example, given["loss_target"])
    else:
        def body(carry, xs):
            loss_sum, grad_sum = carry
            l_k, (gw_k, gx_k) = one_microbatch(xs[0], xs[1])
            with _jax.named_scope("update"):
                return (loss_sum + l_k, _jax.tree.map(_jnp.add, grad_sum, gw_k)), gx_k

        init = (_jnp.zeros((), _jnp.float32), _jax.tree.map(_jnp.zeros_like, weights))
        (loss, grad_w), grad_x = _jax.lax.scan(body, init, (per_example, given["loss_target"]))
    with _jax.named_scope("update"):
        delta_w, new_m, new_v = {}, {}, {}
        for n in TWIN_WEIGHTS:
            delta_w[n], new_m[n], new_v[n] = _adamw(weights[n], grad_w[n], given["m_" + n], given["v_" + n])
    return (loss, grad_x, *[grad_w[n] for n in TWIN_WEIGHTS], *[delta_w[n] for n in TWIN_WEIGHTS],
            *[new_m[n] for n in TWIN_WEIGHTS], *[new_v[n] for n in TWIN_WEIGHTS])
```

```python
import functools
import math

import jax
import jax.numpy as jnp
from jax import lax
from jax.experimental import pallas as pl
from jax.experimental.pallas import tpu as pltpu

F32 = jnp.float32
BF16 = jnp.bfloat16
MESH = pl.DeviceIdType.MESH

N_DEV = 8
D_MODEL = 1024
EPS = 1e-6
POOL_GROUP = 256
S5_GROUPS, S5_GROUP, S5_STATE = 64, 16, 64
LRU_BLOCK = 256
LRU_C = 8.0
SB_HEAD_DIM = 64
SB_BLOCK = 128
FFN_HIDDEN = 2816
ADAM_LR, ADAM_B1, ADAM_B2, ADAM_EPS, ADAM_WD, ADAM_STEP = 0.001, 0.9, 0.999, 1e-08, 0.01, 10
LANES = 128
VMEM_LIMIT = 56 * 1024 * 1024


def _pick(n, target, mult=LANES):
    best = None
    for d in range(mult, min(n, target) + 1, mult):
        if n % d == 0:
            best = d
    return best or n


def _params(*sem):
    return pltpu.CompilerParams(dimension_semantics=sem, vmem_limit_bytes=VMEM_LIMIT)


def mm(a, b, *, ta=False, tb=False, acc=None, out_dtype=F32, name):
    if ta:
        kdim, m = a.shape
    else:
        m, kdim = a.shape
    if tb:
        n, k2 = b.shape
    else:
        k2, n = b.shape
    assert kdim == k2, (a.shape, b.shape)
    tm = _pick(m, 512)
    tn = _pick(n, 1408)
    tk = _pick(kdim, 1408)
    nk = kdim // tk
    a_spec = pl.BlockSpec((tk, tm), lambda i, j, k: (k, i)) if ta else pl.BlockSpec((tm, tk), lambda i, j, k: (i, k))
    b_spec = pl.BlockSpec((tn, tk), lambda i, j, k: (j, k)) if tb else pl.BlockSpec((tk, tn), lambda i, j, k: (k, j))
    o_spec = pl.BlockSpec((tm, tn), lambda i, j, k: (i, j))
    dims = (((0,) if ta else (1,), (1,) if tb else (0,)), ((), ()))
    has_acc = acc is not None

    def body(*refs):
        if has_acc:
            a_ref, b_ref, c_ref, o_ref, acc_ref = refs
        else:
            a_ref, b_ref, o_ref, acc_ref = refs
        k = pl.program_id(2)

        @pl.when(k == 0)
        def _():
            acc_ref[...] = c_ref[...].astype(F32) if has_acc else jnp.zeros_like(acc_ref)

        acc_ref[...] += lax.dot_general(a_ref[...].astype(BF16), b_ref[...].astype(BF16), dims,
                                        preferred_element_type=F32)

        @pl.when(k == nk - 1)
        def _():
            o_ref[...] = acc_ref[...].astype(out_dtype)

    ins = [a, b] + ([acc] if has_acc else [])
    specs = [a_spec, b_spec] + ([o_spec] if has_acc else [])
    return pl.pallas_call(
        body, name=name, grid=(m // tm, n // tn, nk), in_specs=specs, out_specs=o_spec,
        out_shape=jax.ShapeDtypeStruct((m, n), out_dtype), scratch_shapes=[pltpu.VMEM((tm, tn), F32)],
        compiler_params=_params("parallel", "parallel", "arbitrary"),
    )(*ins)


def rowwise(fn, tiled, bcast, outs, reds=(), *, tm=256, name):
    t = tiled[0].shape[0]
    tm = min(tm, t)
    assert t % tm == 0
    n_t, n_b, n_o, n_r = len(tiled), len(bcast), len(outs), len(reds)

    def body(*refs):
        vals = [r[...] for r in refs[:n_t + n_b]]
        res = fn(*vals)
        res = res if isinstance(res, tuple) else (res,)
        o_refs = refs[n_t + n_b:n_t + n_b + n_o]
        r_refs = refs[n_t + n_b + n_o:]
        for o_ref, v in zip(o_refs, res[:n_o]):
            o_ref[...] = v.astype(o_ref.dtype)
        first = pl.program_id(0) == 0
        for r_ref, v in zip(r_refs, res[n_o:]):
            @pl.when(first)
            def _(r_ref=r_ref, v=v):
                r_ref[...] = v.astype(F32)

            @pl.when(jnp.logical_not(first))
            def _(r_ref=r_ref, v=v):
                r_ref[...] += v.astype(F32)

    in_specs = [pl.BlockSpec((tm, a.shape[1]), lambda i: (i, 0)) for a in tiled]
    in_specs += [pl.BlockSpec(a.shape, lambda i, nd=a.ndim: (0,) * nd) for a in bcast]
    out_specs = [pl.BlockSpec((tm, c), lambda i: (i, 0)) for c, _ in outs]
    out_specs += [pl.BlockSpec(s, lambda i: (0, 0)) for s in reds]
    out_shape = [jax.ShapeDtypeStruct((t, c), dt) for c, dt in outs]
    out_shape += [jax.ShapeDtypeStruct(s, F32) for s in reds]
    res = pl.pallas_call(
        body, name=name, grid=(t // tm,), in_specs=in_specs, out_specs=out_specs, out_shape=out_shape,
        compiler_params=_params("arbitrary"),
    )(*tiled, *bcast)
    return res


def seqwise(fn, seqs, chans, outs, reds=(), *, ct, name, fulls=()):
    def split(x):
        return x if isinstance(x, tuple) else (x, 0)

    seqs = [split(s) for s in seqs]
    chans = [split(s) for s in chans]
    bsz, seq = seqs[0][0].shape[:2]
    n_c = outs[0][0] // ct if outs else reds[0][1] // ct
    n_s, n_ch, n_f, n_o, n_r = len(seqs), len(chans), len(fulls), len(outs), len(reds)

    def body(*refs):
        c = pl.program_id(0)
        vals = [r[...] for r in refs[:n_s + n_ch + n_f]]
        res = fn(c, *vals)
        res = res if isinstance(res, tuple) else (res,)
        o_refs = refs[n_s + n_ch + n_f:n_s + n_ch + n_f + n_o]
        r_refs = refs[n_s + n_ch + n_f + n_o:]
        for o_ref, v in zip(o_refs, res[:n_o]):
            o_ref[...] = v.astype(o_ref.dtype)
        first = pl.program_id(1) == 0
        for r_ref, v in zip(r_refs, res[n_o:]):
            @pl.when(first)
            def _(r_ref=r_ref, v=v):
                r_ref[...] = v.astype(F32)

            @pl.when(jnp.logical_not(first))
            def _(r_ref=r_ref, v=v):
                r_ref[...] += v.astype(F32)

    in_specs = [pl.BlockSpec((None, seq, ct), lambda c, b, off=off: (b, 0, c + off)) for _, off in seqs]
    in_specs += [pl.BlockSpec((a.shape[0], ct), lambda c, b, off=off: (0, c + off)) for a, off in chans]
    in_specs += [pl.BlockSpec(a.shape, lambda c, b, nd=a.ndim: (0,) * nd) for a in fulls]
    out_specs = [pl.BlockSpec((None, seq, ct), lambda c, b: (b, 0, c)) for _ in outs]
    out_specs += [pl.BlockSpec((r, ct), lambda c, b: (0, c)) for r, _ in reds]
    out_shape = [jax.ShapeDtypeStruct((bsz, seq, cc), dt) for cc, dt in outs]
    out_shape += [jax.ShapeDtypeStruct(s, F32) for s in reds]
    return pl.pallas_call(
        body, name=name, grid=(n_c, bsz), in_specs=in_specs, out_specs=out_specs, out_shape=out_shape,
        compiler_params=_params("arbitrary", "arbitrary"),
    )(*[a for a, _ in seqs], *[a for a, _ in chans], *fulls)


def _rows(x):
    return lax.broadcasted_iota(jnp.int32, x.shape, 0)


def shift_down(x, d, fill=0.0):
    if d == 0:
        return x
    return jnp.where(_rows(x) >= d, pltpu.roll(x, d, 0), fill)


def shift_up(x, d, fill=0.0):
    if d == 0:
        return x
    s = x.shape[0]
    return jnp.where(_rows(x) < s - d, pltpu.roll(x, s - d, 0), fill)


def conv_fwd(x, w, b):
    kw = w.shape[0]
    y = b + w[kw - 1:kw] * x
    for d in range(1, kw):
        y = y + w[kw - 1 - d:kw - d] * shift_down(x, d)
    return y


def conv_bwd(x, w, dy):
    kw = w.shape[0]
    dx = w[kw - 1:kw] * dy
    dws = [jnp.sum(dy * x, axis=0, keepdims=True)]
    for d in range(1, kw):
        dx = dx + w[kw - 1 - d:kw - d] * shift_up(dy, d)
        dws.append(jnp.sum(dy * shift_down(x, d), axis=0, keepdims=True))
    dw = jnp.concatenate(dws[::-1], axis=0)
    return dx, dw, jnp.sum(dy, axis=0, keepdims=True)


def sigmoid(x):
    return 1.0 / (1.0 + jnp.exp(-x))


def rms_fwd(x, g, out_dtype, name):
    def fn(x, g):
        return x * lax.rsqrt(jnp.mean(x * x, axis=-1, keepdims=True) + EPS) * g

    return rowwise(fn, [x], [g], [(x.shape[1], out_dtype)], name=name)[0]


def rms_bwd(x, g, dh_parts, dres, name):
    n_p = len(dh_parts)

    def fn(x, dres, *rest):
        dh = rest[0].astype(F32)
        for p in rest[1:n_p]:
            dh = dh + p.astype(F32)
        g = rest[n_p]
        r = lax.rsqrt(jnp.mean(x * x, axis=-1, keepdims=True) + EPS)
        xh = x * r
        dxh = dh * g
        dx = r * (dxh - xh * jnp.mean(dxh * xh, axis=-1, keepdims=True))
        return dres + dx, jnp.sum(dh * xh, axis=0, keepdims=True)

    return rowwise(fn, [x, dres, *dh_parts], [g], [(x.shape[1], F32)], [(1, x.shape[1])], name=name)


def ffn_fwd(x, g, w_val, w_gate, cw, cb, w_out, bsz, li):
    t, d = x.shape
    hid = w_val.shape[1]
    xn = rms_fwd(x, g, BF16, f"ffn{li}_rms")
    hv = mm(xn, w_val, name=f"ffn{li}_hv").reshape(bsz, t // bsz, hid)
    hg = mm(xn, w_gate, name=f"ffn{li}_hg").reshape(bsz, t // bsz, hid)
    ct = _pick(hid, 256)
    off = hid // ct

    def fn(c, hv, hg, wv, wg, bv, bg):
        val = conv_fwd(hv, wv, bv)
        gate = conv_fwd(hg, wg, bg)
        return gate * sigmoid(gate) * val

    act = seqwise(fn, [hv, hg], [cw, (cw, off), cb, (cb, off)], [(hid, BF16)], ct=ct, name=f"ffn{li}_gate")[0]
    act = act.reshape(t, hid)
    y = mm(act, w_out, acc=x, name=f"ffn{li}_out")
    return y, (x, xn, hv, hg, act)


def ffn_bwd(dy, saved, g, w_val, w_gate, cw, cb, w_out, li):
    x, xn, hv, hg, act = saved
    bsz, seq, hid = hv.shape
    t = bsz * seq
    d_wout = mm(act, dy, ta=True, name=f"ffn{li}_dwout")
    dact = mm(dy, w_out, tb=True, name=f"ffn{li}_dact").reshape(bsz, seq, hid)
    ct = _pick(hid, 256)
    off = hid // ct

    def fn(c, hv, hg, dact, wv, wg, bv, bg):
        val = conv_fwd(hv, wv, bv)
        gate = conv_fwd(hg, wg, bg)
        sg = sigmoid(gate)
        silu = gate * sg
        dval = dact * silu
        dgate = dact * val * (sg + silu * (1.0 - sg))
        dhv, dwv, dbv = conv_bwd(hv, wv, dval)
        dhg, dwg, dbg = conv_bwd(hg, wg, dgate)
        return dhv, dhg, dwv, dwg, dbv, dbg

    kw = cw.shape[0]
    dhv, dhg, dwv, dwg, dbv, dbg = seqwise(
        fn, [hv, hg, dact], [cw, (cw, off), cb, (cb, off)], [(hid, BF16), (hid, BF16)],
        [(kw, hid), (kw, hid), (1, hid), (1, hid)], ct=ct, name=f"ffn{li}_dgate")
    dhv = dhv.reshape(t, hid)
    dhg = dhg.reshape(t, hid)
    d_wv = mm(xn, dhv, ta=True, name=f"ffn{li}_dwv")
    d_wg = mm(xn, dhg, ta=True, name=f"ffn{li}_dwg")
    dxn = mm(dhv, w_val, tb=True, name=f"ffn{li}_dxn1")
    dxn = mm(dhg, w_gate, tb=True, acc=dxn, name=f"ffn{li}_dxn2")
    dx, dg = rms_bwd(x, g, [dxn], dy, f"ffn{li}_drms")
    d_cw = jnp.concatenate([dwv, dwg], axis=1)
    d_cb = jnp.concatenate([dbv, dbg], axis=1)
    return dx, (dg, d_wv, d_wg, d_cw, d_cb, d_wout)


def _window_sum(x, c, shift):
    s2 = x + shift(x, 1)
    s4 = s2 + shift(s2, 2)
    s8 = s4 + shift(s4, 4)
    s16 = s8 + shift(s8, 8)
    return jnp.where(c == 0, s2, jnp.where(c == 1, s4, jnp.where(c == 2, s8, s16)))


def _pool_inv_count(x, c):
    w = jnp.left_shift(2, c)
    return 1.0 / jnp.minimum(_rows(x) + 1, w).astype(F32)


def pool_fwd(x, g, w, b, scale, bsz):
    t, d = x.shape
    h = rms_fwd(x, g, F32, "pool_rms").reshape(bsz, t // bsz, d)

    def fn(c, h, x, b, scale, w):
        dd = _window_sum(h, c, shift_down) * _pool_inv_count(h, c) - h
        y = jnp.dot(dd.astype(BF16), w[0], preferred_element_type=F32) + b
        return x + scale * y

    wspec = pl.BlockSpec((1, POOL_GROUP, POOL_GROUP), lambda c, bb: (c, 0, 0))
    y = _seqwise_w(fn, [h, x.reshape(bsz, t // bsz, d)], [b, scale], [(w, wspec)], [(d, F32)], [], ct=POOL_GROUP,
                   name="pool_fwd")[0]
    return y.reshape(t, d), (x, h)


def pool_bwd(dy, saved, g, w, b, scale):
    x, h = saved
    bsz, seq, d = h.shape
    t = bsz * seq

    def fn(c, h, dy, b, scale, w):
        inv = _pool_inv_count(h, c)
        dd = _window_sum(h, c, shift_down) * inv - h
        ddb = dd.astype(BF16)
        y = jnp.dot(ddb, w[0], preferred_element_type=F32) + b
        dscale = jnp.sum(dy * y, axis=0, keepdims=True)
        dyy = dy * scale
        db = jnp.sum(dyy, axis=0, keepdims=True)
        dyb = dyy.astype(BF16)
        dw = lax.dot_general(ddb, dyb, (((0,), (0,)), ((), ())), preferred_element_type=F32)
        ddd = lax.dot_general(dyb, w[0], (((1,), (1,)), ((), ())), preferred_element_type=F32)
        dh = _window_sum(ddd * inv, c, shift_up) - ddd
        return dh, db, dscale, dw[None]

    wspec = pl.BlockSpec((1, POOL_GROUP, POOL_GROUP), lambda c, bb: (c, 0, 0))
    dh, db, dscale, dw = _seqwise_w(
        fn, [h, dy.reshape(bsz, seq, d)], [b, scale], [(w, wspec)], [(d, F32)], [(1, d), (1, d)], ct=POOL_GROUP,
        name="pool_bwd", wreds=[((4, POOL_GROUP, POOL_GROUP), wspec)])
    dx, dg = rms_bwd(x, g, [dh.reshape(t, d)], dy, "pool_drms")
    return dx, (dg, dw, db, dscale)


def _seqwise_w(fn, seqs, chans, blocked, outs, reds, *, ct, name, wreds=()):
    bsz, seq = seqs[0].shape[:2]
    n_c = seqs[0].shape[2] // ct
    n_in = len(seqs) + len(chans) + len(blocked)
    n_o, n_r = len(outs), len(reds) + len(wreds)

    def body(*refs):
        c = pl.program_id(0)
        res = fn(c, *[r[...] for r in refs[:n_in]])
        res = res if isinstance(res, tuple) else (res,)
        for o_ref, v in zip(refs[n_in:n_in + n_o], res[:n_o]):
            o_ref[...] = v.astype(o_ref.dtype)
        first = pl.program_id(1) == 0
        for r_ref, v in zip(refs[n_in + n_o:], res[n_o:]):
            @pl.when(first)
            def _(r_ref=r_ref, v=v):
                r_ref[...] = v.astype(F32)

            @pl.when(jnp.logical_not(first))
            def _(r_ref=r_ref, v=v):
                r_ref[...] += v.astype(F32)

    in_specs = [pl.BlockSpec((None, seq, ct), lambda c, b: (b, 0, c)) for _ in seqs]
    in_specs += [pl.BlockSpec((a.shape[0], ct), lambda c, b: (0, c)) for a in chans]
    in_specs += [spec for _, spec in blocked]
    out_specs = [pl.BlockSpec((None, seq, ct), lambda c, b: (b, 0, c)) for _ in outs]
    out_specs += [pl.BlockSpec((r, ct), lambda c, b: (0, c)) for r, _ in reds]
    out_specs += [spec for _, spec in wreds]
    out_shape = [jax.ShapeDtypeStruct((bsz, seq, cc), dt) for cc, dt in outs]
    out_shape += [jax.ShapeDtypeStruct(s, F32) for s in reds]
    out_shape += [jax.ShapeDtypeStruct(s, F32) for s, _ in wreds]
    return pl.pallas_call(
        body, name=name, grid=(n_c, bsz), in_specs=in_specs, out_specs=out_specs, out_shape=out_shape,
        compiler_params=_params("arbitrary", "arbitrary"),
    )(*seqs, *chans, *[a for a, _ in blocked])


S5_TILE_GROUPS = LANES // S5_GROUP
S5_TILES = S5_GROUPS // S5_TILE_GROUPS
S5_TILE_STATE = S5_TILE_GROUPS * S5_STATE


def _s5_discretize(lam_re, lam_im, log_dt, b_re, b_im):
    lr = jnp.minimum(lam_re, -1e-4)
    dt = jnp.exp(log_dt)
    er = jnp.exp(lr * dt)
    ar = er * jnp.cos(lam_im * dt)
    ai = er * jnp.sin(lam_im * dt)
    den = lr * lr + lam_im * lam_im
    cr = ((ar - 1.0) * lr + ai * lam_im) / den
    ci = (ai * lr - (ar - 1.0) * lam_im) / den
    return ar, ai, cr * b_re - ci * b_im, cr * b_im + ci * b_re


def _whole(a):
    return pl.BlockSpec(a.shape, lambda *_: (0,) * a.ndim)


def s5_prep(lam_re, lam_im, log_dt, b_re, b_im):
    def body(lr, li, ld, br, bi, ar_o, ai_o, bbr_o, bbi_o):
        ar, ai, bbr, bbi = _s5_discretize(lr[...], li[...], ld[...], br[...], bi[...])
        ar_o[...] = ar
        ai_o[...] = ai
        bbr_o[...] = bbr
        bbi_o[...] = bbi

    ins = [lam_re, lam_im, log_dt, b_re, b_im]
    outs = [lam_re, lam_im, b_re, b_im]
    return pl.pallas_call(
        body, name="s5_prep", in_specs=[_whole(a) for a in ins], out_specs=[_whole(a) for a in outs],
        out_shape=[jax.ShapeDtypeStruct(a.shape, F32) for a in outs],
        compiler_params=pltpu.CompilerParams(vmem_limit_bytes=VMEM_LIMIT),
    )(*ins)


def s5_prep_bwd(lam_re, lam_im, log_dt, b_re, b_im, d_ar, d_ai, d_bbr, d_bbi):
    def body(lr, li, ld, br, bi, dar, dai, dbbr, dbbi, *outs):
        _, vjp = jax.vjp(_s5_discretize, lr[...], li[...], ld[...], br[...], bi[...])
        for o, v in zip(outs, vjp((dar[...], dai[...], dbbr[...], dbbi[...]))):
            o[...] = v

    ins = [lam_re, lam_im, log_dt, b_re, b_im, d_ar, d_ai, d_bbr, d_bbi]
    return pl.pallas_call(
        body, name="s5_prep_bwd", in_specs=[_whole(a) for a in ins], out_specs=[_whole(a) for a in ins[:5]],
        out_shape=[jax.ShapeDtypeStruct(a.shape, F32) for a in ins[:5]],
        compiler_params=pltpu.CompilerParams(vmem_limit_bytes=VMEM_LIMIT),
    )(*ins)


def scan_lti(br, bi, ar, ai, reverse=False):
    shift = shift_up if reverse else shift_down
    seq = br.shape[0]
    d = 1
    while d < seq:
        sr, si = shift(br, d), shift(bi, d)
        br, bi = br + ar * sr - ai * si, bi + ar * si + ai * sr
        ar, ai = ar * ar - ai * ai, 2.0 * ar * ai
        d *= 2
    return br, bi


def _s5_block_diag(w, rows_first):
    g, a, b = w.shape
    w = w.reshape(S5_TILES, S5_TILE_GROUPS, a, b)
    eye = jnp.eye(S5_TILE_GROUPS, dtype=w.dtype)
    if rows_first:
        return jnp.einsum("tgab,gk->tgakb", w, eye).reshape(S5_TILES, S5_TILE_GROUPS * a, S5_TILE_GROUPS * b)
    return jnp.einsum("tgab,gk->tgbka", w, eye).reshape(S5_TILES, S5_TILE_GROUPS * b, S5_TILE_GROUPS * a)


def _s5_diag_blocks(w, a, b):
    w = w.reshape(S5_TILES, S5_TILE_GROUPS, a, S5_TILE_GROUPS, b)
    eye = jnp.eye(S5_TILE_GROUPS, dtype=w.dtype)
    return jnp.einsum("tgakb,gk->tgab", w, eye).reshape(S5_GROUPS, a, b)


def _gelu(x):
    return jax.nn.gelu(x)


def _s5_tile_specs():
    b_spec = pl.BlockSpec((1, LANES, S5_TILE_STATE), lambda c, b: (c, 0, 0))
    c_spec = pl.BlockSpec((1, S5_TILE_STATE, LANES), lambda c, b: (c, 0, 0))
    a_spec = pl.BlockSpec((1, 1, S5_TILE_STATE), lambda c, b: (c, 0, 0))
    return b_spec, c_spec, a_spec


def s5_fwd(x, g, p, bsz):
    t, d = x.shape
    seq = t // bsz
    h = rms_fwd(x, g, F32, "s5_rms").reshape(bsz, seq, d)
    ar, ai, bbr, bbi = s5_prep(p["lam_re"][..., None], p["lam_im"][..., None], p["log_dt"][:, None, None],
                               p["b_re"], p["b_im"])
    bdr = _s5_block_diag(bbr, False).astype(BF16)
    bdi = _s5_block_diag(bbi, False).astype(BF16)
    cdr = _s5_block_diag(p["c_re"], False).astype(BF16)
    cdi = _s5_block_diag(p["c_im"], False).astype(BF16)
    a_r = ar.reshape(S5_TILES, 1, S5_TILE_STATE)
    a_i = ai.reshape(S5_TILES, 1, S5_TILE_STATE)
    b_spec, c_spec, a_spec = _s5_tile_specs()

    def fn(c, u, dskip, bdr, bdi, cdr, cdi, a_r, a_i):
        ub = u.astype(BF16)
        y = dskip * u
        for k in range(S5_TILE_STATE // LANES):
            sl = slice(k * LANES, (k + 1) * LANES)
            br = jnp.dot(ub, bdr[0][:, sl], preferred_element_type=F32)
            bi = jnp.dot(ub, bdi[0][:, sl], preferred_element_type=F32)
            sr, si = scan_lti(br, bi, a_r[0][:, sl], a_i[0][:, sl])
            y = y + jnp.dot(sr.astype(BF16), cdr[0][sl, :], preferred_element_type=F32)
            y = y - jnp.dot(si.astype(BF16), cdi[0][sl, :], preferred_element_type=F32)
        return y, _gelu(y)

    y, yg = _seqwise_w(fn, [h], [p["d"]], [(bdr, b_spec), (bdi, b_spec), (cdr, c_spec), (cdi, c_spec),
                                          (a_r, a_spec), (a_i, a_spec)],
                       [(d, F32), (d, BF16)], [], ct=LANES, name="s5_core")
    yg = yg.reshape(t, d)
    zv = mm(yg, p["w_val"], name="s5_zv")
    zg = mm(yg, p["w_gate"], name="s5_zg")

    def gate(x, zv, zg, bv, bg):
        return x + (zv + bv) * sigmoid(zg + bg)

    out = rowwise(gate, [x, zv, zg], [p["b_out"][:, :d], p["b_out"][:, d:]], [(d, F32)], name="s5_gate")[0]
    return out, (x, h, y, yg, zv, zg, (ar, ai, bdr, bdi, cdr, cdi, a_r, a_i))


def s5_bwd(dy, saved, g, p):
    x, h, y, yg, zv, zg, (ar, ai, bdr, bdi, cdr, cdi, a_r, a_i) = saved
    bsz, seq, d = h.shape
    t = bsz * seq

    def dgate(dy, zv, zg, bv, bg):
        val = zv + bv
        sg = sigmoid(zg + bg)
        dzv = dy * sg
        dzg = dy * val * sg * (1.0 - sg)
        return dzv, dzg, jnp.sum(dzv, axis=0, keepdims=True), jnp.sum(dzg, axis=0, keepdims=True)

    dzv, dzg, dbv, dbg = rowwise(dgate, [dy, zv, zg], [p["b_out"][:, :d], p["b_out"][:, d:]],
                                 [(d, BF16), (d, BF16)], [(1, d), (1, d)], name="s5_dgate")
    d_wv = mm(yg, dzv, ta=True, name="s5_dwv")
    d_wg = mm(yg, dzg, ta=True, name="s5_dwg")
    dyg = mm(dzv, p["w_val"], tb=True, name="s5_dyg1")
    dyg = mm(dzg, p["w_gate"], tb=True, acc=dyg, name="s5_dyg2").reshape(bsz, seq, d)
    b_spec, c_spec, a_spec = _s5_tile_specs()
    tr = (((0,), (0,)), ((), ()))
    nt = (((1,), (1,)), ((), ()))

    def fn(c, u, y, dyg, dskip, bdr, bdi, cdr, cdi, a_r, a_i):
        _, gelu_vjp = jax.vjp(_gelu, y)
        dyy = gelu_vjp(dyg)[0]
        ddskip = jnp.sum(dyy * u, axis=0, keepdims=True)
        du = dyy * dskip
        dyb = dyy.astype(BF16)
        ub = u.astype(BF16)
        dcr, dci, dbr, dbi, dar, dai = [], [], [], [], [], []
        for k in range(S5_TILE_STATE // LANES):
            sl = slice(k * LANES, (k + 1) * LANES)
            akr, aki = a_r[0][:, sl], a_i[0][:, sl]
            br = jnp.dot(ub, bdr[0][:, sl], preferred_element_type=F32)
            bi = jnp.dot(ub, bdi[0][:, sl], preferred_element_type=F32)
            sr, si = scan_lti(br, bi, akr, aki)
            dcr.append(lax.dot_general(sr.astype(BF16), dyb, tr, preferred_element_type=F32))
            dci.append(-lax.dot_general(si.astype(BF16), dyb, tr, preferred_element_type=F32))
            gr = lax.dot_general(dyb, cdr[0][sl, :], nt, preferred_element_type=F32)
            gi = -lax.dot_general(dyb, cdi[0][sl, :], nt, preferred_element_type=F32)
            gr, gi = scan_lti(gr, gi, akr, -aki, reverse=True)
            spr, spi = shift_down(sr, 1), shift_down(si, 1)
            dar.append(jnp.sum(gr * spr + gi * spi, axis=0, keepdims=True))
            dai.append(jnp.sum(gi * spr - gr * spi, axis=0, keepdims=True))
            grb, gib = gr.astype(BF16), gi.astype(BF16)
            dbr.append(lax.dot_general(ub, grb, tr, preferred_element_type=F32))
            dbi.append(lax.dot_general(ub, gib, tr, preferred_element_type=F32))
            du = du + lax.dot_general(grb, bdr[0][:, sl], nt, preferred_element_type=F32)
            du = du + lax.dot_general(gib, bdi[0][:, sl], nt, preferred_element_type=F32)
        return (du, ddskip, jnp.concatenate(dbr, axis=1)[None], jnp.concatenate(dbi, axis=1)[None],
                jnp.concatenate(dcr, axis=0)[None], jnp.concatenate(dci, axis=0)[None],
                jnp.concatenate(dar, axis=1)[None], jnp.concatenate(dai, axis=1)[None])

    du, ddskip, dbdr, dbdi, dcdr, dcdi, dar, dai = _seqwise_w(
        fn, [h, y, dyg], [p["d"]], [(bdr, b_spec), (bdi, b_spec), (cdr, c_spec), (cdi, c_spec), (a_r, a_spec), (a_i, a_spec)],
        [(d, F32)], [(1, d)], ct=LANES, name="s5_dcore",
        wreds=[((S5_TILES, LANES, S5_TILE_STATE), b_spec), ((S5_TILES, LANES, S5_TILE_STATE), b_spec),
               ((S5_TILES, S5_TILE_STATE, LANES), c_spec), ((S5_TILES, S5_TILE_STATE, LANES), c_spec),
               ((S5_TILES, 1, S5_TILE_STATE), a_spec), ((S5_TILES, 1, S5_TILE_STATE), a_spec)])
    d_bbr = _s5_diag_blocks(dbdr, S5_GROUP, S5_STATE).transpose(0, 2, 1)
    d_bbi = _s5_diag_blocks(dbdi, S5_GROUP, S5_STATE).transpose(0, 2, 1)
    d_cre = _s5_diag_blocks(dcdr, S5_STATE, S5_GROUP).transpose(0, 2, 1)
    d_cim = _s5_diag_blocks(dcdi, S5_STATE, S5_GROUP).transpose(0, 2, 1)
    d_lre, d_lim, d_ldt, d_bre, d_bim = s5_prep_bwd(
        p["lam_re"][..., None], p["lam_im"][..., None], p["log_dt"][:, None, None], p["b_re"], p["b_im"],
        dar.reshape(S5_GROUPS, S5_STATE, 1), dai.reshape(S5_GROUPS, S5_STATE, 1), d_bbr, d_bbi)
    dx, dg = rms_bwd(x, g, [du.reshape(t, d)], dy, "s5_drms")
    grads = dict(g=dg, lam_re=d_lre[..., 0], lam_im=d_lim[..., 0], log_dt=d_ldt[:, 0, 0], b_re=d_bre, b_im=d_bim,
                 c_re=d_cre, c_im=d_cim, d=ddskip, w_val=d_wv, w_gate=d_wg, b_out=jnp.concatenate([dbv, dbg], axis=1))
    return dx, grads


def _expm1_nonpos(x):
    u = jnp.exp(x)
    safe = (u - 1.0) * x / jnp.log(u)
    return jnp.where(u == 1.0, x, jnp.where(x < -20.0, -1.0, safe))


def _softplus(x):
    return jnp.maximum(x, 0.0) + jnp.log(1.0 + jnp.exp(-jnp.abs(x)))


def scan_ltv(a, b, reverse=False):
    shift = shift_up if reverse else shift_down
    seq = a.shape[0]
    d = 1
    while d < seq:
        b = b + a * shift(b, d)
        if 2 * d < seq:
            a = a * shift(a, d, 1.0)
        d *= 2
    return b


def _lru_gates(rec0, cw, cb, ba, bx, lam, wa, wx):
    rec = conv_fwd(rec0, cw, cb)
    recb = rec.astype(BF16)
    r = sigmoid(jnp.dot(recb, wa, preferred_element_type=F32) + ba)
    i = sigmoid(jnp.dot(recb, wx, preferred_element_type=F32) + bx)
    sp = _softplus(-lam)
    log_a = -LRU_C * r * sp
    a = jnp.exp(log_a)
    mult = jnp.sqrt(-_expm1_nonpos(2.0 * log_a))
    return rec, recb, r, i, sp, a, mult


def lru_fwd(x, g, p, bsz):
    t, d = x.shape
    seq = t // bsz
    xn = rms_fwd(x, g, BF16, "lru_rms")
    gb = mm(xn, p["w_gate"], name="lru_gb").reshape(bsz, seq, d)
    rec0 = mm(xn, p["w_rec"], name="lru_rec").reshape(bsz, seq, d)
    wspec = pl.BlockSpec((1, LRU_BLOCK, LRU_BLOCK), lambda c, b: (c, 0, 0))

    def fn(c, rec0, gb, cw, cb, ba, bx, lam, wa, wx):
        rec, _, _, i, _, a, mult = _lru_gates(rec0, cw, cb, ba, bx, lam, wa[0], wx[0])
        h = scan_ltv(a, mult * (i * rec))
        return _gelu(gb) * h

    y = _seqwise_w(fn, [rec0, gb], [p["conv_w"], p["conv_b"], p["b_a"], p["b_x"], p["lam"]],
                   [(p["w_a"], wspec), (p["w_x"], wspec)], [(d, BF16)], [], ct=LRU_BLOCK, name="lru_core")[0]
    y = y.reshape(t, d)
    out = mm(y, p["w_out"], acc=x, name="lru_out")
    return out, (x, xn, gb, rec0, y)


def lru_bwd(dy, saved, g, p):
    x, xn, gb, rec0, y = saved
    bsz, seq, d = gb.shape
    t = bsz * seq
    d_wout = mm(y, dy, ta=True, name="lru_dwout")
    dyy = mm(dy, p["w_out"], tb=True, name="lru_dy").reshape(bsz, seq, d)
    wspec = pl.BlockSpec((1, LRU_BLOCK, LRU_BLOCK), lambda c, b: (c, 0, 0))
    tr = (((0,), (0,)), ((), ()))
    nt = (((1,), (1,)), ((), ()))

    def fn(c, rec0, gb, dyy, cw, cb, ba, bx, lam, wa, wx):
        rec, recb, r, i, sp, a, mult = _lru_gates(rec0, cw, cb, ba, bx, lam, wa[0], wx[0])
        h = scan_ltv(a, mult * (i * rec))
        gg, gelu_vjp = jax.vjp(_gelu, gb)
        dgb = gelu_vjp(dyy * h)[0]
        gr = scan_ltv(shift_up(a, 1), dyy * gg, reverse=True)
        da = gr * shift_down(h, 1)
        dmult = gr * i * rec
        di = gr * mult * rec
        drec = gr * mult * i
        dla = da * a - dmult * (a * a) / mult
        dr = dla * (-LRU_C * sp)
        dlam = jnp.sum(dla * (-LRU_C * r), axis=0, keepdims=True) * (-sigmoid(-lam))
        dpa = dr * r * (1.0 - r)
        dpx = di * i * (1.0 - i)
        dpab, dpxb = dpa.astype(BF16), dpx.astype(BF16)
        dwa = lax.dot_general(recb, dpab, tr, preferred_element_type=F32)
        dwx = lax.dot_general(recb, dpxb, tr, preferred_element_type=F32)
        drec = drec + lax.dot_general(dpab, wa[0], nt, preferred_element_type=F32)
        drec = drec + lax.dot_general(dpxb, wx[0], nt, preferred_element_type=F32)
        drec0, dcw, dcb = conv_bwd(rec0, cw, drec)
        return (dgb, drec0, dcw, dcb, jnp.sum(dpa, axis=0, keepdims=True), jnp.sum(dpx, axis=0, keepdims=True), dlam,
                dwa[None], dwx[None])

    kw = p["conv_w"].shape[0]
    dgb, drec0, dcw, dcb, dba, dbx, dlam, dwa, dwx = _seqwise_w(
        fn, [rec0, gb, dyy], [p["conv_w"], p["conv_b"], p["b_a"], p["b_x"], p["lam"]],
        [(p["w_a"], wspec), (p["w_x"], wspec)], [(d, BF16), (d, BF16)],
        [(kw, d), (1, d), (1, d), (1, d), (1, d)], ct=LRU_BLOCK, name="lru_dcore",
        wreds=[(p["w_a"].shape, wspec), (p["w_x"].shape, wspec)])
    dgb = dgb.reshape(t, d)
    drec0 = drec0.reshape(t, d)
    d_wgate = mm(xn, dgb, ta=True, name="lru_dwgate")
    d_wrec = mm(xn, drec0, ta=True, name="lru_dwrec")
    dxn = mm(dgb, p["w_gate"], tb=True, name="lru_dxn1")
    dxn = mm(drec0, p["w_rec"], tb=True, acc=dxn, name="lru_dxn2")
    dx, dg = rms_bwd(x, g, [dxn], dy, "lru_drms")
    grads = dict(g=dg, w_gate=d_wgate, w_rec=d_wrec, conv_w=dcw, conv_b=dcb, w_a=dwa, b_a=dba, w_x=dwx, b_x=dbx,
                 lam=dlam, w_out=d_wout)
    return dx, grads


_NT = (((1,), (1,)), ((), ()))
_TN = (((0,), (0,)), ((), ()))


def _head_norm(x, g):
    lo = lax.broadcasted_iota(jnp.int32, x.shape, 1) < SB_HEAD_DIM
    x2 = x * x
    s_lo = jnp.sum(jnp.where(lo, x2, 0.0), axis=-1, keepdims=True)
    s_hi = jnp.sum(jnp.where(lo, 0.0, x2), axis=-1, keepdims=True)
    ms = jnp.where(lo, s_lo, s_hi) * (1.0 / SB_HEAD_DIM)
    return x * lax.rsqrt(ms + EPS) * g


def _log_sigmoid(z):
    return jnp.minimum(z, 0.0) - jnp.log(1.0 + jnp.exp(-jnp.abs(z)))


def _dot_split(x, m):
    hi = x.astype(BF16)
    lo = (x - hi.astype(F32)).astype(BF16)
    return jnp.dot(hi, m, preferred_element_type=F32) + jnp.dot(lo, m, preferred_element_type=F32)


def _tri(cmp):
    r = lax.broadcasted_iota(jnp.int32, (SB_BLOCK, SB_BLOCK), 0)
    c = lax.broadcasted_iota(jnp.int32, (SB_BLOCK, SB_BLOCK), 1)
    return cmp(r, c)


def sb_attn_fwd(q, k, v, qg, kg):
    bsz, seq, d = q.shape
    nq = seq // SB_BLOCK
    scale = 1.0 / math.sqrt(SB_HEAD_DIM)

    def body(q_ref, k_ref, v_ref, qg_ref, kg_ref, o_ref, tot_ref, qn, kn, vb):
        qn[...] = _head_norm(q_ref[...], qg_ref[...]).astype(BF16)
        kn[...] = _head_norm(k_ref[...], kg_ref[...]).astype(BF16)
        vb[...] = v_ref[...].astype(BF16)
        lane_lo = lax.broadcasted_iota(jnp.int32, (SB_BLOCK, LANES), 1) < SB_HEAD_DIM
        causal = _tri(lambda r, c: c < r)
        upper = _tri(lambda r, c: r > c).astype(BF16)

        for head in range(2):
            hm = lane_lo if head == 0 else jnp.logical_not(lane_lo)

            def q_block(qi, _, hm=hm, head=head):
                rows = pl.ds(pl.multiple_of(qi * SB_BLOCK, SB_BLOCK), SB_BLOCK)
                qb = jnp.where(hm, qn[rows, :], jnp.zeros((), BF16))

                def block(j, carry, acc, masked):
                    cols = pl.ds(pl.multiple_of(j * SB_BLOCK, SB_BLOCK), SB_BLOCK)
                    z = lax.dot_general(qb, kn[cols, :], _NT, preferred_element_type=F32) * scale
                    ls = _log_sigmoid(z)
                    lg = ls - z
                    if masked:
                        lg = jnp.where(causal, lg, 0.0)
                    att = jnp.exp(ls + carry + _dot_split(lg, upper))
                    if masked:
                        att = jnp.where(causal, att, 0.0)
                    acc = acc + jnp.dot(att.astype(BF16), vb[cols, :], preferred_element_type=F32)
                    return carry + jnp.sum(lg, axis=1, keepdims=True), acc

                carry, acc = block(qi, jnp.zeros((SB_BLOCK, 1), F32), jnp.zeros((SB_BLOCK, LANES), F32), True)
                carry, acc = lax.fori_loop(0, qi, lambda jj, c: block(qi - 1 - jj, c[0], c[1], False), (carry, acc))
                if head == 0:
                    o_ref[rows, :] = acc.astype(o_ref.dtype)
                    tot_ref[rows, :] = jnp.broadcast_to(carry, (SB_BLOCK, LANES))
                else:
                    o_ref[rows, :] = jnp.where(hm, acc.astype(o_ref.dtype), o_ref[rows, :])
                    tot_ref[rows, :] = jnp.where(hm, carry, tot_ref[rows, :])
                return 0

            lax.fori_loop(0, nq, q_block, 0)

    spec = pl.BlockSpec((None, seq, LANES), lambda b, c: (b, 0, c))
    gspec = pl.BlockSpec((1, LANES), lambda b, c: (0, 0))
    return pl.pallas_call(
        body, name="sb_attn_fwd", grid=(bsz, d // LANES), in_specs=[spec, spec, spec, gspec, gspec],
        out_specs=[spec, spec], out_shape=[jax.ShapeDtypeStruct((bsz, seq, d), BF16), jax.ShapeDtypeStruct((bsz, seq, d), F32)],
        scratch_shapes=[pltpu.VMEM((seq, LANES), BF16)] * 3,
        compiler_params=_params("parallel", "parallel"),
    )(q, k, v, qg, kg)


def sb_attn_bwd(q, k, v, qg, kg, tot, do):
    bsz, seq, d = q.shape
    nq = seq // SB_BLOCK
    scale = 1.0 / math.sqrt(SB_HEAD_DIM)

    def body(q_ref, k_ref, v_ref, qg_ref, kg_ref, tot_ref, do_ref, dq_ref, dk_ref, dv_ref, dqg_ref, dkg_ref,
             qn, kn, vb, dqn, dkn, dvv):
        qn[...] = _head_norm(q_ref[...], qg_ref[...]).astype(BF16)
        kn[...] = _head_norm(k_ref[...], kg_ref[...]).astype(BF16)
        vb[...] = v_ref[...].astype(BF16)
        dqn[...] = jnp.zeros_like(dqn)
        dkn[...] = jnp.zeros_like(dkn)
        dvv[...] = jnp.zeros_like(dvv)
        lane_lo = lax.broadcasted_iota(jnp.int32, (SB_BLOCK, LANES), 1) < SB_HEAD_DIM
        causal = _tri(lambda r, c: c < r)
        upto = _tri(lambda r, c: r <= c).astype(BF16)
        before = _tri(lambda r, c: r < c).astype(BF16)

        for head in range(2):
            hm = lane_lo if head == 0 else jnp.logical_not(lane_lo)

            def q_block(qi, _, hm=hm):
                rows = pl.ds(pl.multiple_of(qi * SB_BLOCK, SB_BLOCK), SB_BLOCK)
                qb = jnp.where(hm, qn[rows, :], jnp.zeros((), BF16))
                dob = jnp.where(hm, do_ref[rows, :].astype(BF16), jnp.zeros((), BF16))
                total = jnp.max(jnp.where(hm, tot_ref[rows, :], -jnp.inf), axis=1, keepdims=True)

                def block(j, carry_l, carry_e, dq_acc, masked):
                    cols = pl.ds(pl.multiple_of(j * SB_BLOCK, SB_BLOCK), SB_BLOCK)
                    kb = kn[cols, :]
                    z = lax.dot_general(qb, kb, _NT, preferred_element_type=F32) * scale
                    ls = _log_sigmoid(z)
                    lg = ls - z
                    if masked:
                        lg = jnp.where(causal, lg, 0.0)
                    att = jnp.exp(ls + total - carry_l - _dot_split(lg, upto))
                    if masked:
                        att = jnp.where(causal, att, 0.0)
                    dvv[cols, :] += lax.dot_general(att.astype(BF16), dob, _TN, preferred_element_type=F32)
                    e = att * lax.dot_general(dob, vb[cols, :], _NT, preferred_element_type=F32)
                    dlg = carry_e + _dot_split(e, before)
                    if masked:
                        dlg = jnp.where(causal, dlg, 0.0)
                    beta = jnp.exp(ls)
                    dzb = ((e * (1.0 - beta) - dlg * beta) * scale).astype(BF16)
                    dq_acc = dq_acc + jnp.dot(dzb, kb, preferred_element_type=F32)
                    dkn[cols, :] += lax.dot_general(dzb, qb, _TN, preferred_element_type=F32)
                    return (carry_l + jnp.sum(lg, axis=1, keepdims=True), carry_e + jnp.sum(e, axis=1, keepdims=True),
                            dq_acc)

                zero = jnp.zeros((SB_BLOCK, 1), F32)
                carry = lax.fori_loop(0, qi, lambda j, c: block(j, c[0], c[1], c[2], False),
                                      (zero, zero, jnp.zeros((SB_BLOCK, LANES), F32)))
                _, _, dq_acc = block(qi, carry[0], carry[1], carry[2], True)
                dqn[rows, :] += jnp.where(hm, dq_acc, 0.0)
                return 0

            lax.fori_loop(0, nq, q_block, 0)

        def fold(x):
            return x + pltpu.roll(x, SB_HEAD_DIM, 1)

        _, q_vjp = jax.vjp(_head_norm, q_ref[...], qg_ref[...])
        dq, dqg = q_vjp(dqn[...])
        _, k_vjp = jax.vjp(_head_norm, k_ref[...], kg_ref[...])
        dk, dkg = k_vjp(dkn[...])
        dq_ref[...] = dq.astype(dq_ref.dtype)
        dk_ref[...] = dk.astype(dk_ref.dtype)
        dv_ref[...] = dvv[...].astype(dv_ref.dtype)
        first = jnp.logical_and(pl.program_id(0) == 0, pl.program_id(1) == 0)

        @pl.when(first)
        def _():
            dqg_ref[...] = fold(dqg)
            dkg_ref[...] = fold(dkg)

        @pl.when(jnp.logical_not(first))
        def _():
            dqg_ref[...] += fold(dqg)
            dkg_ref[...] += fold(dkg)

    spec = pl.BlockSpec((None, seq, LANES), lambda b, c: (b, 0, c))
    gspec = pl.BlockSpec((1, LANES), lambda b, c: (0, 0))
    act = jax.ShapeDtypeStruct((bsz, seq, d), BF16)
    gain = jax.ShapeDtypeStruct((1, LANES), F32)
    return pl.pallas_call(
        body, name="sb_attn_bwd", grid=(bsz, d // LANES), in_specs=[spec, spec, spec, gspec, gspec, spec, spec],
        out_specs=[spec, spec, spec, gspec, gspec], out_shape=[act, act, act, gain, gain],
        scratch_shapes=[pltpu.VMEM((seq, LANES), BF16)] * 3 + [pltpu.VMEM((seq, LANES), F32)] * 3,
        compiler_params=_params("arbitrary", "arbitrary"),
    )(q, k, v, qg, kg, tot, do)


def sb_fwd(x, g, p, bsz):
    t, d = x.shape
    seq = t // bsz
    xn = rms_fwd(x, g, BF16, "sb_rms")
    q = mm(xn, p["w_q"], name="sb_q").reshape(bsz, seq, d)
    k = mm(xn, p["w_k"], name="sb_k").reshape(bsz, seq, d)
    v = mm(xn, p["w_v"], name="sb_v").reshape(bsz, seq, d)
    qg = jnp.tile(p["q_g"], (1, 2))
    kg = jnp.tile(p["k_g"], (1, 2))
    o, tot = sb_attn_fwd(q, k, v, qg, kg)
    o = o.reshape(t, d)
    out = mm(o, p["w_o"], acc=x, name="sb_out")
    return out, (x, xn, q, k, v, o, tot)


def sb_bwd(dy, saved, g, p):
    x, xn, q, k, v, o, tot = saved
    bsz, seq, d = q.shape
    t = bsz * seq
    d_wo = mm(o, dy, ta=True, name="sb_dwo")
    do = mm(dy, p["w_o"], tb=True, out_dtype=BF16, name="sb_do").reshape(bsz, seq, d)
    qg = jnp.tile(p["q_g"], (1, 2))
    kg = jnp.tile(p["k_g"], (1, 2))
    dq, dk, dv, dqg, dkg = sb_attn_bwd(q, k, v, qg, kg, tot, do)
    dq, dk, dv = dq.reshape(t, d), dk.reshape(t, d), dv.reshape(t, d)
    d_wq = mm(xn, dq, ta=True, name="sb_dwq")
    d_wk = mm(xn, dk, ta=True, name="sb_dwk")
    d_wv = mm(xn, dv, ta=True, name="sb_dwv")
    dxn = mm(dq, p["w_q"], tb=True, name="sb_dxn1")
    dxn = mm(dk, p["w_k"], tb=True, acc=dxn, name="sb_dxn2")
    dxn = mm(dv, p["w_v"], tb=True, acc=dxn, name="sb_dxn3")
    dx, dg = rms_bwd(x, g, [dxn], dy, "sb_drms")
    grads = dict(g=dg, w_q=d_wq, w_k=d_wk, w_v=d_wv, w_o=d_wo, q_g=dqg[:, :SB_HEAD_DIM], k_g=dkg[:, :SB_HEAD_DIM])
    return dx, grads


def loss_head(y, target):
    d = y.shape[1]

    def fn(y, tgt):
        err = y - tgt
        return err * (1.0 / d), jnp.sum(err * err, axis=0, keepdims=True)

    dy, sq = rowwise(fn, [y, target], [], [(d, F32)], [(1, d)], name="loss_head")
    return sq, dy


def adamw(parts, w, m, v, name):
    n, r, c = parts.shape
    tr = _pick(r, max(8, (1 << 20) // (c * n)), 8)

    def body(p_ref, w_ref, m_ref, v_ref, g_ref, d_ref, nm_ref, nv_ref):
        g = p_ref[0]
        for i in range(1, n):
            g = g + p_ref[i]
        nm = ADAM_B1 * m_ref[...] + (1.0 - ADAM_B1) * g
        nv = ADAM_B2 * v_ref[...] + (1.0 - ADAM_B2) * (g * g)
        m_hat = nm / (1.0 - ADAM_B1 ** ADAM_STEP)
        v_hat = nv / (1.0 - ADAM_B2 ** ADAM_STEP)
        g_ref[...] = g
        d_ref[...] = -ADAM_LR * (m_hat / (jnp.sqrt(v_hat) + ADAM_EPS) + ADAM_WD * w_ref[...])
        nm_ref[...] = nm
        nv_ref[...] = nv

    spec = pl.BlockSpec((tr, c), lambda i: (i, 0))
    return pl.pallas_call(
        body, name=name, grid=(r // tr,), in_specs=[pl.BlockSpec((n, tr, c), lambda i: (0, i, 0)), spec, spec, spec],
        out_specs=[spec] * 4, out_shape=[jax.ShapeDtypeStruct((r, c), F32)] * 4,
        compiler_params=_params("parallel"),
    )(parts, w, m, v)


_HBM = pl.BlockSpec(memory_space=pltpu.HBM)


def _mesh_pos():
    return lax.axis_index("x"), lax.axis_index("y"), lax.axis_index("c")


def all_gather(shards, name):
    n = len(shards)

    def body(*refs):
        ins, outs = refs[:n], refs[n:2 * n]
        send_sems, recv_sems, local_sems = refs[2 * n:]
        x, y, c = _mesh_pos()
        me, sibling = (x, y, c), (x, y, 1 - c)
        chips = [(1 - x, y), (x, 1 - y), (1 - x, 1 - y)]

        def copy(a, k, block, to, src=None):
            px, py, pc = block
            dst = outs[a].at[4 * px + 2 * py + pc]
            return pltpu.make_async_remote_copy(
                src_ref=dst if src is None else src, dst_ref=dst, send_sem=send_sems.at[a, k],
                recv_sem=recv_sems.at[a, k], device_id=to, device_id_type=MESH)

        mine = [pltpu.make_async_copy(ins[a], outs[a].at[4 * x + 2 * y + c], local_sems.at[a]) for a in range(n)]
        for cp in mine:
            cp.start()
        first = []
        for a in range(n):
            first.append(copy(a, 0, me, sibling, src=ins[a]))
            first += [copy(a, 1 + j, me, (*chip, c), src=ins[a]) for j, chip in enumerate(chips)]
        for cp in first:
            cp.start()
        passed = []
        for a in range(n):
            for j, chip in enumerate(chips):
                copy(a, 1 + j, (*chip, c), me).wait_recv()
                cp = copy(a, 4 + j, (*chip, c), sibling)
                cp.start()
                passed.append(cp)
        for a in range(n):
            copy(a, 0, sibling, me).wait_recv()
            for j, chip in enumerate(chips):
                copy(a, 4 + j, (*chip, 1 - c), me).wait_recv()
        for cp in first + passed:
            cp.wait_send()
        for cp in mine:
            cp.wait()

    return pl.pallas_call(
        body, name=name, in_specs=[_HBM] * n, out_specs=[_HBM] * n,
        out_shape=[jax.ShapeDtypeStruct((N_DEV, *s.shape), s.dtype) for s in shards],
        scratch_shapes=[pltpu.SemaphoreType.DMA((n, 7)), pltpu.SemaphoreType.DMA((n, 7)), pltpu.SemaphoreType.DMA((n,))],
    )(*shards)


def exchange_cores(parts, name):
    n = len(parts)

    def body(*refs):
        ins, outs = refs[:n], refs[n:2 * n]
        send_sems, recv_sems = refs[2 * n:]
        x, y, c = _mesh_pos()
        copies = []
        for a in range(n):
            for p in range(4):
                copies.append(pltpu.make_async_remote_copy(
                    src_ref=ins[a].at[p, 1 - c], dst_ref=outs[a].at[p], send_sem=send_sems.at[a, p],
                    recv_sem=recv_sems.at[a, p], device_id=(x, y, 1 - c), device_id_type=MESH))
        for cp in copies:
            cp.start()
        for cp in copies:
            cp.wait()

    return pl.pallas_call(
        body, name=name, in_specs=[_HBM] * n, out_specs=[_HBM] * n,
        out_shape=[jax.ShapeDtypeStruct((4, *s.shape[2:]), s.dtype) for s in parts],
        scratch_shapes=[pltpu.SemaphoreType.DMA((n, 4)), pltpu.SemaphoreType.DMA((n, 4))],
    )(*parts)


def exchange_chips(parts, name):
    n = len(parts)

    def body(*refs):
        ins, outs = refs[:n], refs[n:2 * n]
        send_sems, recv_sems, local_sems = refs[2 * n:]
        x, y, c = _mesh_pos()
        mine = 2 * x + y
        local = [pltpu.make_async_copy(ins[a].at[mine], outs[a].at[mine], local_sems.at[a]) for a in range(n)]
        for cp in local:
            cp.start()
        copies = []
        for a in range(n):
            for r in range(1, 4):
                qx = 1 - x if r & 2 else x
                qy = 1 - y if r & 1 else y
                copies.append(pltpu.make_async_remote_copy(
                    src_ref=ins[a].at[2 * qx + qy], dst_ref=outs[a].at[mine], send_sem=send_sems.at[a, r - 1],
                    recv_sem=recv_sems.at[a, r - 1], device_id=(qx, qy, c), device_id_type=MESH))
        for cp in copies:
            cp.start()
        for cp in copies:
            cp.wait()
        for cp in local:
            cp.wait()

    return pl.pallas_call(
        body, name=name, in_specs=[_HBM] * n, out_specs=[_HBM] * n,
        out_shape=[jax.ShapeDtypeStruct(s.shape, s.dtype) for s in parts],
        scratch_shapes=[pltpu.SemaphoreType.DMA((n, 3)), pltpu.SemaphoreType.DMA((n, 3)), pltpu.SemaphoreType.DMA((n,))],
    )(*parts)


def add_own(core, parts, recv, name):
    _, _, r, c = parts.shape
    tr = _pick(r, max(8, (1 << 19) // c), 8)

    def body(core_ref, p_ref, r_ref, o_ref):
        o_ref[...] = p_ref[...] + r_ref[...]

    grid_spec = pltpu.PrefetchScalarGridSpec(
        num_scalar_prefetch=1, grid=(4, r // tr),
        in_specs=[pl.BlockSpec((None, None, tr, c), lambda p, i, core_ref: (p, core_ref[0], i, 0)),
                  pl.BlockSpec((None, tr, c), lambda p, i, core_ref: (p, i, 0))],
        out_specs=pl.BlockSpec((None, tr, c), lambda p, i, core_ref: (p, i, 0)))
    return pl.pallas_call(
        body, name=name, grid_spec=grid_spec, out_shape=jax.ShapeDtypeStruct((4, r, c), F32),
        compiler_params=_params("parallel", "parallel"),
    )(core, parts, recv)


PARAMS = {
    "norm_mix_g": ((4, 1024), None), "norm_ffn_g": ((4, 1024), None),
    "pool_w": ((1, 4, 256, 256), 2), "pool_b": ((1, 1024), None), "pool_scale": ((1, 1024), None),
    "s5_lam_re": ((1, 64, 64), None), "s5_lam_im": ((1, 64, 64), None), "s5_log_dt": ((1, 64), None),
    "s5_b_re": ((1, 64, 64, 16), None), "s5_b_im": ((1, 64, 64, 16), None),
    "s5_c_re": ((1, 64, 16, 64), None), "s5_c_im": ((1, 64, 16, 64), None),
    "s5_d": ((1, 1024), 1), "s5_w_out": ((1, 1024, 2048), 2), "s5_b_out": ((1, 2048), 1),
    "lru_w_in": ((1, 1024, 2048), 2), "lru_conv_w": ((1, 4, 1024), 2), "lru_conv_b": ((1, 1024), 1),
    "lru_w_a": ((1, 4, 256, 256), 2), "lru_b_a": ((1, 1024), 1), "lru_w_x": ((1, 4, 256, 256), 2),
    "lru_b_x": ((1, 1024), 1), "lru_lam": ((1, 1024), 1), "lru_w_out": ((1, 1024, 1024), 1),
    "sb_w_qkv": ((1, 1024, 3072), 2), "sb_q_g": ((1, 64), None), "sb_k_g": ((1, 64), None),
    "sb_w_o": ((1, 1024, 1024), 1),
    "ffn_w_in": ((4, 1024, 5632), 2), "ffn_conv_w": ((4, 3, 5632), 2), "ffn_conv_b": ((4, 5632), None),
    "ffn_w_out": ((4, 2816, 1024), 1),
}
NAMES = list(PARAMS)
BIG = ["s5_w_out", "lru_w_in", "lru_w_out", "sb_w_qkv", "sb_w_o", "ffn_w_in", "ffn_w_out"]
SMALL = [n for n in NAMES if PARAMS[n][1] is not None and n not in BIG]
REPL = [n for n in NAMES if PARAMS[n][1] is None]


def _local_shape(name):
    shape, ax = PARAMS[name]
    return tuple(s // N_DEV if i == ax else s for i, s in enumerate(shape))


def _to_natural(gathered, name):
    shape, ax = PARAMS[name]
    return jnp.moveaxis(gathered, 0, ax).reshape(shape)


def _to_shards(nat, name):
    shape, ax = PARAMS[name]
    split = shape[:ax] + (N_DEV, shape[ax] // N_DEV) + shape[ax + 1:]
    return jnp.moveaxis(nat.reshape(split), ax, 0)


def _rows2d(shape):
    return (math.prod(shape[:-1]), shape[-1])


def _pack(arrays, lead=()):
    flat = jnp.concatenate([a.reshape(*lead, -1) for a in arrays], axis=-1)
    size = flat.shape[-1]
    rows = -(-size // (8 * LANES)) * 8
    flat = jnp.pad(flat, [(0, 0)] * len(lead) + [(0, rows * LANES - size)])
    return flat.reshape(*lead, rows, LANES)


def _unpack(packed, shapes, lead=()):
    flat = packed.reshape(*lead, -1)
    out, off = [], 0
    for s in shapes:
        size = math.prod(s)
        out.append(flat[..., off:off + size].reshape(*lead, *s))
        off += size
    return out


def kernel(x, norm_mix_g, norm_ffn_g, pool_w, pool_b, pool_scale, s5_lam_re, s5_lam_im, s5_log_dt, s5_b_re, s5_b_im, s5_c_re, s5_c_im, s5_d, s5_w_out, s5_b_out, lru_w_in, lru_conv_w, lru_conv_b, lru_w_a, lru_b_a, lru_w_x, lru_b_x, lru_lam, lru_w_out, sb_w_qkv, sb_q_g, sb_k_g, sb_w_o, ffn_w_in, ffn_conv_w, ffn_conv_b, ffn_w_out, loss_target, m_norm_mix_g, m_norm_ffn_g, m_pool_w, m_pool_b, m_pool_scale, m_s5_lam_re, m_s5_lam_im, m_s5_log_dt, m_s5_b_re, m_s5_b_im, m_s5_c_re, m_s5_c_im, m_s5_d, m_s5_w_out, m_s5_b_out, m_lru_w_in, m_lru_conv_w, m_lru_conv_b, m_lru_w_a, m_lru_b_a, m_lru_w_x, m_lru_b_x, m_lru_lam, m_lru_w_out, m_sb_w_qkv, m_sb_q_g, m_sb_k_g, m_sb_w_o, m_ffn_w_in, m_ffn_conv_w, m_ffn_conv_b, m_ffn_w_out, v_norm_mix_g, v_norm_ffn_g, v_pool_w, v_pool_b, v_pool_scale, v_s5_lam_re, v_s5_lam_im, v_s5_log_dt, v_s5_b_re, v_s5_b_im, v_s5_c_re, v_s5_c_im, v_s5_d, v_s5_w_out, v_s5_b_out, v_lru_w_in, v_lru_conv_w, v_lru_conv_b, v_lru_w_a, v_lru_b_a, v_lru_w_x, v_lru_b_x, v_lru_lam, v_lru_w_out, v_sb_w_qkv, v_sb_q_g, v_sb_k_g, v_sb_w_o, v_ffn_w_in, v_ffn_conv_w, v_ffn_conv_b, v_ffn_w_out):
    w = dict(zip(NAMES, (norm_mix_g, norm_ffn_g, pool_w, pool_b, pool_scale, s5_lam_re, s5_lam_im, s5_log_dt, s5_b_re, s5_b_im, s5_c_re, s5_c_im, s5_d, s5_w_out, s5_b_out, lru_w_in, lru_conv_w, lru_conv_b, lru_w_a, lru_b_a, lru_w_x, lru_b_x, lru_lam, lru_w_out, sb_w_qkv, sb_q_g, sb_k_g, sb_w_o, ffn_w_in, ffn_conv_w, ffn_conv_b, ffn_w_out)))
    m = dict(zip(NAMES, (m_norm_mix_g, m_norm_ffn_g, m_pool_w, m_pool_b, m_pool_scale, m_s5_lam_re, m_s5_lam_im, m_s5_log_dt, m_s5_b_re, m_s5_b_im, m_s5_c_re, m_s5_c_im, m_s5_d, m_s5_w_out, m_s5_b_out, m_lru_w_in, m_lru_conv_w, m_lru_conv_b, m_lru_w_a, m_lru_b_a, m_lru_w_x, m_lru_b_x, m_lru_lam, m_lru_w_out, m_sb_w_qkv, m_sb_q_g, m_sb_k_g, m_sb_w_o, m_ffn_w_in, m_ffn_conv_w, m_ffn_conv_b, m_ffn_w_out)))
    v = dict(zip(NAMES, (v_norm_mix_g, v_norm_ffn_g, v_pool_w, v_pool_b, v_pool_scale, v_s5_lam_re, v_s5_lam_im, v_s5_log_dt, v_s5_b_re, v_s5_b_im, v_s5_c_re, v_s5_c_im, v_s5_d, v_s5_w_out, v_s5_b_out, v_lru_w_in, v_lru_conv_w, v_lru_conv_b, v_lru_w_a, v_lru_b_a, v_lru_w_x, v_lru_b_x, v_lru_lam, v_lru_w_out, v_sb_w_qkv, v_sb_q_g, v_sb_k_g, v_sb_w_o, v_ffn_w_in, v_ffn_conv_w, v_ffn_conv_b, v_ffn_w_out)))
    return train_step(x, loss_target, w, m, v)


def train_step(x, loss_target, w, m, v):
    bsz, seq, d = x.shape
    t = bsz * seq
    n_layers = PARAMS["norm_mix_g"][0][0]

    small_shapes = [_local_shape(n) for n in SMALL]
    gathered = all_gather([w[n].astype(BF16) for n in BIG] + [_pack([w[n] for n in SMALL])], "gather_weights")
    nat = {n: _to_natural(g, n) for n, g in zip(BIG, gathered[:-1])}
    for n, g in zip(SMALL, _unpack(gathered[-1], small_shapes, lead=(N_DEV,))):
        nat[n] = _to_natural(g, n)
    for n in REPL:
        nat[n] = w[n]

    hid = FFN_HIDDEN
    pool_p = (nat["pool_w"][0].astype(BF16), nat["pool_b"], nat["pool_scale"])
    s5_p = dict(lam_re=nat["s5_lam_re"][0], lam_im=nat["s5_lam_im"][0], log_dt=nat["s5_log_dt"][0],
                b_re=nat["s5_b_re"][0], b_im=nat["s5_b_im"][0], c_re=nat["s5_c_re"][0], c_im=nat["s5_c_im"][0],
                d=nat["s5_d"], w_val=nat["s5_w_out"][0, :, :d], w_gate=nat["s5_w_out"][0, :, d:], b_out=nat["s5_b_out"])
    lru_p = dict(w_gate=nat["lru_w_in"][0, :, :d], w_rec=nat["lru_w_in"][0, :, d:], conv_w=nat["lru_conv_w"][0],
                 conv_b=nat["lru_conv_b"], b_a=nat["lru_b_a"], b_x=nat["lru_b_x"], lam=nat["lru_lam"],
                 w_a=nat["lru_w_a"][0].astype(BF16), w_x=nat["lru_w_x"][0].astype(BF16), w_out=nat["lru_w_out"][0])
    sb_p = dict(w_q=nat["sb_w_qkv"][0, :, :d], w_k=nat["sb_w_qkv"][0, :, d:2 * d], w_v=nat["sb_w_qkv"][0, :, 2 * d:],
                w_o=nat["sb_w_o"][0], q_g=nat["sb_q_g"], k_g=nat["sb_k_g"])

    def ffn_p(li):
        return (nat["ffn_w_in"][li, :, :hid], nat["ffn_w_in"][li, :, hid:], nat["ffn_conv_w"][li],
                nat["ffn_conv_b"][li:li + 1], nat["ffn_w_out"][li])

    def gain(name, li):
        return nat[name][li:li + 1]

    h = x.reshape(t, d)
    h, pool_saved = pool_fwd(h, gain("norm_mix_g", 0), *pool_p, bsz)
    h, ffn0 = ffn_fwd(h, gain("norm_ffn_g", 0), *ffn_p(0), bsz, 0)
    h, s5_saved = s5_fwd(h, gain("norm_mix_g", 1), s5_p, bsz)
    h, ffn1 = ffn_fwd(h, gain("norm_ffn_g", 1), *ffn_p(1), bsz, 1)
    h, lru_saved = lru_fwd(h, gain("norm_mix_g", 2), lru_p, bsz)
    h, ffn2 = ffn_fwd(h, gain("norm_ffn_g", 2), *ffn_p(2), bsz, 2)
    h, sb_saved = sb_fwd(h, gain("norm_mix_g", 3), sb_p, bsz)
    h, ffn3 = ffn_fwd(h, gain("norm_ffn_g", 3), *ffn_p(3), bsz, 3)
    sq, dh = loss_head(h, loss_target.reshape(t, d))
    loss = lax.psum(0.5 * jnp.sum(sq) / d, ("x", "y", "c"))

    fg = [None] * n_layers
    dh, fg[3] = ffn_bwd(dh, ffn3, gain("norm_ffn_g", 3), *ffn_p(3), 3)
    dh, sb_g = sb_bwd(dh, sb_saved, gain("norm_mix_g", 3), sb_p)
    dh, fg[2] = ffn_bwd(dh, ffn2, gain("norm_ffn_g", 2), *ffn_p(2), 2)
    dh, lru_g = lru_bwd(dh, lru_saved, gain("norm_mix_g", 2), lru_p)
    dh, fg[1] = ffn_bwd(dh, ffn1, gain("norm_ffn_g", 1), *ffn_p(1), 1)
    dh, s5_g = s5_bwd(dh, s5_saved, gain("norm_mix_g", 1), s5_p)
    dh, fg[0] = ffn_bwd(dh, ffn0, gain("norm_ffn_g", 0), *ffn_p(0), 0)
    dh, pool_g = pool_bwd(dh, pool_saved, gain("norm_mix_g", 0), *pool_p)
    grad_x = dh.reshape(bsz, seq, d)

    part = {
        "norm_mix_g": jnp.concatenate([pool_g[0], s5_g["g"], lru_g["g"], sb_g["g"]], axis=0),
        "norm_ffn_g": jnp.concatenate([g[0] for g in fg], axis=0),
        "pool_w": pool_g[1][None], "pool_b": pool_g[2], "pool_scale": pool_g[3],
        "s5_lam_re": s5_g["lam_re"][None], "s5_lam_im": s5_g["lam_im"][None], "s5_log_dt": s5_g["log_dt"][None],
        "s5_b_re": s5_g["b_re"][None], "s5_b_im": s5_g["b_im"][None], "s5_c_re": s5_g["c_re"][None],
        "s5_c_im": s5_g["c_im"][None], "s5_d": s5_g["d"],
        "s5_w_out": jnp.concatenate([s5_g["w_val"], s5_g["w_gate"]], axis=1)[None], "s5_b_out": s5_g["b_out"],
        "lru_w_in": jnp.concatenate([lru_g["w_gate"], lru_g["w_rec"]], axis=1)[None], "lru_conv_w": lru_g["conv_w"][None],
        "lru_conv_b": lru_g["conv_b"], "lru_w_a": lru_g["w_a"][None], "lru_b_a": lru_g["b_a"],
        "lru_w_x": lru_g["w_x"][None], "lru_b_x": lru_g["b_x"], "lru_lam": lru_g["lam"], "lru_w_out": lru_g["w_out"][None],
        "sb_w_qkv": jnp.concatenate([sb_g["w_q"], sb_g["w_k"], sb_g["w_v"]], axis=1)[None],
        "sb_q_g": sb_g["q_g"], "sb_k_g": sb_g["k_g"], "sb_w_o": sb_g["w_o"][None],
        "ffn_w_in": jnp.stack([jnp.concatenate([g[1], g[2]], axis=1) for g in fg]),
        "ffn_conv_w": jnp.stack([g[3] for g in fg]), "ffn_conv_b": jnp.concatenate([g[4] for g in fg], axis=0),
        "ffn_w_out": jnp.stack([g[5] for g in fg]),
    }

    core = lax.axis_index("c").astype(jnp.int32).reshape(1)
    names2d = BIG + ["small"]
    shapes2d = [_rows2d(_local_shape(n)) for n in BIG]
    small_part = _pack([_to_shards(part[n], n) for n in SMALL], lead=(N_DEV,))
    shapes2d.append(small_part.shape[1:])
    by_dest = [_to_shards(part[n], n).reshape(4, 2, *s) for n, s in zip(BIG, shapes2d)] + [small_part.reshape(4, 2, *shapes2d[-1])]
    from_core = exchange_cores(by_dest, "reduce_cores")
    chip_sums = [add_own(core, p, r, f"add_{n}") for n, p, r in zip(names2d, by_dest, from_core)]
    from_chips = exchange_chips(chip_sums, "reduce_chips")

    def local2d(tree, n):
        return tree[n].reshape(_rows2d(_local_shape(n)))

    out = {}
    for n, parts in zip(BIG, from_chips[:-1]):
        res = adamw(parts, local2d(w, n), local2d(m, n), local2d(v, n), f"adamw_{n}")
        out[n] = [r.reshape(_local_shape(n)) for r in res]
    small_res = adamw(from_chips[-1], *[_pack([tree[n] for n in SMALL]) for tree in (w, m, v)], "adamw_small")
    for i, res in enumerate(small_res):
        for n, r in zip(SMALL, _unpack(res, small_shapes)):
            out.setdefault(n, [None] * 4)[i] = r

    repl_shapes = [PARAMS[n][0] for n in REPL]
    repl_parts = all_gather([_pack([part[n] for n in REPL])], "gather_grads")[0]
    repl_res = adamw(repl_parts, *[_pack([tree[n] for n in REPL]) for tree in (w, m, v)], "adamw_replicated")
    for i, res in enumerate(repl_res):
        for n, r in zip(REPL, _unpack(res, repl_shapes)):
            out.setdefault(n, [None] * 4)[i] = r

    return (loss, grad_x, *[out[n][0] for n in NAMES], *[out[n][1] for n in NAMES], *[out[n][2] for n in NAMES],
            *[out[n][3] for n in NAMES])
```

```python
import functools
import math

import jax
import jax.numpy as jnp
from jax import lax
from jax.experimental import pallas as pl
from jax.experimental.pallas import tpu as pltpu

F32 = jnp.float32
BF16 = jnp.bfloat16
MESH = pl.DeviceIdType.MESH

N_DEV = 8
D_MODEL = 1024
EPS = 1e-6
POOL_GROUP = 256
S5_GROUPS, S5_GROUP, S5_STATE = 64, 16, 64
LRU_BLOCK = 256
LRU_C = 8.0
SB_HEAD_DIM = 64
ATT_BLOCK = 256
FFN_HIDDEN = 2816
ADAM_LR, ADAM_B1, ADAM_B2, ADAM_EPS, ADAM_WD, ADAM_STEP = 0.001, 0.9, 0.999, 1e-08, 0.01, 10
LANES = 128
VMEM_LIMIT = 56 * 1024 * 1024


def _pick(n, target, mult=LANES):
    best = None
    for d in range(mult, min(n, target) + 1, mult):
        if n % d == 0:
            best = d
    return best or n


def _params(*sem):
    return pltpu.CompilerParams(dimension_semantics=sem, vmem_limit_bytes=VMEM_LIMIT)


def mm(a, b, *, ta=False, tb=False, acc=None, out_dtype=F32, name):
    if ta:
        kdim, m = a.shape
    else:
        m, kdim = a.shape
    if tb:
        n, k2 = b.shape
    else:
        k2, n = b.shape
    assert kdim == k2, (a.shape, b.shape)
    tm = _pick(m, 512)
    tn = _pick(n, 1408)
    tk = _pick(kdim, 1408)
    nk = kdim // tk
    a_spec = pl.BlockSpec((tk, tm), lambda i, j, k: (k, i)) if ta else pl.BlockSpec((tm, tk), lambda i, j, k: (i, k))
    b_spec = pl.BlockSpec((tn, tk), lambda i, j, k: (j, k)) if tb else pl.BlockSpec((tk, tn), lambda i, j, k: (k, j))
    o_spec = pl.BlockSpec((tm, tn), lambda i, j, k: (i, j))
    dims = (((0,) if ta else (1,), (1,) if tb else (0,)), ((), ()))
    has_acc = acc is not None

    def body(*refs):
        if has_acc:
            a_ref, b_ref, c_ref, o_ref, acc_ref = refs
        else:
            a_ref, b_ref, o_ref, acc_ref = refs
        k = pl.program_id(2)

        @pl.when(k == 0)
        def _():
            acc_ref[...] = c_ref[...].astype(F32) if has_acc else jnp.zeros_like(acc_ref)

        acc_ref[...] += lax.dot_general(a_ref[...].astype(BF16), b_ref[...].astype(BF16), dims,
                                        preferred_element_type=F32)

        @pl.when(k == nk - 1)
        def _():
            o_ref[...] = acc_ref[...].astype(out_dtype)

    ins = [a, b] + ([acc] if has_acc else [])
    specs = [a_spec, b_spec] + ([o_spec] if has_acc else [])
    return pl.pallas_call(
        body, name=name, grid=(m // tm, n // tn, nk), in_specs=specs, out_specs=o_spec,
        out_shape=jax.ShapeDtypeStruct((m, n), out_dtype), scratch_shapes=[pltpu.VMEM((tm, tn), F32)],
        compiler_params=_params("parallel", "parallel", "arbitrary"),
    )(*ins)


def rowwise(fn, tiled, bcast, outs, reds=(), *, tm=256, name):
    t = tiled[0].shape[0]
    tm = min(tm, t)
    assert t % tm == 0
    n_t, n_b, n_o, n_r = len(tiled), len(bcast), len(outs), len(reds)

    def body(*refs):
        vals = [r[...] for r in refs[:n_t + n_b]]
        res = fn(*vals)
        res = res if isinstance(res, tuple) else (res,)
        o_refs = refs[n_t + n_b:n_t + n_b + n_o]
        r_refs = refs[n_t + n_b + n_o:]
        for o_ref, v in zip(o_refs, res[:n_o]):
            o_ref[...] = v.astype(o_ref.dtype)
        first = pl.program_id(0) == 0
        for r_ref, v in zip(r_refs, res[n_o:]):
            @pl.when(first)
            def _(r_ref=r_ref, v=v):
                r_ref[...] = v.astype(F32)

            @pl.when(jnp.logical_not(first))
            def _(r_ref=r_ref, v=v):
                r_ref[...] += v.astype(F32)

    in_specs = [pl.BlockSpec((tm, a.shape[1]), lambda i: (i, 0)) for a in tiled]
    in_specs += [pl.BlockSpec(a.shape, lambda i, nd=a.ndim: (0,) * nd) for a in bcast]
    out_specs = [pl.BlockSpec((tm, c), lambda i: (i, 0)) for c, _ in outs]
    out_specs += [pl.BlockSpec(s, lambda i: (0, 0)) for s in reds]
    out_shape = [jax.ShapeDtypeStruct((t, c), dt) for c, dt in outs]
    out_shape += [jax.ShapeDtypeStruct(s, F32) for s in reds]
    res = pl.pallas_call(
        body, name=name, grid=(t // tm,), in_specs=in_specs, out_specs=out_specs, out_shape=out_shape,
        compiler_params=_params("arbitrary"),
    )(*tiled, *bcast)
    return res


def seqwise(fn, seqs, chans, outs, reds=(), *, ct, name, fulls=()):
    def split(x):
        return x if isinstance(x, tuple) else (x, 0)

    seqs = [split(s) for s in seqs]
    chans = [split(s) for s in chans]
    bsz, seq = seqs[0][0].shape[:2]
    n_c = outs[0][0] // ct if outs else reds[0][1] // ct
    n_s, n_ch, n_f, n_o, n_r = len(seqs), len(chans), len(fulls), len(outs), len(reds)

    def body(*refs):
        c = pl.program_id(0)
        vals = [r[...] for r in refs[:n_s + n_ch + n_f]]
        res = fn(c, *vals)
        res = res if isinstance(res, tuple) else (res,)
        o_refs = refs[n_s + n_ch + n_f:n_s + n_ch + n_f + n_o]
        r_refs = refs[n_s + n_ch + n_f + n_o:]
        for o_ref, v in zip(o_refs, res[:n_o]):
            o_ref[...] = v.astype(o_ref.dtype)
        first = pl.program_id(1) == 0
        for r_ref, v in zip(r_refs, res[n_o:]):
            @pl.when(first)
            def _(r_ref=r_ref, v=v):
                r_ref[...] = v.astype(F32)

            @pl.when(jnp.logical_not(first))
            def _(r_ref=r_ref, v=v):
                r_ref[...] += v.astype(F32)

    in_specs = [pl.BlockSpec((None, seq, ct), lambda c, b, off=off: (b, 0, c + off)) for _, off in seqs]
    in_specs += [pl.BlockSpec((a.shape[0], ct), lambda c, b, off=off: (0, c + off)) for a, off in chans]
    in_specs += [pl.BlockSpec(a.shape, lambda c, b, nd=a.ndim: (0,) * nd) for a in fulls]
    out_specs = [pl.BlockSpec((None, seq, ct), lambda c, b: (b, 0, c)) for _ in outs]
    out_specs += [pl.BlockSpec((r, ct), lambda c, b: (0, c)) for r, _ in reds]
    out_shape = [jax.ShapeDtypeStruct((bsz, seq, cc), dt) for cc, dt in outs]
    out_shape += [jax.ShapeDtypeStruct(s, F32) for s in reds]
    return pl.pallas_call(
        body, name=name, grid=(n_c, bsz), in_specs=in_specs, out_specs=out_specs, out_shape=out_shape,
        compiler_params=_params("arbitrary", "arbitrary"),
    )(*[a for a, _ in seqs], *[a for a, _ in chans], *fulls)


def _rows(x):
    return lax.broadcasted_iota(jnp.int32, x.shape, 0)


def shift_down(x, d, fill=0.0):
    if d == 0:
        return x
    return jnp.where(_rows(x) >= d, pltpu.roll(x, d, 0), fill)


def shift_up(x, d, fill=0.0):
    if d == 0:
        return x
    s = x.shape[0]
    return jnp.where(_rows(x) < s - d, pltpu.roll(x, s - d, 0), fill)


def conv_fwd(x, w, b):
    kw = w.shape[0]
    y = b + w[kw - 1:kw] * x
    for d in range(1, kw):
        y = y + w[kw - 1 - d:kw - d] * shift_down(x, d)
    return y


def conv_bwd(x, w, dy):
    kw = w.shape[0]
    dx = w[kw - 1:kw] * dy
    dws = [jnp.sum(dy * x, axis=0, keepdims=True)]
    for d in range(1, kw):
        dx = dx + w[kw - 1 - d:kw - d] * shift_up(dy, d)
        dws.append(jnp.sum(dy * shift_down(x, d), axis=0, keepdims=True))
    dw = jnp.concatenate(dws[::-1], axis=0)
    return dx, dw, jnp.sum(dy, axis=0, keepdims=True)


def sigmoid(x):
    return 1.0 / (1.0 + jnp.exp(-x))


def rms_fwd(x, g, out_dtype, name):
    def fn(x, g):
        return x * lax.rsqrt(jnp.mean(x * x, axis=-1, keepdims=True) + EPS) * g

    return rowwise(fn, [x], [g], [(x.shape[1], out_dtype)], name=name)[0]


def rms_bwd(x, g, dh_parts, dres, name):
    n_p = len(dh_parts)

    def fn(x, dres, *rest):
        dh = rest[0].astype(F32)
        for p in rest[1:n_p]:
            dh = dh + p.astype(F32)
        g = rest[n_p]
        r = lax.rsqrt(jnp.mean(x * x, axis=-1, keepdims=True) + EPS)
        xh = x * r
        dxh = dh * g
        dx = r * (dxh - xh * jnp.mean(dxh * xh, axis=-1, keepdims=True))
        return dres + dx, jnp.sum(dh * xh, axis=0, keepdims=True)

    return rowwise(fn, [x, dres, *dh_parts], [g], [(x.shape[1], F32)], [(1, x.shape[1])], name=name)


def ffn_fwd(x, g, w_val, w_gate, cw, cb, w_out, bsz, li):
    t, d = x.shape
    hid = w_val.shape[1]
    xn = rms_fwd(x, g, BF16, f"ffn{li}_rms")
    hv = mm(xn, w_val, name=f"ffn{li}_hv").reshape(bsz, t // bsz, hid)
    hg = mm(xn, w_gate, name=f"ffn{li}_hg").reshape(bsz, t // bsz, hid)
    ct = _pick(hid, 256)
    off = hid // ct

    def fn(c, hv, hg, wv, wg, bv, bg):
        val = conv_fwd(hv, wv, bv)
        gate = conv_fwd(hg, wg, bg)
        return gate * sigmoid(gate) * val

    act = seqwise(fn, [hv, hg], [cw, (cw, off), cb, (cb, off)], [(hid, BF16)], ct=ct, name=f"ffn{li}_gate")[0]
    act = act.reshape(t, hid)
    y = mm(act, w_out, acc=x, name=f"ffn{li}_out")
    return y, (x, xn, hv, hg, act)


def ffn_bwd(dy, saved, g, w_val, w_gate, cw, cb, w_out, li):
    x, xn, hv, hg, act = saved
    bsz, seq, hid = hv.shape
    t = bsz * seq
    d_wout = mm(act, dy, ta=True, name=f"ffn{li}_dwout")
    dact = mm(dy, w_out, tb=True, name=f"ffn{li}_dact").reshape(bsz, seq, hid)
    ct = _pick(hid, 256)
    off = hid // ct

    def fn(c, hv, hg, dact, wv, wg, bv, bg):
        val = conv_fwd(hv, wv, bv)
        gate = conv_fwd(hg, wg, bg)
        sg = sigmoid(gate)
        silu = gate * sg
        dval = dact * silu
        dgate = dact * val * (sg + silu * (1.0 - sg))
        dhv, dwv, dbv = conv_bwd(hv, wv, dval)
        dhg, dwg, dbg = conv_bwd(hg, wg, dgate)
        return dhv, dhg, dwv, dwg, dbv, dbg

    kw = cw.shape[0]
    dhv, dhg, dwv, dwg, dbv, dbg = seqwise(
        fn, [hv, hg, dact], [cw, (cw, off), cb, (cb, off)], [(hid, BF16), (hid, BF16)],
        [(kw, hid), (kw, hid), (1, hid), (1, hid)], ct=ct, name=f"ffn{li}_dgate")
    dhv = dhv.reshape(t, hid)
    dhg = dhg.reshape(t, hid)
    d_wv = mm(xn, dhv, ta=True, name=f"ffn{li}_dwv")
    d_wg = mm(xn, dhg, ta=True, name=f"ffn{li}_dwg")
    dxn = mm(dhv, w_val, tb=True, name=f"ffn{li}_dxn1")
    dxn = mm(dhg, w_gate, tb=True, acc=dxn, name=f"ffn{li}_dxn2")
    dx, dg = rms_bwd(x, g, [dxn], dy, f"ffn{li}_drms")
    d_cw = jnp.concatenate([dwv, dwg], axis=1)
    d_cb = jnp.concatenate([dbv, dbg], axis=1)
    return dx, (dg, d_wv, d_wg, d_cw, d_cb, d_wout)


def _window_sum(x, c, shift):
    s2 = x + shift(x, 1)
    s4 = s2 + shift(s2, 2)
    s8 = s4 + shift(s4, 4)
    s16 = s8 + shift(s8, 8)
    return jnp.where(c == 0, s2, jnp.where(c == 1, s4, jnp.where(c == 2, s8, s16)))


def _pool_inv_count(x, c):
    w = jnp.left_shift(2, c)
    return 1.0 / jnp.minimum(_rows(x) + 1, w).astype(F32)


def pool_fwd(x, g, w, b, scale, bsz):
    t, d = x.shape
    h = rms_fwd(x, g, F32, "pool_rms").reshape(bsz, t // bsz, d)

    def fn(c, h, x, b, scale, w):
        dd = _window_sum(h, c, shift_down) * _pool_inv_count(h, c) - h
        y = jnp.dot(dd.astype(BF16), w[0], preferred_element_type=F32) + b
        return x + scale * y

    wspec = pl.BlockSpec((1, POOL_GROUP, POOL_GROUP), lambda c, bb: (c, 0, 0))
    y = _seqwise_w(fn, [h, x.reshape(bsz, t // bsz, d)], [b, scale], [(w, wspec)], [(d, F32)], [], ct=POOL_GROUP,
                   name="pool_fwd")[0]
    return y.reshape(t, d), (x, h)


def pool_bwd(dy, saved, g, w, b, scale):
    x, h = saved
    bsz, seq, d = h.shape
    t = bsz * seq

    def fn(c, h, dy, b, scale, w):
        inv = _pool_inv_count(h, c)
        dd = _window_sum(h, c, shift_down) * inv - h
        ddb = dd.astype(BF16)
        y = jnp.dot(ddb, w[0], preferred_element_type=F32) + b
        dscale = jnp.sum(dy * y, axis=0, keepdims=True)
        dyy = dy * scale
        db = jnp.sum(dyy, axis=0, keepdims=True)
        dyb = dyy.astype(BF16)
        dw = lax.dot_general(ddb, dyb, (((0,), (0,)), ((), ())), preferred_element_type=F32)
        ddd = lax.dot_general(dyb, w[0], (((1,), (1,)), ((), ())), preferred_element_type=F32)
        dh = _window_sum(ddd * inv, c, shift_up) - ddd
        return dh, db, dscale, dw[None]

    wspec = pl.BlockSpec((1, POOL_GROUP, POOL_GROUP), lambda c, bb: (c, 0, 0))
    dh, db, dscale, dw = _seqwise_w(
        fn, [h, dy.reshape(bsz, seq, d)], [b, scale], [(w, wspec)], [(d, F32)], [(1, d), (1, d)], ct=POOL_GROUP,
        name="pool_bwd", wreds=[((4, POOL_GROUP, POOL_GROUP), wspec)])
    dx, dg = rms_bwd(x, g, [dh.reshape(t, d)], dy, "pool_drms")
    return dx, (dg, dw, db, dscale)


def _seqwise_w(fn, seqs, chans, blocked, outs, reds, *, ct, name, wreds=(), phased=False):
    bsz, seq = seqs[0].shape[:2]
    n_c = seqs[0].shape[2] // ct
    n_s = len(seqs)
    n_in = n_s + len(chans) + len(blocked)
    n_o, n_r = len(outs), len(reds) + len(wreds)

    def body(*refs):
        c = pl.program_id(0)
        seq_vals = [_load_phases(r) if phased else r[...] for r in refs[:n_s]]
        res = fn(c, *seq_vals, *[r[...] for r in refs[n_s:n_in]])
        res = res if isinstance(res, tuple) else (res,)
        for o_ref, v in zip(refs[n_in:n_in + n_o], res[:n_o]):
            if phased:
                _store_phases(o_ref, v)
            else:
                o_ref[...] = v.astype(o_ref.dtype)
        first = pl.program_id(1) == 0
        for r_ref, v in zip(refs[n_in + n_o:], res[n_o:]):
            @pl.when(first)
            def _(r_ref=r_ref, v=v):
                r_ref[...] = v.astype(F32)

            @pl.when(jnp.logical_not(first))
            def _(r_ref=r_ref, v=v):
                r_ref[...] += v.astype(F32)

    in_specs = [pl.BlockSpec((None, seq, ct), lambda c, b: (b, 0, c)) for _ in seqs]
    in_specs += [pl.BlockSpec((a.shape[0], ct), lambda c, b: (0, c)) for a in chans]
    in_specs += [spec for _, spec in blocked]
    out_specs = [pl.BlockSpec((None, seq, ct), lambda c, b: (b, 0, c)) for _ in outs]
    out_specs += [pl.BlockSpec((r, ct), lambda c, b: (0, c)) for r, _ in reds]
    out_specs += [spec for _, spec in wreds]
    out_shape = [jax.ShapeDtypeStruct((bsz, seq, cc), dt) for cc, dt in outs]
    out_shape += [jax.ShapeDtypeStruct(s, F32) for s in reds]
    out_shape += [jax.ShapeDtypeStruct(s, F32) for s, _ in wreds]
    return pl.pallas_call(
        body, name=name, grid=(n_c, bsz), in_specs=in_specs, out_specs=out_specs, out_shape=out_shape,
        compiler_params=_params("arbitrary", "arbitrary"),
    )(*seqs, *chans, *[a for a, _ in blocked])


S5_TILE_GROUPS = LANES // S5_GROUP
S5_TILES = S5_GROUPS // S5_TILE_GROUPS
S5_TILE_STATE = S5_TILE_GROUPS * S5_STATE


def _s5_discretize(lam_re, lam_im, log_dt, b_re, b_im):
    lr = jnp.minimum(lam_re, -1e-4)
    dt = jnp.exp(log_dt)
    er = jnp.exp(lr * dt)
    ar = er * jnp.cos(lam_im * dt)
    ai = er * jnp.sin(lam_im * dt)
    den = lr * lr + lam_im * lam_im
    cr = ((ar - 1.0) * lr + ai * lam_im) / den
    ci = (ai * lr - (ar - 1.0) * lam_im) / den
    return ar, ai, cr * b_re - ci * b_im, cr * b_im + ci * b_re


def _whole(a):
    return pl.BlockSpec(a.shape, lambda *_: (0,) * a.ndim)


def s5_prep(lam_re, lam_im, log_dt, b_re, b_im):
    def body(lr, li, ld, br, bi, ar_o, ai_o, bbr_o, bbi_o):
        ar, ai, bbr, bbi = _s5_discretize(lr[...], li[...], ld[...], br[...], bi[...])
        ar_o[...] = ar
        ai_o[...] = ai
        bbr_o[...] = bbr
        bbi_o[...] = bbi

    ins = [lam_re, lam_im, log_dt, b_re, b_im]
    outs = [lam_re, lam_im, b_re, b_im]
    return pl.pallas_call(
        body, name="s5_prep", in_specs=[_whole(a) for a in ins], out_specs=[_whole(a) for a in outs],
        out_shape=[jax.ShapeDtypeStruct(a.shape, F32) for a in outs],
        compiler_params=pltpu.CompilerParams(vmem_limit_bytes=VMEM_LIMIT),
    )(*ins)


def s5_prep_bwd(lam_re, lam_im, log_dt, b_re, b_im, d_ar, d_ai, d_bbr, d_bbi):
    def body(lr, li, ld, br, bi, dar, dai, dbbr, dbbi, *outs):
        _, vjp = jax.vjp(_s5_discretize, lr[...], li[...], ld[...], br[...], bi[...])
        for o, v in zip(outs, vjp((dar[...], dai[...], dbbr[...], dbbi[...]))):
            o[...] = v

    ins = [lam_re, lam_im, log_dt, b_re, b_im, d_ar, d_ai, d_bbr, d_bbi]
    return pl.pallas_call(
        body, name="s5_prep_bwd", in_specs=[_whole(a) for a in ins], out_specs=[_whole(a) for a in ins[:5]],
        out_shape=[jax.ShapeDtypeStruct(a.shape, F32) for a in ins[:5]],
        compiler_params=pltpu.CompilerParams(vmem_limit_bytes=VMEM_LIMIT),
    )(*ins)


def scan_lti(br, bi, ar, ai, reverse=False):
    shift = shift_up if reverse else shift_down
    seq = br.shape[0]
    d = 1
    while d < seq:
        sr, si = shift(br, d), shift(bi, d)
        br, bi = br + ar * sr - ai * si, bi + ar * si + ai * sr
        ar, ai = ar * ar - ai * ai, 2.0 * ar * ai
        d *= 2
    return br, bi


PHASES = 8


def _load_phases(ref):
    groups = ref.shape[0] // PHASES
    return jnp.concatenate([ref[pl.ds(r, groups, stride=PHASES), :] for r in range(PHASES)], axis=0)


def _store_phases(ref, x):
    groups = ref.shape[0] // PHASES
    for r in range(PHASES):
        ref[pl.ds(r, groups, stride=PHASES), :] = x[r * groups:(r + 1) * groups].astype(ref.dtype)


def _split_phases(x):
    groups = x.shape[0] // PHASES
    return [x[r * groups:(r + 1) * groups] for r in range(PHASES)]


def _prev_phased(x):
    ph = _split_phases(x)
    return jnp.concatenate([shift_down(ph[PHASES - 1], 1)] + ph[:PHASES - 1], axis=0)


def _cmul_add(xr, xi, ar, ai, yr, yi):
    return xr + ar * yr - ai * yi, xi + ar * yi + ai * yr


def scan_phases(br, bi, ar, ai, reverse=False):
    rs, im = _split_phases(br), _split_phases(bi)
    order = range(PHASES - 2, -1, -1) if reverse else range(1, PHASES)
    step = 1 if reverse else -1
    for r in order:
        rs[r], im[r] = _cmul_add(rs[r], im[r], ar, ai, rs[r + step], im[r + step])
    powers = [(ar, ai)]
    for _ in range(PHASES - 1):
        pr, pi = powers[-1]
        powers.append((pr * ar - pi * ai, pr * ai + pi * ar))
    end = 0 if reverse else PHASES - 1
    cr, ci = scan_lti(rs[end], im[end], *powers[PHASES - 1], reverse)
    shift = shift_up if reverse else shift_down
    inr, ini = shift(cr, 1), shift(ci, 1)
    for r in range(PHASES):
        if r == end:
            rs[r], im[r] = cr, ci
        else:
            pr, pi = powers[PHASES - 1 - r if reverse else r]
            rs[r], im[r] = _cmul_add(rs[r], im[r], pr, pi, inr, ini)
    return jnp.concatenate(rs, axis=0), jnp.concatenate(im, axis=0)


def _s5_block_diag(w, rows_first):
    g, a, b = w.shape
    w = w.reshape(S5_TILES, S5_TILE_GROUPS, a, b)
    eye = jnp.eye(S5_TILE_GROUPS, dtype=w.dtype)
    if rows_first:
        return jnp.einsum("tgab,gk->tgakb", w, eye).reshape(S5_TILES, S5_TILE_GROUPS * a, S5_TILE_GROUPS * b)
    return jnp.einsum("tgab,gk->tgbka", w, eye).reshape(S5_TILES, S5_TILE_GROUPS * b, S5_TILE_GROUPS * a)


def _s5_diag_blocks(w, a, b):
    w = w.reshape(S5_TILES, S5_TILE_GROUPS, a, S5_TILE_GROUPS, b)
    eye = jnp.eye(S5_TILE_GROUPS, dtype=w.dtype)
    return jnp.einsum("tgakb,gk->tgab", w, eye).reshape(S5_GROUPS, a, b)


def _gelu(x):
    return jax.nn.gelu(x)


def _s5_tile_specs():
    b_spec = pl.BlockSpec((1, LANES, S5_TILE_STATE), lambda c, b: (c, 0, 0))
    c_spec = pl.BlockSpec((1, S5_TILE_STATE, LANES), lambda c, b: (c, 0, 0))
    a_spec = pl.BlockSpec((1, 1, S5_TILE_STATE), lambda c, b: (c, 0, 0))
    return b_spec, c_spec, a_spec


def s5_fwd(x, g, p, bsz):
    t, d = x.shape
    seq = t // bsz
    h = rms_fwd(x, g, F32, "s5_rms").reshape(bsz, seq, d)
    ar, ai, bbr, bbi = s5_prep(p["lam_re"][..., None], p["lam_im"][..., None], p["log_dt"][:, None, None],
                               p["b_re"], p["b_im"])
    bdr = _s5_block_diag(bbr, False).astype(BF16)
    bdi = _s5_block_diag(bbi, False).astype(BF16)
    cdr = _s5_block_diag(p["c_re"], False).astype(BF16)
    cdi = _s5_block_diag(p["c_im"], False).astype(BF16)
    a_r = ar.reshape(S5_TILES, 1, S5_TILE_STATE)
    a_i = ai.reshape(S5_TILES, 1, S5_TILE_STATE)
    b_spec, c_spec, a_spec = _s5_tile_specs()

    def body(u_ref, dskip_ref, bdr, bdi, cdr, cdi, a_r, a_i, y_ref, yg_ref):
        u = _load_phases(u_ref)
        ub = u.astype(BF16)
        y = dskip_ref[...] * u
        for k in range(S5_TILE_STATE // LANES):
            sl = slice(k * LANES, (k + 1) * LANES)
            br = jnp.dot(ub, bdr[0][:, sl], preferred_element_type=F32)
            bi = jnp.dot(ub, bdi[0][:, sl], preferred_element_type=F32)
            sr, si = scan_phases(br, bi, a_r[0][:, sl], a_i[0][:, sl])
            y = y + jnp.dot(sr.astype(BF16), cdr[0][sl, :], preferred_element_type=F32)
            y = y - jnp.dot(si.astype(BF16), cdi[0][sl, :], preferred_element_type=F32)
        _store_phases(y_ref, y)
        _store_phases(yg_ref, _gelu(y))

    seq_spec = pl.BlockSpec((None, seq, LANES), lambda c, b: (b, 0, c))
    y, yg = pl.pallas_call(
        body, name="s5_core", grid=(S5_TILES, bsz),
        in_specs=[seq_spec, pl.BlockSpec((1, LANES), lambda c, b: (0, c)), b_spec, b_spec, c_spec, c_spec, a_spec, a_spec],
        out_specs=[seq_spec, seq_spec],
        out_shape=[jax.ShapeDtypeStruct((bsz, seq, d), F32)] * 2,
        compiler_params=_params("parallel", "parallel"),
    )(h, p["d"], bdr, bdi, cdr, cdi, a_r, a_i)
    yg = yg.reshape(t, d)
    zv = mm(yg, p["w_val"], name="s5_zv")
    zg = mm(yg, p["w_gate"], name="s5_zg")

    def gate(x, zv, zg, bv, bg):
        return x + (zv + bv) * sigmoid(zg + bg)

    out = rowwise(gate, [x, zv, zg], [p["b_out"][:, :d], p["b_out"][:, d:]], [(d, F32)], name="s5_gate")[0]
    return out, (x, h, y, yg, zv, zg, (ar, ai, bdr, bdi, cdr, cdi, a_r, a_i))


def s5_bwd(dy, saved, g, p):
    x, h, y, yg, zv, zg, (ar, ai, bdr, bdi, cdr, cdi, a_r, a_i) = saved
    bsz, seq, d = h.shape
    t = bsz * seq

    def dgate(dy, zv, zg, bv, bg):
        val = zv + bv
        sg = sigmoid(zg + bg)
        dzv = dy * sg
        dzg = dy * val * sg * (1.0 - sg)
        return dzv, dzg, jnp.sum(dzv, axis=0, keepdims=True), jnp.sum(dzg, axis=0, keepdims=True)

    dzv, dzg, dbv, dbg = rowwise(dgate, [dy, zv, zg], [p["b_out"][:, :d], p["b_out"][:, d:]],
                                 [(d, BF16), (d, BF16)], [(1, d), (1, d)], name="s5_dgate")
    d_wv = mm(yg, dzv, ta=True, name="s5_dwv")
    d_wg = mm(yg, dzg, ta=True, name="s5_dwg")
    dyg = mm(dzv, p["w_val"], tb=True, name="s5_dyg1")
    dyg = mm(dzg, p["w_gate"], tb=True, acc=dyg, name="s5_dyg2").reshape(bsz, seq, d)
    b_spec, c_spec, a_spec = _s5_tile_specs()
    tr = (((0,), (0,)), ((), ()))
    nt = (((1,), (1,)), ((), ()))

    def fn(c, u, y, dyg, dskip, bdr, bdi, cdr, cdi, a_r, a_i):
        _, gelu_vjp = jax.vjp(_gelu, y)
        dyy = gelu_vjp(dyg)[0]
        ddskip = jnp.sum(dyy * u, axis=0, keepdims=True)
        du = dyy * dskip
        dyb = dyy.astype(BF16)
        ub = u.astype(BF16)
        dcr, dci, dbr, dbi, dar, dai = [], [], [], [], [], []
        for k in range(S5_TILE_STATE // LANES):
            sl = slice(k * LANES, (k + 1) * LANES)
            akr, aki = a_r[0][:, sl], a_i[0][:, sl]
            br = jnp.dot(ub, bdr[0][:, sl], preferred_element_type=F32)
            bi = jnp.dot(ub, bdi[0][:, sl], preferred_element_type=F32)
            sr, si = scan_phases(br, bi, akr, aki)
            dcr.append(lax.dot_general(sr.astype(BF16), dyb, tr, preferred_element_type=F32))
            dci.append(-lax.dot_general(si.astype(BF16), dyb, tr, preferred_element_type=F32))
            gr = lax.dot_general(dyb, cdr[0][sl, :], nt, preferred_element_type=F32)
            gi = -lax.dot_general(dyb, cdi[0][sl, :], nt, preferred_element_type=F32)
            gr, gi = scan_phases(gr, gi, akr, -aki, reverse=True)
            spr, spi = _prev_phased(sr), _prev_phased(si)
            dar.append(jnp.sum(gr * spr + gi * spi, axis=0, keepdims=True))
            dai.append(jnp.sum(gi * spr - gr * spi, axis=0, keepdims=True))
            grb, gib = gr.astype(BF16), gi.astype(BF16)
            dbr.append(lax.dot_general(ub, grb, tr, preferred_element_type=F32))
            dbi.append(lax.dot_general(ub, gib, tr, preferred_element_type=F32))
            du = du + lax.dot_general(grb, bdr[0][:, sl], nt, preferred_element_type=F32)
            du = du + lax.dot_general(gib, bdi[0][:, sl], nt, preferred_element_type=F32)
        return (du, ddskip, jnp.concatenate(dbr, axis=1)[None], jnp.concatenate(dbi, axis=1)[None],
                jnp.concatenate(dcr, axis=0)[None], jnp.concatenate(dci, axis=0)[None],
                jnp.concatenate(dar, axis=1)[None], jnp.concatenate(dai, axis=1)[None])

    du, ddskip, dbdr, dbdi, dcdr, dcdi, dar, dai = _seqwise_w(
        fn, [h, y, dyg], [p["d"]], [(bdr, b_spec), (bdi, b_spec), (cdr, c_spec), (cdi, c_spec), (a_r, a_spec), (a_i, a_spec)],
        [(d, F32)], [(1, d)], ct=LANES, name="s5_dcore", phased=True,
        wreds=[((S5_TILES, LANES, S5_TILE_STATE), b_spec), ((S5_TILES, LANES, S5_TILE_STATE), b_spec),
               ((S5_TILES, S5_TILE_STATE, LANES), c_spec), ((S5_TILES, S5_TILE_STATE, LANES), c_spec),
               ((S5_TILES, 1, S5_TILE_STATE), a_spec), ((S5_TILES, 1, S5_TILE_STATE), a_spec)])
    d_bbr = _s5_diag_blocks(dbdr, S5_GROUP, S5_STATE).transpose(0, 2, 1)
    d_bbi = _s5_diag_blocks(dbdi, S5_GROUP, S5_STATE).transpose(0, 2, 1)
    d_cre = _s5_diag_blocks(dcdr, S5_STATE, S5_GROUP).transpose(0, 2, 1)
    d_cim = _s5_diag_blocks(dcdi, S5_STATE, S5_GROUP).transpose(0, 2, 1)
    d_lre, d_lim, d_ldt, d_bre, d_bim = s5_prep_bwd(
        p["lam_re"][..., None], p["lam_im"][..., None], p["log_dt"][:, None, None], p["b_re"], p["b_im"],
        dar.reshape(S5_GROUPS, S5_STATE, 1), dai.reshape(S5_GROUPS, S5_STATE, 1), d_bbr, d_bbi)
    dx, dg = rms_bwd(x, g, [du.reshape(t, d)], dy, "s5_drms")
    grads = dict(g=dg, lam_re=d_lre[..., 0], lam_im=d_lim[..., 0], log_dt=d_ldt[:, 0, 0], b_re=d_bre, b_im=d_bim,
                 c_re=d_cre, c_im=d_cim, d=ddskip, w_val=d_wv, w_gate=d_wg, b_out=jnp.concatenate([dbv, dbg], axis=1))
    return dx, grads


def _expm1_nonpos(x):
    u = jnp.exp(x)
    safe = (u - 1.0) * x / jnp.log(u)
    return jnp.where(u == 1.0, x, jnp.where(x < -20.0, -1.0, safe))


def _softplus(x):
    return jnp.maximum(x, 0.0) + jnp.log(1.0 + jnp.exp(-jnp.abs(x)))


def scan_ltv(a, b, reverse=False):
    shift = shift_up if reverse else shift_down
    seq = a.shape[0]
    d = 1
    while d < seq:
        b = b + a * shift(b, d)
        if 2 * d < seq:
            a = a * shift(a, d, 1.0)
        d *= 2
    return b


def _lru_gates(rec0, cw, cb, ba, bx, lam, wa, wx):
    rec = conv_fwd(rec0, cw, cb)
    recb = rec.astype(BF16)
    r = sigmoid(jnp.dot(recb, wa, preferred_element_type=F32) + ba)
    i = sigmoid(jnp.dot(recb, wx, preferred_element_type=F32) + bx)
    sp = _softplus(-lam)
    log_a = -LRU_C * r * sp
    a = jnp.exp(log_a)
    mult = jnp.sqrt(-_expm1_nonpos(2.0 * log_a))
    return rec, recb, r, i, sp, a, mult


def lru_fwd(x, g, p, bsz):
    t, d = x.shape
    seq = t // bsz
    xn = rms_fwd(x, g, BF16, "lru_rms")
    gb = mm(xn, p["w_gate"], name="lru_gb").reshape(bsz, seq, d)
    rec0 = mm(xn, p["w_rec"], name="lru_rec").reshape(bsz, seq, d)
    wspec = pl.BlockSpec((1, LRU_BLOCK, LRU_BLOCK), lambda c, b: (c, 0, 0))

    def fn(c, rec0, gb, cw, cb, ba, bx, lam, wa, wx):
        rec, _, _, i, _, a, mult = _lru_gates(rec0, cw, cb, ba, bx, lam, wa[0], wx[0])
        h = scan_ltv(a, mult * (i * rec))
        return _gelu(gb) * h

    y = _seqwise_w(fn, [rec0, gb], [p["conv_w"], p["conv_b"], p["b_a"], p["b_x"], p["lam"]],
                   [(p["w_a"], wspec), (p["w_x"], wspec)], [(d, BF16)], [], ct=LRU_BLOCK, name="lru_core")[0]
    y = y.reshape(t, d)
    out = mm(y, p["w_out"], acc=x, name="lru_out")
    return out, (x, xn, gb, rec0, y)


def lru_bwd(dy, saved, g, p):
    x, xn, gb, rec0, y = saved
    bsz, seq, d = gb.shape
    t = bsz * seq
    d_wout = mm(y, dy, ta=True, name="lru_dwout")
    dyy = mm(dy, p["w_out"], tb=True, name="lru_dy").reshape(bsz, seq, d)
    wspec = pl.BlockSpec((1, LRU_BLOCK, LRU_BLOCK), lambda c, b: (c, 0, 0))
    tr = (((0,), (0,)), ((), ()))
    nt = (((1,), (1,)), ((), ()))

    def fn(c, rec0, gb, dyy, cw, cb, ba, bx, lam, wa, wx):
        rec, recb, r, i, sp, a, mult = _lru_gates(rec0, cw, cb, ba, bx, lam, wa[0], wx[0])
        h = scan_ltv(a, mult * (i * rec))
        gg, gelu_vjp = jax.vjp(_gelu, gb)
        dgb = gelu_vjp(dyy * h)[0]
        gr = scan_ltv(shift_up(a, 1), dyy * gg, reverse=True)
        da = gr * shift_down(h, 1)
        dmult = gr * i * rec
        di = gr * mult * rec
        drec = gr * mult * i
        dla = da * a - dmult * (a * a) / mult
        dr = dla * (-LRU_C * sp)
        dlam = jnp.sum(dla * (-LRU_C * r), axis=0, keepdims=True) * (-sigmoid(-lam))
        dpa = dr * r * (1.0 - r)
        dpx = di * i * (1.0 - i)
        dpab, dpxb = dpa.astype(BF16), dpx.astype(BF16)
        dwa = lax.dot_general(recb, dpab, tr, preferred_element_type=F32)
        dwx = lax.dot_general(recb, dpxb, tr, preferred_element_type=F32)
        drec = drec + lax.dot_general(dpab, wa[0], nt, preferred_element_type=F32)
        drec = drec + lax.dot_general(dpxb, wx[0], nt, preferred_element_type=F32)
        drec0, dcw, dcb = conv_bwd(rec0, cw, drec)
        return (dgb, drec0, dcw, dcb, jnp.sum(dpa, axis=0, keepdims=True), jnp.sum(dpx, axis=0, keepdims=True), dlam,
                dwa[None], dwx[None])

    kw = p["conv_w"].shape[0]
    dgb, drec0, dcw, dcb, dba, dbx, dlam, dwa, dwx = _seqwise_w(
        fn, [rec0, gb, dyy], [p["conv_w"], p["conv_b"], p["b_a"], p["b_x"], p["lam"]],
        [(p["w_a"], wspec), (p["w_x"], wspec)], [(d, BF16), (d, BF16)],
        [(kw, d), (1, d), (1, d), (1, d), (1, d)], ct=LRU_BLOCK, name="lru_dcore",
        wreds=[(p["w_a"].shape, wspec), (p["w_x"].shape, wspec)])
    dgb = dgb.reshape(t, d)
    drec0 = drec0.reshape(t, d)
    d_wgate = mm(xn, dgb, ta=True, name="lru_dwgate")
    d_wrec = mm(xn, drec0, ta=True, name="lru_dwrec")
    dxn = mm(dgb, p["w_gate"], tb=True, name="lru_dxn1")
    dxn = mm(drec0, p["w_rec"], tb=True, acc=dxn, name="lru_dxn2")
    dx, dg = rms_bwd(x, g, [dxn], dy, "lru_drms")
    grads = dict(g=dg, w_gate=d_wgate, w_rec=d_wrec, conv_w=dcw, conv_b=dcb, w_a=dwa, b_a=dba, w_x=dwx, b_x=dbx,
                 lam=dlam, w_out=d_wout)
    return dx, grads


_NT = (((1,), (1,)), ((), ()))
_TN = (((0,), (0,)), ((), ()))


def _head_norm(x, g):
    lo = lax.broadcasted_iota(jnp.int32, x.shape, 1) < SB_HEAD_DIM
    x2 = x * x
    s_lo = jnp.sum(jnp.where(lo, x2, 0.0), axis=-1, keepdims=True)
    s_hi = jnp.sum(jnp.where(lo, 0.0, x2), axis=-1, keepdims=True)
    ms = jnp.where(lo, s_lo, s_hi) * (1.0 / SB_HEAD_DIM)
    return x * lax.rsqrt(ms + EPS) * g


def _log_sigmoid(z):
    return jnp.minimum(z, 0.0) - jnp.log(1.0 + jnp.exp(-jnp.abs(z)))


def _dot_split(x, m):
    hi = x.astype(BF16)
    lo = (x - hi.astype(F32)).astype(BF16)
    return jnp.dot(hi, m, preferred_element_type=F32) + jnp.dot(lo, m, preferred_element_type=F32)


def _tri(cmp, n):
    r = lax.broadcasted_iota(jnp.int32, (n, n), 0)
    c = lax.broadcasted_iota(jnp.int32, (n, n), 1)
    return cmp(r, c)


def sb_attn_fwd(q, k, v, qg, kg):
    bsz, seq, d = q.shape
    blk = min(ATT_BLOCK, seq)
    nq = seq // blk
    scale = 1.0 / math.sqrt(SB_HEAD_DIM)

    def body(q_ref, k_ref, v_ref, qg_ref, kg_ref, o_ref, tot_ref, qn, kn, vb):
        qn[...] = _head_norm(q_ref[...], qg_ref[...]).astype(BF16)
        kn[...] = _head_norm(k_ref[...], kg_ref[...]).astype(BF16)
        vb[...] = v_ref[...].astype(BF16)
        lane_lo = lax.broadcasted_iota(jnp.int32, (blk, LANES), 1) < SB_HEAD_DIM
        causal = _tri(lambda r, c: c < r, blk)
        upper = _tri(lambda r, c: r > c, blk).astype(BF16)

        for head in range(2):
            hm = lane_lo if head == 0 else jnp.logical_not(lane_lo)

            def q_block(qi, _, hm=hm, head=head):
                rows = pl.ds(pl.multiple_of(qi * blk, blk), blk)
                qb = jnp.where(hm, qn[rows, :], jnp.zeros((), BF16))

                def block(j, carry, acc, masked):
                    cols = pl.ds(pl.multiple_of(j * blk, blk), blk)
                    z = lax.dot_general(qb, kn[cols, :], _NT, preferred_element_type=F32) * scale
                    ls = _log_sigmoid(z)
                    lg = ls - z
                    if masked:
                        lg = jnp.where(causal, lg, 0.0)
                    att = jnp.exp(ls + carry + _dot_split(lg, upper))
                    if masked:
                        att = jnp.where(causal, att, 0.0)
                    acc = acc + jnp.dot(att.astype(BF16), vb[cols, :], preferred_element_type=F32)
                    return carry + jnp.sum(lg, axis=1, keepdims=True), acc

                carry, acc = block(qi, jnp.zeros((blk, 1), F32), jnp.zeros((blk, LANES), F32), True)
                carry, acc = lax.fori_loop(0, qi, lambda jj, c: block(qi - 1 - jj, c[0], c[1], False), (carry, acc))
                if head == 0:
                    o_ref[rows, :] = acc.astype(o_ref.dtype)
                    tot_ref[rows, :] = jnp.broadcast_to(carry, (blk, LANES))
                else:
                    o_ref[rows, :] = jnp.where(hm, acc.astype(o_ref.dtype), o_ref[rows, :])
                    tot_ref[rows, :] = jnp.where(hm, carry, tot_ref[rows, :])
                return 0

            lax.fori_loop(0, nq, q_block, 0)

    spec = pl.BlockSpec((None, seq, LANES), lambda b, c: (b, 0, c))
    gspec = pl.BlockSpec((1, LANES), lambda b, c: (0, 0))
    return pl.pallas_call(
        body, name="sb_attn_fwd", grid=(bsz, d // LANES), in_specs=[spec, spec, spec, gspec, gspec],
        out_specs=[spec, spec], out_shape=[jax.ShapeDtypeStruct((bsz, seq, d), BF16), jax.ShapeDtypeStruct((bsz, seq, d), F32)],
        scratch_shapes=[pltpu.VMEM((seq, LANES), BF16)] * 3,
        compiler_params=_params("parallel", "parallel"),
    )(q, k, v, qg, kg)


def sb_attn_bwd(q, k, v, qg, kg, tot, do):
    bsz, seq, d = q.shape
    blk = min(ATT_BLOCK, seq)
    nq = seq // blk
    scale = 1.0 / math.sqrt(SB_HEAD_DIM)

    def body(q_ref, k_ref, v_ref, qg_ref, kg_ref, tot_ref, do_ref, dq_ref, dk_ref, dv_ref, dqg_ref, dkg_ref,
             qn, kn, vb, dqn, dkn, dvv):
        qn[...] = _head_norm(q_ref[...], qg_ref[...]).astype(BF16)
        kn[...] = _head_norm(k_ref[...], kg_ref[...]).astype(BF16)
        vb[...] = v_ref[...].astype(BF16)
        dqn[...] = jnp.zeros_like(dqn)
        dkn[...] = jnp.zeros_like(dkn)
        dvv[...] = jnp.zeros_like(dvv)
        lane_lo = lax.broadcasted_iota(jnp.int32, (blk, LANES), 1) < SB_HEAD_DIM
        causal = _tri(lambda r, c: c < r, blk)
        upto = _tri(lambda r, c: r <= c, blk).astype(BF16)
        before = _tri(lambda r, c: r < c, blk).astype(BF16)

        for head in range(2):
            hm = lane_lo if head == 0 else jnp.logical_not(lane_lo)

            def q_block(qi, _, hm=hm):
                rows = pl.ds(pl.multiple_of(qi * blk, blk), blk)
                qb = jnp.where(hm, qn[rows, :], jnp.zeros((), BF16))
                dob = jnp.where(hm, do_ref[rows, :].astype(BF16), jnp.zeros((), BF16))
                total = jnp.max(jnp.where(hm, tot_ref[rows, :], -jnp.inf), axis=1, keepdims=True)

                def block(j, carry_l, carry_e, dq_acc, masked):
                    cols = pl.ds(pl.multiple_of(j * blk, blk), blk)
                    kb = kn[cols, :]
                    z = lax.dot_general(qb, kb, _NT, preferred_element_type=F32) * scale
                    ls = _log_sigmoid(z)
                    lg = ls - z
                    if masked:
                        lg = jnp.where(causal, lg, 0.0)
                    att = jnp.exp(ls + total - carry_l - _dot_split(lg, upto))
                    if masked:
                        att = jnp.where(causal, att, 0.0)
                    dvv[cols, :] += lax.dot_general(att.astype(BF16), dob, _TN, preferred_element_type=F32)
                    e = att * lax.dot_general(dob, vb[cols, :], _NT, preferred_element_type=F32)
                    dlg = carry_e + _dot_split(e, before)
                    if masked:
                        dlg = jnp.where(causal, dlg, 0.0)
                    beta = jnp.exp(ls)
                    dzb = ((e * (1.0 - beta) - dlg * beta) * scale).astype(BF16)
                    dq_acc = dq_acc + jnp.dot(dzb, kb, preferred_element_type=F32)
                    dkn[cols, :] += lax.dot_general(dzb, qb, _TN, preferred_element_type=F32)
                    return (carry_l + jnp.sum(lg, axis=1, keepdims=True), carry_e + jnp.sum(e, axis=1, keepdims=True),
                            dq_acc)

                zero = jnp.zeros((blk, 1), F32)
                carry = lax.fori_loop(0, qi, lambda j, c: block(j, c[0], c[1], c[2], False),
                                      (zero, zero, jnp.zeros((blk, LANES), F32)))
                _, _, dq_acc = block(qi, carry[0], carry[1], carry[2], True)
                dqn[rows, :] += jnp.where(hm, dq_acc, 0.0)
                return 0

            lax.fori_loop(0, nq, q_block, 0)

        def fold(x):
            return x + pltpu.roll(x, SB_HEAD_DIM, 1)

        _, q_vjp = jax.vjp(_head_norm, q_ref[...], qg_ref[...])
        dq, dqg = q_vjp(dqn[...])
        _, k_vjp = jax.vjp(_head_norm, k_ref[...], kg_ref[...])
        dk, dkg = k_vjp(dkn[...])
        dq_ref[...] = dq.astype(dq_ref.dtype)
        dk_ref[...] = dk.astype(dk_ref.dtype)
        dv_ref[...] = dvv[...].astype(dv_ref.dtype)
        first = jnp.logical_and(pl.program_id(0) == 0, pl.program_id(1) == 0)

        @pl.when(first)
        def _():
            dqg_ref[...] = fold(dqg)
            dkg_ref[...] = fold(dkg)

        @pl.when(jnp.logical_not(first))
        def _():
            dqg_ref[...] += fold(dqg)
            dkg_ref[...] += fold(dkg)

    spec = pl.BlockSpec((None, seq, LANES), lambda b, c: (b, 0, c))
    gspec = pl.BlockSpec((1, LANES), lambda b, c: (0, 0))
    act = jax.ShapeDtypeStruct((bsz, seq, d), BF16)
    gain = jax.ShapeDtypeStruct((1, LANES), F32)
    return pl.pallas_call(
        body, name="sb_attn_bwd", grid=(bsz, d // LANES), in_specs=[spec, spec, spec, gspec, gspec, spec, spec],
        out_specs=[spec, spec, spec, gspec, gspec], out_shape=[act, act, act, gain, gain],
        scratch_shapes=[pltpu.VMEM((seq, LANES), BF16)] * 3 + [pltpu.VMEM((seq, LANES), F32)] * 3,
        compiler_params=_params("arbitrary", "arbitrary"),
    )(q, k, v, qg, kg, tot, do)


def sb_fwd(x, g, p, bsz):
    t, d = x.shape
    seq = t // bsz
    xn = rms_fwd(x, g, BF16, "sb_rms")
    q = mm(xn, p["w_q"], name="sb_q").reshape(bsz, seq, d)
    k = mm(xn, p["w_k"], name="sb_k").reshape(bsz, seq, d)
    v = mm(xn, p["w_v"], name="sb_v").reshape(bsz, seq, d)
    qg = jnp.tile(p["q_g"], (1, 2))
    kg = jnp.tile(p["k_g"], (1, 2))
    o, tot = sb_attn_fwd(q, k, v, qg, kg)
    o = o.reshape(t, d)
    out = mm(o, p["w_o"], acc=x, name="sb_out")
    return out, (x, xn, q, k, v, o, tot)


def sb_bwd(dy, saved, g, p):
    x, xn, q, k, v, o, tot = saved
    bsz, seq, d = q.shape
    t = bsz * seq
    d_wo = mm(o, dy, ta=True, name="sb_dwo")
    do = mm(dy, p["w_o"], tb=True, out_dtype=BF16, name="sb_do").reshape(bsz, seq, d)
    qg = jnp.tile(p["q_g"], (1, 2))
    kg = jnp.tile(p["k_g"], (1, 2))
    dq, dk, dv, dqg, dkg = sb_attn_bwd(q, k, v, qg, kg, tot, do)
    dq, dk, dv = dq.reshape(t, d), dk.reshape(t, d), dv.reshape(t, d)
    d_wq = mm(xn, dq, ta=True, name="sb_dwq")
    d_wk = mm(xn, dk, ta=True, name="sb_dwk")
    d_wv = mm(xn, dv, ta=True, name="sb_dwv")
    dxn = mm(dq, p["w_q"], tb=True, name="sb_dxn1")
    dxn = mm(dk, p["w_k"], tb=True, acc=dxn, name="sb_dxn2")
    dxn = mm(dv, p["w_v"], tb=True, acc=dxn, name="sb_dxn3")
    dx, dg = rms_bwd(x, g, [dxn], dy, "sb_drms")
    grads = dict(g=dg, w_q=d_wq, w_k=d_wk, w_v=d_wv, w_o=d_wo, q_g=dqg[:, :SB_HEAD_DIM], k_g=dkg[:, :SB_HEAD_DIM])
    return dx, grads


def loss_head(y, target):
    d = y.shape[1]

    def fn(y, tgt):
        err = y - tgt
        return err * (1.0 / d), jnp.sum(err * err, axis=0, keepdims=True)

    dy, sq = rowwise(fn, [y, target], [], [(d, F32)], [(1, d)], name="loss_head")
    return sq, dy


def adamw(parts, w, m, v, name):
    n, r, c = parts.shape
    tr = _pick(r, max(16, (1 << 20) // (c * n)), 16 if parts.dtype == BF16 else 8)

    def body(p_ref, w_ref, m_ref, v_ref, g_ref, d_ref, nm_ref, nv_ref):
        g = p_ref[0].astype(F32)
        for i in range(1, n):
            g = g + p_ref[i].astype(F32)
        nm = ADAM_B1 * m_ref[...] + (1.0 - ADAM_B1) * g
        nv = ADAM_B2 * v_ref[...] + (1.0 - ADAM_B2) * (g * g)
        m_hat = nm / (1.0 - ADAM_B1 ** ADAM_STEP)
        v_hat = nv / (1.0 - ADAM_B2 ** ADAM_STEP)
        g_ref[...] = g
        d_ref[...] = -ADAM_LR * (m_hat / (jnp.sqrt(v_hat) + ADAM_EPS) + ADAM_WD * w_ref[...])
        nm_ref[...] = nm
        nv_ref[...] = nv

    spec = pl.BlockSpec((tr, c), lambda i: (i, 0))
    return pl.pallas_call(
        body, name=name, grid=(r // tr,), in_specs=[pl.BlockSpec((n, tr, c), lambda i: (0, i, 0)), spec, spec, spec],
        out_specs=[spec] * 4, out_shape=[jax.ShapeDtypeStruct((r, c), F32)] * 4,
        compiler_params=_params("parallel"),
    )(parts, w, m, v)


_HBM = pl.BlockSpec(memory_space=pltpu.HBM)


def _mesh_pos():
    return lax.axis_index("x"), lax.axis_index("y"), lax.axis_index("c")


def all_gather(shards, name):
    n = len(shards)

    def body(*refs):
        ins, outs = refs[:n], refs[n:2 * n]
        send_sems, recv_sems, local_sems = refs[2 * n:]
        x, y, c = _mesh_pos()
        me, sibling = (x, y, c), (x, y, 1 - c)
        chips = [(1 - x, y), (x, 1 - y), (1 - x, 1 - y)]

        def copy(a, k, block, to, src=None):
            px, py, pc = block
            dst = outs[a].at[4 * px + 2 * py + pc]
            return pltpu.make_async_remote_copy(
                src_ref=dst if src is None else src, dst_ref=dst, send_sem=send_sems.at[a, k],
                recv_sem=recv_sems.at[a, k], device_id=to, device_id_type=MESH)

        mine = [pltpu.make_async_copy(ins[a], outs[a].at[4 * x + 2 * y + c], local_sems.at[a]) for a in range(n)]
        for cp in mine:
            cp.start()
        first = []
        for a in range(n):
            first.append(copy(a, 0, me, sibling, src=ins[a]))
            first += [copy(a, 1 + j, me, (*chip, c), src=ins[a]) for j, chip in enumerate(chips)]
        for cp in first:
            cp.start()
        passed = []
        for a in range(n):
            for j, chip in enumerate(chips):
                copy(a, 1 + j, (*chip, c), me).wait_recv()
                cp = copy(a, 4 + j, (*chip, c), sibling)
                cp.start()
                passed.append(cp)
        for a in range(n):
            copy(a, 0, sibling, me).wait_recv()
            for j, chip in enumerate(chips):
                copy(a, 4 + j, (*chip, 1 - c), me).wait_recv()
        for cp in first + passed:
            cp.wait_send()
        for cp in mine:
            cp.wait()

    return pl.pallas_call(
        body, name=name, in_specs=[_HBM] * n, out_specs=[_HBM] * n,
        out_shape=[jax.ShapeDtypeStruct((N_DEV, *s.shape), s.dtype) for s in shards],
        scratch_shapes=[pltpu.SemaphoreType.DMA((n, 7)), pltpu.SemaphoreType.DMA((n, 7)), pltpu.SemaphoreType.DMA((n,))],
    )(*shards)


def exchange_cores(parts, name):
    n = len(parts)

    def body(*refs):
        ins, outs = refs[:n], refs[n:2 * n]
        send_sems, recv_sems = refs[2 * n:]
        x, y, c = _mesh_pos()
        copies = []
        for a in range(n):
            for p in range(4):
                copies.append(pltpu.make_async_remote_copy(
                    src_ref=ins[a].at[p, 1 - c], dst_ref=outs[a].at[p], send_sem=send_sems.at[a, p],
                    recv_sem=recv_sems.at[a, p], device_id=(x, y, 1 - c), device_id_type=MESH))
        for cp in copies:
            cp.start()
        for cp in copies:
            cp.wait()

    return pl.pallas_call(
        body, name=name, in_specs=[_HBM] * n, out_specs=[_HBM] * n,
        out_shape=[jax.ShapeDtypeStruct((4, *s.shape[2:]), s.dtype) for s in parts],
        scratch_shapes=[pltpu.SemaphoreType.DMA((n, 4)), pltpu.SemaphoreType.DMA((n, 4))],
    )(*parts)


def exchange_chips(parts, name):
    n = len(parts)

    def body(*refs):
        ins, outs = refs[:n], refs[n:2 * n]
        send_sems, recv_sems, local_sems = refs[2 * n:]
        x, y, c = _mesh_pos()
        mine = 2 * x + y
        local = [pltpu.make_async_copy(ins[a].at[mine], outs[a].at[mine], local_sems.at[a]) for a in range(n)]
        for cp in local:
            cp.start()
        copies = []
        for a in range(n):
            for r in range(1, 4):
                qx = 1 - x if r & 2 else x
                qy = 1 - y if r & 1 else y
                copies.append(pltpu.make_async_remote_copy(
                    src_ref=ins[a].at[2 * qx + qy], dst_ref=outs[a].at[mine], send_sem=send_sems.at[a, r - 1],
                    recv_sem=recv_sems.at[a, r - 1], device_id=(qx, qy, c), device_id_type=MESH))
        for cp in copies:
            cp.start()
        for cp in copies:
            cp.wait()
        for cp in local:
            cp.wait()

    return pl.pallas_call(
        body, name=name, in_specs=[_HBM] * n, out_specs=[_HBM] * n,
        out_shape=[jax.ShapeDtypeStruct(s.shape, s.dtype) for s in parts],
        scratch_shapes=[pltpu.SemaphoreType.DMA((n, 3)), pltpu.SemaphoreType.DMA((n, 3)), pltpu.SemaphoreType.DMA((n,))],
    )(*parts)


def add_own(core, parts, recv, out_dtype, name):
    _, _, r, c = parts.shape
    tr = _pick(r, max(16, (1 << 19) // c), 16)

    def body(core_ref, p_ref, r_ref, o_ref):
        o_ref[...] = (p_ref[...] + r_ref[...]).astype(out_dtype)

    grid_spec = pltpu.PrefetchScalarGridSpec(
        num_scalar_prefetch=1, grid=(4, r // tr),
        in_specs=[pl.BlockSpec((None, None, tr, c), lambda p, i, core_ref: (p, core_ref[0], i, 0)),
                  pl.BlockSpec((None, tr, c), lambda p, i, core_ref: (p, i, 0))],
        out_specs=pl.BlockSpec((None, tr, c), lambda p, i, core_ref: (p, i, 0)))
    return pl.pallas_call(
        body, name=name, grid_spec=grid_spec, out_shape=jax.ShapeDtypeStruct((4, r, c), out_dtype),
        compiler_params=_params("parallel", "parallel"),
    )(core, parts, recv)


PARAMS = {
    "norm_mix_g": ((4, 1024), None), "norm_ffn_g": ((4, 1024), None),
    "pool_w": ((1, 4, 256, 256), 2), "pool_b": ((1, 1024), None), "pool_scale": ((1, 1024), None),
    "s5_lam_re": ((1, 64, 64), None), "s5_lam_im": ((1, 64, 64), None), "s5_log_dt": ((1, 64), None),
    "s5_b_re": ((1, 64, 64, 16), None), "s5_b_im": ((1, 64, 64, 16), None),
    "s5_c_re": ((1, 64, 16, 64), None), "s5_c_im": ((1, 64, 16, 64), None),
    "s5_d": ((1, 1024), 1), "s5_w_out": ((1, 1024, 2048), 2), "s5_b_out": ((1, 2048), 1),
    "lru_w_in": ((1, 1024, 2048), 2), "lru_conv_w": ((1, 4, 1024), 2), "lru_conv_b": ((1, 1024), 1),
    "lru_w_a": ((1, 4, 256, 256), 2), "lru_b_a": ((1, 1024), 1), "lru_w_x": ((1, 4, 256, 256), 2),
    "lru_b_x": ((1, 1024), 1), "lru_lam": ((1, 1024), 1), "lru_w_out": ((1, 1024, 1024), 1),
    "sb_w_qkv": ((1, 1024, 3072), 2), "sb_q_g": ((1, 64), None), "sb_k_g": ((1, 64), None),
    "sb_w_o": ((1, 1024, 1024), 1),
    "ffn_w_in": ((4, 1024, 5632), 2), "ffn_conv_w": ((4, 3, 5632), 2), "ffn_conv_b": ((4, 5632), None),
    "ffn_w_out": ((4, 2816, 1024), 1),
}
NAMES = list(PARAMS)
BIG = ["s5_w_out", "lru_w_in", "lru_w_out", "sb_w_qkv", "sb_w_o", "ffn_w_in", "ffn_w_out"]
SMALL = [n for n in NAMES if PARAMS[n][1] is not None and n not in BIG]
REPL = [n for n in NAMES if PARAMS[n][1] is None]


def _local_shape(name):
    shape, ax = PARAMS[name]
    return tuple(s // N_DEV if i == ax else s for i, s in enumerate(shape))


def _to_natural(gathered, name):
    shape, ax = PARAMS[name]
    return jnp.moveaxis(gathered, 0, ax).reshape(shape)


def _to_shards(nat, name):
    shape, ax = PARAMS[name]
    split = shape[:ax] + (N_DEV, shape[ax] // N_DEV) + shape[ax + 1:]
    return jnp.moveaxis(nat.reshape(split), ax, 0)


def _rows2d(shape):
    return (math.prod(shape[:-1]), shape[-1])


def _pack(arrays, lead=()):
    flat = jnp.concatenate([a.reshape(*lead, -1) for a in arrays], axis=-1)
    size = flat.shape[-1]
    rows = -(-size // (8 * LANES)) * 8
    flat = jnp.pad(flat, [(0, 0)] * len(lead) + [(0, rows * LANES - size)])
    return flat.reshape(*lead, rows, LANES)


def _unpack(packed, shapes, lead=()):
    flat = packed.reshape(*lead, -1)
    out, off = [], 0
    for s in shapes:
        size = math.prod(s)
        out.append(flat[..., off:off + size].reshape(*lead, *s))
        off += size
    return out


def kernel(x, norm_mix_g, norm_ffn_g, pool_w, pool_b, pool_scale, s5_lam_re, s5_lam_im, s5_log_dt, s5_b_re, s5_b_im, s5_c_re, s5_c_im, s5_d, s5_w_out, s5_b_out, lru_w_in, lru_conv_w, lru_conv_b, lru_w_a, lru_b_a, lru_w_x, lru_b_x, lru_lam, lru_w_out, sb_w_qkv, sb_q_g, sb_k_g, sb_w_o, ffn_w_in, ffn_conv_w, ffn_conv_b, ffn_w_out, loss_target, m_norm_mix_g, m_norm_ffn_g, m_pool_w, m_pool_b, m_pool_scale, m_s5_lam_re, m_s5_lam_im, m_s5_log_dt, m_s5_b_re, m_s5_b_im, m_s5_c_re, m_s5_c_im, m_s5_d, m_s5_w_out, m_s5_b_out, m_lru_w_in, m_lru_conv_w, m_lru_conv_b, m_lru_w_a, m_lru_b_a, m_lru_w_x, m_lru_b_x, m_lru_lam, m_lru_w_out, m_sb_w_qkv, m_sb_q_g, m_sb_k_g, m_sb_w_o, m_ffn_w_in, m_ffn_conv_w, m_ffn_conv_b, m_ffn_w_out, v_norm_mix_g, v_norm_ffn_g, v_pool_w, v_pool_b, v_pool_scale, v_s5_lam_re, v_s5_lam_im, v_s5_log_dt, v_s5_b_re, v_s5_b_im, v_s5_c_re, v_s5_c_im, v_s5_d, v_s5_w_out, v_s5_b_out, v_lru_w_in, v_lru_conv_w, v_lru_conv_b, v_lru_w_a, v_lru_b_a, v_lru_w_x, v_lru_b_x, v_lru_lam, v_lru_w_out, v_sb_w_qkv, v_sb_q_g, v_sb_k_g, v_sb_w_o, v_ffn_w_in, v_ffn_conv_w, v_ffn_conv_b, v_ffn_w_out):
    w = dict(zip(NAMES, (norm_mix_g, norm_ffn_g, pool_w, pool_b, pool_scale, s5_lam_re, s5_lam_im, s5_log_dt, s5_b_re, s5_b_im, s5_c_re, s5_c_im, s5_d, s5_w_out, s5_b_out, lru_w_in, lru_conv_w, lru_conv_b, lru_w_a, lru_b_a, lru_w_x, lru_b_x, lru_lam, lru_w_out, sb_w_qkv, sb_q_g, sb_k_g, sb_w_o, ffn_w_in, ffn_conv_w, ffn_conv_b, ffn_w_out)))
    m = dict(zip(NAMES, (m_norm_mix_g, m_norm_ffn_g, m_pool_w, m_pool_b, m_pool_scale, m_s5_lam_re, m_s5_lam_im, m_s5_log_dt, m_s5_b_re, m_s5_b_im, m_s5_c_re, m_s5_c_im, m_s5_d, m_s5_w_out, m_s5_b_out, m_lru_w_in, m_lru_conv_w, m_lru_conv_b, m_lru_w_a, m_lru_b_a, m_lru_w_x, m_lru_b_x, m_lru_lam, m_lru_w_out, m_sb_w_qkv, m_sb_q_g, m_sb_k_g, m_sb_w_o, m_ffn_w_in, m_ffn_conv_w, m_ffn_conv_b, m_ffn_w_out)))
    v = dict(zip(NAMES, (v_norm_mix_g, v_norm_ffn_g, v_pool_w, v_pool_b, v_pool_scale, v_s5_lam_re, v_s5_lam_im, v_s5_log_dt, v_s5_b_re, v_s5_b_im, v_s5_c_re, v_s5_c_im, v_s5_d, v_s5_w_out, v_s5_b_out, v_lru_w_in, v_lru_conv_w, v_lru_conv_b, v_lru_w_a, v_lru_b_a, v_lru_w_x, v_lru_b_x, v_lru_lam, v_lru_w_out, v_sb_w_qkv, v_sb_q_g, v_sb_k_g, v_sb_w_o, v_ffn_w_in, v_ffn_conv_w, v_ffn_conv_b, v_ffn_w_out)))
    return train_step(x, loss_target, w, m, v)


def train_step(x, loss_target, w, m, v):
    bsz, seq, d = x.shape
    t = bsz * seq
    n_layers = PARAMS["norm_mix_g"][0][0]

    small_shapes = [_local_shape(n) for n in SMALL]
    gathered = all_gather([w[n].astype(BF16) for n in BIG] + [_pack([w[n] for n in SMALL])], "gather_weights")
    nat = {n: _to_natural(g, n) for n, g in zip(BIG, gathered[:-1])}
    for n, g in zip(SMALL, _unpack(gathered[-1], small_shapes, lead=(N_DEV,))):
        nat[n] = _to_natural(g, n)
    for n in REPL:
        nat[n] = w[n]

    hid = FFN_HIDDEN
    pool_p = (nat["pool_w"][0].astype(BF16), nat["pool_b"], nat["pool_scale"])
    s5_p = dict(lam_re=nat["s5_lam_re"][0], lam_im=nat["s5_lam_im"][0], log_dt=nat["s5_log_dt"][0],
                b_re=nat["s5_b_re"][0], b_im=nat["s5_b_im"][0], c_re=nat["s5_c_re"][0], c_im=nat["s5_c_im"][0],
                d=nat["s5_d"], w_val=nat["s5_w_out"][0, :, :d], w_gate=nat["s5_w_out"][0, :, d:], b_out=nat["s5_b_out"])
    lru_p = dict(w_gate=nat["lru_w_in"][0, :, :d], w_rec=nat["lru_w_in"][0, :, d:], conv_w=nat["lru_conv_w"][0],
                 conv_b=nat["lru_conv_b"], b_a=nat["lru_b_a"], b_x=nat["lru_b_x"], lam=nat["lru_lam"],
                 w_a=nat["lru_w_a"][0].astype(BF16), w_x=nat["lru_w_x"][0].astype(BF16), w_out=nat["lru_w_out"][0])
    sb_p = dict(w_q=nat["sb_w_qkv"][0, :, :d], w_k=nat["sb_w_qkv"][0, :, d:2 * d], w_v=nat["sb_w_qkv"][0, :, 2 * d:],
                w_o=nat["sb_w_o"][0], q_g=nat["sb_q_g"], k_g=nat["sb_k_g"])

    def ffn_p(li):
        return (nat["ffn_w_in"][li, :, :hid], nat["ffn_w_in"][li, :, hid:], nat["ffn_conv_w"][li],
                nat["ffn_conv_b"][li:li + 1], nat["ffn_w_out"][li])

    def gain(name, li):
        return nat[name][li:li + 1]

    h = x.reshape(t, d)
    h, pool_saved = pool_fwd(h, gain("norm_mix_g", 0), *pool_p, bsz)
    h, ffn0 = ffn_fwd(h, gain("norm_ffn_g", 0), *ffn_p(0), bsz, 0)
    h, s5_saved = s5_fwd(h, gain("norm_mix_g", 1), s5_p, bsz)
    h, ffn1 = ffn_fwd(h, gain("norm_ffn_g", 1), *ffn_p(1), bsz, 1)
    h, lru_saved = lru_fwd(h, gain("norm_mix_g", 2), lru_p, bsz)
    h, ffn2 = ffn_fwd(h, gain("norm_ffn_g", 2), *ffn_p(2), bsz, 2)
    h, sb_saved = sb_fwd(h, gain("norm_mix_g", 3), sb_p, bsz)
    h, ffn3 = ffn_fwd(h, gain("norm_ffn_g", 3), *ffn_p(3), bsz, 3)
    sq, dh = loss_head(h, loss_target.reshape(t, d))
    loss = lax.psum(0.5 * jnp.sum(sq) / d, ("x", "y", "c"))

    fg = [None] * n_layers
    dh, fg[3] = ffn_bwd(dh, ffn3, gain("norm_ffn_g", 3), *ffn_p(3), 3)
    dh, sb_g = sb_bwd(dh, sb_saved, gain("norm_mix_g", 3), sb_p)
    dh, fg[2] = ffn_bwd(dh, ffn2, gain("norm_ffn_g", 2), *ffn_p(2), 2)
    dh, lru_g = lru_bwd(dh, lru_saved, gain("norm_mix_g", 2), lru_p)
    dh, fg[1] = ffn_bwd(dh, ffn1, gain("norm_ffn_g", 1), *ffn_p(1), 1)
    dh, s5_g = s5_bwd(dh, s5_saved, gain("norm_mix_g", 1), s5_p)
    dh, fg[0] = ffn_bwd(dh, ffn0, gain("norm_ffn_g", 0), *ffn_p(0), 0)
    dh, pool_g = pool_bwd(dh, pool_saved, gain("norm_mix_g", 0), *pool_p)
    grad_x = dh.reshape(bsz, seq, d)

    part = {
        "norm_mix_g": jnp.concatenate([pool_g[0], s5_g["g"], lru_g["g"], sb_g["g"]], axis=0),
        "norm_ffn_g": jnp.concatenate([g[0] for g in fg], axis=0),
        "pool_w": pool_g[1][None], "pool_b": pool_g[2], "pool_scale": pool_g[3],
        "s5_lam_re": s5_g["lam_re"][None], "s5_lam_im": s5_g["lam_im"][None], "s5_log_dt": s5_g["log_dt"][None],
        "s5_b_re": s5_g["b_re"][None], "s5_b_im": s5_g["b_im"][None], "s5_c_re": s5_g["c_re"][None],
        "s5_c_im": s5_g["c_im"][None], "s5_d": s5_g["d"],
        "s5_w_out": jnp.concatenate([s5_g["w_val"], s5_g["w_gate"]], axis=1)[None], "s5_b_out": s5_g["b_out"],
        "lru_w_in": jnp.concatenate([lru_g["w_gate"], lru_g["w_rec"]], axis=1)[None], "lru_conv_w": lru_g["conv_w"][None],
        "lru_conv_b": lru_g["conv_b"], "lru_w_a": lru_g["w_a"][None], "lru_b_a": lru_g["b_a"],
        "lru_w_x": lru_g["w_x"][None], "lru_b_x": lru_g["b_x"], "lru_lam": lru_g["lam"], "lru_w_out": lru_g["w_out"][None],
        "sb_w_qkv": jnp.concatenate([sb_g["w_q"], sb_g["w_k"], sb_g["w_v"]], axis=1)[None],
        "sb_q_g": sb_g["q_g"], "sb_k_g": sb_g["k_g"], "sb_w_o": sb_g["w_o"][None],
        "ffn_w_in": jnp.stack([jnp.concatenate([g[1], g[2]], axis=1) for g in fg]),
        "ffn_conv_w": jnp.stack([g[3] for g in fg]), "ffn_conv_b": jnp.concatenate([g[4] for g in fg], axis=0),
        "ffn_w_out": jnp.stack([g[5] for g in fg]),
    }

    core = lax.axis_index("c").astype(jnp.int32).reshape(1)
    names2d = BIG + ["small"]
    shapes2d = [_rows2d(_local_shape(n)) for n in BIG]
    small_part = _pack([_to_shards(part[n], n) for n in SMALL], lead=(N_DEV,))
    shapes2d.append(small_part.shape[1:])
    by_dest = [_to_shards(part[n], n).reshape(4, 2, *s) for n, s in zip(BIG, shapes2d)] + [small_part.reshape(4, 2, *shapes2d[-1])]
    from_core = exchange_cores(by_dest, "reduce_cores")
    chip_sums = [add_own(core, p, r, F32 if n == "small" else BF16, f"add_{n}")
                 for n, p, r in zip(names2d, by_dest, from_core)]
    from_chips = exchange_chips(chip_sums, "reduce_chips")

    def local2d(tree, n):
        return tree[n].reshape(_rows2d(_local_shape(n)))

    out = {}
    for n, parts in zip(BIG, from_chips[:-1]):
        res = adamw(parts, local2d(w, n), local2d(m, n), local2d(v, n), f"adamw_{n}")
        out[n] = [r.reshape(_local_shape(n)) for r in res]
    small_res = adamw(from_chips[-1], *[_pack([tree[n] for n in SMALL]) for tree in (w, m, v)], "adamw_small")
    for i, res in enumerate(small_res):
        for n, r in zip(SMALL, _unpack(res, small_shapes)):
            out.setdefault(n, [None] * 4)[i] = r

    repl_shapes = [PARAMS[n][0] for n in REPL]
    repl_parts = all_gather([_pack([part[n] for n in REPL])], "gather_grads")[0]
    repl_res = adamw(repl_parts, *[_pack([tree[n] for n in REPL]) for tree in (w, m, v)], "adamw_replicated")
    for i, res in enumerate(repl_res):
        for n, r in zip(REPL, _unpack(res, repl_shapes)):
            out.setdefault(n, [None] * 4)[i] = r

    return (loss, grad_x, *[out[n][0] for n in NAMES], *[out[n][1] for n in NAMES], *[out[n][2] for n in NAMES],
            *[out[n][3] for n in NAMES])
```

```python
import functools
import math

import jax
import jax.numpy as jnp
from jax import lax
from jax.experimental import pallas as pl
from jax.experimental.pallas import tpu as pltpu

F32 = jnp.float32
BF16 = jnp.bfloat16
MESH = pl.DeviceIdType.MESH

N_DEV = 8
D_MODEL = 1024
EPS = 1e-6
POOL_GROUP = 256
S5_GROUPS, S5_GROUP, S5_STATE = 64, 16, 64
LRU_BLOCK = 256
LRU_C = 8.0
SB_HEAD_DIM = 64
ATT_BLOCK = 256
FFN_HIDDEN = 2816
ADAM_LR, ADAM_B1, ADAM_B2, ADAM_EPS, ADAM_WD, ADAM_STEP = 0.001, 0.9, 0.999, 1e-08, 0.01, 10
LANES = 128
VMEM_LIMIT = 56 * 1024 * 1024


def _pick(n, target, mult=LANES):
    best = None
    for d in range(mult, min(n, target) + 1, mult):
        if n % d == 0:
            best = d
    return best or n


def _params(*sem):
    return pltpu.CompilerParams(dimension_semantics=sem, vmem_limit_bytes=VMEM_LIMIT)


def mm(a, b, *, ta=False, tb=False, acc=None, out_dtype=F32, name):
    if ta:
        kdim, m = a.shape
    else:
        m, kdim = a.shape
    if tb:
        n, k2 = b.shape
    else:
        k2, n = b.shape
    assert kdim == k2, (a.shape, b.shape)
    tm = _pick(m, 1408 if ta else 512)
    tn = _pick(n, 1408)
    tk = _pick(kdim, 1024 if ta else 1408)
    nk = kdim // tk
    a_spec = pl.BlockSpec((tk, tm), lambda i, j, k: (k, i)) if ta else pl.BlockSpec((tm, tk), lambda i, j, k: (i, k))
    b_spec = pl.BlockSpec((tn, tk), lambda i, j, k: (j, k)) if tb else pl.BlockSpec((tk, tn), lambda i, j, k: (k, j))
    o_spec = pl.BlockSpec((tm, tn), lambda i, j, k: (i, j))
    dims = (((0,) if ta else (1,), (1,) if tb else (0,)), ((), ()))
    has_acc = acc is not None

    def body(*refs):
        if has_acc:
            a_ref, b_ref, c_ref, o_ref, acc_ref = refs
        else:
            a_ref, b_ref, o_ref, acc_ref = refs
        k = pl.program_id(2)

        @pl.when(k == 0)
        def _():
            acc_ref[...] = c_ref[...].astype(F32) if has_acc else jnp.zeros_like(acc_ref)

        acc_ref[...] += lax.dot_general(a_ref[...].astype(BF16), b_ref[...].astype(BF16), dims,
                                        preferred_element_type=F32)

        @pl.when(k == nk - 1)
        def _():
            o_ref[...] = acc_ref[...].astype(out_dtype)

    ins = [a, b] + ([acc] if has_acc else [])
    specs = [a_spec, b_spec] + ([o_spec] if has_acc else [])
    return pl.pallas_call(
        body, name=name, grid=(m // tm, n // tn, nk), in_specs=specs, out_specs=o_spec,
        out_shape=jax.ShapeDtypeStruct((m, n), out_dtype), scratch_shapes=[pltpu.VMEM((tm, tn), F32)],
        compiler_params=_params("parallel", "parallel", "arbitrary"),
    )(*ins)


def rowwise(fn, tiled, bcast, outs, reds=(), *, tm=256, name):
    t = tiled[0].shape[0]
    tm = min(tm, t)
    assert t % tm == 0
    n_t, n_b, n_o, n_r = len(tiled), len(bcast), len(outs), len(reds)

    def body(*refs):
        vals = [r[...] for r in refs[:n_t + n_b]]
        res = fn(*vals)
        res = res if isinstance(res, tuple) else (res,)
        o_refs = refs[n_t + n_b:n_t + n_b + n_o]
        r_refs = refs[n_t + n_b + n_o:]
        for o_ref, v in zip(o_refs, res[:n_o]):
            o_ref[...] = v.astype(o_ref.dtype)
        first = pl.program_id(0) == 0
        for r_ref, v in zip(r_refs, res[n_o:]):
            @pl.when(first)
            def _(r_ref=r_ref, v=v):
                r_ref[...] = v.astype(F32)

            @pl.when(jnp.logical_not(first))
            def _(r_ref=r_ref, v=v):
                r_ref[...] += v.astype(F32)

    in_specs = [pl.BlockSpec((tm, a.shape[1]), lambda i: (i, 0)) for a in tiled]
    in_specs += [pl.BlockSpec(a.shape, lambda i, nd=a.ndim: (0,) * nd) for a in bcast]
    out_specs = [pl.BlockSpec((tm, c), lambda i: (i, 0)) for c, _ in outs]
    out_specs += [pl.BlockSpec(s, lambda i: (0, 0)) for s in reds]
    out_shape = [jax.ShapeDtypeStruct((t, c), dt) for c, dt in outs]
    out_shape += [jax.ShapeDtypeStruct(s, F32) for s in reds]
    res = pl.pallas_call(
        body, name=name, grid=(t // tm,), in_specs=in_specs, out_specs=out_specs, out_shape=out_shape,
        compiler_params=_params("arbitrary"),
    )(*tiled, *bcast)
    return res


def seqwise(fn, seqs, chans, outs, reds=(), *, ct, name, fulls=()):
    def split(x):
        return x if isinstance(x, tuple) else (x, 0)

    seqs = [split(s) for s in seqs]
    chans = [split(s) for s in chans]
    bsz, seq = seqs[0][0].shape[:2]
    n_c = outs[0][0] // ct if outs else reds[0][1] // ct
    n_s, n_ch, n_f, n_o, n_r = len(seqs), len(chans), len(fulls), len(outs), len(reds)

    def body(*refs):
        c = pl.program_id(0)
        vals = [r[...] for r in refs[:n_s + n_ch + n_f]]
        res = fn(c, *vals)
        res = res if isinstance(res, tuple) else (res,)
        o_refs = refs[n_s + n_ch + n_f:n_s + n_ch + n_f + n_o]
        r_refs = refs[n_s + n_ch + n_f + n_o:]
        for o_ref, v in zip(o_refs, res[:n_o]):
            o_ref[...] = v.astype(o_ref.dtype)
        first = pl.program_id(1) == 0
        for r_ref, v in zip(r_refs, res[n_o:]):
            @pl.when(first)
            def _(r_ref=r_ref, v=v):
                r_ref[...] = v.astype(F32)

            @pl.when(jnp.logical_not(first))
            def _(r_ref=r_ref, v=v):
                r_ref[...] += v.astype(F32)

    in_specs = [pl.BlockSpec((None, seq, ct), lambda c, b, off=off: (b, 0, c + off)) for _, off in seqs]
    in_specs += [pl.BlockSpec((a.shape[0], ct), lambda c, b, off=off: (0, c + off)) for a, off in chans]
    in_specs += [pl.BlockSpec(a.shape, lambda c, b, nd=a.ndim: (0,) * nd) for a in fulls]
    out_specs = [pl.BlockSpec((None, seq, ct), lambda c, b: (b, 0, c)) for _ in outs]
    out_specs += [pl.BlockSpec((r, ct), lambda c, b: (0, c)) for r, _ in reds]
    out_shape = [jax.ShapeDtypeStruct((bsz, seq, cc), dt) for cc, dt in outs]
    out_shape += [jax.ShapeDtypeStruct(s, F32) for s in reds]
    return pl.pallas_call(
        body, name=name, grid=(n_c, bsz), in_specs=in_specs, out_specs=out_specs, out_shape=out_shape,
        compiler_params=_params("arbitrary", "arbitrary"),
    )(*[a for a, _ in seqs], *[a for a, _ in chans], *fulls)


def _rows(x):
    return lax.broadcasted_iota(jnp.int32, x.shape, 0)


def shift_down(x, d, fill=0.0):
    if d == 0:
        return x
    return jnp.where(_rows(x) >= d, pltpu.roll(x, d, 0), fill)


def shift_up(x, d, fill=0.0):
    if d == 0:
        return x
    s = x.shape[0]
    return jnp.where(_rows(x) < s - d, pltpu.roll(x, s - d, 0), fill)


def conv_fwd(x, w, b):
    kw = w.shape[0]
    y = b + w[kw - 1:kw] * x
    for d in range(1, kw):
        y = y + w[kw - 1 - d:kw - d] * shift_down(x, d)
    return y


def conv_bwd(x, w, dy):
    kw = w.shape[0]
    dx = w[kw - 1:kw] * dy
    dws = [jnp.sum(dy * x, axis=0, keepdims=True)]
    for d in range(1, kw):
        dx = dx + w[kw - 1 - d:kw - d] * shift_up(dy, d)
        dws.append(jnp.sum(dy * shift_down(x, d), axis=0, keepdims=True))
    dw = jnp.concatenate(dws[::-1], axis=0)
    return dx, dw, jnp.sum(dy, axis=0, keepdims=True)


def sigmoid(x):
    return 1.0 / (1.0 + jnp.exp(-x))


def rms_fwd(x, g, out_dtype, name):
    def fn(x, g):
        return x * lax.rsqrt(jnp.mean(x * x, axis=-1, keepdims=True) + EPS) * g

    return rowwise(fn, [x], [g], [(x.shape[1], out_dtype)], name=name)[0]


def rms_bwd(x, g, dh_parts, dres, name):
    n_p = len(dh_parts)

    def fn(x, dres, *rest):
        dh = rest[0].astype(F32)
        for p in rest[1:n_p]:
            dh = dh + p.astype(F32)
        g = rest[n_p]
        r = lax.rsqrt(jnp.mean(x * x, axis=-1, keepdims=True) + EPS)
        xh = x * r
        dxh = dh * g
        dx = r * (dxh - xh * jnp.mean(dxh * xh, axis=-1, keepdims=True))
        return dres + dx, jnp.sum(dh * xh, axis=0, keepdims=True)

    return rowwise(fn, [x, dres, *dh_parts], [g], [(x.shape[1], F32)], [(1, x.shape[1])], name=name)


def ffn_fwd(x, g, w_val, w_gate, cw, cb, w_out, bsz, li):
    t, d = x.shape
    hid = w_val.shape[1]
    xn = rms_fwd(x, g, BF16, f"ffn{li}_rms")
    hv = mm(xn, w_val, name=f"ffn{li}_hv").reshape(bsz, t // bsz, hid)
    hg = mm(xn, w_gate, name=f"ffn{li}_hg").reshape(bsz, t // bsz, hid)
    ct = _pick(hid, 256)
    off = hid // ct

    def fn(c, hv, hg, wv, wg, bv, bg):
        val = conv_fwd(hv, wv, bv)
        gate = conv_fwd(hg, wg, bg)
        return gate * sigmoid(gate) * val

    act = seqwise(fn, [hv, hg], [cw, (cw, off), cb, (cb, off)], [(hid, BF16)], ct=ct, name=f"ffn{li}_gate")[0]
    act = act.reshape(t, hid)
    y = mm(act, w_out, acc=x, name=f"ffn{li}_out")
    return y, (x, xn, hv, hg, act)


def ffn_bwd(dy, saved, g, w_val, w_gate, cw, cb, w_out, li):
    x, xn, hv, hg, act = saved
    bsz, seq, hid = hv.shape
    t = bsz * seq
    d_wout = mm(act, dy, ta=True, name=f"ffn{li}_dwout")
    dact = mm(dy, w_out, tb=True, name=f"ffn{li}_dact").reshape(bsz, seq, hid)
    ct = _pick(hid, 256)
    off = hid // ct

    def fn(c, hv, hg, dact, wv, wg, bv, bg):
        val = conv_fwd(hv, wv, bv)
        gate = conv_fwd(hg, wg, bg)
        sg = sigmoid(gate)
        silu = gate * sg
        dval = dact * silu
        dgate = dact * val * (sg + silu * (1.0 - sg))
        dhv, dwv, dbv = conv_bwd(hv, wv, dval)
        dhg, dwg, dbg = conv_bwd(hg, wg, dgate)
        return dhv, dhg, dwv, dwg, dbv, dbg

    kw = cw.shape[0]
    dhv, dhg, dwv, dwg, dbv, dbg = seqwise(
        fn, [hv, hg, dact], [cw, (cw, off), cb, (cb, off)], [(hid, BF16), (hid, BF16)],
        [(kw, hid), (kw, hid), (1, hid), (1, hid)], ct=ct, name=f"ffn{li}_dgate")
    dhv = dhv.reshape(t, hid)
    dhg = dhg.reshape(t, hid)
    d_wv = mm(xn, dhv, ta=True, name=f"ffn{li}_dwv")
    d_wg = mm(xn, dhg, ta=True, name=f"ffn{li}_dwg")
    dxn = mm(dhv, w_val, tb=True, name=f"ffn{li}_dxn1")
    dxn = mm(dhg, w_gate, tb=True, acc=dxn, name=f"ffn{li}_dxn2")
    dx, dg = rms_bwd(x, g, [dxn], dy, f"ffn{li}_drms")
    d_cw = jnp.concatenate([dwv, dwg], axis=1)
    d_cb = jnp.concatenate([dbv, dbg], axis=1)
    return dx, (dg, d_wv, d_wg, d_cw, d_cb, d_wout)


def _window_sum(x, c, shift):
    s2 = x + shift(x, 1)
    s4 = s2 + shift(s2, 2)
    s8 = s4 + shift(s4, 4)
    s16 = s8 + shift(s8, 8)
    return jnp.where(c == 0, s2, jnp.where(c == 1, s4, jnp.where(c == 2, s8, s16)))


def _pool_inv_count(x, c):
    w = jnp.left_shift(2, c)
    return 1.0 / jnp.minimum(_rows(x) + 1, w).astype(F32)


def pool_fwd(x, g, w, b, scale, bsz):
    t, d = x.shape
    h = rms_fwd(x, g, F32, "pool_rms").reshape(bsz, t // bsz, d)

    def fn(c, h, x, b, scale, w):
        dd = _window_sum(h, c, shift_down) * _pool_inv_count(h, c) - h
        y = jnp.dot(dd.astype(BF16), w[0], preferred_element_type=F32) + b
        return x + scale * y

    wspec = pl.BlockSpec((1, POOL_GROUP, POOL_GROUP), lambda c, bb: (c, 0, 0))
    y = _seqwise_w(fn, [h, x.reshape(bsz, t // bsz, d)], [b, scale], [(w, wspec)], [(d, F32)], [], ct=POOL_GROUP,
                   name="pool_fwd")[0]
    return y.reshape(t, d), (x, h)


def pool_bwd(dy, saved, g, w, b, scale):
    x, h = saved
    bsz, seq, d = h.shape
    t = bsz * seq

    def fn(c, h, dy, b, scale, w):
        inv = _pool_inv_count(h, c)
        dd = _window_sum(h, c, shift_down) * inv - h
        ddb = dd.astype(BF16)
        y = jnp.dot(ddb, w[0], preferred_element_type=F32) + b
        dscale = jnp.sum(dy * y, axis=0, keepdims=True)
        dyy = dy * scale
        db = jnp.sum(dyy, axis=0, keepdims=True)
        dyb = dyy.astype(BF16)
        dw = lax.dot_general(ddb, dyb, (((0,), (0,)), ((), ())), preferred_element_type=F32)
        ddd = lax.dot_general(dyb, w[0], (((1,), (1,)), ((), ())), preferred_element_type=F32)
        dh = _window_sum(ddd * inv, c, shift_up) - ddd
        return dh, db, dscale, dw[None]

    wspec = pl.BlockSpec((1, POOL_GROUP, POOL_GROUP), lambda c, bb: (c, 0, 0))
    dh, db, dscale, dw = _seqwise_w(
        fn, [h, dy.reshape(bsz, seq, d)], [b, scale], [(w, wspec)], [(d, F32)], [(1, d), (1, d)], ct=POOL_GROUP,
        name="pool_bwd", wreds=[((4, POOL_GROUP, POOL_GROUP), wspec)])
    dx, dg = rms_bwd(x, g, [dh.reshape(t, d)], dy, "pool_drms")
    return dx, (dg, dw, db, dscale)


def _seqwise_w(fn, seqs, chans, blocked, outs, reds, *, ct, name, wreds=(), phased=False):
    bsz, seq = seqs[0].shape[:2]
    n_c = seqs[0].shape[2] // ct
    n_s = len(seqs)
    n_in = n_s + len(chans) + len(blocked)
    n_o, n_r = len(outs), len(reds) + len(wreds)

    def body(*refs):
        c = pl.program_id(0)
        seq_vals = [_load_phases(r) if phased else r[...] for r in refs[:n_s]]
        res = fn(c, *seq_vals, *[r[...] for r in refs[n_s:n_in]])
        res = res if isinstance(res, tuple) else (res,)
        for o_ref, v in zip(refs[n_in:n_in + n_o], res[:n_o]):
            if phased:
                _store_phases(o_ref, v)
            else:
                o_ref[...] = v.astype(o_ref.dtype)
        first = pl.program_id(1) == 0
        for r_ref, v in zip(refs[n_in + n_o:], res[n_o:]):
            @pl.when(first)
            def _(r_ref=r_ref, v=v):
                r_ref[...] = v.astype(F32)

            @pl.when(jnp.logical_not(first))
            def _(r_ref=r_ref, v=v):
                r_ref[...] += v.astype(F32)

    in_specs = [pl.BlockSpec((None, seq, ct), lambda c, b: (b, 0, c)) for _ in seqs]
    in_specs += [pl.BlockSpec((a.shape[0], ct), lambda c, b: (0, c)) for a in chans]
    in_specs += [spec for _, spec in blocked]
    out_specs = [pl.BlockSpec((None, seq, ct), lambda c, b: (b, 0, c)) for _ in outs]
    out_specs += [pl.BlockSpec((r, ct), lambda c, b: (0, c)) for r, _ in reds]
    out_specs += [spec for _, spec in wreds]
    out_shape = [jax.ShapeDtypeStruct((bsz, seq, cc), dt) for cc, dt in outs]
    out_shape += [jax.ShapeDtypeStruct(s, F32) for s in reds]
    out_shape += [jax.ShapeDtypeStruct(s, F32) for s, _ in wreds]
    return pl.pallas_call(
        body, name=name, grid=(n_c, bsz), in_specs=in_specs, out_specs=out_specs, out_shape=out_shape,
        compiler_params=_params("arbitrary", "arbitrary"),
    )(*seqs, *chans, *[a for a, _ in blocked])


S5_TILE_GROUPS = LANES // S5_GROUP
S5_TILES = S5_GROUPS // S5_TILE_GROUPS
S5_TILE_STATE = S5_TILE_GROUPS * S5_STATE


def _s5_discretize(lam_re, lam_im, log_dt, b_re, b_im):
    lr = jnp.minimum(lam_re, -1e-4)
    dt = jnp.exp(log_dt)
    er = jnp.exp(lr * dt)
    ar = er * jnp.cos(lam_im * dt)
    ai = er * jnp.sin(lam_im * dt)
    den = lr * lr + lam_im * lam_im
    cr = ((ar - 1.0) * lr + ai * lam_im) / den
    ci = (ai * lr - (ar - 1.0) * lam_im) / den
    return ar, ai, cr * b_re - ci * b_im, cr * b_im + ci * b_re


def _whole(a):
    return pl.BlockSpec(a.shape, lambda *_: (0,) * a.ndim)


def s5_prep(lam_re, lam_im, log_dt, b_re, b_im):
    def body(lr, li, ld, br, bi, ar_o, ai_o, bbr_o, bbi_o):
        ar, ai, bbr, bbi = _s5_discretize(lr[...], li[...], ld[...], br[...], bi[...])
        ar_o[...] = ar
        ai_o[...] = ai
        bbr_o[...] = bbr
        bbi_o[...] = bbi

    ins = [lam_re, lam_im, log_dt, b_re, b_im]
    outs = [lam_re, lam_im, b_re, b_im]
    return pl.pallas_call(
        body, name="s5_prep", in_specs=[_whole(a) for a in ins], out_specs=[_whole(a) for a in outs],
        out_shape=[jax.ShapeDtypeStruct(a.shape, F32) for a in outs],
        compiler_params=pltpu.CompilerParams(vmem_limit_bytes=VMEM_LIMIT),
    )(*ins)


def s5_prep_bwd(lam_re, lam_im, log_dt, b_re, b_im, d_ar, d_ai, d_bbr, d_bbi):
    def body(lr, li, ld, br, bi, dar, dai, dbbr, dbbi, *outs):
        _, vjp = jax.vjp(_s5_discretize, lr[...], li[...], ld[...], br[...], bi[...])
        for o, v in zip(outs, vjp((dar[...], dai[...], dbbr[...], dbbi[...]))):
            o[...] = v

    ins = [lam_re, lam_im, log_dt, b_re, b_im, d_ar, d_ai, d_bbr, d_bbi]
    return pl.pallas_call(
        body, name="s5_prep_bwd", in_specs=[_whole(a) for a in ins], out_specs=[_whole(a) for a in ins[:5]],
        out_shape=[jax.ShapeDtypeStruct(a.shape, F32) for a in ins[:5]],
        compiler_params=pltpu.CompilerParams(vmem_limit_bytes=VMEM_LIMIT),
    )(*ins)


def scan_lti(br, bi, ar, ai, reverse=False):
    shift = shift_up if reverse else shift_down
    seq = br.shape[0]
    d = 1
    while d < seq:
        sr, si = shift(br, d), shift(bi, d)
        br, bi = br + ar * sr - ai * si, bi + ar * si + ai * sr
        ar, ai = ar * ar - ai * ai, 2.0 * ar * ai
        d *= 2
    return br, bi


PHASES = 8


def _load_phases(ref):
    groups = ref.shape[0] // PHASES
    return jnp.concatenate([ref[pl.ds(r, groups, stride=PHASES), :] for r in range(PHASES)], axis=0)


def _store_phases(ref, x):
    groups = ref.shape[0] // PHASES
    for r in range(PHASES):
        ref[pl.ds(r, groups, stride=PHASES), :] = x[r * groups:(r + 1) * groups].astype(ref.dtype)


def _split_phases(x):
    groups = x.shape[0] // PHASES
    return [x[r * groups:(r + 1) * groups] for r in range(PHASES)]


def _prev_phased(x):
    ph = _split_phases(x)
    return jnp.concatenate([shift_down(ph[PHASES - 1], 1)] + ph[:PHASES - 1], axis=0)


def _cmul_add(xr, xi, ar, ai, yr, yi):
    return xr + ar * yr - ai * yi, xi + ar * yi + ai * yr


def scan_phases(br, bi, ar, ai, reverse=False):
    rs, im = _split_phases(br), _split_phases(bi)
    order = range(PHASES - 2, -1, -1) if reverse else range(1, PHASES)
    step = 1 if reverse else -1
    for r in order:
        rs[r], im[r] = _cmul_add(rs[r], im[r], ar, ai, rs[r + step], im[r + step])
    powers = [(ar, ai)]
    for _ in range(PHASES - 1):
        pr, pi = powers[-1]
        powers.append((pr * ar - pi * ai, pr * ai + pi * ar))
    end = 0 if reverse else PHASES - 1
    cr, ci = scan_lti(rs[end], im[end], *powers[PHASES - 1], reverse)
    shift = shift_up if reverse else shift_down
    inr, ini = shift(cr, 1), shift(ci, 1)
    for r in range(PHASES):
        if r == end:
            rs[r], im[r] = cr, ci
        else:
            pr, pi = powers[PHASES - 1 - r if reverse else r]
            rs[r], im[r] = _cmul_add(rs[r], im[r], pr, pi, inr, ini)
    return jnp.concatenate(rs, axis=0), jnp.concatenate(im, axis=0)


def _s5_block_diag(w, rows_first):
    g, a, b = w.shape
    w = w.reshape(S5_TILES, S5_TILE_GROUPS, a, b)
    eye = jnp.eye(S5_TILE_GROUPS, dtype=w.dtype)
    if rows_first:
        return jnp.einsum("tgab,gk->tgakb", w, eye).reshape(S5_TILES, S5_TILE_GROUPS * a, S5_TILE_GROUPS * b)
    return jnp.einsum("tgab,gk->tgbka", w, eye).reshape(S5_TILES, S5_TILE_GROUPS * b, S5_TILE_GROUPS * a)


def _s5_diag_blocks(w, a, b):
    w = w.reshape(S5_TILES, S5_TILE_GROUPS, a, S5_TILE_GROUPS, b)
    eye = jnp.eye(S5_TILE_GROUPS, dtype=w.dtype)
    return jnp.einsum("tgakb,gk->tgab", w, eye).reshape(S5_GROUPS, a, b)


def _gelu(x):
    return jax.nn.gelu(x)


def _s5_tile_specs():
    b_spec = pl.BlockSpec((1, LANES, S5_TILE_STATE), lambda c, b: (c, 0, 0))
    c_spec = pl.BlockSpec((1, S5_TILE_STATE, LANES), lambda c, b: (c, 0, 0))
    a_spec = pl.BlockSpec((1, 1, S5_TILE_STATE), lambda c, b: (c, 0, 0))
    return b_spec, c_spec, a_spec


def s5_fwd(x, g, p, bsz):
    t, d = x.shape
    seq = t // bsz
    h = rms_fwd(x, g, F32, "s5_rms").reshape(bsz, seq, d)
    ar, ai, bbr, bbi = s5_prep(p["lam_re"][..., None], p["lam_im"][..., None], p["log_dt"][:, None, None],
                               p["b_re"], p["b_im"])
    bdr = _s5_block_diag(bbr, False).astype(BF16)
    bdi = _s5_block_diag(bbi, False).astype(BF16)
    cdr = _s5_block_diag(p["c_re"], False).astype(BF16)
    cdi = _s5_block_diag(p["c_im"], False).astype(BF16)
    a_r = ar.reshape(S5_TILES, 1, S5_TILE_STATE)
    a_i = ai.reshape(S5_TILES, 1, S5_TILE_STATE)
    b_spec, c_spec, a_spec = _s5_tile_specs()

    def body(u_ref, dskip_ref, bdr, bdi, cdr, cdi, a_r, a_i, y_ref, yg_ref):
        u = _load_phases(u_ref)
        ub = u.astype(BF16)
        y = dskip_ref[...] * u
        for k in range(S5_TILE_STATE // LANES):
            sl = slice(k * LANES, (k + 1) * LANES)
            br = jnp.dot(ub, bdr[0][:, sl], preferred_element_type=F32)
            bi = jnp.dot(ub, bdi[0][:, sl], preferred_element_type=F32)
            sr, si = scan_phases(br, bi, a_r[0][:, sl], a_i[0][:, sl])
            y = y + jnp.dot(sr.astype(BF16), cdr[0][sl, :], preferred_element_type=F32)
            y = y - jnp.dot(si.astype(BF16), cdi[0][sl, :], preferred_element_type=F32)
        _store_phases(y_ref, y)
        _store_phases(yg_ref, _gelu(y))

    seq_spec = pl.BlockSpec((None, seq, LANES), lambda c, b: (b, 0, c))
    y, yg = pl.pallas_call(
        body, name="s5_core", grid=(S5_TILES, bsz),
        in_specs=[seq_spec, pl.BlockSpec((1, LANES), lambda c, b: (0, c)), b_spec, b_spec, c_spec, c_spec, a_spec, a_spec],
        out_specs=[seq_spec, seq_spec],
        out_shape=[jax.ShapeDtypeStruct((bsz, seq, d), F32)] * 2,
        compiler_params=_params("parallel", "parallel"),
    )(h, p["d"], bdr, bdi, cdr, cdi, a_r, a_i)
    yg = yg.reshape(t, d)
    zv = mm(yg, p["w_val"], name="s5_zv")
    zg = mm(yg, p["w_gate"], name="s5_zg")

    def gate(x, zv, zg, bv, bg):
        return x + (zv + bv) * sigmoid(zg + bg)

    out = rowwise(gate, [x, zv, zg], [p["b_out"][:, :d], p["b_out"][:, d:]], [(d, F32)], name="s5_gate")[0]
    return out, (x, h, y, yg, zv, zg, (ar, ai, bdr, bdi, cdr, cdi, a_r, a_i))


def s5_bwd(dy, saved, g, p):
    x, h, y, yg, zv, zg, (ar, ai, bdr, bdi, cdr, cdi, a_r, a_i) = saved
    bsz, seq, d = h.shape
    t = bsz * seq

    def dgate(dy, zv, zg, bv, bg):
        val = zv + bv
        sg = sigmoid(zg + bg)
        dzv = dy * sg
        dzg = dy * val * sg * (1.0 - sg)
        return dzv, dzg, jnp.sum(dzv, axis=0, keepdims=True), jnp.sum(dzg, axis=0, keepdims=True)

    dzv, dzg, dbv, dbg = rowwise(dgate, [dy, zv, zg], [p["b_out"][:, :d], p["b_out"][:, d:]],
                                 [(d, BF16), (d, BF16)], [(1, d), (1, d)], name="s5_dgate")
    d_wv = mm(yg, dzv, ta=True, name="s5_dwv")
    d_wg = mm(yg, dzg, ta=True, name="s5_dwg")
    dyg = mm(dzv, p["w_val"], tb=True, name="s5_dyg1")
    dyg = mm(dzg, p["w_gate"], tb=True, acc=dyg, name="s5_dyg2").reshape(bsz, seq, d)
    b_spec, c_spec, a_spec = _s5_tile_specs()
    tr = (((0,), (0,)), ((), ()))
    nt = (((1,), (1,)), ((), ()))

    def fn(c, u, y, dyg, dskip, bdr, bdi, cdr, cdi, a_r, a_i):
        _, gelu_vjp = jax.vjp(_gelu, y)
        dyy = gelu_vjp(dyg)[0]
        ddskip = jnp.sum(dyy * u, axis=0, keepdims=True)
        du = dyy * dskip
        dyb = dyy.astype(BF16)
        ub = u.astype(BF16)
        dcr, dci, dbr, dbi, dar, dai = [], [], [], [], [], []
        for k in range(S5_TILE_STATE // LANES):
            sl = slice(k * LANES, (k + 1) * LANES)
            akr, aki = a_r[0][:, sl], a_i[0][:, sl]
            br = jnp.dot(ub, bdr[0][:, sl], preferred_element_type=F32)
            bi = jnp.dot(ub, bdi[0][:, sl], preferred_element_type=F32)
            sr, si = scan_phases(br, bi, akr, aki)
            dcr.append(lax.dot_general(sr.astype(BF16), dyb, tr, preferred_element_type=F32))
            dci.append(-lax.dot_general(si.astype(BF16), dyb, tr, preferred_element_type=F32))
            gr = lax.dot_general(dyb, cdr[0][sl, :], nt, preferred_element_type=F32)
            gi = -lax.dot_general(dyb, cdi[0][sl, :], nt, preferred_element_type=F32)
            gr, gi = scan_phases(gr, gi, akr, -aki, reverse=True)
            spr, spi = _prev_phased(sr), _prev_phased(si)
            dar.append(jnp.sum(gr * spr + gi * spi, axis=0, keepdims=True))
            dai.append(jnp.sum(gi * spr - gr * spi, axis=0, keepdims=True))
            grb, gib = gr.astype(BF16), gi.astype(BF16)
            dbr.append(lax.dot_general(ub, grb, tr, preferred_element_type=F32))
            dbi.append(lax.dot_general(ub, gib, tr, preferred_element_type=F32))
            du = du + lax.dot_general(grb, bdr[0][:, sl], nt, preferred_element_type=F32)
            du = du + lax.dot_general(gib, bdi[0][:, sl], nt, preferred_element_type=F32)
        return (du, ddskip, jnp.concatenate(dbr, axis=1)[None], jnp.concatenate(dbi, axis=1)[None],
                jnp.concatenate(dcr, axis=0)[None], jnp.concatenate(dci, axis=0)[None],
                jnp.concatenate(dar, axis=1)[None], jnp.concatenate(dai, axis=1)[None])

    du, ddskip, dbdr, dbdi, dcdr, dcdi, dar, dai = _seqwise_w(
        fn, [h, y, dyg], [p["d"]], [(bdr, b_spec), (bdi, b_spec), (cdr, c_spec), (cdi, c_spec), (a_r, a_spec), (a_i, a_spec)],
        [(d, F32)], [(1, d)], ct=LANES, name="s5_dcore", phased=True,
        wreds=[((S5_TILES, LANES, S5_TILE_STATE), b_spec), ((S5_TILES, LANES, S5_TILE_STATE), b_spec),
               ((S5_TILES, S5_TILE_STATE, LANES), c_spec), ((S5_TILES, S5_TILE_STATE, LANES), c_spec),
               ((S5_TILES, 1, S5_TILE_STATE), a_spec), ((S5_TILES, 1, S5_TILE_STATE), a_spec)])
    d_bbr = _s5_diag_blocks(dbdr, S5_GROUP, S5_STATE).transpose(0, 2, 1)
    d_bbi = _s5_diag_blocks(dbdi, S5_GROUP, S5_STATE).transpose(0, 2, 1)
    d_cre = _s5_diag_blocks(dcdr, S5_STATE, S5_GROUP).transpose(0, 2, 1)
    d_cim = _s5_diag_blocks(dcdi, S5_STATE, S5_GROUP).transpose(0, 2, 1)
    d_lre, d_lim, d_ldt, d_bre, d_bim = s5_prep_bwd(
        p["lam_re"][..., None], p["lam_im"][..., None], p["log_dt"][:, None, None], p["b_re"], p["b_im"],
        dar.reshape(S5_GROUPS, S5_STATE, 1), dai.reshape(S5_GROUPS, S5_STATE, 1), d_bbr, d_bbi)
    dx, dg = rms_bwd(x, g, [du.reshape(t, d)], dy, "s5_drms")
    grads = dict(g=dg, lam_re=d_lre[..., 0], lam_im=d_lim[..., 0], log_dt=d_ldt[:, 0, 0], b_re=d_bre, b_im=d_bim,
                 c_re=d_cre, c_im=d_cim, d=ddskip, w_val=d_wv, w_gate=d_wg, b_out=jnp.concatenate([dbv, dbg], axis=1))
    return dx, grads


def _expm1_nonpos(x):
    u = jnp.exp(x)
    safe = (u - 1.0) * x / jnp.log(u)
    return jnp.where(u == 1.0, x, jnp.where(x < -20.0, -1.0, safe))


def _softplus(x):
    return jnp.maximum(x, 0.0) + jnp.log(1.0 + jnp.exp(-jnp.abs(x)))


def scan_ltv(a, b, reverse=False):
    shift = shift_up if reverse else shift_down
    seq = a.shape[0]
    d = 1
    while d < seq:
        b = b + a * shift(b, d)
        if 2 * d < seq:
            a = a * shift(a, d, 1.0)
        d *= 2
    return b


def _lru_gates(rec0, cw, cb, ba, bx, lam, wa, wx):
    rec = conv_fwd(rec0, cw, cb)
    recb = rec.astype(BF16)
    r = sigmoid(jnp.dot(recb, wa, preferred_element_type=F32) + ba)
    i = sigmoid(jnp.dot(recb, wx, preferred_element_type=F32) + bx)
    sp = _softplus(-lam)
    log_a = -LRU_C * r * sp
    a = jnp.exp(log_a)
    mult = jnp.sqrt(-_expm1_nonpos(2.0 * log_a))
    return rec, recb, r, i, sp, a, mult


def lru_fwd(x, g, p, bsz):
    t, d = x.shape
    seq = t // bsz
    xn = rms_fwd(x, g, BF16, "lru_rms")
    gb = mm(xn, p["w_gate"], name="lru_gb").reshape(bsz, seq, d)
    rec0 = mm(xn, p["w_rec"], name="lru_rec").reshape(bsz, seq, d)
    wspec = pl.BlockSpec((1, LRU_BLOCK, LRU_BLOCK), lambda c, b: (c, 0, 0))

    def fn(c, rec0, gb, cw, cb, ba, bx, lam, wa, wx):
        rec, _, _, i, _, a, mult = _lru_gates(rec0, cw, cb, ba, bx, lam, wa[0], wx[0])
        h = scan_ltv(a, mult * (i * rec))
        return _gelu(gb) * h

    y = _seqwise_w(fn, [rec0, gb], [p["conv_w"], p["conv_b"], p["b_a"], p["b_x"], p["lam"]],
                   [(p["w_a"], wspec), (p["w_x"], wspec)], [(d, BF16)], [], ct=LRU_BLOCK, name="lru_core")[0]
    y = y.reshape(t, d)
    out = mm(y, p["w_out"], acc=x, name="lru_out")
    return out, (x, xn, gb, rec0, y)


def lru_bwd(dy, saved, g, p):
    x, xn, gb, rec0, y = saved
    bsz, seq, d = gb.shape
    t = bsz * seq
    d_wout = mm(y, dy, ta=True, name="lru_dwout")
    dyy = mm(dy, p["w_out"], tb=True, name="lru_dy").reshape(bsz, seq, d)
    wspec = pl.BlockSpec((1, LRU_BLOCK, LRU_BLOCK), lambda c, b: (c, 0, 0))
    tr = (((0,), (0,)), ((), ()))
    nt = (((1,), (1,)), ((), ()))

    def fn(c, rec0, gb, dyy, cw, cb, ba, bx, lam, wa, wx):
        rec, recb, r, i, sp, a, mult = _lru_gates(rec0, cw, cb, ba, bx, lam, wa[0], wx[0])
        h = scan_ltv(a, mult * (i * rec))
        gg, gelu_vjp = jax.vjp(_gelu, gb)
        dgb = gelu_vjp(dyy * h)[0]
        gr = scan_ltv(shift_up(a, 1), dyy * gg, reverse=True)
        da = gr * shift_down(h, 1)
        dmult = gr * i * rec
        di = gr * mult * rec
        drec = gr * mult * i
        dla = da * a - dmult * (a * a) / mult
        dr = dla * (-LRU_C * sp)
        dlam = jnp.sum(dla * (-LRU_C * r), axis=0, keepdims=True) * (-sigmoid(-lam))
        dpa = dr * r * (1.0 - r)
        dpx = di * i * (1.0 - i)
        dpab, dpxb = dpa.astype(BF16), dpx.astype(BF16)
        dwa = lax.dot_general(recb, dpab, tr, preferred_element_type=F32)
        dwx = lax.dot_general(recb, dpxb, tr, preferred_element_type=F32)
        drec = drec + lax.dot_general(dpab, wa[0], nt, preferred_element_type=F32)
        drec = drec + lax.dot_general(dpxb, wx[0], nt, preferred_element_type=F32)
        drec0, dcw, dcb = conv_bwd(rec0, cw, drec)
        return (dgb, drec0, dcw, dcb, jnp.sum(dpa, axis=0, keepdims=True), jnp.sum(dpx, axis=0, keepdims=True), dlam,
                dwa[None], dwx[None])

    kw = p["conv_w"].shape[0]
    dgb, drec0, dcw, dcb, dba, dbx, dlam, dwa, dwx = _seqwise_w(
        fn, [rec0, gb, dyy], [p["conv_w"], p["conv_b"], p["b_a"], p["b_x"], p["lam"]],
        [(p["w_a"], wspec), (p["w_x"], wspec)], [(d, BF16), (d, BF16)],
        [(kw, d), (1, d), (1, d), (1, d), (1, d)], ct=LRU_BLOCK, name="lru_dcore",
        wreds=[(p["w_a"].shape, wspec), (p["w_x"].shape, wspec)])
    dgb = dgb.reshape(t, d)
    drec0 = drec0.reshape(t, d)
    d_wgate = mm(xn, dgb, ta=True, name="lru_dwgate")
    d_wrec = mm(xn, drec0, ta=True, name="lru_dwrec")
    dxn = mm(dgb, p["w_gate"], tb=True, name="lru_dxn1")
    dxn = mm(drec0, p["w_rec"], tb=True, acc=dxn, name="lru_dxn2")
    dx, dg = rms_bwd(x, g, [dxn], dy, "lru_drms")
    grads = dict(g=dg, w_gate=d_wgate, w_rec=d_wrec, conv_w=dcw, conv_b=dcb, w_a=dwa, b_a=dba, w_x=dwx, b_x=dbx,
                 lam=dlam, w_out=d_wout)
    return dx, grads


_NT = (((1,), (1,)), ((), ()))
_TN = (((0,), (0,)), ((), ()))


def _head_norm(x, g):
    lo = lax.broadcasted_iota(jnp.int32, x.shape, 1) < SB_HEAD_DIM
    x2 = x * x
    s_lo = jnp.sum(jnp.where(lo, x2, 0.0), axis=-1, keepdims=True)
    s_hi = jnp.sum(jnp.where(lo, 0.0, x2), axis=-1, keepdims=True)
    ms = jnp.where(lo, s_lo, s_hi) * (1.0 / SB_HEAD_DIM)
    return x * lax.rsqrt(ms + EPS) * g


def _log_sigmoid(z):
    return jnp.minimum(z, 0.0) - jnp.log(1.0 + jnp.exp(-jnp.abs(z)))


def _dot_split(x, m):
    hi = x.astype(BF16)
    lo = (x - hi.astype(F32)).astype(BF16)
    return jnp.dot(hi, m, preferred_element_type=F32) + jnp.dot(lo, m, preferred_element_type=F32)


def _tri(cmp, n):
    r = lax.broadcasted_iota(jnp.int32, (n, n), 0)
    c = lax.broadcasted_iota(jnp.int32, (n, n), 1)
    return cmp(r, c)


def sb_attn_fwd(q, k, v, qg, kg):
    bsz, seq, d = q.shape
    blk = min(ATT_BLOCK, seq)
    nq = seq // blk
    scale = 1.0 / math.sqrt(SB_HEAD_DIM)

    def body(q_ref, k_ref, v_ref, qg_ref, kg_ref, o_ref, tot_ref, qn, kn, vb):
        qn[...] = _head_norm(q_ref[...], qg_ref[...]).astype(BF16)
        kn[...] = _head_norm(k_ref[...], kg_ref[...]).astype(BF16)
        vb[...] = v_ref[...].astype(BF16)
        lane_lo = lax.broadcasted_iota(jnp.int32, (blk, LANES), 1) < SB_HEAD_DIM
        causal = _tri(lambda r, c: c < r, blk)
        upper = _tri(lambda r, c: r > c, blk).astype(BF16)

        def q_block(qi, _):
            rows = pl.ds(pl.multiple_of(qi * blk, blk), blk)
            q_all = qn[rows, :]
            zero = jnp.zeros((), BF16)
            qbs = (jnp.where(lane_lo, q_all, zero), jnp.where(lane_lo, zero, q_all))

            def block(j, state, masked):
                cols = pl.ds(pl.multiple_of(j * blk, blk), blk)
                kb, vv = kn[cols, :], vb[cols, :]
                zs = [lax.dot_general(qb, kb, _NT, preferred_element_type=F32) * scale for qb in qbs]
                lss = [_log_sigmoid(z) for z in zs]
                lgs = [ls - z for ls, z in zip(lss, zs)]
                if masked:
                    lgs = [jnp.where(causal, lg, 0.0) for lg in lgs]
                later = [_dot_split(lg, upper) for lg in lgs]
                atts = [jnp.exp(ls + carry + cs) for ls, (carry, _), cs in zip(lss, state, later)]
                if masked:
                    atts = [jnp.where(causal, att, 0.0) for att in atts]
                outs = [jnp.dot(att.astype(BF16), vv, preferred_element_type=F32) for att in atts]
                return tuple((carry + jnp.sum(lg, axis=1, keepdims=True), acc + out)
                             for (carry, acc), lg, out in zip(state, lgs, outs))

            init = (jnp.zeros((blk, 1), F32), jnp.zeros((blk, LANES), F32))
            state = block(qi, (init, init), True)
            state = lax.fori_loop(0, qi, lambda jj, s: block(qi - 1 - jj, s, False), state)
            (carry0, acc0), (carry1, acc1) = state
            o_ref[rows, :] = jnp.where(lane_lo, acc0, acc1).astype(o_ref.dtype)
            tot_ref[rows, :] = jnp.where(lane_lo, carry0, carry1)
            return 0

        lax.fori_loop(0, nq, q_block, 0)

    spec = pl.BlockSpec((None, seq, LANES), lambda b, c: (b, 0, c))
    gspec = pl.BlockSpec((1, LANES), lambda b, c: (0, 0))
    return pl.pallas_call(
        body, name="sb_attn_fwd", grid=(bsz, d // LANES), in_specs=[spec, spec, spec, gspec, gspec],
        out_specs=[spec, spec], out_shape=[jax.ShapeDtypeStruct((bsz, seq, d), BF16), jax.ShapeDtypeStruct((bsz, seq, d), F32)],
        scratch_shapes=[pltpu.VMEM((seq, LANES), BF16)] * 3,
        compiler_params=_params("parallel", "parallel"),
    )(q, k, v, qg, kg)


def sb_attn_bwd(q, k, v, qg, kg, tot, do):
    bsz, seq, d = q.shape
    blk = min(ATT_BLOCK, seq)
    nq = seq // blk
    scale = 1.0 / math.sqrt(SB_HEAD_DIM)

    def body(q_ref, k_ref, v_ref, qg_ref, kg_ref, tot_ref, do_ref, dq_ref, dk_ref, dv_ref, dqg_ref, dkg_ref,
             qn, kn, vb, dqn, dkn, dvv):
        qn[...] = _head_norm(q_ref[...], qg_ref[...]).astype(BF16)
        kn[...] = _head_norm(k_ref[...], kg_ref[...]).astype(BF16)
        vb[...] = v_ref[...].astype(BF16)
        dkn[...] = jnp.zeros_like(dkn)
        dvv[...] = jnp.zeros_like(dvv)
        lane_lo = lax.broadcasted_iota(jnp.int32, (blk, LANES), 1) < SB_HEAD_DIM
        causal = _tri(lambda r, c: c < r, blk)
        upto = _tri(lambda r, c: r <= c, blk).astype(BF16)
        before = _tri(lambda r, c: r < c, blk).astype(BF16)

        def q_block(qi, _):
            rows = pl.ds(pl.multiple_of(qi * blk, blk), blk)
            zero = jnp.zeros((), BF16)
            q_all, do_all, tot_all = qn[rows, :], do_ref[rows, :].astype(BF16), tot_ref[rows, :]
            heads = []
            for hm in (lane_lo, jnp.logical_not(lane_lo)):
                heads.append((jnp.where(hm, q_all, zero), jnp.where(hm, do_all, zero),
                              jnp.max(jnp.where(hm, tot_all, -jnp.inf), axis=1, keepdims=True)))

            def block(j, state, masked):
                cols = pl.ds(pl.multiple_of(j * blk, blk), blk)
                kb, vv = kn[cols, :], vb[cols, :]
                two = range(2)
                zs = [lax.dot_general(heads[h][0], kb, _NT, preferred_element_type=F32) * scale for h in two]
                datts = [lax.dot_general(heads[h][1], vv, _NT, preferred_element_type=F32) for h in two]
                lss = [_log_sigmoid(z) for z in zs]
                lgs = [ls - z for ls, z in zip(lss, zs)]
                if masked:
                    lgs = [jnp.where(causal, lg, 0.0) for lg in lgs]
                sofar = [_dot_split(lg, upto) for lg in lgs]
                atts = [jnp.exp(lss[h] + heads[h][2] - state[h][0] - sofar[h]) for h in two]
                if masked:
                    atts = [jnp.where(causal, att, 0.0) for att in atts]
                es = [att * datt for att, datt in zip(atts, datts)]
                earlier = [_dot_split(e, before) for e in es]
                dlgs = [state[h][1] + earlier[h] for h in two]
                if masked:
                    dlgs = [jnp.where(causal, dlg, 0.0) for dlg in dlgs]
                betas = [jnp.exp(ls) for ls in lss]
                dzbs = [((es[h] * (1.0 - betas[h]) - dlgs[h] * betas[h]) * scale).astype(BF16) for h in two]
                dvv[cols, :] += sum(lax.dot_general(atts[h].astype(BF16), heads[h][1], _TN, preferred_element_type=F32)
                                    for h in two)
                dkn[cols, :] += sum(lax.dot_general(dzbs[h], heads[h][0], _TN, preferred_element_type=F32) for h in two)
                dqs = [jnp.dot(dzb, kb, preferred_element_type=F32) for dzb in dzbs]
                return tuple((state[h][0] + jnp.sum(lgs[h], axis=1, keepdims=True),
                              state[h][1] + jnp.sum(es[h], axis=1, keepdims=True), state[h][2] + dqs[h]) for h in two)

            col0 = jnp.zeros((blk, 1), F32)
            init = (col0, col0, jnp.zeros((blk, LANES), F32))
            state = lax.fori_loop(0, qi, lambda j, s: block(j, s, False), (init, init))
            state = block(qi, state, True)
            dqn[rows, :] = jnp.where(lane_lo, state[0][2], state[1][2])
            return 0

        lax.fori_loop(0, nq, q_block, 0)

        def fold(x):
            return x + pltpu.roll(x, SB_HEAD_DIM, 1)

        _, q_vjp = jax.vjp(_head_norm, q_ref[...], qg_ref[...])
        dq, dqg = q_vjp(dqn[...])
        _, k_vjp = jax.vjp(_head_norm, k_ref[...], kg_ref[...])
        dk, dkg = k_vjp(dkn[...])
        dq_ref[...] = dq.astype(dq_ref.dtype)
        dk_ref[...] = dk.astype(dk_ref.dtype)
        dv_ref[...] = dvv[...].astype(dv_ref.dtype)
        first = jnp.logical_and(pl.program_id(0) == 0, pl.program_id(1) == 0)

        @pl.when(first)
        def _():
            dqg_ref[...] = fold(dqg)
            dkg_ref[...] = fold(dkg)

        @pl.when(jnp.logical_not(first))
        def _():
            dqg_ref[...] += fold(dqg)
            dkg_ref[...] += fold(dkg)

    spec = pl.BlockSpec((None, seq, LANES), lambda b, c: (b, 0, c))
    gspec = pl.BlockSpec((1, LANES), lambda b, c: (0, 0))
    act = jax.ShapeDtypeStruct((bsz, seq, d), BF16)
    gain = jax.ShapeDtypeStruct((1, LANES), F32)
    return pl.pallas_call(
        body, name="sb_attn_bwd", grid=(bsz, d // LANES), in_specs=[spec, spec, spec, gspec, gspec, spec, spec],
        out_specs=[spec, spec, spec, gspec, gspec], out_shape=[act, act, act, gain, gain],
        scratch_shapes=[pltpu.VMEM((seq, LANES), BF16)] * 3 + [pltpu.VMEM((seq, LANES), F32)] * 3,
        compiler_params=_params("arbitrary", "arbitrary"),
    )(q, k, v, qg, kg, tot, do)


def sb_fwd(x, g, p, bsz):
    t, d = x.shape
    seq = t // bsz
    xn = rms_fwd(x, g, BF16, "sb_rms")
    q = mm(xn, p["w_q"], name="sb_q").reshape(bsz, seq, d)
    k = mm(xn, p["w_k"], name="sb_k").reshape(bsz, seq, d)
    v = mm(xn, p["w_v"], name="sb_v").reshape(bsz, seq, d)
    qg = jnp.tile(p["q_g"], (1, 2))
    kg = jnp.tile(p["k_g"], (1, 2))
    o, tot = sb_attn_fwd(q, k, v, qg, kg)
    o = o.reshape(t, d)
    out = mm(o, p["w_o"], acc=x, name="sb_out")
    return out, (x, xn, q, k, v, o, tot)


def sb_bwd(dy, saved, g, p):
    x, xn, q, k, v, o, tot = saved
    bsz, seq, d = q.shape
    t = bsz * seq
    d_wo = mm(o, dy, ta=True, name="sb_dwo")
    do = mm(dy, p["w_o"], tb=True, out_dtype=BF16, name="sb_do").reshape(bsz, seq, d)
    qg = jnp.tile(p["q_g"], (1, 2))
    kg = jnp.tile(p["k_g"], (1, 2))
    dq, dk, dv, dqg, dkg = sb_attn_bwd(q, k, v, qg, kg, tot, do)
    dq, dk, dv = dq.reshape(t, d), dk.reshape(t, d), dv.reshape(t, d)
    d_wq = mm(xn, dq, ta=True, name="sb_dwq")
    d_wk = mm(xn, dk, ta=True, name="sb_dwk")
    d_wv = mm(xn, dv, ta=True, name="sb_dwv")
    dxn = mm(dq, p["w_q"], tb=True, name="sb_dxn1")
    dxn = mm(dk, p["w_k"], tb=True, acc=dxn, name="sb_dxn2")
    dxn = mm(dv, p["w_v"], tb=True, acc=dxn, name="sb_dxn3")
    dx, dg = rms_bwd(x, g, [dxn], dy, "sb_drms")
    grads = dict(g=dg, w_q=d_wq, w_k=d_wk, w_v=d_wv, w_o=d_wo, q_g=dqg[:, :SB_HEAD_DIM], k_g=dkg[:, :SB_HEAD_DIM])
    return dx, grads


def loss_head(y, target):
    d = y.shape[1]

    def fn(y, tgt):
        err = y - tgt
        return err * (1.0 / d), jnp.sum(err * err, axis=0, keepdims=True)

    dy, sq = rowwise(fn, [y, target], [], [(d, F32)], [(1, d)], name="loss_head")
    return sq, dy


def adamw(parts, w, m, v, name):
    n, r, c = parts.shape
    tr = _pick(r, max(16, (1 << 20) // (c * n)), 16 if parts.dtype == BF16 else 8)

    def body(p_ref, w_ref, m_ref, v_ref, g_ref, d_ref, nm_ref, nv_ref):
        g = p_ref[0].astype(F32)
        for i in range(1, n):
            g = g + p_ref[i].astype(F32)
        nm = ADAM_B1 * m_ref[...] + (1.0 - ADAM_B1) * g
        nv = ADAM_B2 * v_ref[...] + (1.0 - ADAM_B2) * (g * g)
        m_hat = nm / (1.0 - ADAM_B1 ** ADAM_STEP)
        v_hat = nv / (1.0 - ADAM_B2 ** ADAM_STEP)
        g_ref[...] = g
        d_ref[...] = -ADAM_LR * (m_hat / (jnp.sqrt(v_hat) + ADAM_EPS) + ADAM_WD * w_ref[...])
        nm_ref[...] = nm
        nv_ref[...] = nv

    spec = pl.BlockSpec((tr, c), lambda i: (i, 0))
    return pl.pallas_call(
        body, name=name, grid=(r // tr,), in_specs=[pl.BlockSpec((n, tr, c), lambda i: (0, i, 0)), spec, spec, spec],
        out_specs=[spec] * 4, out_shape=[jax.ShapeDtypeStruct((r, c), F32)] * 4,
        compiler_params=_params("parallel"),
    )(parts, w, m, v)


_HBM = pl.BlockSpec(memory_space=pltpu.HBM)


def _mesh_pos():
    return lax.axis_index("x"), lax.axis_index("y"), lax.axis_index("c")


def all_gather(shards, name):
    n = len(shards)

    def body(*refs):
        ins, outs = refs[:n], refs[n:2 * n]
        send_sems, recv_sems, local_sems = refs[2 * n:]
        x, y, c = _mesh_pos()
        me, sibling = (x, y, c), (x, y, 1 - c)
        chips = [(1 - x, y), (x, 1 - y), (1 - x, 1 - y)]

        def copy(a, k, block, to, src=None):
            px, py, pc = block
            dst = outs[a].at[4 * px + 2 * py + pc]
            return pltpu.make_async_remote_copy(
                src_ref=dst if src is None else src, dst_ref=dst, send_sem=send_sems.at[a, k],
                recv_sem=recv_sems.at[a, k], device_id=to, device_id_type=MESH)

        mine = [pltpu.make_async_copy(ins[a], outs[a].at[4 * x + 2 * y + c], local_sems.at[a]) for a in range(n)]
        for cp in mine:
            cp.start()
        first = []
        for a in range(n):
            first.append(copy(a, 0, me, sibling, src=ins[a]))
            first += [copy(a, 1 + j, me, (*chip, c), src=ins[a]) for j, chip in enumerate(chips)]
        for cp in first:
            cp.start()
        passed = []
        for a in range(n):
            for j, chip in enumerate(chips):
                copy(a, 1 + j, (*chip, c), me).wait_recv()
                cp = copy(a, 4 + j, (*chip, c), sibling)
                cp.start()
                passed.append(cp)
        for a in range(n):
            copy(a, 0, sibling, me).wait_recv()
            for j, chip in enumerate(chips):
                copy(a, 4 + j, (*chip, 1 - c), me).wait_recv()
        for cp in first + passed:
            cp.wait_send()
        for cp in mine:
            cp.wait()

    return pl.pallas_call(
        body, name=name, in_specs=[_HBM] * n, out_specs=[_HBM] * n,
        out_shape=[jax.ShapeDtypeStruct((N_DEV, *s.shape), s.dtype) for s in shards],
        scratch_shapes=[pltpu.SemaphoreType.DMA((n, 7)), pltpu.SemaphoreType.DMA((n, 7)), pltpu.SemaphoreType.DMA((n,))],
    )(*shards)


def exchange_cores(parts, name):
    n = len(parts)

    def body(*refs):
        ins, outs = refs[:n], refs[n:2 * n]
        send_sems, recv_sems = refs[2 * n:]
        x, y, c = _mesh_pos()
        copies = []
        for a in range(n):
            for p in range(4):
                copies.append(pltpu.make_async_remote_copy(
                    src_ref=ins[a].at[p, 1 - c], dst_ref=outs[a].at[p], send_sem=send_sems.at[a, p],
                    recv_sem=recv_sems.at[a, p], device_id=(x, y, 1 - c), device_id_type=MESH))
        for cp in copies:
            cp.start()
        for cp in copies:
            cp.wait()

    return pl.pallas_call(
        body, name=name, in_specs=[_HBM] * n, out_specs=[_HBM] * n,
        out_shape=[jax.ShapeDtypeStruct((4, *s.shape[2:]), s.dtype) for s in parts],
        scratch_shapes=[pltpu.SemaphoreType.DMA((n, 4)), pltpu.SemaphoreType.DMA((n, 4))],
    )(*parts)


def exchange_chips(parts, name):
    n = len(parts)

    def body(*refs):
        ins, outs = refs[:n], refs[n:2 * n]
        send_sems, recv_sems, local_sems = refs[2 * n:]
        x, y, c = _mesh_pos()
        mine = 2 * x + y
        local = [pltpu.make_async_copy(ins[a].at[mine], outs[a].at[mine], local_sems.at[a]) for a in range(n)]
        for cp in local:
            cp.start()
        copies = []
        for a in range(n):
            for r in range(1, 4):
                qx = 1 - x if r & 2 else x
                qy = 1 - y if r & 1 else y
                copies.append(pltpu.make_async_remote_copy(
                    src_ref=ins[a].at[2 * qx + qy], dst_ref=outs[a].at[mine], send_sem=send_sems.at[a, r - 1],
                    recv_sem=recv_sems.at[a, r - 1], device_id=(qx, qy, c), device_id_type=MESH))
        for cp in copies:
            cp.start()
        for cp in copies:
            cp.wait()
        for cp in local:
            cp.wait()

    return pl.pallas_call(
        body, name=name, in_specs=[_HBM] * n, out_specs=[_HBM] * n,
        out_shape=[jax.ShapeDtypeStruct(s.shape, s.dtype) for s in parts],
        scratch_shapes=[pltpu.SemaphoreType.DMA((n, 3)), pltpu.SemaphoreType.DMA((n, 3)), pltpu.SemaphoreType.DMA((n,))],
    )(*parts)


def add_own(core, parts, recv, out_dtype, name):
    _, _, r, c = parts.shape
    tr = _pick(r, max(16, (1 << 19) // c), 16)

    def body(core_ref, p_ref, r_ref, o_ref):
        o_ref[...] = (p_ref[...] + r_ref[...]).astype(out_dtype)

    grid_spec = pltpu.PrefetchScalarGridSpec(
        num_scalar_prefetch=1, grid=(4, r // tr),
        in_specs=[pl.BlockSpec((None, None, tr, c), lambda p, i, core_ref: (p, core_ref[0], i, 0)),
                  pl.BlockSpec((None, tr, c), lambda p, i, core_ref: (p, i, 0))],
        out_specs=pl.BlockSpec((None, tr, c), lambda p, i, core_ref: (p, i, 0)))
    return pl.pallas_call(
        body, name=name, grid_spec=grid_spec, out_shape=jax.ShapeDtypeStruct((4, r, c), out_dtype),
        compiler_params=_params("parallel", "parallel"),
    )(core, parts, recv)


PARAMS = {
    "norm_mix_g": ((4, 1024), None), "norm_ffn_g": ((4, 1024), None),
    "pool_w": ((1, 4, 256, 256), 2), "pool_b": ((1, 1024), None), "pool_scale": ((1, 1024), None),
    "s5_lam_re": ((1, 64, 64), None), "s5_lam_im": ((1, 64, 64), None), "s5_log_dt": ((1, 64), None),
    "s5_b_re": ((1, 64, 64, 16), None), "s5_b_im": ((1, 64, 64, 16), None),
    "s5_c_re": ((1, 64, 16, 64), None), "s5_c_im": ((1, 64, 16, 64), None),
    "s5_d": ((1, 1024), 1), "s5_w_out": ((1, 1024, 2048), 2), "s5_b_out": ((1, 2048), 1),
    "lru_w_in": ((1, 1024, 2048), 2), "lru_conv_w": ((1, 4, 1024), 2), "lru_conv_b": ((1, 1024), 1),
    "lru_w_a": ((1, 4, 256, 256), 2), "lru_b_a": ((1, 1024), 1), "lru_w_x": ((1, 4, 256, 256), 2),
    "lru_b_x": ((1, 1024), 1), "lru_lam": ((1, 1024), 1), "lru_w_out": ((1, 1024, 1024), 1),
    "sb_w_qkv": ((1, 1024, 3072), 2), "sb_q_g": ((1, 64), None), "sb_k_g": ((1, 64), None),
    "sb_w_o": ((1, 1024, 1024), 1),
    "ffn_w_in": ((4, 1024, 5632), 2), "ffn_conv_w": ((4, 3, 5632), 2), "ffn_conv_b": ((4, 5632), None),
    "ffn_w_out": ((4, 2816, 1024), 1),
}
NAMES = list(PARAMS)
BIG = ["s5_w_out", "lru_w_in", "lru_w_out", "sb_w_qkv", "sb_w_o", "ffn_w_in", "ffn_w_out"]
SMALL = [n for n in NAMES if PARAMS[n][1] is not None and n not in BIG]
REPL = [n for n in NAMES if PARAMS[n][1] is None]


def _local_shape(name):
    shape, ax = PARAMS[name]
    return tuple(s // N_DEV if i == ax else s for i, s in enumerate(shape))


def _to_natural(gathered, name):
    shape, ax = PARAMS[name]
    return jnp.moveaxis(gathered, 0, ax).reshape(shape)


def _to_shards(nat, name):
    shape, ax = PARAMS[name]
    split = shape[:ax] + (N_DEV, shape[ax] // N_DEV) + shape[ax + 1:]
    return jnp.moveaxis(nat.reshape(split), ax, 0)


def _rows2d(shape):
    return (math.prod(shape[:-1]), shape[-1])


def _pack(arrays, lead=()):
    flat = jnp.concatenate([a.reshape(*lead, -1) for a in arrays], axis=-1)
    size = flat.shape[-1]
    rows = -(-size // (8 * LANES)) * 8
    flat = jnp.pad(flat, [(0, 0)] * len(lead) + [(0, rows * LANES - size)])
    return flat.reshape(*lead, rows, LANES)


def _unpack(packed, shapes, lead=()):
    flat = packed.reshape(*lead, -1)
    out, off = [], 0
    for s in shapes:
        size = math.prod(s)
        out.append(flat[..., off:off + size].reshape(*lead, *s))
        off += size
    return out


def kernel(x, norm_mix_g, norm_ffn_g, pool_w, pool_b, pool_scale, s5_lam_re, s5_lam_im, s5_log_dt, s5_b_re, s5_b_im, s5_c_re, s5_c_im, s5_d, s5_w_out, s5_b_out, lru_w_in, lru_conv_w, lru_conv_b, lru_w_a, lru_b_a, lru_w_x, lru_b_x, lru_lam, lru_w_out, sb_w_qkv, sb_q_g, sb_k_g, sb_w_o, ffn_w_in, ffn_conv_w, ffn_conv_b, ffn_w_out, loss_target, m_norm_mix_g, m_norm_ffn_g, m_pool_w, m_pool_b, m_pool_scale, m_s5_lam_re, m_s5_lam_im, m_s5_log_dt, m_s5_b_re, m_s5_b_im, m_s5_c_re, m_s5_c_im, m_s5_d, m_s5_w_out, m_s5_b_out, m_lru_w_in, m_lru_conv_w, m_lru_conv_b, m_lru_w_a, m_lru_b_a, m_lru_w_x, m_lru_b_x, m_lru_lam, m_lru_w_out, m_sb_w_qkv, m_sb_q_g, m_sb_k_g, m_sb_w_o, m_ffn_w_in, m_ffn_conv_w, m_ffn_conv_b, m_ffn_w_out, v_norm_mix_g, v_norm_ffn_g, v_pool_w, v_pool_b, v_pool_scale, v_s5_lam_re, v_s5_lam_im, v_s5_log_dt, v_s5_b_re, v_s5_b_im, v_s5_c_re, v_s5_c_im, v_s5_d, v_s5_w_out, v_s5_b_out, v_lru_w_in, v_lru_conv_w, v_lru_conv_b, v_lru_w_a, v_lru_b_a, v_lru_w_x, v_lru_b_x, v_lru_lam, v_lru_w_out, v_sb_w_qkv, v_sb_q_g, v_sb_k_g, v_sb_w_o, v_ffn_w_in, v_ffn_conv_w, v_ffn_conv_b, v_ffn_w_out):
    w = dict(zip(NAMES, (norm_mix_g, norm_ffn_g, pool_w, pool_b, pool_scale, s5_lam_re, s5_lam_im, s5_log_dt, s5_b_re, s5_b_im, s5_c_re, s5_c_im, s5_d, s5_w_out, s5_b_out, lru_w_in, lru_conv_w, lru_conv_b, lru_w_a, lru_b_a, lru_w_x, lru_b_x, lru_lam, lru_w_out, sb_w_qkv, sb_q_g, sb_k_g, sb_w_o, ffn_w_in, ffn_conv_w, ffn_conv_b, ffn_w_out)))
    m = dict(zip(NAMES, (m_norm_mix_g, m_norm_ffn_g, m_pool_w, m_pool_b, m_pool_scale, m_s5_lam_re, m_s5_lam_im, m_s5_log_dt, m_s5_b_re, m_s5_b_im, m_s5_c_re, m_s5_c_im, m_s5_d, m_s5_w_out, m_s5_b_out, m_lru_w_in, m_lru_conv_w, m_lru_conv_b, m_lru_w_a, m_lru_b_a, m_lru_w_x, m_lru_b_x, m_lru_lam, m_lru_w_out, m_sb_w_qkv, m_sb_q_g, m_sb_k_g, m_sb_w_o, m_ffn_w_in, m_ffn_conv_w, m_ffn_conv_b, m_ffn_w_out)))
    v = dict(zip(NAMES, (v_norm_mix_g, v_norm_ffn_g, v_pool_w, v_pool_b, v_pool_scale, v_s5_lam_re, v_s5_lam_im, v_s5_log_dt, v_s5_b_re, v_s5_b_im, v_s5_c_re, v_s5_c_im, v_s5_d, v_s5_w_out, v_s5_b_out, v_lru_w_in, v_lru_conv_w, v_lru_conv_b, v_lru_w_a, v_lru_b_a, v_lru_w_x, v_lru_b_x, v_lru_lam, v_lru_w_out, v_sb_w_qkv, v_sb_q_g, v_sb_k_g, v_sb_w_o, v_ffn_w_in, v_ffn_conv_w, v_ffn_conv_b, v_ffn_w_out)))
    return train_step(x, loss_target, w, m, v)


def train_step(x, loss_target, w, m, v):
    bsz, seq, d = x.shape
    t = bsz * seq
    n_layers = PARAMS["norm_mix_g"][0][0]

    small_shapes = [_local_shape(n) for n in SMALL]
    gathered = all_gather([w[n].astype(BF16) for n in BIG] + [_pack([w[n] for n in SMALL])], "gather_weights")
    nat = {n: _to_natural(g, n) for n, g in zip(BIG, gathered[:-1])}
    for n, g in zip(SMALL, _unpack(gathered[-1], small_shapes, lead=(N_DEV,))):
        nat[n] = _to_natural(g, n)
    for n in REPL:
        nat[n] = w[n]

    hid = FFN_HIDDEN
    pool_p = (nat["pool_w"][0].astype(BF16), nat["pool_b"], nat["pool_scale"])
    s5_p = dict(lam_re=nat["s5_lam_re"][0], lam_im=nat["s5_lam_im"][0], log_dt=nat["s5_log_dt"][0],
                b_re=nat["s5_b_re"][0], b_im=nat["s5_b_im"][0], c_re=nat["s5_c_re"][0], c_im=nat["s5_c_im"][0],
                d=nat["s5_d"], w_val=nat["s5_w_out"][0, :, :d], w_gate=nat["s5_w_out"][0, :, d:], b_out=nat["s5_b_out"])
    lru_p = dict(w_gate=nat["lru_w_in"][0, :, :d], w_rec=nat["lru_w_in"][0, :, d:], conv_w=nat["lru_conv_w"][0],
                 conv_b=nat["lru_conv_b"], b_a=nat["lru_b_a"], b_x=nat["lru_b_x"], lam=nat["lru_lam"],
                 w_a=nat["lru_w_a"][0].astype(BF16), w_x=nat["lru_w_x"][0].astype(BF16), w_out=nat["lru_w_out"][0])
    sb_p = dict(w_q=nat["sb_w_qkv"][0, :, :d], w_k=nat["sb_w_qkv"][0, :, d:2 * d], w_v=nat["sb_w_qkv"][0, :, 2 * d:],
                w_o=nat["sb_w_o"][0], q_g=nat["sb_q_g"], k_g=nat["sb_k_g"])

    def ffn_p(li):
        return (nat["ffn_w_in"][li, :, :hid], nat["ffn_w_in"][li, :, hid:], nat["ffn_conv_w"][li],
                nat["ffn_conv_b"][li:li + 1], nat["ffn_w_out"][li])

    def gain(name, li):
        return nat[name][li:li + 1]

    h = x.reshape(t, d)
    h, pool_saved = pool_fwd(h, gain("norm_mix_g", 0), *pool_p, bsz)
    h, ffn0 = ffn_fwd(h, gain("norm_ffn_g", 0), *ffn_p(0), bsz, 0)
    h, s5_saved = s5_fwd(h, gain("norm_mix_g", 1), s5_p, bsz)
    h, ffn1 = ffn_fwd(h, gain("norm_ffn_g", 1), *ffn_p(1), bsz, 1)
    h, lru_saved = lru_fwd(h, gain("norm_mix_g", 2), lru_p, bsz)
    h, ffn2 = ffn_fwd(h, gain("norm_ffn_g", 2), *ffn_p(2), bsz, 2)
    h, sb_saved = sb_fwd(h, gain("norm_mix_g", 3), sb_p, bsz)
    h, ffn3 = ffn_fwd(h, gain("norm_ffn_g", 3), *ffn_p(3), bsz, 3)
    sq, dh = loss_head(h, loss_target.reshape(t, d))
    loss = lax.psum(0.5 * jnp.sum(sq) / d, ("x", "y", "c"))

    fg = [None] * n_layers
    dh, fg[3] = ffn_bwd(dh, ffn3, gain("norm_ffn_g", 3), *ffn_p(3), 3)
    dh, sb_g = sb_bwd(dh, sb_saved, gain("norm_mix_g", 3), sb_p)
    dh, fg[2] = ffn_bwd(dh, ffn2, gain("norm_ffn_g", 2), *ffn_p(2), 2)
    dh, lru_g = lru_bwd(dh, lru_saved, gain("norm_mix_g", 2), lru_p)
    dh, fg[1] = ffn_bwd(dh, ffn1, gain("norm_ffn_g", 1), *ffn_p(1), 1)
    dh, s5_g = s5_bwd(dh, s5_saved, gain("norm_mix_g", 1), s5_p)
    dh, fg[0] = ffn_bwd(dh, ffn0, gain("norm_ffn_g", 0), *ffn_p(0), 0)
    dh, pool_g = pool_bwd(dh, pool_saved, gain("norm_mix_g", 0), *pool_p)
    grad_x = dh.reshape(bsz, seq, d)

    part = {
        "norm_mix_g": jnp.concatenate([pool_g[0], s5_g["g"], lru_g["g"], sb_g["g"]], axis=0),
        "norm_ffn_g": jnp.concatenate([g[0] for g in fg], axis=0),
        "pool_w": pool_g[1][None], "pool_b": pool_g[2], "pool_scale": pool_g[3],
        "s5_lam_re": s5_g["lam_re"][None], "s5_lam_im": s5_g["lam_im"][None], "s5_log_dt": s5_g["log_dt"][None],
        "s5_b_re": s5_g["b_re"][None], "s5_b_im": s5_g["b_im"][None], "s5_c_re": s5_g["c_re"][None],
        "s5_c_im": s5_g["c_im"][None], "s5_d": s5_g["d"],
        "s5_w_out": jnp.concatenate([s5_g["w_val"], s5_g["w_gate"]], axis=1)[None], "s5_b_out": s5_g["b_out"],
        "lru_w_in": jnp.concatenate([lru_g["w_gate"], lru_g["w_rec"]], axis=1)[None], "lru_conv_w": lru_g["conv_w"][None],
        "lru_conv_b": lru_g["conv_b"], "lru_w_a": lru_g["w_a"][None], "lru_b_a": lru_g["b_a"],
        "lru_w_x": lru_g["w_x"][None], "lru_b_x": lru_g["b_x"], "lru_lam": lru_g["lam"], "lru_w_out": lru_g["w_out"][None],
        "sb_w_qkv": jnp.concatenate([sb_g["w_q"], sb_g["w_k"], sb_g["w_v"]], axis=1)[None],
        "sb_q_g": sb_g["q_g"], "sb_k_g": sb_g["k_g"], "sb_w_o": sb_g["w_o"][None],
        "ffn_w_in": jnp.stack([jnp.concatenate([g[1], g[2]], axis=1) for g in fg]),
        "ffn_conv_w": jnp.stack([g[3] for g in fg]), "ffn_conv_b": jnp.concatenate([g[4] for g in fg], axis=0),
        "ffn_w_out": jnp.stack([g[5] for g in fg]),
    }

    core = lax.axis_index("c").astype(jnp.int32).reshape(1)
    names2d = BIG + ["small"]
    shapes2d = [_rows2d(_local_shape(n)) for n in BIG]
    small_part = _pack([_to_shards(part[n], n) for n in SMALL], lead=(N_DEV,))
    shapes2d.append(small_part.shape[1:])
    by_dest = [_to_shards(part[n], n).reshape(4, 2, *s) for n, s in zip(BIG, shapes2d)] + [small_part.reshape(4, 2, *shapes2d[-1])]
    from_core = exchange_cores(by_dest, "reduce_cores")
    chip_sums = [add_own(core, p, r, F32 if n == "small" else BF16, f"add_{n}")
                 for n, p, r in zip(names2d, by_dest, from_core)]
    from_chips = exchange_chips(chip_sums, "reduce_chips")

    def local2d(tree, n):
        return tree[n].reshape(_rows2d(_local_shape(n)))

    out = {}
    for n, parts in zip(BIG, from_chips[:-1]):
        res = adamw(parts, local2d(w, n), local2d(m, n), local2d(v, n), f"adamw_{n}")
        out[n] = [r.reshape(_local_shape(n)) for r in res]
    small_res = adamw(from_chips[-1], *[_pack([tree[n] for n in SMALL]) for tree in (w, m, v)], "adamw_small")
    for i, res in enumerate(small_res):
        for n, r in zip(SMALL, _unpack(res, small_shapes)):
            out.setdefault(n, [None] * 4)[i] = r

    repl_shapes = [PARAMS[n][0] for n in REPL]
    repl_parts = all_gather([_pack([part[n] for n in REPL])], "gather_grads")[0]
    repl_res = adamw(repl_parts, *[_pack([tree[n] for n in REPL]) for tree in (w, m, v)], "adamw_replicated")
    for i, res in enumerate(repl_res):
        for n, r in zip(REPL, _unpack(res, repl_shapes)):
            out.setdefault(n, [None] * 4)[i] = r

    return (loss, grad_x, *[out[n][0] for n in NAMES], *[out[n][1] for n in NAMES], *[out[n][2] for n in NAMES],
            *[out[n][3] for n in NAMES])
```

```python
import functools
import math

import jax
import jax.numpy as jnp
from jax import lax
from jax.experimental import pallas as pl
from jax.experimental.pallas import tpu as pltpu

F32 = jnp.float32
BF16 = jnp.bfloat16
MESH = pl.DeviceIdType.MESH

N_DEV = 8
D_MODEL = 1024
EPS = 1e-6
POOL_GROUP = 256
S5_GROUPS, S5_GROUP, S5_STATE = 64, 16, 64
LRU_BLOCK = 256
LRU_C = 8.0
SB_HEAD_DIM = 64
ATT_BLOCK = 256
FFN_HIDDEN = 2816
ADAM_LR, ADAM_B1, ADAM_B2, ADAM_EPS, ADAM_WD, ADAM_STEP = 0.001, 0.9, 0.999, 1e-08, 0.01, 10
LANES = 128
SUBLANES = 8
VMEM_LIMIT = 56 * 1024 * 1024


def _pick(n, target, mult=LANES):
    best = None
    for d in range(mult, min(n, target) + 1, mult):
        if n % d == 0:
            best = d
    return best or n


def _params(*sem):
    return pltpu.CompilerParams(dimension_semantics=sem, vmem_limit_bytes=VMEM_LIMIT)


def mm(a, b, *, ta=False, tb=False, acc=None, out_dtype=F32, name):
    if ta:
        kdim, m = a.shape
    else:
        m, kdim = a.shape
    if tb:
        n, k2 = b.shape
    else:
        k2, n = b.shape
    assert kdim == k2, (a.shape, b.shape)
    tm = _pick(m, 1408 if ta else 512)
    tn = _pick(n, 1408)
    tk = _pick(kdim, 1024) if ta else kdim
    nk = kdim // tk
    a_spec = pl.BlockSpec((tk, tm), lambda j, i, k: (k, i)) if ta else pl.BlockSpec((tm, tk), lambda j, i, k: (i, k))
    b_spec = pl.BlockSpec((tn, tk), lambda j, i, k: (j, k)) if tb else pl.BlockSpec((tk, tn), lambda j, i, k: (k, j))
    o_spec = pl.BlockSpec((tm, tn), lambda j, i, k: (i, j))
    dims = (((0,) if ta else (1,), (1,) if tb else (0,)), ((), ()))
    has_acc = acc is not None

    def body(*refs):
        if has_acc:
            a_ref, b_ref, c_ref, o_ref, acc_ref = refs
        else:
            a_ref, b_ref, o_ref, acc_ref = refs
        k = pl.program_id(2)

        @pl.when(k == 0)
        def _():
            acc_ref[...] = c_ref[...].astype(F32) if has_acc else jnp.zeros_like(acc_ref)

        acc_ref[...] += lax.dot_general(a_ref[...].astype(BF16), b_ref[...].astype(BF16), dims,
                                        preferred_element_type=F32)

        @pl.when(k == nk - 1)
        def _():
            o_ref[...] = acc_ref[...].astype(out_dtype)

    ins = [a, b] + ([acc] if has_acc else [])
    specs = [a_spec, b_spec] + ([o_spec] if has_acc else [])
    return pl.pallas_call(
        body, name=name, grid=(n // tn, m // tm, nk), in_specs=specs, out_specs=o_spec,
        out_shape=jax.ShapeDtypeStruct((m, n), out_dtype), scratch_shapes=[pltpu.VMEM((tm, tn), F32)],
        compiler_params=_params("parallel", "parallel", "arbitrary"),
    )(*ins)


def rowwise(fn, tiled, bcast, outs, reds=(), *, tm=256, name):
    t = tiled[0].shape[0]
    tm = min(tm, t)
    assert t % tm == 0
    n_t, n_b, n_o, n_r = len(tiled), len(bcast), len(outs), len(reds)

    def body(*refs):
        vals = [r[...] for r in refs[:n_t + n_b]]
        res = fn(*vals)
        res = res if isinstance(res, tuple) else (res,)
        o_refs = refs[n_t + n_b:n_t + n_b + n_o]
        r_refs = refs[n_t + n_b + n_o:]
        for o_ref, v in zip(o_refs, res[:n_o]):
            o_ref[...] = v.astype(o_ref.dtype)
        first = pl.program_id(0) == 0
        for r_ref, v in zip(r_refs, res[n_o:]):
            @pl.when(first)
            def _(r_ref=r_ref, v=v):
                r_ref[...] = v.astype(F32)

            @pl.when(jnp.logical_not(first))
            def _(r_ref=r_ref, v=v):
                r_ref[...] += v.astype(F32)

    in_specs = [pl.BlockSpec((tm, a.shape[1]), lambda i: (i, 0)) for a in tiled]
    in_specs += [pl.BlockSpec(a.shape, lambda i, nd=a.ndim: (0,) * nd) for a in bcast]
    out_specs = [pl.BlockSpec((tm, c), lambda i: (i, 0)) for c, _ in outs]
    out_specs += [pl.BlockSpec(s, lambda i: (0, 0)) for s in reds]
    out_shape = [jax.ShapeDtypeStruct((t, c), dt) for c, dt in outs]
    out_shape += [jax.ShapeDtypeStruct(s, F32) for s in reds]
    res = pl.pallas_call(
        body, name=name, grid=(t // tm,), in_specs=in_specs, out_specs=out_specs, out_shape=out_shape,
        compiler_params=_params("arbitrary"),
    )(*tiled, *bcast)
    return res


def seqwise(fn, seqs, chans, outs, reds=(), *, ct, name, fulls=()):
    def split(x):
        return x if isinstance(x, tuple) else (x, 0)

    seqs = [split(s) for s in seqs]
    chans = [split(s) for s in chans]
    bsz, seq = seqs[0][0].shape[:2]
    n_c = outs[0][0] // ct if outs else reds[0][1] // ct
    n_s, n_ch, n_f, n_o, n_r = len(seqs), len(chans), len(fulls), len(outs), len(reds)

    def body(*refs):
        c = pl.program_id(0)
        vals = [r[...] for r in refs[:n_s + n_ch + n_f]]
        res = fn(c, *vals)
        res = res if isinstance(res, tuple) else (res,)
        o_refs = refs[n_s + n_ch + n_f:n_s + n_ch + n_f + n_o]
        r_refs = refs[n_s + n_ch + n_f + n_o:]
        for o_ref, v in zip(o_refs, res[:n_o]):
            o_ref[...] = v.astype(o_ref.dtype)
        first = pl.program_id(1) == 0
        for r_ref, v in zip(r_refs, res[n_o:]):
            @pl.when(first)
            def _(r_ref=r_ref, v=v):
                r_ref[...] = v.astype(F32)

            @pl.when(jnp.logical_not(first))
            def _(r_ref=r_ref, v=v):
                r_ref[...] += v.astype(F32)

    in_specs = [pl.BlockSpec((None, seq, ct), lambda c, b, off=off: (b, 0, c + off)) for _, off in seqs]
    in_specs += [pl.BlockSpec((a.shape[0], ct), lambda c, b, off=off: (0, c + off)) for a, off in chans]
    in_specs += [pl.BlockSpec(a.shape, lambda c, b, nd=a.ndim: (0,) * nd) for a in fulls]
    out_specs = [pl.BlockSpec((None, seq, ct), lambda c, b: (b, 0, c)) for _ in outs]
    out_specs += [pl.BlockSpec((r, ct), lambda c, b: (0, c)) for r, _ in reds]
    out_shape = [jax.ShapeDtypeStruct((bsz, seq, cc), dt) for cc, dt in outs]
    out_shape += [jax.ShapeDtypeStruct(s, F32) for s in reds]
    return pl.pallas_call(
        body, name=name, grid=(n_c, bsz), in_specs=in_specs, out_specs=out_specs, out_shape=out_shape,
        compiler_params=_params("arbitrary", "arbitrary"),
    )(*[a for a, _ in seqs], *[a for a, _ in chans], *fulls)


def _rows(x):
    return lax.broadcasted_iota(jnp.int32, x.shape, 0)


def shift_down(x, d, fill=0.0):
    if d == 0:
        return x
    s = x.shape[0]
    if d % SUBLANES == 0:
        return jnp.concatenate([jnp.full((d,) + x.shape[1:], fill, x.dtype), x[:s - d]], axis=0)
    rolled = pltpu.roll(x, d, 0)
    if d > SUBLANES or s <= SUBLANES:
        return jnp.where(_rows(x) >= d, rolled, fill)
    head = rolled[:SUBLANES]
    return jnp.concatenate([jnp.where(_rows(head) >= d, head, fill), rolled[SUBLANES:]], axis=0)


def shift_up(x, d, fill=0.0):
    if d == 0:
        return x
    s = x.shape[0]
    if d % SUBLANES == 0:
        return jnp.concatenate([x[d:], jnp.full((d,) + x.shape[1:], fill, x.dtype)], axis=0)
    rolled = pltpu.roll(x, s - d, 0)
    if d > SUBLANES or s <= SUBLANES:
        return jnp.where(_rows(x) < s - d, rolled, fill)
    tail = rolled[s - SUBLANES:]
    return jnp.concatenate([rolled[:s - SUBLANES], jnp.where(_rows(tail) < SUBLANES - d, tail, fill)], axis=0)


def conv_fwd(x, w, b):
    kw = w.shape[0]
    y = b + w[kw - 1:kw] * x
    for d in range(1, kw):
        y = y + w[kw - 1 - d:kw - d] * shift_down(x, d)
    return y


def conv_bwd(x, w, dy):
    kw = w.shape[0]
    dx = w[kw - 1:kw] * dy
    dws = [jnp.sum(dy * x, axis=0, keepdims=True)]
    for d in range(1, kw):
        dx = dx + w[kw - 1 - d:kw - d] * shift_up(dy, d)
        dws.append(jnp.sum(dy * shift_down(x, d), axis=0, keepdims=True))
    dw = jnp.concatenate(dws[::-1], axis=0)
    return dx, dw, jnp.sum(dy, axis=0, keepdims=True)


def sigmoid(x):
    return 0.5 * jnp.tanh(0.5 * x) + 0.5


def rms_fwd(x, g, out_dtype, name):
    def fn(x, g):
        return x * lax.rsqrt(jnp.mean(x * x, axis=-1, keepdims=True) + EPS) * g

    return rowwise(fn, [x], [g], [(x.shape[1], out_dtype)], name=name)[0]


def rms_bwd(x, g, dh_parts, dres, name):
    n_p = len(dh_parts)

    def fn(x, dres, *rest):
        dh = rest[0].astype(F32)
        for p in rest[1:n_p]:
            dh = dh + p.astype(F32)
        g = rest[n_p]
        r = lax.rsqrt(jnp.mean(x * x, axis=-1, keepdims=True) + EPS)
        xh = x * r
        dxh = dh * g
        dx = r * (dxh - xh * jnp.mean(dxh * xh, axis=-1, keepdims=True))
        return dres + dx, jnp.sum(dh * xh, axis=0, keepdims=True)

    return rowwise(fn, [x, dres, *dh_parts], [g], [(x.shape[1], F32)], [(1, x.shape[1])], name=name)


def ffn_fwd(x, g, w_val, w_gate, cw, cb, w_out, bsz, li):
    t, d = x.shape
    hid = w_val.shape[1]
    xn = rms_fwd(x, g, BF16, f"ffn{li}_rms")
    hv = mm(xn, w_val, name=f"ffn{li}_hv").reshape(bsz, t // bsz, hid)
    hg = mm(xn, w_gate, name=f"ffn{li}_hg").reshape(bsz, t // bsz, hid)
    ct = _pick(hid, 256)
    off = hid // ct

    def fn(c, hv, hg, wv, wg, bv, bg):
        val = conv_fwd(hv, wv, bv)
        gate = conv_fwd(hg, wg, bg)
        return gate * sigmoid(gate) * val

    act = seqwise(fn, [hv, hg], [cw, (cw, off), cb, (cb, off)], [(hid, BF16)], ct=ct, name=f"ffn{li}_gate")[0]
    act = act.reshape(t, hid)
    y = mm(act, w_out, acc=x, name=f"ffn{li}_out")
    return y, (x, xn, hv, hg, act)


def ffn_bwd(dy, saved, g, w_val, w_gate, cw, cb, w_out, li):
    x, xn, hv, hg, act = saved
    bsz, seq, hid = hv.shape
    t = bsz * seq
    d_wout = mm(act, dy, ta=True, name=f"ffn{li}_dwout")
    dact = mm(dy, w_out, tb=True, name=f"ffn{li}_dact").reshape(bsz, seq, hid)
    ct = _pick(hid, 256)
    off = hid // ct

    def fn(c, hv, hg, dact, wv, wg, bv, bg):
        val = conv_fwd(hv, wv, bv)
        gate = conv_fwd(hg, wg, bg)
        sg = sigmoid(gate)
        silu = gate * sg
        dval = dact * silu
        dgate = dact * val * (sg + silu * (1.0 - sg))
        dhv, dwv, dbv = conv_bwd(hv, wv, dval)
        dhg, dwg, dbg = conv_bwd(hg, wg, dgate)
        return dhv, dhg, dwv, dwg, dbv, dbg

    kw = cw.shape[0]
    dhv, dhg, dwv, dwg, dbv, dbg = seqwise(
        fn, [hv, hg, dact], [cw, (cw, off), cb, (cb, off)], [(hid, BF16), (hid, BF16)],
        [(kw, hid), (kw, hid), (1, hid), (1, hid)], ct=ct, name=f"ffn{li}_dgate")
    dhv = dhv.reshape(t, hid)
    dhg = dhg.reshape(t, hid)
    d_wv = mm(xn, dhv, ta=True, name=f"ffn{li}_dwv")
    d_wg = mm(xn, dhg, ta=True, name=f"ffn{li}_dwg")
    dxn = mm(dhv, w_val, tb=True, name=f"ffn{li}_dxn1")
    dxn = mm(dhg, w_gate, tb=True, acc=dxn, name=f"ffn{li}_dxn2")
    dx, dg = rms_bwd(x, g, [dxn], dy, f"ffn{li}_drms")
    d_cw = jnp.concatenate([dwv, dwg], axis=1)
    d_cb = jnp.concatenate([dbv, dbg], axis=1)
    return dx, (dg, d_wv, d_wg, d_cw, d_cb, d_wout)


def _window_sum(x, c, shift):
    s2 = x + shift(x, 1)
    s4 = s2 + shift(s2, 2)
    s8 = s4 + shift(s4, 4)
    s16 = s8 + shift(s8, 8)
    return jnp.where(c == 0, s2, jnp.where(c == 1, s4, jnp.where(c == 2, s8, s16)))


def _pool_inv_count(x, c):
    w = jnp.left_shift(2, c)
    return 1.0 / jnp.minimum(_rows(x) + 1, w).astype(F32)


def pool_fwd(x, g, w, b, scale, bsz):
    t, d = x.shape
    h = rms_fwd(x, g, F32, "pool_rms").reshape(bsz, t // bsz, d)

    def fn(c, h, x, b, scale, w):
        dd = _window_sum(h, c, shift_down) * _pool_inv_count(h, c) - h
        y = jnp.dot(dd.astype(BF16), w[0], preferred_element_type=F32) + b
        return x + scale * y

    wspec = pl.BlockSpec((1, POOL_GROUP, POOL_GROUP), lambda c, bb: (c, 0, 0))
    y = _seqwise_w(fn, [h, x.reshape(bsz, t // bsz, d)], [b, scale], [(w, wspec)], [(d, F32)], [], ct=POOL_GROUP,
                   name="pool_fwd")[0]
    return y.reshape(t, d), (x, h)


def pool_bwd(dy, saved, g, w, b, scale):
    x, h = saved
    bsz, seq, d = h.shape
    t = bsz * seq

    def fn(c, h, dy, b, scale, w):
        inv = _pool_inv_count(h, c)
        dd = _window_sum(h, c, shift_down) * inv - h
        ddb = dd.astype(BF16)
        y = jnp.dot(ddb, w[0], preferred_element_type=F32) + b
        dscale = jnp.sum(dy * y, axis=0, keepdims=True)
        dyy = dy * scale
        db = jnp.sum(dyy, axis=0, keepdims=True)
        dyb = dyy.astype(BF16)
        dw = lax.dot_general(ddb, dyb, (((0,), (0,)), ((), ())), preferred_element_type=F32)
        ddd = lax.dot_general(dyb, w[0], (((1,), (1,)), ((), ())), preferred_element_type=F32)
        dh = _window_sum(ddd * inv, c, shift_up) - ddd
        return dh, db, dscale, dw[None]

    wspec = pl.BlockSpec((1, POOL_GROUP, POOL_GROUP), lambda c, bb: (c, 0, 0))
    dh, db, dscale, dw = _seqwise_w(
        fn, [h, dy.reshape(bsz, seq, d)], [b, scale], [(w, wspec)], [(d, F32)], [(1, d), (1, d)], ct=POOL_GROUP,
        name="pool_bwd", wreds=[((4, POOL_GROUP, POOL_GROUP), wspec)])
    dx, dg = rms_bwd(x, g, [dh.reshape(t, d)], dy, "pool_drms")
    return dx, (dg, dw, db, dscale)


def _seqwise_w(fn, seqs, chans, blocked, outs, reds, *, ct, name, wreds=(), phased=False):
    bsz, seq = seqs[0].shape[:2]
    n_c = seqs[0].shape[2] // ct
    n_s = len(seqs)
    n_in = n_s + len(chans) + len(blocked)
    n_o, n_r = len(outs), len(reds) + len(wreds)

    def body(*refs):
        c = pl.program_id(0)
        seq_vals = [_load_phases(r) if phased else r[...] for r in refs[:n_s]]
        res = fn(c, *seq_vals, *[r[...] for r in refs[n_s:n_in]])
        res = res if isinstance(res, tuple) else (res,)
        for o_ref, v in zip(refs[n_in:n_in + n_o], res[:n_o]):
            if phased:
                _store_phases(o_ref, v)
            else:
                o_ref[...] = v.astype(o_ref.dtype)
        first = pl.program_id(1) == 0
        for r_ref, v in zip(refs[n_in + n_o:], res[n_o:]):
            @pl.when(first)
            def _(r_ref=r_ref, v=v):
                r_ref[...] = v.astype(F32)

            @pl.when(jnp.logical_not(first))
            def _(r_ref=r_ref, v=v):
                r_ref[...] += v.astype(F32)

    in_specs = [pl.BlockSpec((None, seq, ct), lambda c, b: (b, 0, c)) for _ in seqs]
    in_specs += [pl.BlockSpec((a.shape[0], ct), lambda c, b: (0, c)) for a in chans]
    in_specs += [spec for _, spec in blocked]
    out_specs = [pl.BlockSpec((None, seq, ct), lambda c, b: (b, 0, c)) for _ in outs]
    out_specs += [pl.BlockSpec((r, ct), lambda c, b: (0, c)) for r, _ in reds]
    out_specs += [spec for _, spec in wreds]
    out_shape = [jax.ShapeDtypeStruct((bsz, seq, cc), dt) for cc, dt in outs]
    out_shape += [jax.ShapeDtypeStruct(s, F32) for s in reds]
    out_shape += [jax.ShapeDtypeStruct(s, F32) for s, _ in wreds]
    return pl.pallas_call(
        body, name=name, grid=(n_c, bsz), in_specs=in_specs, out_specs=out_specs, out_shape=out_shape,
        compiler_params=_params("arbitrary", "arbitrary"),
    )(*seqs, *chans, *[a for a, _ in blocked])


S5_TILE_GROUPS = LANES // S5_GROUP
S5_TILES = S5_GROUPS // S5_TILE_GROUPS
S5_TILE_STATE = S5_TILE_GROUPS * S5_STATE


def _s5_discretize(lam_re, lam_im, log_dt, b_re, b_im):
    lr = jnp.minimum(lam_re, -1e-4)
    dt = jnp.exp(log_dt)
    er = jnp.exp(lr * dt)
    ar = er * jnp.cos(lam_im * dt)
    ai = er * jnp.sin(lam_im * dt)
    den = lr * lr + lam_im * lam_im
    cr = ((ar - 1.0) * lr + ai * lam_im) / den
    ci = (ai * lr - (ar - 1.0) * lam_im) / den
    return ar, ai, cr * b_re - ci * b_im, cr * b_im + ci * b_re


def _whole(a):
    return pl.BlockSpec(a.shape, lambda *_: (0,) * a.ndim)


def s5_prep(lam_re, lam_im, log_dt, b_re, b_im):
    def body(lr, li, ld, br, bi, ar_o, ai_o, bbr_o, bbi_o):
        ar, ai, bbr, bbi = _s5_discretize(lr[...], li[...], ld[...], br[...], bi[...])
        ar_o[...] = ar
        ai_o[...] = ai
        bbr_o[...] = bbr
        bbi_o[...] = bbi

    ins = [lam_re, lam_im, log_dt, b_re, b_im]
    outs = [lam_re, lam_im, b_re, b_im]
    return pl.pallas_call(
        body, name="s5_prep", in_specs=[_whole(a) for a in ins], out_specs=[_whole(a) for a in outs],
        out_shape=[jax.ShapeDtypeStruct(a.shape, F32) for a in outs],
        compiler_params=pltpu.CompilerParams(vmem_limit_bytes=VMEM_LIMIT),
    )(*ins)


def s5_prep_bwd(lam_re, lam_im, log_dt, b_re, b_im, d_ar, d_ai, d_bbr, d_bbi):
    def body(lr, li, ld, br, bi, dar, dai, dbbr, dbbi, *outs):
        _, vjp = jax.vjp(_s5_discretize, lr[...], li[...], ld[...], br[...], bi[...])
        for o, v in zip(outs, vjp((dar[...], dai[...], dbbr[...], dbbi[...]))):
            o[...] = v

    ins = [lam_re, lam_im, log_dt, b_re, b_im, d_ar, d_ai, d_bbr, d_bbi]
    return pl.pallas_call(
        body, name="s5_prep_bwd", in_specs=[_whole(a) for a in ins], out_specs=[_whole(a) for a in ins[:5]],
        out_shape=[jax.ShapeDtypeStruct(a.shape, F32) for a in ins[:5]],
        compiler_params=pltpu.CompilerParams(vmem_limit_bytes=VMEM_LIMIT),
    )(*ins)


def scan_lti(br, bi, ar, ai, reverse=False):
    shift = shift_up if reverse else shift_down
    seq = br.shape[0]
    d = 1
    while d < seq:
        sr, si = shift(br, d), shift(bi, d)
        br, bi = br + ar * sr - ai * si, bi + ar * si + ai * sr
        ar, ai = ar * ar - ai * ai, 2.0 * ar * ai
        d *= 2
    return br, bi


PHASES = SUBLANES


def _load_phases(ref):
    groups = ref.shape[0] // PHASES
    return jnp.concatenate([ref[pl.ds(r, groups, stride=PHASES), :] for r in range(PHASES)], axis=0)


def _store_phases(ref, x):
    groups = ref.shape[0] // PHASES
    for r in range(PHASES):
        ref[pl.ds(r, groups, stride=PHASES), :] = x[r * groups:(r + 1) * groups].astype(ref.dtype)


def _split_phases(x):
    groups = x.shape[0] // PHASES
    return [x[r * groups:(r + 1) * groups] for r in range(PHASES)]


def _prev_phased(x):
    ph = _split_phases(x)
    return jnp.concatenate([shift_down(ph[PHASES - 1], 1)] + ph[:PHASES - 1], axis=0)


def _cmul_add(xr, xi, ar, ai, yr, yi):
    return xr + ar * yr - ai * yi, xi + ar * yi + ai * yr


def scan_phases(br, bi, ar, ai, reverse=False):
    rs, im = _split_phases(br), _split_phases(bi)
    order = range(PHASES - 2, -1, -1) if reverse else range(1, PHASES)
    step = 1 if reverse else -1
    for r in order:
        rs[r], im[r] = _cmul_add(rs[r], im[r], ar, ai, rs[r + step], im[r + step])
    powers = [(ar, ai)]
    for _ in range(PHASES - 1):
        pr, pi = powers[-1]
        powers.append((pr * ar - pi * ai, pr * ai + pi * ar))
    end = 0 if reverse else PHASES - 1
    cr, ci = scan_lti(rs[end], im[end], *powers[PHASES - 1], reverse)
    shift = shift_up if reverse else shift_down
    inr, ini = shift(cr, 1), shift(ci, 1)
    for r in range(PHASES):
        if r == end:
            rs[r], im[r] = cr, ci
        else:
            pr, pi = powers[PHASES - 1 - r if reverse else r]
            rs[r], im[r] = _cmul_add(rs[r], im[r], pr, pi, inr, ini)
    return jnp.concatenate(rs, axis=0), jnp.concatenate(im, axis=0)


def _s5_block_diag(w, rows_first):
    g, a, b = w.shape
    w = w.reshape(S5_TILES, S5_TILE_GROUPS, a, b)
    eye = jnp.eye(S5_TILE_GROUPS, dtype=w.dtype)
    if rows_first:
        return jnp.einsum("tgab,gk->tgakb", w, eye).reshape(S5_TILES, S5_TILE_GROUPS * a, S5_TILE_GROUPS * b)
    return jnp.einsum("tgab,gk->tgbka", w, eye).reshape(S5_TILES, S5_TILE_GROUPS * b, S5_TILE_GROUPS * a)


def _s5_diag_blocks(w, a, b):
    w = w.reshape(S5_TILES, S5_TILE_GROUPS, a, S5_TILE_GROUPS, b)
    eye = jnp.eye(S5_TILE_GROUPS, dtype=w.dtype)
    return jnp.einsum("tgakb,gk->tgab", w, eye).reshape(S5_GROUPS, a, b)


def _gelu(x):
    return jax.nn.gelu(x)


def _s5_tile_specs():
    b_spec = pl.BlockSpec((1, LANES, S5_TILE_STATE), lambda c, b: (c, 0, 0))
    c_spec = pl.BlockSpec((1, S5_TILE_STATE, LANES), lambda c, b: (c, 0, 0))
    a_spec = pl.BlockSpec((1, 1, S5_TILE_STATE), lambda c, b: (c, 0, 0))
    return b_spec, c_spec, a_spec


def s5_fwd(x, g, p, bsz):
    t, d = x.shape
    seq = t // bsz
    h = rms_fwd(x, g, F32, "s5_rms").reshape(bsz, seq, d)
    ar, ai, bbr, bbi = s5_prep(p["lam_re"][..., None], p["lam_im"][..., None], p["log_dt"][:, None, None],
                               p["b_re"], p["b_im"])
    bdr = _s5_block_diag(bbr, False).astype(BF16)
    bdi = _s5_block_diag(bbi, False).astype(BF16)
    cdr = _s5_block_diag(p["c_re"], False).astype(BF16)
    cdi = _s5_block_diag(p["c_im"], False).astype(BF16)
    a_r = ar.reshape(S5_TILES, 1, S5_TILE_STATE)
    a_i = ai.reshape(S5_TILES, 1, S5_TILE_STATE)
    b_spec, c_spec, a_spec = _s5_tile_specs()

    def body(u_ref, dskip_ref, bdr, bdi, cdr, cdi, a_r, a_i, y_ref, yg_ref):
        u = _load_phases(u_ref)
        ub = u.astype(BF16)
        y = dskip_ref[...] * u
        for k in range(S5_TILE_STATE // LANES):
            sl = slice(k * LANES, (k + 1) * LANES)
            br = jnp.dot(ub, bdr[0][:, sl], preferred_element_type=F32)
            bi = jnp.dot(ub, bdi[0][:, sl], preferred_element_type=F32)
            sr, si = scan_phases(br, bi, a_r[0][:, sl], a_i[0][:, sl])
            y = y + jnp.dot(sr.astype(BF16), cdr[0][sl, :], preferred_element_type=F32)
            y = y - jnp.dot(si.astype(BF16), cdi[0][sl, :], preferred_element_type=F32)
        _store_phases(y_ref, y)
        _store_phases(yg_ref, _gelu(y))

    seq_spec = pl.BlockSpec((None, seq, LANES), lambda c, b: (b, 0, c))
    y, yg = pl.pallas_call(
        body, name="s5_core", grid=(S5_TILES, bsz),
        in_specs=[seq_spec, pl.BlockSpec((1, LANES), lambda c, b: (0, c)), b_spec, b_spec, c_spec, c_spec, a_spec, a_spec],
        out_specs=[seq_spec, seq_spec],
        out_shape=[jax.ShapeDtypeStruct((bsz, seq, d), F32)] * 2,
        compiler_params=_params("parallel", "parallel"),
    )(h, p["d"], bdr, bdi, cdr, cdi, a_r, a_i)
    yg = yg.reshape(t, d)
    zv = mm(yg, p["w_val"], name="s5_zv")
    zg = mm(yg, p["w_gate"], name="s5_zg")

    def gate(x, zv, zg, bv, bg):
        return x + (zv + bv) * sigmoid(zg + bg)

    out = rowwise(gate, [x, zv, zg], [p["b_out"][:, :d], p["b_out"][:, d:]], [(d, F32)], name="s5_gate")[0]
    return out, (x, h, y, yg, zv, zg, (ar, ai, bdr, bdi, cdr, cdi, a_r, a_i))


def s5_bwd(dy, saved, g, p):
    x, h, y, yg, zv, zg, (ar, ai, bdr, bdi, cdr, cdi, a_r, a_i) = saved
    bsz, seq, d = h.shape
    t = bsz * seq

    def dgate(dy, zv, zg, bv, bg):
        val = zv + bv
        sg = sigmoid(zg + bg)
        dzv = dy * sg
        dzg = dy * val * sg * (1.0 - sg)
        return dzv, dzg, jnp.sum(dzv, axis=0, keepdims=True), jnp.sum(dzg, axis=0, keepdims=True)

    dzv, dzg, dbv, dbg = rowwise(dgate, [dy, zv, zg], [p["b_out"][:, :d], p["b_out"][:, d:]],
                                 [(d, BF16), (d, BF16)], [(1, d), (1, d)], name="s5_dgate")
    d_wv = mm(yg, dzv, ta=True, name="s5_dwv")
    d_wg = mm(yg, dzg, ta=True, name="s5_dwg")
    dyg = mm(dzv, p["w_val"], tb=True, name="s5_dyg1")
    dyg = mm(dzg, p["w_gate"], tb=True, acc=dyg, name="s5_dyg2").reshape(bsz, seq, d)
    b_spec, c_spec, a_spec = _s5_tile_specs()
    tr = (((0,), (0,)), ((), ()))
    nt = (((1,), (1,)), ((), ()))

    def fn(c, u, y, dyg, dskip, bdr, bdi, cdr, cdi, a_r, a_i):
        _, gelu_vjp = jax.vjp(_gelu, y)
        dyy = gelu_vjp(dyg)[0]
        ddskip = jnp.sum(dyy * u, axis=0, keepdims=True)
        du = dyy * dskip
        dyb = dyy.astype(BF16)
        ub = u.astype(BF16)
        dcr, dci, dbr, dbi, dar, dai = [], [], [], [], [], []
        for k in range(S5_TILE_STATE // LANES):
            sl = slice(k * LANES, (k + 1) * LANES)
            akr, aki = a_r[0][:, sl], a_i[0][:, sl]
            br = jnp.dot(ub, bdr[0][:, sl], preferred_element_type=F32)
            bi = jnp.dot(ub, bdi[0][:, sl], preferred_element_type=F32)
            sr, si = scan_phases(br, bi, akr, aki)
            dcr.append(lax.dot_general(sr.astype(BF16), dyb, tr, preferred_element_type=F32))
            dci.append(-lax.dot_general(si.astype(BF16), dyb, tr, preferred_element_type=F32))
            gr = lax.dot_general(dyb, cdr[0][sl, :], nt, preferred_element_type=F32)
            gi = -lax.dot_general(dyb, cdi[0][sl, :], nt, preferred_element_type=F32)
            gr, gi = scan_phases(gr, gi, akr, -aki, reverse=True)
            spr, spi = _prev_phased(sr), _prev_phased(si)
            dar.append(jnp.sum(gr * spr + gi * spi, axis=0, keepdims=True))
            dai.append(jnp.sum(gi * spr - gr * spi, axis=0, keepdims=True))
            grb, gib = gr.astype(BF16), gi.astype(BF16)
            dbr.append(lax.dot_general(ub, grb, tr, preferred_element_type=F32))
            dbi.append(lax.dot_general(ub, gib, tr, preferred_element_type=F32))
            du = du + lax.dot_general(grb, bdr[0][:, sl], nt, preferred_element_type=F32)
            du = du + lax.dot_general(gib, bdi[0][:, sl], nt, preferred_element_type=F32)
        return (du, ddskip, jnp.concatenate(dbr, axis=1)[None], jnp.concatenate(dbi, axis=1)[None],
                jnp.concatenate(dcr, axis=0)[None], jnp.concatenate(dci, axis=0)[None],
                jnp.concatenate(dar, axis=1)[None], jnp.concatenate(dai, axis=1)[None])

    du, ddskip, dbdr, dbdi, dcdr, dcdi, dar, dai = _seqwise_w(
        fn, [h, y, dyg], [p["d"]], [(bdr, b_spec), (bdi, b_spec), (cdr, c_spec), (cdi, c_spec), (a_r, a_spec), (a_i, a_spec)],
        [(d, F32)], [(1, d)], ct=LANES, name="s5_dcore", phased=True,
        wreds=[((S5_TILES, LANES, S5_TILE_STATE), b_spec), ((S5_TILES, LANES, S5_TILE_STATE), b_spec),
               ((S5_TILES, S5_TILE_STATE, LANES), c_spec), ((S5_TILES, S5_TILE_STATE, LANES), c_spec),
               ((S5_TILES, 1, S5_TILE_STATE), a_spec), ((S5_TILES, 1, S5_TILE_STATE), a_spec)])
    d_bbr = _s5_diag_blocks(dbdr, S5_GROUP, S5_STATE).transpose(0, 2, 1)
    d_bbi = _s5_diag_blocks(dbdi, S5_GROUP, S5_STATE).transpose(0, 2, 1)
    d_cre = _s5_diag_blocks(dcdr, S5_STATE, S5_GROUP).transpose(0, 2, 1)
    d_cim = _s5_diag_blocks(dcdi, S5_STATE, S5_GROUP).transpose(0, 2, 1)
    d_lre, d_lim, d_ldt, d_bre, d_bim = s5_prep_bwd(
        p["lam_re"][..., None], p["lam_im"][..., None], p["log_dt"][:, None, None], p["b_re"], p["b_im"],
        dar.reshape(S5_GROUPS, S5_STATE, 1), dai.reshape(S5_GROUPS, S5_STATE, 1), d_bbr, d_bbi)
    dx, dg = rms_bwd(x, g, [du.reshape(t, d)], dy, "s5_drms")
    grads = dict(g=dg, lam_re=d_lre[..., 0], lam_im=d_lim[..., 0], log_dt=d_ldt[:, 0, 0], b_re=d_bre, b_im=d_bim,
                 c_re=d_cre, c_im=d_cim, d=ddskip, w_val=d_wv, w_gate=d_wg, b_out=jnp.concatenate([dbv, dbg], axis=1))
    return dx, grads


def _expm1_nonpos(x):
    u = jnp.exp(x)
    safe = (u - 1.0) * x / jnp.log(u)
    return jnp.where(u == 1.0, x, jnp.where(x < -20.0, -1.0, safe))


def _softplus(x):
    return jnp.maximum(x, 0.0) + jnp.log(1.0 + jnp.exp(-jnp.abs(x)))


def scan_ltv(a, b, reverse=False):
    shift = shift_up if reverse else shift_down
    seq = a.shape[0]
    d = 1
    while d < seq:
        b = b + a * shift(b, d)
        if 2 * d < seq:
            a = a * shift(a, d, 1.0)
        d *= 2
    return b


def _lru_gates(rec0, cw, cb, ba, bx, lam, wa, wx):
    rec = conv_fwd(rec0, cw, cb)
    recb = rec.astype(BF16)
    r = sigmoid(jnp.dot(recb, wa, preferred_element_type=F32) + ba)
    i = sigmoid(jnp.dot(recb, wx, preferred_element_type=F32) + bx)
    sp = _softplus(-lam)
    log_a = -LRU_C * r * sp
    a = jnp.exp(log_a)
    mult = jnp.sqrt(-_expm1_nonpos(2.0 * log_a))
    return rec, recb, r, i, sp, a, mult


def lru_fwd(x, g, p, bsz):
    t, d = x.shape
    seq = t // bsz
    xn = rms_fwd(x, g, BF16, "lru_rms")
    gb = mm(xn, p["w_gate"], name="lru_gb").reshape(bsz, seq, d)
    rec0 = mm(xn, p["w_rec"], name="lru_rec").reshape(bsz, seq, d)
    wspec = pl.BlockSpec((1, LRU_BLOCK, LRU_BLOCK), lambda c, b: (c, 0, 0))

    def fn(c, rec0, gb, cw, cb, ba, bx, lam, wa, wx):
        rec, _, _, i, _, a, mult = _lru_gates(rec0, cw, cb, ba, bx, lam, wa[0], wx[0])
        h = scan_ltv(a, mult * (i * rec))
        return _gelu(gb) * h

    y = _seqwise_w(fn, [rec0, gb], [p["conv_w"], p["conv_b"], p["b_a"], p["b_x"], p["lam"]],
                   [(p["w_a"], wspec), (p["w_x"], wspec)], [(d, BF16)], [], ct=LRU_BLOCK, name="lru_core")[0]
    y = y.reshape(t, d)
    out = mm(y, p["w_out"], acc=x, name="lru_out")
    return out, (x, xn, gb, rec0, y)


def lru_bwd(dy, saved, g, p):
    x, xn, gb, rec0, y = saved
    bsz, seq, d = gb.shape
    t = bsz * seq
    d_wout = mm(y, dy, ta=True, name="lru_dwout")
    dyy = mm(dy, p["w_out"], tb=True, name="lru_dy").reshape(bsz, seq, d)
    wspec = pl.BlockSpec((1, LRU_BLOCK, LRU_BLOCK), lambda c, b: (c, 0, 0))
    tr = (((0,), (0,)), ((), ()))
    nt = (((1,), (1,)), ((), ()))

    def fn(c, rec0, gb, dyy, cw, cb, ba, bx, lam, wa, wx):
        rec, recb, r, i, sp, a, mult = _lru_gates(rec0, cw, cb, ba, bx, lam, wa[0], wx[0])
        h = scan_ltv(a, mult * (i * rec))
        gg, gelu_vjp = jax.vjp(_gelu, gb)
        dgb = gelu_vjp(dyy * h)[0]
        gr = scan_ltv(shift_up(a, 1), dyy * gg, reverse=True)
        da = gr * shift_down(h, 1)
        dmult = gr * i * rec
        di = gr * mult * rec
        drec = gr * mult * i
        dla = da * a - dmult * (a * a) / mult
        dr = dla * (-LRU_C * sp)
        dlam = jnp.sum(dla * (-LRU_C * r), axis=0, keepdims=True) * (-sigmoid(-lam))
        dpa = dr * r * (1.0 - r)
        dpx = di * i * (1.0 - i)
        dpab, dpxb = dpa.astype(BF16), dpx.astype(BF16)
        dwa = lax.dot_general(recb, dpab, tr, preferred_element_type=F32)
        dwx = lax.dot_general(recb, dpxb, tr, preferred_element_type=F32)
        drec = drec + lax.dot_general(dpab, wa[0], nt, preferred_element_type=F32)
        drec = drec + lax.dot_general(dpxb, wx[0], nt, preferred_element_type=F32)
        drec0, dcw, dcb = conv_bwd(rec0, cw, drec)
        return (dgb, drec0, dcw, dcb, jnp.sum(dpa, axis=0, keepdims=True), jnp.sum(dpx, axis=0, keepdims=True), dlam,
                dwa[None], dwx[None])

    kw = p["conv_w"].shape[0]
    dgb, drec0, dcw, dcb, dba, dbx, dlam, dwa, dwx = _seqwise_w(
        fn, [rec0, gb, dyy], [p["conv_w"], p["conv_b"], p["b_a"], p["b_x"], p["lam"]],
        [(p["w_a"], wspec), (p["w_x"], wspec)], [(d, BF16), (d, BF16)],
        [(kw, d), (1, d), (1, d), (1, d), (1, d)], ct=LRU_BLOCK, name="lru_dcore",
        wreds=[(p["w_a"].shape, wspec), (p["w_x"].shape, wspec)])
    dgb = dgb.reshape(t, d)
    drec0 = drec0.reshape(t, d)
    d_wgate = mm(xn, dgb, ta=True, name="lru_dwgate")
    d_wrec = mm(xn, drec0, ta=True, name="lru_dwrec")
    dxn = mm(dgb, p["w_gate"], tb=True, name="lru_dxn1")
    dxn = mm(drec0, p["w_rec"], tb=True, acc=dxn, name="lru_dxn2")
    dx, dg = rms_bwd(x, g, [dxn], dy, "lru_drms")
    grads = dict(g=dg, w_gate=d_wgate, w_rec=d_wrec, conv_w=dcw, conv_b=dcb, w_a=dwa, b_a=dba, w_x=dwx, b_x=dbx,
                 lam=dlam, w_out=d_wout)
    return dx, grads


_NT = (((1,), (1,)), ((), ()))
_TN = (((0,), (0,)), ((), ()))


def _head_norm(x, g):
    lo = lax.broadcasted_iota(jnp.int32, x.shape, 1) < SB_HEAD_DIM
    x2 = x * x
    s_lo = jnp.sum(jnp.where(lo, x2, 0.0), axis=-1, keepdims=True)
    s_hi = jnp.sum(jnp.where(lo, 0.0, x2), axis=-1, keepdims=True)
    ms = jnp.where(lo, s_lo, s_hi) * (1.0 / SB_HEAD_DIM)
    return x * lax.rsqrt(ms + EPS) * g


def _log_sigmoid(z):
    return jnp.minimum(z, 0.0) - jnp.log(1.0 + jnp.exp(-jnp.abs(z)))


def _dot_split(x, m):
    hi = x.astype(BF16)
    lo = (x - hi.astype(F32)).astype(BF16)
    return jnp.dot(hi, m, preferred_element_type=F32) + jnp.dot(lo, m, preferred_element_type=F32)


def _tri(cmp, n):
    r = lax.broadcasted_iota(jnp.int32, (n, n), 0)
    c = lax.broadcasted_iota(jnp.int32, (n, n), 1)
    return cmp(r, c)


def sb_attn_fwd(q, k, v, qg, kg):
    bsz, seq, d = q.shape
    blk = min(ATT_BLOCK, seq)
    nq = seq // blk
    scale = 1.0 / math.sqrt(SB_HEAD_DIM)

    def body(q_ref, k_ref, v_ref, qg_ref, kg_ref, o_ref, tot_ref, qn, kn, vb):
        qn[...] = _head_norm(q_ref[...], qg_ref[...]).astype(BF16)
        kn[...] = _head_norm(k_ref[...], kg_ref[...]).astype(BF16)
        vb[...] = v_ref[...].astype(BF16)
        lane_lo = lax.broadcasted_iota(jnp.int32, (blk, LANES), 1) < SB_HEAD_DIM
        causal = _tri(lambda r, c: c < r, blk)
        upper = _tri(lambda r, c: r > c, blk).astype(BF16)

        def q_block(qi, _):
            rows = pl.ds(pl.multiple_of(qi * blk, blk), blk)
            q_all = qn[rows, :]
            zero = jnp.zeros((), BF16)
            qbs = (jnp.where(lane_lo, q_all, zero), jnp.where(lane_lo, zero, q_all))

            def block(j, state, masked):
                cols = pl.ds(pl.multiple_of(j * blk, blk), blk)
                kb, vv = kn[cols, :], vb[cols, :]
                zs = [lax.dot_general(qb, kb, _NT, preferred_element_type=F32) * scale for qb in qbs]
                lss = [_log_sigmoid(z) for z in zs]
                lgs = [ls - z for ls, z in zip(lss, zs)]
                if masked:
                    lgs = [jnp.where(causal, lg, 0.0) for lg in lgs]
                later = [_dot_split(lg, upper) for lg in lgs]
                atts = [jnp.exp(ls + carry + cs) for ls, (carry, _), cs in zip(lss, state, later)]
                if masked:
                    atts = [jnp.where(causal, att, 0.0) for att in atts]
                outs = [jnp.dot(att.astype(BF16), vv, preferred_element_type=F32) for att in atts]
                return tuple((carry + jnp.sum(lg, axis=1, keepdims=True), acc + out)
                             for (carry, acc), lg, out in zip(state, lgs, outs))

            init = (jnp.zeros((blk, 1), F32), jnp.zeros((blk, LANES), F32))
            state = block(qi, (init, init), True)
            state = lax.fori_loop(0, qi, lambda jj, s: block(qi - 1 - jj, s, False), state)
            (carry0, acc0), (carry1, acc1) = state
            o_ref[rows, :] = jnp.where(lane_lo, acc0, acc1).astype(o_ref.dtype)
            tot_ref[rows, :] = jnp.where(lane_lo, carry0, carry1)
            return 0

        lax.fori_loop(0, nq, q_block, 0)

    spec = pl.BlockSpec((None, seq, LANES), lambda b, c: (b, 0, c))
    gspec = pl.BlockSpec((1, LANES), lambda b, c: (0, 0))
    return pl.pallas_call(
        body, name="sb_attn_fwd", grid=(bsz, d // LANES), in_specs=[spec, spec, spec, gspec, gspec],
        out_specs=[spec, spec], out_shape=[jax.ShapeDtypeStruct((bsz, seq, d), BF16), jax.ShapeDtypeStruct((bsz, seq, d), F32)],
        scratch_shapes=[pltpu.VMEM((seq, LANES), BF16)] * 3,
        compiler_params=_params("parallel", "parallel"),
    )(q, k, v, qg, kg)


def sb_attn_bwd(q, k, v, qg, kg, tot, do):
    bsz, seq, d = q.shape
    blk = min(ATT_BLOCK, seq)
    nq = seq // blk
    scale = 1.0 / math.sqrt(SB_HEAD_DIM)

    def body(q_ref, k_ref, v_ref, qg_ref, kg_ref, tot_ref, do_ref, dq_ref, dk_ref, dv_ref, dqg_ref, dkg_ref,
             qn, kn, vb, dqn, dkn, dvv):
        qn[...] = _head_norm(q_ref[...], qg_ref[...]).astype(BF16)
        kn[...] = _head_norm(k_ref[...], kg_ref[...]).astype(BF16)
        vb[...] = v_ref[...].astype(BF16)
        dkn[...] = jnp.zeros_like(dkn)
        dvv[...] = jnp.zeros_like(dvv)
        lane_lo = lax.broadcasted_iota(jnp.int32, (blk, LANES), 1) < SB_HEAD_DIM
        causal = _tri(lambda r, c: c < r, blk)
        upto = _tri(lambda r, c: r <= c, blk).astype(BF16)
        before = _tri(lambda r, c: r < c, blk).astype(BF16)

        def q_block(qi, _):
            rows = pl.ds(pl.multiple_of(qi * blk, blk), blk)
            zero = jnp.zeros((), BF16)
            q_all, do_all, tot_all = qn[rows, :], do_ref[rows, :].astype(BF16), tot_ref[rows, :]
            heads = []
            for hm in (lane_lo, jnp.logical_not(lane_lo)):
                heads.append((jnp.where(hm, q_all, zero), jnp.where(hm, do_all, zero),
                              jnp.max(jnp.where(hm, tot_all, -jnp.inf), axis=1, keepdims=True)))

            def block(j, state, masked):
                cols = pl.ds(pl.multiple_of(j * blk, blk), blk)
                kb, vv = kn[cols, :], vb[cols, :]
                two = range(2)
                zs = [lax.dot_general(heads[h][0], kb, _NT, preferred_element_type=F32) * scale for h in two]
                datts = [lax.dot_general(heads[h][1], vv, _NT, preferred_element_type=F32) for h in two]
                lss = [_log_sigmoid(z) for z in zs]
                lgs = [ls - z for ls, z in zip(lss, zs)]
                if masked:
                    lgs = [jnp.where(causal, lg, 0.0) for lg in lgs]
                sofar = [_dot_split(lg, upto) for lg in lgs]
                atts = [jnp.exp(lss[h] + heads[h][2] - state[h][0] - sofar[h]) for h in two]
                if masked:
                    atts = [jnp.where(causal, att, 0.0) for att in atts]
                es = [att * datt for att, datt in zip(atts, datts)]
                earlier = [_dot_split(e, before) for e in es]
                dlgs = [state[h][1] + earlier[h] for h in two]
                if masked:
                    dlgs = [jnp.where(causal, dlg, 0.0) for dlg in dlgs]
                betas = [jnp.exp(ls) for ls in lss]
                dzbs = [((es[h] * (1.0 - betas[h]) - dlgs[h] * betas[h]) * scale).astype(BF16) for h in two]
                dvv[cols, :] += sum(lax.dot_general(atts[h].astype(BF16), heads[h][1], _TN, preferred_element_type=F32)
                                    for h in two)
                dkn[cols, :] += sum(lax.dot_general(dzbs[h], heads[h][0], _TN, preferred_element_type=F32) for h in two)
                dqs = [jnp.dot(dzb, kb, preferred_element_type=F32) for dzb in dzbs]
                return tuple((state[h][0] + jnp.sum(lgs[h], axis=1, keepdims=True),
                              state[h][1] + jnp.sum(es[h], axis=1, keepdims=True), state[h][2] + dqs[h]) for h in two)

            col0 = jnp.zeros((blk, 1), F32)
            init = (col0, col0, jnp.zeros((blk, LANES), F32))
            state = lax.fori_loop(0, qi, lambda j, s: block(j, s, False), (init, init))
            state = block(qi, state, True)
            dqn[rows, :] = jnp.where(lane_lo, state[0][2], state[1][2])
            return 0

        lax.fori_loop(0, nq, q_block, 0)

        def fold(x):
            return x + pltpu.roll(x, SB_HEAD_DIM, 1)

        _, q_vjp = jax.vjp(_head_norm, q_ref[...], qg_ref[...])
        dq, dqg = q_vjp(dqn[...])
        _, k_vjp = jax.vjp(_head_norm, k_ref[...], kg_ref[...])
        dk, dkg = k_vjp(dkn[...])
        dq_ref[...] = dq.astype(dq_ref.dtype)
        dk_ref[...] = dk.astype(dk_ref.dtype)
        dv_ref[...] = dvv[...].astype(dv_ref.dtype)
        first = jnp.logical_and(pl.program_id(0) == 0, pl.program_id(1) == 0)

        @pl.when(first)
        def _():
            dqg_ref[...] = fold(dqg)
            dkg_ref[...] = fold(dkg)

        @pl.when(jnp.logical_not(first))
        def _():
            dqg_ref[...] += fold(dqg)
            dkg_ref[...] += fold(dkg)

    spec = pl.BlockSpec((None, seq, LANES), lambda b, c: (b, 0, c))
    gspec = pl.BlockSpec((1, LANES), lambda b, c: (0, 0))
    act = jax.ShapeDtypeStruct((bsz, seq, d), BF16)
    gain = jax.ShapeDtypeStruct((1, LANES), F32)
    return pl.pallas_call(
        body, name="sb_attn_bwd", grid=(bsz, d // LANES), in_specs=[spec, spec, spec, gspec, gspec, spec, spec],
        out_specs=[spec, spec, spec, gspec, gspec], out_shape=[act, act, act, gain, gain],
        scratch_shapes=[pltpu.VMEM((seq, LANES), BF16)] * 3 + [pltpu.VMEM((seq, LANES), F32)] * 3,
        compiler_params=_params("arbitrary", "arbitrary"),
    )(q, k, v, qg, kg, tot, do)


def sb_fwd(x, g, p, bsz):
    t, d = x.shape
    seq = t // bsz
    xn = rms_fwd(x, g, BF16, "sb_rms")
    q = mm(xn, p["w_q"], name="sb_q").reshape(bsz, seq, d)
    k = mm(xn, p["w_k"], name="sb_k").reshape(bsz, seq, d)
    v = mm(xn, p["w_v"], name="sb_v").reshape(bsz, seq, d)
    qg = jnp.tile(p["q_g"], (1, 2))
    kg = jnp.tile(p["k_g"], (1, 2))
    o, tot = sb_attn_fwd(q, k, v, qg, kg)
    o = o.reshape(t, d)
    out = mm(o, p["w_o"], acc=x, name="sb_out")
    return out, (x, xn, q, k, v, o, tot)


def sb_bwd(dy, saved, g, p):
    x, xn, q, k, v, o, tot = saved
    bsz, seq, d = q.shape
    t = bsz * seq
    d_wo = mm(o, dy, ta=True, name="sb_dwo")
    do = mm(dy, p["w_o"], tb=True, out_dtype=BF16, name="sb_do").reshape(bsz, seq, d)
    qg = jnp.tile(p["q_g"], (1, 2))
    kg = jnp.tile(p["k_g"], (1, 2))
    dq, dk, dv, dqg, dkg = sb_attn_bwd(q, k, v, qg, kg, tot, do)
    dq, dk, dv = dq.reshape(t, d), dk.reshape(t, d), dv.reshape(t, d)
    d_wq = mm(xn, dq, ta=True, name="sb_dwq")
    d_wk = mm(xn, dk, ta=True, name="sb_dwk")
    d_wv = mm(xn, dv, ta=True, name="sb_dwv")
    dxn = mm(dq, p["w_q"], tb=True, name="sb_dxn1")
    dxn = mm(dk, p["w_k"], tb=True, acc=dxn, name="sb_dxn2")
    dxn = mm(dv, p["w_v"], tb=True, acc=dxn, name="sb_dxn3")
    dx, dg = rms_bwd(x, g, [dxn], dy, "sb_drms")
    grads = dict(g=dg, w_q=d_wq, w_k=d_wk, w_v=d_wv, w_o=d_wo, q_g=dqg[:, :SB_HEAD_DIM], k_g=dkg[:, :SB_HEAD_DIM])
    return dx, grads


def loss_head(y, target):
    d = y.shape[1]

    def fn(y, tgt):
        err = y - tgt
        return err * (1.0 / d), jnp.sum(err * err, axis=0, keepdims=True)

    dy, sq = rowwise(fn, [y, target], [], [(d, F32)], [(1, d)], name="loss_head")
    return sq, dy


def adamw(parts, w, m, v, name):
    n, r, c = parts.shape
    tr = _pick(r, max(16, (1 << 20) // (c * n)), 16 if parts.dtype == BF16 else 8)

    def body(p_ref, w_ref, m_ref, v_ref, g_ref, d_ref, nm_ref, nv_ref):
        g = p_ref[0].astype(F32)
        for i in range(1, n):
            g = g + p_ref[i].astype(F32)
        nm = ADAM_B1 * m_ref[...] + (1.0 - ADAM_B1) * g
        nv = ADAM_B2 * v_ref[...] + (1.0 - ADAM_B2) * (g * g)
        m_hat = nm / (1.0 - ADAM_B1 ** ADAM_STEP)
        v_hat = nv / (1.0 - ADAM_B2 ** ADAM_STEP)
        g_ref[...] = g
        d_ref[...] = -ADAM_LR * (m_hat / (jnp.sqrt(v_hat) + ADAM_EPS) + ADAM_WD * w_ref[...])
        nm_ref[...] = nm
        nv_ref[...] = nv

    spec = pl.BlockSpec((tr, c), lambda i: (i, 0))
    return pl.pallas_call(
        body, name=name, grid=(r // tr,), in_specs=[pl.BlockSpec((n, tr, c), lambda i: (0, i, 0)), spec, spec, spec],
        out_specs=[spec] * 4, out_shape=[jax.ShapeDtypeStruct((r, c), F32)] * 4,
        compiler_params=_params("parallel"),
    )(parts, w, m, v)


_HBM = pl.BlockSpec(memory_space=pltpu.HBM)


def _mesh_pos():
    return lax.axis_index("x"), lax.axis_index("y"), lax.axis_index("c")


def all_gather(shards, name):
    n = len(shards)

    def body(*refs):
        ins, outs = refs[:n], refs[n:2 * n]
        send_sems, recv_sems, local_sems = refs[2 * n:]
        x, y, c = _mesh_pos()
        me, sibling = (x, y, c), (x, y, 1 - c)
        chips = [(1 - x, y), (x, 1 - y), (1 - x, 1 - y)]

        def copy(a, k, block, to, src=None):
            px, py, pc = block
            dst = outs[a].at[4 * px + 2 * py + pc]
            return pltpu.make_async_remote_copy(
                src_ref=dst if src is None else src, dst_ref=dst, send_sem=send_sems.at[a, k],
                recv_sem=recv_sems.at[a, k], device_id=to, device_id_type=MESH)

        mine = [pltpu.make_async_copy(ins[a], outs[a].at[4 * x + 2 * y + c], local_sems.at[a]) for a in range(n)]
        for cp in mine:
            cp.start()
        first = []
        for a in range(n):
            first.append(copy(a, 0, me, sibling, src=ins[a]))
            first += [copy(a, 1 + j, me, (*chip, c), src=ins[a]) for j, chip in enumerate(chips)]
        for cp in first:
            cp.start()
        passed = []
        for a in range(n):
            for j, chip in enumerate(chips):
                copy(a, 1 + j, (*chip, c), me).wait_recv()
                cp = copy(a, 4 + j, (*chip, c), sibling)
                cp.start()
                passed.append(cp)
        for a in range(n):
            copy(a, 0, sibling, me).wait_recv()
            for j, chip in enumerate(chips):
                copy(a, 4 + j, (*chip, 1 - c), me).wait_recv()
        for cp in first + passed:
            cp.wait_send()
        for cp in mine:
            cp.wait()

    return pl.pallas_call(
        body, name=name, in_specs=[_HBM] * n, out_specs=[_HBM] * n,
        out_shape=[jax.ShapeDtypeStruct((N_DEV, *s.shape), s.dtype) for s in shards],
        scratch_shapes=[pltpu.SemaphoreType.DMA((n, 7)), pltpu.SemaphoreType.DMA((n, 7)), pltpu.SemaphoreType.DMA((n,))],
    )(*shards)


def exchange_cores(parts, name):
    n = len(parts)

    def body(*refs):
        ins, outs = refs[:n], refs[n:2 * n]
        send_sems, recv_sems = refs[2 * n:]
        x, y, c = _mesh_pos()
        copies = []
        for a in range(n):
            for p in range(4):
                copies.append(pltpu.make_async_remote_copy(
                    src_ref=ins[a].at[p, 1 - c], dst_ref=outs[a].at[p], send_sem=send_sems.at[a, p],
                    recv_sem=recv_sems.at[a, p], device_id=(x, y, 1 - c), device_id_type=MESH))
        for cp in copies:
            cp.start()
        for cp in copies:
            cp.wait()

    return pl.pallas_call(
        body, name=name, in_specs=[_HBM] * n, out_specs=[_HBM] * n,
        out_shape=[jax.ShapeDtypeStruct((4, *s.shape[2:]), s.dtype) for s in parts],
        scratch_shapes=[pltpu.SemaphoreType.DMA((n, 4)), pltpu.SemaphoreType.DMA((n, 4))],
    )(*parts)


def exchange_chips(parts, name):
    n = len(parts)

    def body(*refs):
        ins, outs = refs[:n], refs[n:2 * n]
        send_sems, recv_sems, local_sems = refs[2 * n:]
        x, y, c = _mesh_pos()
        mine = 2 * x + y
        local = [pltpu.make_async_copy(ins[a].at[mine], outs[a].at[mine], local_sems.at[a]) for a in range(n)]
        for cp in local:
            cp.start()
        copies = []
        for a in range(n):
            for r in range(1, 4):
                qx = 1 - x if r & 2 else x
                qy = 1 - y if r & 1 else y
                copies.append(pltpu.make_async_remote_copy(
                    src_ref=ins[a].at[2 * qx + qy], dst_ref=outs[a].at[mine], send_sem=send_sems.at[a, r - 1],
                    recv_sem=recv_sems.at[a, r - 1], device_id=(qx, qy, c), device_id_type=MESH))
        for cp in copies:
            cp.start()
        for cp in copies:
            cp.wait()
        for cp in local:
            cp.wait()

    return pl.pallas_call(
        body, name=name, in_specs=[_HBM] * n, out_specs=[_HBM] * n,
        out_shape=[jax.ShapeDtypeStruct(s.shape, s.dtype) for s in parts],
        scratch_shapes=[pltpu.SemaphoreType.DMA((n, 3)), pltpu.SemaphoreType.DMA((n, 3)), pltpu.SemaphoreType.DMA((n,))],
    )(*parts)


def add_own(core, parts, recv, out_dtype, name):
    _, _, r, c = parts.shape
    tr = _pick(r, max(16, (1 << 19) // c), 16)

    def body(core_ref, p_ref, r_ref, o_ref):
        o_ref[...] = (p_ref[...] + r_ref[...]).astype(out_dtype)

    grid_spec = pltpu.PrefetchScalarGridSpec(
        num_scalar_prefetch=1, grid=(4, r // tr),
        in_specs=[pl.BlockSpec((None, None, tr, c), lambda p, i, core_ref: (p, core_ref[0], i, 0)),
                  pl.BlockSpec((None, tr, c), lambda p, i, core_ref: (p, i, 0))],
        out_specs=pl.BlockSpec((None, tr, c), lambda p, i, core_ref: (p, i, 0)))
    return pl.pallas_call(
        body, name=name, grid_spec=grid_spec, out_shape=jax.ShapeDtypeStruct((4, r, c), out_dtype),
        compiler_params=_params("parallel", "parallel"),
    )(core, parts, recv)


PARAMS = {
    "norm_mix_g": ((4, 1024), None), "norm_ffn_g": ((4, 1024), None),
    "pool_w": ((1, 4, 256, 256), 2), "pool_b": ((1, 1024), None), "pool_scale": ((1, 1024), None),
    "s5_lam_re": ((1, 64, 64), None), "s5_lam_im": ((1, 64, 64), None), "s5_log_dt": ((1, 64), None),
    "s5_b_re": ((1, 64, 64, 16), None), "s5_b_im": ((1, 64, 64, 16), None),
    "s5_c_re": ((1, 64, 16, 64), None), "s5_c_im": ((1, 64, 16, 64), None),
    "s5_d": ((1, 1024), 1), "s5_w_out": ((1, 1024, 2048), 2), "s5_b_out": ((1, 2048), 1),
    "lru_w_in": ((1, 1024, 2048), 2), "lru_conv_w": ((1, 4, 1024), 2), "lru_conv_b": ((1, 1024), 1),
    "lru_w_a": ((1, 4, 256, 256), 2), "lru_b_a": ((1, 1024), 1), "lru_w_x": ((1, 4, 256, 256), 2),
    "lru_b_x": ((1, 1024), 1), "lru_lam": ((1, 1024), 1), "lru_w_out": ((1, 1024, 1024), 1),
    "sb_w_qkv": ((1, 1024, 3072), 2), "sb_q_g": ((1, 64), None), "sb_k_g": ((1, 64), None),
    "sb_w_o": ((1, 1024, 1024), 1),
    "ffn_w_in": ((4, 1024, 5632), 2), "ffn_conv_w": ((4, 3, 5632), 2), "ffn_conv_b": ((4, 5632), None),
    "ffn_w_out": ((4, 2816, 1024), 1),
}
NAMES = list(PARAMS)
BIG = ["s5_w_out", "lru_w_in", "lru_w_out", "sb_w_qkv", "sb_w_o", "ffn_w_in", "ffn_w_out"]
SMALL = [n for n in NAMES if PARAMS[n][1] is not None and n not in BIG]
REPL = [n for n in NAMES if PARAMS[n][1] is None]


def _local_shape(name):
    shape, ax = PARAMS[name]
    return tuple(s // N_DEV if i == ax else s for i, s in enumerate(shape))


def _to_natural(gathered, name):
    shape, ax = PARAMS[name]
    return jnp.moveaxis(gathered, 0, ax).reshape(shape)


def _to_shards(nat, name):
    shape, ax = PARAMS[name]
    split = shape[:ax] + (N_DEV, shape[ax] // N_DEV) + shape[ax + 1:]
    return jnp.moveaxis(nat.reshape(split), ax, 0)


def _rows2d(shape):
    return (math.prod(shape[:-1]), shape[-1])


def _pack(arrays, lead=()):
    flat = jnp.concatenate([a.reshape(*lead, -1) for a in arrays], axis=-1)
    size = flat.shape[-1]
    rows = -(-size // (8 * LANES)) * 8
    flat = jnp.pad(flat, [(0, 0)] * len(lead) + [(0, rows * LANES - size)])
    return flat.reshape(*lead, rows, LANES)


def _unpack(packed, shapes, lead=()):
    flat = packed.reshape(*lead, -1)
    out, off = [], 0
    for s in shapes:
        size = math.prod(s)
        out.append(flat[..., off:off + size].reshape(*lead, *s))
        off += size
    return out


def kernel(x, norm_mix_g, norm_ffn_g, pool_w, pool_b, pool_scale, s5_lam_re, s5_lam_im, s5_log_dt, s5_b_re, s5_b_im, s5_c_re, s5_c_im, s5_d, s5_w_out, s5_b_out, lru_w_in, lru_conv_w, lru_conv_b, lru_w_a, lru_b_a, lru_w_x, lru_b_x, lru_lam, lru_w_out, sb_w_qkv, sb_q_g, sb_k_g, sb_w_o, ffn_w_in, ffn_conv_w, ffn_conv_b, ffn_w_out, loss_target, m_norm_mix_g, m_norm_ffn_g, m_pool_w, m_pool_b, m_pool_scale, m_s5_lam_re, m_s5_lam_im, m_s5_log_dt, m_s5_b_re, m_s5_b_im, m_s5_c_re, m_s5_c_im, m_s5_d, m_s5_w_out, m_s5_b_out, m_lru_w_in, m_lru_conv_w, m_lru_conv_b, m_lru_w_a, m_lru_b_a, m_lru_w_x, m_lru_b_x, m_lru_lam, m_lru_w_out, m_sb_w_qkv, m_sb_q_g, m_sb_k_g, m_sb_w_o, m_ffn_w_in, m_ffn_conv_w, m_ffn_conv_b, m_ffn_w_out, v_norm_mix_g, v_norm_ffn_g, v_pool_w, v_pool_b, v_pool_scale, v_s5_lam_re, v_s5_lam_im, v_s5_log_dt, v_s5_b_re, v_s5_b_im, v_s5_c_re, v_s5_c_im, v_s5_d, v_s5_w_out, v_s5_b_out, v_lru_w_in, v_lru_conv_w, v_lru_conv_b, v_lru_w_a, v_lru_b_a, v_lru_w_x, v_lru_b_x, v_lru_lam, v_lru_w_out, v_sb_w_qkv, v_sb_q_g, v_sb_k_g, v_sb_w_o, v_ffn_w_in, v_ffn_conv_w, v_ffn_conv_b, v_ffn_w_out):
    w = dict(zip(NAMES, (norm_mix_g, norm_ffn_g, pool_w, pool_b, pool_scale, s5_lam_re, s5_lam_im, s5_log_dt, s5_b_re, s5_b_im, s5_c_re, s5_c_im, s5_d, s5_w_out, s5_b_out, lru_w_in, lru_conv_w, lru_conv_b, lru_w_a, lru_b_a, lru_w_x, lru_b_x, lru_lam, lru_w_out, sb_w_qkv, sb_q_g, sb_k_g, sb_w_o, ffn_w_in, ffn_conv_w, ffn_conv_b, ffn_w_out)))
    m = dict(zip(NAMES, (m_norm_mix_g, m_norm_ffn_g, m_pool_w, m_pool_b, m_pool_scale, m_s5_lam_re, m_s5_lam_im, m_s5_log_dt, m_s5_b_re, m_s5_b_im, m_s5_c_re, m_s5_c_im, m_s5_d, m_s5_w_out, m_s5_b_out, m_lru_w_in, m_lru_conv_w, m_lru_conv_b, m_lru_w_a, m_lru_b_a, m_lru_w_x, m_lru_b_x, m_lru_lam, m_lru_w_out, m_sb_w_qkv, m_sb_q_g, m_sb_k_g, m_sb_w_o, m_ffn_w_in, m_ffn_conv_w, m_ffn_conv_b, m_ffn_w_out)))
    v = dict(zip(NAMES, (v_norm_mix_g, v_norm_ffn_g, v_pool_w, v_pool_b, v_pool_scale, v_s5_lam_re, v_s5_lam_im, v_s5_log_dt, v_s5_b_re, v_s5_b_im, v_s5_c_re, v_s5_c_im, v_s5_d, v_s5_w_out, v_s5_b_out, v_lru_w_in, v_lru_conv_w, v_lru_conv_b, v_lru_w_a, v_lru_b_a, v_lru_w_x, v_lru_b_x, v_lru_lam, v_lru_w_out, v_sb_w_qkv, v_sb_q_g, v_sb_k_g, v_sb_w_o, v_ffn_w_in, v_ffn_conv_w, v_ffn_conv_b, v_ffn_w_out)))
    return train_step(x, loss_target, w, m, v)


def train_step(x, loss_target, w, m, v):
    bsz, seq, d = x.shape
    t = bsz * seq
    n_layers = PARAMS["norm_mix_g"][0][0]

    small_shapes = [_local_shape(n) for n in SMALL]
    gathered = all_gather([w[n].astype(BF16) for n in BIG] + [_pack([w[n] for n in SMALL])], "gather_weights")
    nat = {n: _to_natural(g, n) for n, g in zip(BIG, gathered[:-1])}
    for n, g in zip(SMALL, _unpack(gathered[-1], small_shapes, lead=(N_DEV,))):
        nat[n] = _to_natural(g, n)
    for n in REPL:
        nat[n] = w[n]

    hid = FFN_HIDDEN
    pool_p = (nat["pool_w"][0].astype(BF16), nat["pool_b"], nat["pool_scale"])
    s5_p = dict(lam_re=nat["s5_lam_re"][0], lam_im=nat["s5_lam_im"][0], log_dt=nat["s5_log_dt"][0],
                b_re=nat["s5_b_re"][0], b_im=nat["s5_b_im"][0], c_re=nat["s5_c_re"][0], c_im=nat["s5_c_im"][0],
                d=nat["s5_d"], w_val=nat["s5_w_out"][0, :, :d], w_gate=nat["s5_w_out"][0, :, d:], b_out=nat["s5_b_out"])
    lru_p = dict(w_gate=nat["lru_w_in"][0, :, :d], w_rec=nat["lru_w_in"][0, :, d:], conv_w=nat["lru_conv_w"][0],
                 conv_b=nat["lru_conv_b"], b_a=nat["lru_b_a"], b_x=nat["lru_b_x"], lam=nat["lru_lam"],
                 w_a=nat["lru_w_a"][0].astype(BF16), w_x=nat["lru_w_x"][0].astype(BF16), w_out=nat["lru_w_out"][0])
    sb_p = dict(w_q=nat["sb_w_qkv"][0, :, :d], w_k=nat["sb_w_qkv"][0, :, d:2 * d], w_v=nat["sb_w_qkv"][0, :, 2 * d:],
                w_o=nat["sb_w_o"][0], q_g=nat["sb_q_g"], k_g=nat["sb_k_g"])

    def ffn_p(li):
        return (nat["ffn_w_in"][li, :, :hid], nat["ffn_w_in"][li, :, hid:], nat["ffn_conv_w"][li],
                nat["ffn_conv_b"][li:li + 1], nat["ffn_w_out"][li])

    def gain(name, li):
        return nat[name][li:li + 1]

    h = x.reshape(t, d)
    h, pool_saved = pool_fwd(h, gain("norm_mix_g", 0), *pool_p, bsz)
    h, ffn0 = ffn_fwd(h, gain("norm_ffn_g", 0), *ffn_p(0), bsz, 0)
    h, s5_saved = s5_fwd(h, gain("norm_mix_g", 1), s5_p, bsz)
    h, ffn1 = ffn_fwd(h, gain("norm_ffn_g", 1), *ffn_p(1), bsz, 1)
    h, lru_saved = lru_fwd(h, gain("norm_mix_g", 2), lru_p, bsz)
    h, ffn2 = ffn_fwd(h, gain("norm_ffn_g", 2), *ffn_p(2), bsz, 2)
    h, sb_saved = sb_fwd(h, gain("norm_mix_g", 3), sb_p, bsz)
    h, ffn3 = ffn_fwd(h, gain("norm_ffn_g", 3), *ffn_p(3), bsz, 3)
    sq, dh = loss_head(h, loss_target.reshape(t, d))
    loss = lax.psum(0.5 * jnp.sum(sq) / d, ("x", "y", "c"))

    fg = [None] * n_layers
    dh, fg[3] = ffn_bwd(dh, ffn3, gain("norm_ffn_g", 3), *ffn_p(3), 3)
    dh, sb_g = sb_bwd(dh, sb_saved, gain("norm_mix_g", 3), sb_p)
    dh, fg[2] = ffn_bwd(dh, ffn2, gain("norm_ffn_g", 2), *ffn_p(2), 2)
    dh, lru_g = lru_bwd(dh, lru_saved, gain("norm_mix_g", 2), lru_p)
    dh, fg[1] = ffn_bwd(dh, ffn1, gain("norm_ffn_g", 1), *ffn_p(1), 1)
    dh, s5_g = s5_bwd(dh, s5_saved, gain("norm_mix_g", 1), s5_p)
    dh, fg[0] = ffn_bwd(dh, ffn0, gain("norm_ffn_g", 0), *ffn_p(0), 0)
    dh, pool_g = pool_bwd(dh, pool_saved, gain("norm_mix_g", 0), *pool_p)
    grad_x = dh.reshape(bsz, seq, d)

    part = {
        "norm_mix_g": jnp.concatenate([pool_g[0], s5_g["g"], lru_g["g"], sb_g["g"]], axis=0),
        "norm_ffn_g": jnp.concatenate([g[0] for g in fg], axis=0),
        "pool_w": pool_g[1][None], "pool_b": pool_g[2], "pool_scale": pool_g[3],
        "s5_lam_re": s5_g["lam_re"][None], "s5_lam_im": s5_g["lam_im"][None], "s5_log_dt": s5_g["log_dt"][None],
        "s5_b_re": s5_g["b_re"][None], "s5_b_im": s5_g["b_im"][None], "s5_c_re": s5_g["c_re"][None],
        "s5_c_im": s5_g["c_im"][None], "s5_d": s5_g["d"],
        "s5_w_out": jnp.concatenate([s5_g["w_val"], s5_g["w_gate"]], axis=1)[None], "s5_b_out": s5_g["b_out"],
        "lru_w_in": jnp.concatenate([lru_g["w_gate"], lru_g["w_rec"]], axis=1)[None], "lru_conv_w": lru_g["conv_w"][None],
        "lru_conv_b": lru_g["conv_b"], "lru_w_a": lru_g["w_a"][None], "lru_b_a": lru_g["b_a"],
        "lru_w_x": lru_g["w_x"][None], "lru_b_x": lru_g["b_x"], "lru_lam": lru_g["lam"], "lru_w_out": lru_g["w_out"][None],
        "sb_w_qkv": jnp.concatenate([sb_g["w_q"], sb_g["w_k"], sb_g["w_v"]], axis=1)[None],
        "sb_q_g": sb_g["q_g"], "sb_k_g": sb_g["k_g"], "sb_w_o": sb_g["w_o"][None],
        "ffn_w_in": jnp.stack([jnp.concatenate([g[1], g[2]], axis=1) for g in fg]),
        "ffn_conv_w": jnp.stack([g[3] for g in fg]), "ffn_conv_b": jnp.concatenate([g[4] for g in fg], axis=0),
        "ffn_w_out": jnp.stack([g[5] for g in fg]),
    }

    core = lax.axis_index("c").astype(jnp.int32).reshape(1)
    names2d = BIG + ["small"]
    shapes2d = [_rows2d(_local_shape(n)) for n in BIG]
    small_part = _pack([_to_shards(part[n], n) for n in SMALL], lead=(N_DEV,))
    shapes2d.append(small_part.shape[1:])
    by_dest = [_to_shards(part[n], n).reshape(4, 2, *s) for n, s in zip(BIG, shapes2d)] + [small_part.reshape(4, 2, *shapes2d[-1])]
    from_core = exchange_cores(by_dest, "reduce_cores")
    chip_sums = [add_own(core, p, r, F32 if n == "small" else BF16, f"add_{n}")
                 for n, p, r in zip(names2d, by_dest, from_core)]
    from_chips = exchange_chips(chip_sums, "reduce_chips")

    def local2d(tree, n):
        return tree[n].reshape(_rows2d(_local_shape(n)))

    out = {}
    for n, parts in zip(BIG, from_chips[:-1]):
        res = adamw(parts, local2d(w, n), local2d(m, n), local2d(v, n), f"adamw_{n}")
        out[n] = [r.reshape(_local_shape(n)) for r in res]
    small_res = adamw(from_chips[-1], *[_pack([tree[n] for n in SMALL]) for tree in (w, m, v)], "adamw_small")
    for i, res in enumerate(small_res):
        for n, r in zip(SMALL, _unpack(res, small_shapes)):
            out.setdefault(n, [None] * 4)[i] = r

    repl_shapes = [PARAMS[n][0] for n in REPL]
    repl_parts = all_gather([_pack([part[n] for n in REPL])], "gather_grads")[0]
    repl_res = adamw(repl_parts, *[_pack([tree[n] for n in REPL]) for tree in (w, m, v)], "adamw_replicated")
    for i, res in enumerate(repl_res):
        for n, r in zip(REPL, _unpack(res, repl_shapes)):
            out.setdefault(n, [None] * 4)[i] = r

    return (loss, grad_x, *[out[n][0] for n in NAMES], *[out[n][1] for n in NAMES], *[out[n][2] for n in NAMES],
            *[out[n][3] for n in NAMES])
```

```python
import functools
import math

import jax
import jax.numpy as jnp
from jax import lax
from jax.experimental import pallas as pl
from jax.experimental.pallas import tpu as pltpu

F32 = jnp.float32
BF16 = jnp.bfloat16
MESH = pl.DeviceIdType.MESH

N_DEV = 8
D_MODEL = 1024
EPS = 1e-6
POOL_GROUP = 256
S5_GROUPS, S5_GROUP, S5_STATE = 64, 16, 64
LRU_BLOCK = 256
LRU_C = 8.0
SB_HEAD_DIM = 64
ATT_BLOCK = 256
FFN_HIDDEN = 2816
ADAM_LR, ADAM_B1, ADAM_B2, ADAM_EPS, ADAM_WD, ADAM_STEP = 0.001, 0.9, 0.999, 1e-08, 0.01, 10
LANES = 128
SUBLANES = 8
VMEM_LIMIT = 56 * 1024 * 1024


def _pick(n, target, mult=LANES):
    best = None
    for d in range(mult, min(n, target) + 1, mult):
        if n % d == 0:
            best = d
    return best or n


def _params(*sem):
    return pltpu.CompilerParams(dimension_semantics=sem, vmem_limit_bytes=VMEM_LIMIT)


def mm(a, b, *, ta=False, tb=False, acc=None, out_dtype=F32, name):
    if ta:
        kdim, m = a.shape
    else:
        m, kdim = a.shape
    if tb:
        n, k2 = b.shape
    else:
        k2, n = b.shape
    assert kdim == k2, (a.shape, b.shape)
    tm = _pick(m, 1408 if ta else 512)
    tn = _pick(n, 1408)
    tk = _pick(kdim, 1024) if ta else kdim
    nk = kdim // tk
    a_spec = pl.BlockSpec((tk, tm), lambda j, i, k: (k, i)) if ta else pl.BlockSpec((tm, tk), lambda j, i, k: (i, k))
    b_spec = pl.BlockSpec((tn, tk), lambda j, i, k: (j, k)) if tb else pl.BlockSpec((tk, tn), lambda j, i, k: (k, j))
    o_spec = pl.BlockSpec((tm, tn), lambda j, i, k: (i, j))
    dims = (((0,) if ta else (1,), (1,) if tb else (0,)), ((), ()))
    has_acc = acc is not None

    def body(*refs):
        if has_acc:
            a_ref, b_ref, c_ref, o_ref, acc_ref = refs
        else:
            a_ref, b_ref, o_ref, acc_ref = refs
        k = pl.program_id(2)

        @pl.when(k == 0)
        def _():
            acc_ref[...] = c_ref[...].astype(F32) if has_acc else jnp.zeros_like(acc_ref)

        acc_ref[...] += lax.dot_general(a_ref[...].astype(BF16), b_ref[...].astype(BF16), dims,
                                        preferred_element_type=F32)

        @pl.when(k == nk - 1)
        def _():
            o_ref[...] = acc_ref[...].astype(out_dtype)

    ins = [a, b] + ([acc] if has_acc else [])
    specs = [a_spec, b_spec] + ([o_spec] if has_acc else [])
    return pl.pallas_call(
        body, name=name, grid=(n // tn, m // tm, nk), in_specs=specs, out_specs=o_spec,
        out_shape=jax.ShapeDtypeStruct((m, n), out_dtype), scratch_shapes=[pltpu.VMEM((tm, tn), F32)],
        compiler_params=_params("parallel", "parallel", "arbitrary"),
    )(*ins)


def rowwise(fn, tiled, bcast, outs, reds=(), *, tm=256, name):
    t = tiled[0].shape[0]
    tm = min(tm, t)
    assert t % tm == 0
    n_t, n_b, n_o, n_r = len(tiled), len(bcast), len(outs), len(reds)

    def body(*refs):
        vals = [r[...] for r in refs[:n_t + n_b]]
        res = fn(*vals)
        res = res if isinstance(res, tuple) else (res,)
        o_refs = refs[n_t + n_b:n_t + n_b + n_o]
        r_refs = refs[n_t + n_b + n_o:]
        for o_ref, v in zip(o_refs, res[:n_o]):
            o_ref[...] = v.astype(o_ref.dtype)
        first = pl.program_id(0) == 0
        for r_ref, v in zip(r_refs, res[n_o:]):
            @pl.when(first)
            def _(r_ref=r_ref, v=v):
                r_ref[...] = v.astype(F32)

            @pl.when(jnp.logical_not(first))
            def _(r_ref=r_ref, v=v):
                r_ref[...] += v.astype(F32)

    in_specs = [pl.BlockSpec((tm, a.shape[1]), lambda i: (i, 0)) for a in tiled]
    in_specs += [pl.BlockSpec(a.shape, lambda i, nd=a.ndim: (0,) * nd) for a in bcast]
    out_specs = [pl.BlockSpec((tm, c), lambda i: (i, 0)) for c, _ in outs]
    out_specs += [pl.BlockSpec(s, lambda i: (0, 0)) for s in reds]
    out_shape = [jax.ShapeDtypeStruct((t, c), dt) for c, dt in outs]
    out_shape += [jax.ShapeDtypeStruct(s, F32) for s in reds]
    res = pl.pallas_call(
        body, name=name, grid=(t // tm,), in_specs=in_specs, out_specs=out_specs, out_shape=out_shape,
        compiler_params=_params("arbitrary"),
    )(*tiled, *bcast)
    return res


def seqwise(fn, seqs, chans, outs, reds=(), *, ct, name, fulls=()):
    def split(x):
        return x if isinstance(x, tuple) else (x, 0)

    seqs = [split(s) for s in seqs]
    chans = [split(s) for s in chans]
    bsz, seq = seqs[0][0].shape[:2]
    n_c = outs[0][0] // ct if outs else reds[0][1] // ct
    n_s, n_ch, n_f, n_o, n_r = len(seqs), len(chans), len(fulls), len(outs), len(reds)

    def body(*refs):
        c = pl.program_id(0)
        vals = [r[...] for r in refs[:n_s + n_ch + n_f]]
        res = fn(c, *vals)
        res = res if isinstance(res, tuple) else (res,)
        o_refs = refs[n_s + n_ch + n_f:n_s + n_ch + n_f + n_o]
        r_refs = refs[n_s + n_ch + n_f + n_o:]
        for o_ref, v in zip(o_refs, res[:n_o]):
            o_ref[...] = v.astype(o_ref.dtype)
        first = pl.program_id(1) == 0
        for r_ref, v in zip(r_refs, res[n_o:]):
            @pl.when(first)
            def _(r_ref=r_ref, v=v):
                r_ref[...] = v.astype(F32)

            @pl.when(jnp.logical_not(first))
            def _(r_ref=r_ref, v=v):
                r_ref[...] += v.astype(F32)

    in_specs = [pl.BlockSpec((None, seq, ct), lambda c, b, off=off: (b, 0, c + off)) for _, off in seqs]
    in_specs += [pl.BlockSpec((a.shape[0], ct), lambda c, b, off=off: (0, c + off)) for a, off in chans]
    in_specs += [pl.BlockSpec(a.shape, lambda c, b, nd=a.ndim: (0,) * nd) for a in fulls]
    out_specs = [pl.BlockSpec((None, seq, ct), lambda c, b: (b, 0, c)) for _ in outs]
    out_specs += [pl.BlockSpec((r, ct), lambda c, b: (0, c)) for r, _ in reds]
    out_shape = [jax.ShapeDtypeStruct((bsz, seq, cc), dt) for cc, dt in outs]
    out_shape += [jax.ShapeDtypeStruct(s, F32) for s in reds]
    return pl.pallas_call(
        body, name=name, grid=(n_c, bsz), in_specs=in_specs, out_specs=out_specs, out_shape=out_shape,
        compiler_params=_params("arbitrary", "arbitrary"),
    )(*[a for a, _ in seqs], *[a for a, _ in chans], *fulls)


def _rows(x):
    return lax.broadcasted_iota(jnp.int32, x.shape, 0)


def shift_down(x, d, fill=0.0):
    if d == 0:
        return x
    s = x.shape[0]
    if d % SUBLANES == 0:
        return jnp.concatenate([jnp.full((d,) + x.shape[1:], fill, x.dtype), x[:s - d]], axis=0)
    rolled = pltpu.roll(x, d, 0)
    if d > SUBLANES or s <= SUBLANES:
        return jnp.where(_rows(x) >= d, rolled, fill)
    head = rolled[:SUBLANES]
    return jnp.concatenate([jnp.where(_rows(head) >= d, head, fill), rolled[SUBLANES:]], axis=0)


def shift_up(x, d, fill=0.0):
    if d == 0:
        return x
    s = x.shape[0]
    if d % SUBLANES == 0:
        return jnp.concatenate([x[d:], jnp.full((d,) + x.shape[1:], fill, x.dtype)], axis=0)
    rolled = pltpu.roll(x, s - d, 0)
    if d > SUBLANES or s <= SUBLANES:
        return jnp.where(_rows(x) < s - d, rolled, fill)
    tail = rolled[s - SUBLANES:]
    return jnp.concatenate([rolled[:s - SUBLANES], jnp.where(_rows(tail) < SUBLANES - d, tail, fill)], axis=0)


def conv_fwd(x, w, b):
    kw = w.shape[0]
    y = b + w[kw - 1:kw] * x
    for d in range(1, kw):
        y = y + w[kw - 1 - d:kw - d] * shift_down(x, d)
    return y


def conv_bwd(x, w, dy):
    kw = w.shape[0]
    dx = w[kw - 1:kw] * dy
    dws = [jnp.sum(dy * x, axis=0, keepdims=True)]
    for d in range(1, kw):
        dx = dx + w[kw - 1 - d:kw - d] * shift_up(dy, d)
        dws.append(jnp.sum(dy * shift_down(x, d), axis=0, keepdims=True))
    dw = jnp.concatenate(dws[::-1], axis=0)
    return dx, dw, jnp.sum(dy, axis=0, keepdims=True)


def sigmoid(x):
    return 0.5 * jnp.tanh(0.5 * x) + 0.5


def rms_fwd(x, g, out_dtype, name):
    def fn(x, g):
        return x * lax.rsqrt(jnp.mean(x * x, axis=-1, keepdims=True) + EPS) * g

    return rowwise(fn, [x], [g], [(x.shape[1], out_dtype)], name=name)[0]


def rms_bwd(x, g, dh_parts, dres, name):
    n_p = len(dh_parts)

    def fn(x, dres, *rest):
        dh = rest[0].astype(F32)
        for p in rest[1:n_p]:
            dh = dh + p.astype(F32)
        g = rest[n_p]
        r = lax.rsqrt(jnp.mean(x * x, axis=-1, keepdims=True) + EPS)
        xh = x * r
        dxh = dh * g
        dx = r * (dxh - xh * jnp.mean(dxh * xh, axis=-1, keepdims=True))
        return dres + dx, jnp.sum(dh * xh, axis=0, keepdims=True)

    return rowwise(fn, [x, dres, *dh_parts], [g], [(x.shape[1], F32)], [(1, x.shape[1])], name=name)


def ffn_fwd(x, g, w_val, w_gate, cw, cb, w_out, bsz, li):
    t, d = x.shape
    hid = w_val.shape[1]
    xn = rms_fwd(x, g, BF16, f"ffn{li}_rms")
    hv = mm(xn, w_val, name=f"ffn{li}_hv").reshape(bsz, t // bsz, hid)
    hg = mm(xn, w_gate, name=f"ffn{li}_hg").reshape(bsz, t // bsz, hid)
    ct = _pick(hid, 256)
    off = hid // ct

    def fn(c, hv, hg, wv, wg, bv, bg):
        val = conv_fwd(hv, wv, bv)
        gate = conv_fwd(hg, wg, bg)
        return gate * sigmoid(gate) * val

    act = seqwise(fn, [hv, hg], [cw, (cw, off), cb, (cb, off)], [(hid, BF16)], ct=ct, name=f"ffn{li}_gate")[0]
    act = act.reshape(t, hid)
    y = mm(act, w_out, acc=x, name=f"ffn{li}_out")
    return y, (x, xn, hv, hg, act)


def ffn_bwd(dy, saved, g, w_val, w_gate, cw, cb, w_out, li):
    x, xn, hv, hg, act = saved
    bsz, seq, hid = hv.shape
    t = bsz * seq
    d_wout = mm(act, dy, ta=True, name=f"ffn{li}_dwout")
    dact = mm(dy, w_out, tb=True, name=f"ffn{li}_dact").reshape(bsz, seq, hid)
    ct = _pick(hid, 256)
    off = hid // ct

    def fn(c, hv, hg, dact, wv, wg, bv, bg):
        val = conv_fwd(hv, wv, bv)
        gate = conv_fwd(hg, wg, bg)
        sg = sigmoid(gate)
        silu = gate * sg
        dval = dact * silu
        dgate = dact * val * (sg + silu * (1.0 - sg))
        dhv, dwv, dbv = conv_bwd(hv, wv, dval)
        dhg, dwg, dbg = conv_bwd(hg, wg, dgate)
        return dhv, dhg, dwv, dwg, dbv, dbg

    kw = cw.shape[0]
    dhv, dhg, dwv, dwg, dbv, dbg = seqwise(
        fn, [hv, hg, dact], [cw, (cw, off), cb, (cb, off)], [(hid, BF16), (hid, BF16)],
        [(kw, hid), (kw, hid), (1, hid), (1, hid)], ct=ct, name=f"ffn{li}_dgate")
    dhv = dhv.reshape(t, hid)
    dhg = dhg.reshape(t, hid)
    d_wv = mm(xn, dhv, ta=True, name=f"ffn{li}_dwv")
    d_wg = mm(xn, dhg, ta=True, name=f"ffn{li}_dwg")
    dxn = mm(dhv, w_val, tb=True, name=f"ffn{li}_dxn1")
    dxn = mm(dhg, w_gate, tb=True, acc=dxn, name=f"ffn{li}_dxn2")
    dx, dg = rms_bwd(x, g, [dxn], dy, f"ffn{li}_drms")
    d_cw = jnp.concatenate([dwv, dwg], axis=1)
    d_cb = jnp.concatenate([dbv, dbg], axis=1)
    return dx, (dg, d_wv, d_wg, d_cw, d_cb, d_wout)


def _window_sum(x, c, shift):
    s2 = x + shift(x, 1)
    s4 = s2 + shift(s2, 2)
    s8 = s4 + shift(s4, 4)
    s16 = s8 + shift(s8, 8)
    return jnp.where(c == 0, s2, jnp.where(c == 1, s4, jnp.where(c == 2, s8, s16)))


def _pool_inv_count(x, c):
    w = jnp.left_shift(2, c)
    return 1.0 / jnp.minimum(_rows(x) + 1, w).astype(F32)


def pool_fwd(x, g, w, b, scale, bsz):
    t, d = x.shape
    h = rms_fwd(x, g, F32, "pool_rms").reshape(bsz, t // bsz, d)

    def fn(c, h, x, b, scale, w):
        dd = _window_sum(h, c, shift_down) * _pool_inv_count(h, c) - h
        y = jnp.dot(dd.astype(BF16), w[0], preferred_element_type=F32) + b
        return x + scale * y

    wspec = pl.BlockSpec((1, POOL_GROUP, POOL_GROUP), lambda c, bb: (c, 0, 0))
    y = _seqwise_w(fn, [h, x.reshape(bsz, t // bsz, d)], [b, scale], [(w, wspec)], [(d, F32)], [], ct=POOL_GROUP,
                   name="pool_fwd")[0]
    return y.reshape(t, d), (x, h)


def pool_bwd(dy, saved, g, w, b, scale):
    x, h = saved
    bsz, seq, d = h.shape
    t = bsz * seq

    def fn(c, h, dy, b, scale, w):
        inv = _pool_inv_count(h, c)
        dd = _window_sum(h, c, shift_down) * inv - h
        ddb = dd.astype(BF16)
        y = jnp.dot(ddb, w[0], preferred_element_type=F32) + b
        dscale = jnp.sum(dy * y, axis=0, keepdims=True)
        dyy = dy * scale
        db = jnp.sum(dyy, axis=0, keepdims=True)
        dyb = dyy.astype(BF16)
        dw = lax.dot_general(ddb, dyb, (((0,), (0,)), ((), ())), preferred_element_type=F32)
        ddd = lax.dot_general(dyb, w[0], (((1,), (1,)), ((), ())), preferred_element_type=F32)
        dh = _window_sum(ddd * inv, c, shift_up) - ddd
        return dh, db, dscale, dw[None]

    wspec = pl.BlockSpec((1, POOL_GROUP, POOL_GROUP), lambda c, bb: (c, 0, 0))
    dh, db, dscale, dw = _seqwise_w(
        fn, [h, dy.reshape(bsz, seq, d)], [b, scale], [(w, wspec)], [(d, F32)], [(1, d), (1, d)], ct=POOL_GROUP,
        name="pool_bwd", wreds=[((4, POOL_GROUP, POOL_GROUP), wspec)])
    dx, dg = rms_bwd(x, g, [dh.reshape(t, d)], dy, "pool_drms")
    return dx, (dg, dw, db, dscale)


def _seqwise_w(fn, seqs, chans, blocked, outs, reds, *, ct, name, wreds=(), phased=False, side=()):
    bsz, seq = seqs[0].shape[:2]
    n_c = seqs[0].shape[2] // ct
    n_s = len(seqs)
    n_in = n_s + len(chans) + len(blocked)
    n_o, n_r, n_x = len(outs), len(reds) + len(wreds), len(side)

    def body(*refs):
        c = pl.program_id(0)
        if side:
            ends = n_in + n_x + n_o + n_r
            start, wait = _chip_exchange(refs[n_in:n_in + n_x], refs[ends:ends + n_x], *refs[ends + n_x:])
            pl.when(jnp.logical_and(c == 0, pl.program_id(1) == 0))(start)
        seq_vals = [_load_phases(r) if phased else r[...] for r in refs[:n_s]]
        res = fn(c, *seq_vals, *[r[...] for r in refs[n_s:n_in]])
        res = res if isinstance(res, tuple) else (res,)
        own = refs[n_in + n_x:n_in + n_x + n_o + n_r]
        for o_ref, v in zip(own[:n_o], res[:n_o]):
            if phased:
                _store_phases(o_ref, v)
            else:
                o_ref[...] = v.astype(o_ref.dtype)
        first = pl.program_id(1) == 0
        for r_ref, v in zip(own[n_o:], res[n_o:]):
            @pl.when(first)
            def _(r_ref=r_ref, v=v):
                r_ref[...] = v.astype(F32)

            @pl.when(jnp.logical_not(first))
            def _(r_ref=r_ref, v=v):
                r_ref[...] += v.astype(F32)
        if side:
            pl.when(jnp.logical_and(c == n_c - 1, pl.program_id(1) == bsz - 1))(wait)

    in_specs = [pl.BlockSpec((None, seq, ct), lambda c, b: (b, 0, c)) for _ in seqs]
    in_specs += [pl.BlockSpec((a.shape[0], ct), lambda c, b: (0, c)) for a in chans]
    in_specs += [spec for _, spec in blocked]
    in_specs += [_HBM] * n_x
    out_specs = [pl.BlockSpec((None, seq, ct), lambda c, b: (b, 0, c)) for _ in outs]
    out_specs += [pl.BlockSpec((r, ct), lambda c, b: (0, c)) for r, _ in reds]
    out_specs += [spec for _, spec in wreds]
    out_specs += [_HBM] * n_x
    out_shape = [jax.ShapeDtypeStruct((bsz, seq, cc), dt) for cc, dt in outs]
    out_shape += [jax.ShapeDtypeStruct(s, F32) for s in reds]
    out_shape += [jax.ShapeDtypeStruct(s, F32) for s, _ in wreds]
    out_shape += [jax.ShapeDtypeStruct(a.shape, a.dtype) for a in side]
    return pl.pallas_call(
        body, name=name, grid=(n_c, bsz), in_specs=in_specs, out_specs=out_specs, out_shape=out_shape,
        scratch_shapes=_chip_exchange_sems(n_x) if side else [],
        compiler_params=_params("arbitrary", "arbitrary"),
    )(*seqs, *chans, *[a for a, _ in blocked], *side)


S5_TILE_GROUPS = LANES // S5_GROUP
S5_TILES = S5_GROUPS // S5_TILE_GROUPS
S5_TILE_STATE = S5_TILE_GROUPS * S5_STATE


def _s5_discretize(lam_re, lam_im, log_dt, b_re, b_im):
    lr = jnp.minimum(lam_re, -1e-4)
    dt = jnp.exp(log_dt)
    er = jnp.exp(lr * dt)
    ar = er * jnp.cos(lam_im * dt)
    ai = er * jnp.sin(lam_im * dt)
    den = lr * lr + lam_im * lam_im
    cr = ((ar - 1.0) * lr + ai * lam_im) / den
    ci = (ai * lr - (ar - 1.0) * lam_im) / den
    return ar, ai, cr * b_re - ci * b_im, cr * b_im + ci * b_re


def _whole(a):
    return pl.BlockSpec(a.shape, lambda *_: (0,) * a.ndim)


def s5_prep(lam_re, lam_im, log_dt, b_re, b_im):
    def body(lr, li, ld, br, bi, ar_o, ai_o, bbr_o, bbi_o):
        ar, ai, bbr, bbi = _s5_discretize(lr[...], li[...], ld[...], br[...], bi[...])
        ar_o[...] = ar
        ai_o[...] = ai
        bbr_o[...] = bbr
        bbi_o[...] = bbi

    ins = [lam_re, lam_im, log_dt, b_re, b_im]
    outs = [lam_re, lam_im, b_re, b_im]
    return pl.pallas_call(
        body, name="s5_prep", in_specs=[_whole(a) for a in ins], out_specs=[_whole(a) for a in outs],
        out_shape=[jax.ShapeDtypeStruct(a.shape, F32) for a in outs],
        compiler_params=pltpu.CompilerParams(vmem_limit_bytes=VMEM_LIMIT),
    )(*ins)


def s5_prep_bwd(lam_re, lam_im, log_dt, b_re, b_im, d_ar, d_ai, d_bbr, d_bbi):
    def body(lr, li, ld, br, bi, dar, dai, dbbr, dbbi, *outs):
        _, vjp = jax.vjp(_s5_discretize, lr[...], li[...], ld[...], br[...], bi[...])
        for o, v in zip(outs, vjp((dar[...], dai[...], dbbr[...], dbbi[...]))):
            o[...] = v

    ins = [lam_re, lam_im, log_dt, b_re, b_im, d_ar, d_ai, d_bbr, d_bbi]
    return pl.pallas_call(
        body, name="s5_prep_bwd", in_specs=[_whole(a) for a in ins], out_specs=[_whole(a) for a in ins[:5]],
        out_shape=[jax.ShapeDtypeStruct(a.shape, F32) for a in ins[:5]],
        compiler_params=pltpu.CompilerParams(vmem_limit_bytes=VMEM_LIMIT),
    )(*ins)


def scan_lti(br, bi, ar, ai, reverse=False):
    shift = shift_up if reverse else shift_down
    seq = br.shape[0]
    d = 1
    while d < seq:
        sr, si = shift(br, d), shift(bi, d)
        br, bi = br + ar * sr - ai * si, bi + ar * si + ai * sr
        ar, ai = ar * ar - ai * ai, 2.0 * ar * ai
        d *= 2
    return br, bi


PHASES = SUBLANES


def _load_phases(ref):
    groups = ref.shape[0] // PHASES
    return jnp.concatenate([ref[pl.ds(r, groups, stride=PHASES), :] for r in range(PHASES)], axis=0)


def _store_phases(ref, x):
    groups = ref.shape[0] // PHASES
    for r in range(PHASES):
        ref[pl.ds(r, groups, stride=PHASES), :] = x[r * groups:(r + 1) * groups].astype(ref.dtype)


def _split_phases(x):
    groups = x.shape[0] // PHASES
    return [x[r * groups:(r + 1) * groups] for r in range(PHASES)]


def _prev_phased(x):
    ph = _split_phases(x)
    return jnp.concatenate([shift_down(ph[PHASES - 1], 1)] + ph[:PHASES - 1], axis=0)


def _cmul_add(xr, xi, ar, ai, yr, yi):
    return xr + ar * yr - ai * yi, xi + ar * yi + ai * yr


def scan_phases(br, bi, ar, ai, reverse=False):
    rs, im = _split_phases(br), _split_phases(bi)
    order = range(PHASES - 2, -1, -1) if reverse else range(1, PHASES)
    step = 1 if reverse else -1
    for r in order:
        rs[r], im[r] = _cmul_add(rs[r], im[r], ar, ai, rs[r + step], im[r + step])
    powers = [(ar, ai)]
    for _ in range(PHASES - 1):
        pr, pi = powers[-1]
        powers.append((pr * ar - pi * ai, pr * ai + pi * ar))
    end = 0 if reverse else PHASES - 1
    cr, ci = scan_lti(rs[end], im[end], *powers[PHASES - 1], reverse)
    shift = shift_up if reverse else shift_down
    inr, ini = shift(cr, 1), shift(ci, 1)
    for r in range(PHASES):
        if r == end:
            rs[r], im[r] = cr, ci
        else:
            pr, pi = powers[PHASES - 1 - r if reverse else r]
            rs[r], im[r] = _cmul_add(rs[r], im[r], pr, pi, inr, ini)
    return jnp.concatenate(rs, axis=0), jnp.concatenate(im, axis=0)


def _s5_block_diag(w, rows_first):
    g, a, b = w.shape
    w = w.reshape(S5_TILES, S5_TILE_GROUPS, a, b)
    eye = jnp.eye(S5_TILE_GROUPS, dtype=w.dtype)
    if rows_first:
        return jnp.einsum("tgab,gk->tgakb", w, eye).reshape(S5_TILES, S5_TILE_GROUPS * a, S5_TILE_GROUPS * b)
    return jnp.einsum("tgab,gk->tgbka", w, eye).reshape(S5_TILES, S5_TILE_GROUPS * b, S5_TILE_GROUPS * a)


def _s5_diag_blocks(w, a, b):
    w = w.reshape(S5_TILES, S5_TILE_GROUPS, a, S5_TILE_GROUPS, b)
    eye = jnp.eye(S5_TILE_GROUPS, dtype=w.dtype)
    return jnp.einsum("tgakb,gk->tgab", w, eye).reshape(S5_GROUPS, a, b)


def _gelu(x):
    return jax.nn.gelu(x)


def _s5_tile_specs():
    b_spec = pl.BlockSpec((1, LANES, S5_TILE_STATE), lambda c, b: (c, 0, 0))
    c_spec = pl.BlockSpec((1, S5_TILE_STATE, LANES), lambda c, b: (c, 0, 0))
    a_spec = pl.BlockSpec((1, 1, S5_TILE_STATE), lambda c, b: (c, 0, 0))
    return b_spec, c_spec, a_spec


def s5_fwd(x, g, p, bsz):
    t, d = x.shape
    seq = t // bsz
    h = rms_fwd(x, g, F32, "s5_rms").reshape(bsz, seq, d)
    ar, ai, bbr, bbi = s5_prep(p["lam_re"][..., None], p["lam_im"][..., None], p["log_dt"][:, None, None],
                               p["b_re"], p["b_im"])
    bdr = _s5_block_diag(bbr, False).astype(BF16)
    bdi = _s5_block_diag(bbi, False).astype(BF16)
    cdr = _s5_block_diag(p["c_re"], False).astype(BF16)
    cdi = _s5_block_diag(p["c_im"], False).astype(BF16)
    a_r = ar.reshape(S5_TILES, 1, S5_TILE_STATE)
    a_i = ai.reshape(S5_TILES, 1, S5_TILE_STATE)
    b_spec, c_spec, a_spec = _s5_tile_specs()

    def body(u_ref, dskip_ref, bdr, bdi, cdr, cdi, a_r, a_i, y_ref, yg_ref):
        u = _load_phases(u_ref)
        ub = u.astype(BF16)
        y = dskip_ref[...] * u
        for k in range(S5_TILE_STATE // LANES):
            sl = slice(k * LANES, (k + 1) * LANES)
            br = jnp.dot(ub, bdr[0][:, sl], preferred_element_type=F32)
            bi = jnp.dot(ub, bdi[0][:, sl], preferred_element_type=F32)
            sr, si = scan_phases(br, bi, a_r[0][:, sl], a_i[0][:, sl])
            y = y + jnp.dot(sr.astype(BF16), cdr[0][sl, :], preferred_element_type=F32)
            y = y - jnp.dot(si.astype(BF16), cdi[0][sl, :], preferred_element_type=F32)
        _store_phases(y_ref, y)
        _store_phases(yg_ref, _gelu(y))

    seq_spec = pl.BlockSpec((None, seq, LANES), lambda c, b: (b, 0, c))
    y, yg = pl.pallas_call(
        body, name="s5_core", grid=(S5_TILES, bsz),
        in_specs=[seq_spec, pl.BlockSpec((1, LANES), lambda c, b: (0, c)), b_spec, b_spec, c_spec, c_spec, a_spec, a_spec],
        out_specs=[seq_spec, seq_spec],
        out_shape=[jax.ShapeDtypeStruct((bsz, seq, d), F32)] * 2,
        compiler_params=_params("parallel", "parallel"),
    )(h, p["d"], bdr, bdi, cdr, cdi, a_r, a_i)
    yg = yg.reshape(t, d)
    zv = mm(yg, p["w_val"], name="s5_zv")
    zg = mm(yg, p["w_gate"], name="s5_zg")

    def gate(x, zv, zg, bv, bg):
        return x + (zv + bv) * sigmoid(zg + bg)

    out = rowwise(gate, [x, zv, zg], [p["b_out"][:, :d], p["b_out"][:, d:]], [(d, F32)], name="s5_gate")[0]
    return out, (x, h, y, yg, zv, zg, (ar, ai, bdr, bdi, cdr, cdi, a_r, a_i))


def s5_bwd(dy, saved, g, p, side=()):
    x, h, y, yg, zv, zg, (ar, ai, bdr, bdi, cdr, cdi, a_r, a_i) = saved
    bsz, seq, d = h.shape
    t = bsz * seq

    def dgate(dy, zv, zg, bv, bg):
        val = zv + bv
        sg = sigmoid(zg + bg)
        dzv = dy * sg
        dzg = dy * val * sg * (1.0 - sg)
        return dzv, dzg, jnp.sum(dzv, axis=0, keepdims=True), jnp.sum(dzg, axis=0, keepdims=True)

    dzv, dzg, dbv, dbg = rowwise(dgate, [dy, zv, zg], [p["b_out"][:, :d], p["b_out"][:, d:]],
                                 [(d, BF16), (d, BF16)], [(1, d), (1, d)], name="s5_dgate")
    d_wv = mm(yg, dzv, ta=True, name="s5_dwv")
    d_wg = mm(yg, dzg, ta=True, name="s5_dwg")
    dyg = mm(dzv, p["w_val"], tb=True, name="s5_dyg1")
    dyg = mm(dzg, p["w_gate"], tb=True, acc=dyg, name="s5_dyg2").reshape(bsz, seq, d)
    b_spec, c_spec, a_spec = _s5_tile_specs()
    tr = (((0,), (0,)), ((), ()))
    nt = (((1,), (1,)), ((), ()))

    def fn(c, u, y, dyg, dskip, bdr, bdi, cdr, cdi, a_r, a_i):
        _, gelu_vjp = jax.vjp(_gelu, y)
        dyy = gelu_vjp(dyg)[0]
        ddskip = jnp.sum(dyy * u, axis=0, keepdims=True)
        du = dyy * dskip
        dyb = dyy.astype(BF16)
        ub = u.astype(BF16)
        dcr, dci, dbr, dbi, dar, dai = [], [], [], [], [], []
        for k in range(S5_TILE_STATE // LANES):
            sl = slice(k * LANES, (k + 1) * LANES)
            akr, aki = a_r[0][:, sl], a_i[0][:, sl]
            br = jnp.dot(ub, bdr[0][:, sl], preferred_element_type=F32)
            bi = jnp.dot(ub, bdi[0][:, sl], preferred_element_type=F32)
            sr, si = scan_phases(br, bi, akr, aki)
            dcr.append(lax.dot_general(sr.astype(BF16), dyb, tr, preferred_element_type=F32))
            dci.append(-lax.dot_general(si.astype(BF16), dyb, tr, preferred_element_type=F32))
            gr = lax.dot_general(dyb, cdr[0][sl, :], nt, preferred_element_type=F32)
            gi = -lax.dot_general(dyb, cdi[0][sl, :], nt, preferred_element_type=F32)
            gr, gi = scan_phases(gr, gi, akr, -aki, reverse=True)
            spr, spi = _prev_phased(sr), _prev_phased(si)
            dar.append(jnp.sum(gr * spr + gi * spi, axis=0, keepdims=True))
            dai.append(jnp.sum(gi * spr - gr * spi, axis=0, keepdims=True))
            grb, gib = gr.astype(BF16), gi.astype(BF16)
            dbr.append(lax.dot_general(ub, grb, tr, preferred_element_type=F32))
            dbi.append(lax.dot_general(ub, gib, tr, preferred_element_type=F32))
            du = du + lax.dot_general(grb, bdr[0][:, sl], nt, preferred_element_type=F32)
            du = du + lax.dot_general(gib, bdi[0][:, sl], nt, preferred_element_type=F32)
        return (du, ddskip, jnp.concatenate(dbr, axis=1)[None], jnp.concatenate(dbi, axis=1)[None],
                jnp.concatenate(dcr, axis=0)[None], jnp.concatenate(dci, axis=0)[None],
                jnp.concatenate(dar, axis=1)[None], jnp.concatenate(dai, axis=1)[None])

    du, ddskip, dbdr, dbdi, dcdr, dcdi, dar, dai, *side_out = _seqwise_w(
        fn, [h, y, dyg], [p["d"]], [(bdr, b_spec), (bdi, b_spec), (cdr, c_spec), (cdi, c_spec), (a_r, a_spec), (a_i, a_spec)],
        [(d, F32)], [(1, d)], ct=LANES, name="s5_dcore", phased=True,
        wreds=[((S5_TILES, LANES, S5_TILE_STATE), b_spec), ((S5_TILES, LANES, S5_TILE_STATE), b_spec),
               ((S5_TILES, S5_TILE_STATE, LANES), c_spec), ((S5_TILES, S5_TILE_STATE, LANES), c_spec),
               ((S5_TILES, 1, S5_TILE_STATE), a_spec), ((S5_TILES, 1, S5_TILE_STATE), a_spec)], side=side)
    d_bbr = _s5_diag_blocks(dbdr, S5_GROUP, S5_STATE).transpose(0, 2, 1)
    d_bbi = _s5_diag_blocks(dbdi, S5_GROUP, S5_STATE).transpose(0, 2, 1)
    d_cre = _s5_diag_blocks(dcdr, S5_STATE, S5_GROUP).transpose(0, 2, 1)
    d_cim = _s5_diag_blocks(dcdi, S5_STATE, S5_GROUP).transpose(0, 2, 1)
    d_lre, d_lim, d_ldt, d_bre, d_bim = s5_prep_bwd(
        p["lam_re"][..., None], p["lam_im"][..., None], p["log_dt"][:, None, None], p["b_re"], p["b_im"],
        dar.reshape(S5_GROUPS, S5_STATE, 1), dai.reshape(S5_GROUPS, S5_STATE, 1), d_bbr, d_bbi)
    dx, dg = rms_bwd(x, g, [du.reshape(t, d)], dy, "s5_drms")
    grads = dict(g=dg, lam_re=d_lre[..., 0], lam_im=d_lim[..., 0], log_dt=d_ldt[:, 0, 0], b_re=d_bre, b_im=d_bim,
                 c_re=d_cre, c_im=d_cim, d=ddskip, w_val=d_wv, w_gate=d_wg, b_out=jnp.concatenate([dbv, dbg], axis=1))
    return dx, grads, side_out


def _expm1_nonpos(x):
    u = jnp.exp(x)
    safe = (u - 1.0) * x / jnp.log(u)
    return jnp.where(u == 1.0, x, jnp.where(x < -20.0, -1.0, safe))


def _softplus(x):
    return jnp.maximum(x, 0.0) + jnp.log(1.0 + jnp.exp(-jnp.abs(x)))


def scan_ltv(a, b, reverse=False):
    shift = shift_up if reverse else shift_down
    seq = a.shape[0]
    d = 1
    while d < seq:
        b = b + a * shift(b, d)
        if 2 * d < seq:
            a = a * shift(a, d, 1.0)
        d *= 2
    return b


def _lru_gates(rec0, cw, cb, ba, bx, lam, wa, wx):
    rec = conv_fwd(rec0, cw, cb)
    recb = rec.astype(BF16)
    r = sigmoid(jnp.dot(recb, wa, preferred_element_type=F32) + ba)
    i = sigmoid(jnp.dot(recb, wx, preferred_element_type=F32) + bx)
    sp = _softplus(-lam)
    log_a = -LRU_C * r * sp
    a = jnp.exp(log_a)
    mult = jnp.sqrt(-_expm1_nonpos(2.0 * log_a))
    return rec, recb, r, i, sp, a, mult


def lru_fwd(x, g, p, bsz):
    t, d = x.shape
    seq = t // bsz
    xn = rms_fwd(x, g, BF16, "lru_rms")
    gb = mm(xn, p["w_gate"], name="lru_gb").reshape(bsz, seq, d)
    rec0 = mm(xn, p["w_rec"], name="lru_rec").reshape(bsz, seq, d)
    wspec = pl.BlockSpec((1, LRU_BLOCK, LRU_BLOCK), lambda c, b: (c, 0, 0))

    def fn(c, rec0, gb, cw, cb, ba, bx, lam, wa, wx):
        rec, _, _, i, _, a, mult = _lru_gates(rec0, cw, cb, ba, bx, lam, wa[0], wx[0])
        h = scan_ltv(a, mult * (i * rec))
        return _gelu(gb) * h

    y = _seqwise_w(fn, [rec0, gb], [p["conv_w"], p["conv_b"], p["b_a"], p["b_x"], p["lam"]],
                   [(p["w_a"], wspec), (p["w_x"], wspec)], [(d, BF16)], [], ct=LRU_BLOCK, name="lru_core")[0]
    y = y.reshape(t, d)
    out = mm(y, p["w_out"], acc=x, name="lru_out")
    return out, (x, xn, gb, rec0, y)


def lru_bwd(dy, saved, g, p, side=()):
    x, xn, gb, rec0, y = saved
    bsz, seq, d = gb.shape
    t = bsz * seq
    d_wout = mm(y, dy, ta=True, name="lru_dwout")
    dyy = mm(dy, p["w_out"], tb=True, name="lru_dy").reshape(bsz, seq, d)
    wspec = pl.BlockSpec((1, LRU_BLOCK, LRU_BLOCK), lambda c, b: (c, 0, 0))
    tr = (((0,), (0,)), ((), ()))
    nt = (((1,), (1,)), ((), ()))

    def fn(c, rec0, gb, dyy, cw, cb, ba, bx, lam, wa, wx):
        rec, recb, r, i, sp, a, mult = _lru_gates(rec0, cw, cb, ba, bx, lam, wa[0], wx[0])
        h = scan_ltv(a, mult * (i * rec))
        gg, gelu_vjp = jax.vjp(_gelu, gb)
        dgb = gelu_vjp(dyy * h)[0]
        gr = scan_ltv(shift_up(a, 1), dyy * gg, reverse=True)
        da = gr * shift_down(h, 1)
        dmult = gr * i * rec
        di = gr * mult * rec
        drec = gr * mult * i
        dla = da * a - dmult * (a * a) / mult
        dr = dla * (-LRU_C * sp)
        dlam = jnp.sum(dla * (-LRU_C * r), axis=0, keepdims=True) * (-sigmoid(-lam))
        dpa = dr * r * (1.0 - r)
        dpx = di * i * (1.0 - i)
        dpab, dpxb = dpa.astype(BF16), dpx.astype(BF16)
        dwa = lax.dot_general(recb, dpab, tr, preferred_element_type=F32)
        dwx = lax.dot_general(recb, dpxb, tr, preferred_element_type=F32)
        drec = drec + lax.dot_general(dpab, wa[0], nt, preferred_element_type=F32)
        drec = drec + lax.dot_general(dpxb, wx[0], nt, preferred_element_type=F32)
        drec0, dcw, dcb = conv_bwd(rec0, cw, drec)
        return (dgb, drec0, dcw, dcb, jnp.sum(dpa, axis=0, keepdims=True), jnp.sum(dpx, axis=0, keepdims=True), dlam,
                dwa[None], dwx[None])

    kw = p["conv_w"].shape[0]
    dgb, drec0, dcw, dcb, dba, dbx, dlam, dwa, dwx, *side_out = _seqwise_w(
        fn, [rec0, gb, dyy], [p["conv_w"], p["conv_b"], p["b_a"], p["b_x"], p["lam"]],
        [(p["w_a"], wspec), (p["w_x"], wspec)], [(d, BF16), (d, BF16)],
        [(kw, d), (1, d), (1, d), (1, d), (1, d)], ct=LRU_BLOCK, name="lru_dcore",
        wreds=[(p["w_a"].shape, wspec), (p["w_x"].shape, wspec)], side=side)
    dgb = dgb.reshape(t, d)
    drec0 = drec0.reshape(t, d)
    d_wgate = mm(xn, dgb, ta=True, name="lru_dwgate")
    d_wrec = mm(xn, drec0, ta=True, name="lru_dwrec")
    dxn = mm(dgb, p["w_gate"], tb=True, name="lru_dxn1")
    dxn = mm(drec0, p["w_rec"], tb=True, acc=dxn, name="lru_dxn2")
    dx, dg = rms_bwd(x, g, [dxn], dy, "lru_drms")
    grads = dict(g=dg, w_gate=d_wgate, w_rec=d_wrec, conv_w=dcw, conv_b=dcb, w_a=dwa, b_a=dba, w_x=dwx, b_x=dbx,
                 lam=dlam, w_out=d_wout)
    return dx, grads, side_out


_NT = (((1,), (1,)), ((), ()))
_TN = (((0,), (0,)), ((), ()))


def _head_norm(x, g):
    lo = lax.broadcasted_iota(jnp.int32, x.shape, 1) < SB_HEAD_DIM
    x2 = x * x
    s_lo = jnp.sum(jnp.where(lo, x2, 0.0), axis=-1, keepdims=True)
    s_hi = jnp.sum(jnp.where(lo, 0.0, x2), axis=-1, keepdims=True)
    ms = jnp.where(lo, s_lo, s_hi) * (1.0 / SB_HEAD_DIM)
    return x * lax.rsqrt(ms + EPS) * g


def _log_sigmoid(z):
    return jnp.minimum(z, 0.0) - jnp.log(1.0 + jnp.exp(-jnp.abs(z)))


def _dot_split(x, m):
    hi = x.astype(BF16)
    lo = (x - hi.astype(F32)).astype(BF16)
    return jnp.dot(hi, m, preferred_element_type=F32) + jnp.dot(lo, m, preferred_element_type=F32)


def _tri(cmp, n):
    r = lax.broadcasted_iota(jnp.int32, (n, n), 0)
    c = lax.broadcasted_iota(jnp.int32, (n, n), 1)
    return cmp(r, c)


def sb_attn_fwd(q, k, v, qg, kg):
    bsz, seq, d = q.shape
    blk = min(ATT_BLOCK, seq)
    nq = seq // blk
    scale = 1.0 / math.sqrt(SB_HEAD_DIM)

    def body(q_ref, k_ref, v_ref, qg_ref, kg_ref, o_ref, tot_ref, qn, kn, vb):
        qn[...] = _head_norm(q_ref[...], qg_ref[...]).astype(BF16)
        kn[...] = _head_norm(k_ref[...], kg_ref[...]).astype(BF16)
        vb[...] = v_ref[...].astype(BF16)
        lane_lo = lax.broadcasted_iota(jnp.int32, (blk, LANES), 1) < SB_HEAD_DIM
        causal = _tri(lambda r, c: c < r, blk)
        upper = _tri(lambda r, c: r > c, blk).astype(BF16)

        def q_block(qi, _):
            rows = pl.ds(pl.multiple_of(qi * blk, blk), blk)
            q_all = qn[rows, :]
            zero = jnp.zeros((), BF16)
            qbs = (jnp.where(lane_lo, q_all, zero), jnp.where(lane_lo, zero, q_all))

            def block(j, state, masked):
                cols = pl.ds(pl.multiple_of(j * blk, blk), blk)
                kb, vv = kn[cols, :], vb[cols, :]
                zs = [lax.dot_general(qb, kb, _NT, preferred_element_type=F32) * scale for qb in qbs]
                lss = [_log_sigmoid(z) for z in zs]
                lgs = [ls - z for ls, z in zip(lss, zs)]
                if masked:
                    lgs = [jnp.where(causal, lg, 0.0) for lg in lgs]
                later = [_dot_split(lg, upper) for lg in lgs]
                atts = [jnp.exp(ls + carry + cs) for ls, (carry, _), cs in zip(lss, state, later)]
                if masked:
                    atts = [jnp.where(causal, att, 0.0) for att in atts]
                outs = [jnp.dot(att.astype(BF16), vv, preferred_element_type=F32) for att in atts]
                return tuple((carry + jnp.sum(lg, axis=1, keepdims=True), acc + out)
                             for (carry, acc), lg, out in zip(state, lgs, outs))

            init = (jnp.zeros((blk, 1), F32), jnp.zeros((blk, LANES), F32))
            state = block(qi, (init, init), True)
            state = lax.fori_loop(0, qi, lambda jj, s: block(qi - 1 - jj, s, False), state)
            (carry0, acc0), (carry1, acc1) = state
            o_ref[rows, :] = jnp.where(lane_lo, acc0, acc1).astype(o_ref.dtype)
            tot_ref[rows, :] = jnp.where(lane_lo, carry0, carry1)
            return 0

        lax.fori_loop(0, nq, q_block, 0)

    spec = pl.BlockSpec((None, seq, LANES), lambda b, c: (b, 0, c))
    gspec = pl.BlockSpec((1, LANES), lambda b, c: (0, 0))
    return pl.pallas_call(
        body, name="sb_attn_fwd", grid=(bsz, d // LANES), in_specs=[spec, spec, spec, gspec, gspec],
        out_specs=[spec, spec], out_shape=[jax.ShapeDtypeStruct((bsz, seq, d), BF16), jax.ShapeDtypeStruct((bsz, seq, d), F32)],
        scratch_shapes=[pltpu.VMEM((seq, LANES), BF16)] * 3,
        compiler_params=_params("parallel", "parallel"),
    )(q, k, v, qg, kg)


def sb_attn_bwd(q, k, v, qg, kg, tot, do, side=()):
    bsz, seq, d = q.shape
    blk = min(ATT_BLOCK, seq)
    nq = seq // blk
    scale = 1.0 / math.sqrt(SB_HEAD_DIM)
    n_x = len(side)
    n_in, n_out, n_scr = 7, 5, 6

    def body(*refs):
        q_ref, k_ref, v_ref, qg_ref, kg_ref, tot_ref, do_ref = refs[:n_in]
        dq_ref, dk_ref, dv_ref, dqg_ref, dkg_ref = refs[n_in + n_x:n_in + n_x + n_out]
        qn, kn, vb, dqn, dkn, dvv = refs[n_in + 2 * n_x + n_out:n_in + 2 * n_x + n_out + n_scr]
        if side:
            start, wait = _chip_exchange(refs[n_in:n_in + n_x], refs[n_in + n_x + n_out:n_in + 2 * n_x + n_out],
                                         *refs[n_in + 2 * n_x + n_out + n_scr:])
            pl.when(jnp.logical_and(pl.program_id(0) == 0, pl.program_id(1) == 0))(start)
        qn[...] = _head_norm(q_ref[...], qg_ref[...]).astype(BF16)
        kn[...] = _head_norm(k_ref[...], kg_ref[...]).astype(BF16)
        vb[...] = v_ref[...].astype(BF16)
        dkn[...] = jnp.zeros_like(dkn)
        dvv[...] = jnp.zeros_like(dvv)
        lane_lo = lax.broadcasted_iota(jnp.int32, (blk, LANES), 1) < SB_HEAD_DIM
        causal = _tri(lambda r, c: c < r, blk)
        upto = _tri(lambda r, c: r <= c, blk).astype(BF16)
        before = _tri(lambda r, c: r < c, blk).astype(BF16)

        def q_block(qi, _):
            rows = pl.ds(pl.multiple_of(qi * blk, blk), blk)
            zero = jnp.zeros((), BF16)
            q_all, do_all, tot_all = qn[rows, :], do_ref[rows, :].astype(BF16), tot_ref[rows, :]
            heads = []
            for hm in (lane_lo, jnp.logical_not(lane_lo)):
                heads.append((jnp.where(hm, q_all, zero), jnp.where(hm, do_all, zero),
                              jnp.max(jnp.where(hm, tot_all, -jnp.inf), axis=1, keepdims=True)))

            def block(j, state, masked):
                cols = pl.ds(pl.multiple_of(j * blk, blk), blk)
                kb, vv = kn[cols, :], vb[cols, :]
                two = range(2)
                zs = [lax.dot_general(heads[h][0], kb, _NT, preferred_element_type=F32) * scale for h in two]
                datts = [lax.dot_general(heads[h][1], vv, _NT, preferred_element_type=F32) for h in two]
                lss = [_log_sigmoid(z) for z in zs]
                lgs = [ls - z for ls, z in zip(lss, zs)]
                if masked:
                    lgs = [jnp.where(causal, lg, 0.0) for lg in lgs]
                sofar = [_dot_split(lg, upto) for lg in lgs]
                atts = [jnp.exp(lss[h] + heads[h][2] - state[h][0] - sofar[h]) for h in two]
                if masked:
                    atts = [jnp.where(causal, att, 0.0) for att in atts]
                es = [att * datt for att, datt in zip(atts, datts)]
                earlier = [_dot_split(e, before) for e in es]
                dlgs = [state[h][1] + earlier[h] for h in two]
                if masked:
                    dlgs = [jnp.where(causal, dlg, 0.0) for dlg in dlgs]
                betas = [jnp.exp(ls) for ls in lss]
                dzbs = [((es[h] * (1.0 - betas[h]) - dlgs[h] * betas[h]) * scale).astype(BF16) for h in two]
                dvv[cols, :] += sum(lax.dot_general(atts[h].astype(BF16), heads[h][1], _TN, preferred_element_type=F32)
                                    for h in two)
                dkn[cols, :] += sum(lax.dot_general(dzbs[h], heads[h][0], _TN, preferred_element_type=F32) for h in two)
                dqs = [jnp.dot(dzb, kb, preferred_element_type=F32) for dzb in dzbs]
                return tuple((state[h][0] + jnp.sum(lgs[h], axis=1, keepdims=True),
                              state[h][1] + jnp.sum(es[h], axis=1, keepdims=True), state[h][2] + dqs[h]) for h in two)

            col0 = jnp.zeros((blk, 1), F32)
            init = (col0, col0, jnp.zeros((blk, LANES), F32))
            state = lax.fori_loop(0, qi, lambda j, s: block(j, s, False), (init, init))
            state = block(qi, state, True)
            dqn[rows, :] = jnp.where(lane_lo, state[0][2], state[1][2])
            return 0

        lax.fori_loop(0, nq, q_block, 0)

        def fold(x):
            return x + pltpu.roll(x, SB_HEAD_DIM, 1)

        _, q_vjp = jax.vjp(_head_norm, q_ref[...], qg_ref[...])
        dq, dqg = q_vjp(dqn[...])
        _, k_vjp = jax.vjp(_head_norm, k_ref[...], kg_ref[...])
        dk, dkg = k_vjp(dkn[...])
        dq_ref[...] = dq.astype(dq_ref.dtype)
        dk_ref[...] = dk.astype(dk_ref.dtype)
        dv_ref[...] = dvv[...].astype(dv_ref.dtype)
        first = jnp.logical_and(pl.program_id(0) == 0, pl.program_id(1) == 0)

        @pl.when(first)
        def _():
            dqg_ref[...] = fold(dqg)
            dkg_ref[...] = fold(dkg)

        @pl.when(jnp.logical_not(first))
        def _():
            dqg_ref[...] += fold(dqg)
            dkg_ref[...] += fold(dkg)

        if side:
            pl.when(jnp.logical_and(pl.program_id(0) == bsz - 1, pl.program_id(1) == d // LANES - 1))(wait)

    spec = pl.BlockSpec((None, seq, LANES), lambda b, c: (b, 0, c))
    gspec = pl.BlockSpec((1, LANES), lambda b, c: (0, 0))
    act = jax.ShapeDtypeStruct((bsz, seq, d), BF16)
    gain = jax.ShapeDtypeStruct((1, LANES), F32)
    return pl.pallas_call(
        body, name="sb_attn_bwd", grid=(bsz, d // LANES),
        in_specs=[spec, spec, spec, gspec, gspec, spec, spec] + [_HBM] * n_x,
        out_specs=[spec, spec, spec, gspec, gspec] + [_HBM] * n_x,
        out_shape=[act, act, act, gain, gain] + [jax.ShapeDtypeStruct(a.shape, a.dtype) for a in side],
        scratch_shapes=([pltpu.VMEM((seq, LANES), BF16)] * 3 + [pltpu.VMEM((seq, LANES), F32)] * 3
                        + (_chip_exchange_sems(n_x) if side else [])),
        compiler_params=_params("arbitrary", "arbitrary"),
    )(q, k, v, qg, kg, tot, do, *side)


def sb_fwd(x, g, p, bsz):
    t, d = x.shape
    seq = t // bsz
    xn = rms_fwd(x, g, BF16, "sb_rms")
    q = mm(xn, p["w_q"], name="sb_q").reshape(bsz, seq, d)
    k = mm(xn, p["w_k"], name="sb_k").reshape(bsz, seq, d)
    v = mm(xn, p["w_v"], name="sb_v").reshape(bsz, seq, d)
    qg = jnp.tile(p["q_g"], (1, 2))
    kg = jnp.tile(p["k_g"], (1, 2))
    o, tot = sb_attn_fwd(q, k, v, qg, kg)
    o = o.reshape(t, d)
    out = mm(o, p["w_o"], acc=x, name="sb_out")
    return out, (x, xn, q, k, v, o, tot)


def sb_bwd(dy, saved, g, p, side=()):
    x, xn, q, k, v, o, tot = saved
    bsz, seq, d = q.shape
    t = bsz * seq
    d_wo = mm(o, dy, ta=True, name="sb_dwo")
    do = mm(dy, p["w_o"], tb=True, out_dtype=BF16, name="sb_do").reshape(bsz, seq, d)
    qg = jnp.tile(p["q_g"], (1, 2))
    kg = jnp.tile(p["k_g"], (1, 2))
    dq, dk, dv, dqg, dkg, *side_out = sb_attn_bwd(q, k, v, qg, kg, tot, do, side)
    dq, dk, dv = dq.reshape(t, d), dk.reshape(t, d), dv.reshape(t, d)
    d_wq = mm(xn, dq, ta=True, name="sb_dwq")
    d_wk = mm(xn, dk, ta=True, name="sb_dwk")
    d_wv = mm(xn, dv, ta=True, name="sb_dwv")
    dxn = mm(dq, p["w_q"], tb=True, name="sb_dxn1")
    dxn = mm(dk, p["w_k"], tb=True, acc=dxn, name="sb_dxn2")
    dxn = mm(dv, p["w_v"], tb=True, acc=dxn, name="sb_dxn3")
    dx, dg = rms_bwd(x, g, [dxn], dy, "sb_drms")
    grads = dict(g=dg, w_q=d_wq, w_k=d_wk, w_v=d_wv, w_o=d_wo, q_g=dqg[:, :SB_HEAD_DIM], k_g=dkg[:, :SB_HEAD_DIM])
    return dx, grads, side_out


def loss_head(y, target):
    d = y.shape[1]

    def fn(y, tgt):
        err = y - tgt
        return err * (1.0 / d), jnp.sum(err * err, axis=0, keepdims=True)

    dy, sq = rowwise(fn, [y, target], [], [(d, F32)], [(1, d)], name="loss_head")
    return sq, dy


def adamw(parts, w, m, v, name):
    n, r, c = parts.shape
    tr = _pick(r, max(16, (1 << 20) // (c * n)), 16 if parts.dtype == BF16 else 8)

    def body(p_ref, w_ref, m_ref, v_ref, g_ref, d_ref, nm_ref, nv_ref):
        g = p_ref[0].astype(F32)
        for i in range(1, n):
            g = g + p_ref[i].astype(F32)
        nm = ADAM_B1 * m_ref[...] + (1.0 - ADAM_B1) * g
        nv = ADAM_B2 * v_ref[...] + (1.0 - ADAM_B2) * (g * g)
        m_hat = nm / (1.0 - ADAM_B1 ** ADAM_STEP)
        v_hat = nv / (1.0 - ADAM_B2 ** ADAM_STEP)
        g_ref[...] = g
        d_ref[...] = -ADAM_LR * (m_hat / (jnp.sqrt(v_hat) + ADAM_EPS) + ADAM_WD * w_ref[...])
        nm_ref[...] = nm
        nv_ref[...] = nv

    spec = pl.BlockSpec((tr, c), lambda i: (i, 0))
    return pl.pallas_call(
        body, name=name, grid=(r // tr,), in_specs=[pl.BlockSpec((n, tr, c), lambda i: (0, i, 0)), spec, spec, spec],
        out_specs=[spec] * 4, out_shape=[jax.ShapeDtypeStruct((r, c), F32)] * 4,
        compiler_params=_params("parallel"),
    )(parts, w, m, v)


_HBM = pl.BlockSpec(memory_space=pltpu.HBM)


def _mesh_pos():
    return lax.axis_index("x"), lax.axis_index("y"), lax.axis_index("c")


def all_gather(shards, name):
    n = len(shards)

    def body(*refs):
        ins, outs = refs[:n], refs[n:2 * n]
        send_sems, recv_sems, local_sems = refs[2 * n:]
        x, y, c = _mesh_pos()
        me, sibling = (x, y, c), (x, y, 1 - c)
        chips = [(1 - x, y), (x, 1 - y), (1 - x, 1 - y)]

        def copy(a, k, block, to, src=None):
            px, py, pc = block
            dst = outs[a].at[4 * px + 2 * py + pc]
            return pltpu.make_async_remote_copy(
                src_ref=dst if src is None else src, dst_ref=dst, send_sem=send_sems.at[a, k],
                recv_sem=recv_sems.at[a, k], device_id=to, device_id_type=MESH)

        mine = [pltpu.make_async_copy(ins[a], outs[a].at[4 * x + 2 * y + c], local_sems.at[a]) for a in range(n)]
        for cp in mine:
            cp.start()
        first = []
        for a in range(n):
            first.append(copy(a, 0, me, sibling, src=ins[a]))
            first += [copy(a, 1 + j, me, (*chip, c), src=ins[a]) for j, chip in enumerate(chips)]
        for cp in first:
            cp.start()
        passed = []
        for a in range(n):
            for j, chip in enumerate(chips):
                copy(a, 1 + j, (*chip, c), me).wait_recv()
                cp = copy(a, 4 + j, (*chip, c), sibling)
                cp.start()
                passed.append(cp)
        for a in range(n):
            copy(a, 0, sibling, me).wait_recv()
            for j, chip in enumerate(chips):
                copy(a, 4 + j, (*chip, 1 - c), me).wait_recv()
        for cp in first + passed:
            cp.wait_send()
        for cp in mine:
            cp.wait()

    return pl.pallas_call(
        body, name=name, in_specs=[_HBM] * n, out_specs=[_HBM] * n,
        out_shape=[jax.ShapeDtypeStruct((N_DEV, *s.shape), s.dtype) for s in shards],
        scratch_shapes=[pltpu.SemaphoreType.DMA((n, 7)), pltpu.SemaphoreType.DMA((n, 7)), pltpu.SemaphoreType.DMA((n,))],
    )(*shards)


def exchange_cores(parts, name):
    n = len(parts)

    def body(*refs):
        ins, outs = refs[:n], refs[n:2 * n]
        send_sems, recv_sems = refs[2 * n:]
        x, y, c = _mesh_pos()
        copies = []
        for a in range(n):
            for p in range(4):
                copies.append(pltpu.make_async_remote_copy(
                    src_ref=ins[a].at[p, 1 - c], dst_ref=outs[a].at[p], send_sem=send_sems.at[a, p],
                    recv_sem=recv_sems.at[a, p], device_id=(x, y, 1 - c), device_id_type=MESH))
        for cp in copies:
            cp.start()
        for cp in copies:
            cp.wait()

    return pl.pallas_call(
        body, name=name, in_specs=[_HBM] * n, out_specs=[_HBM] * n,
        out_shape=[jax.ShapeDtypeStruct((4, *s.shape[2:]), s.dtype) for s in parts],
        scratch_shapes=[pltpu.SemaphoreType.DMA((n, 4)), pltpu.SemaphoreType.DMA((n, 4))],
    )(*parts)


def _chip_exchange(ins, outs, send_sems, recv_sems, local_sems):
    x, y, c = _mesh_pos()
    mine = 2 * x + y
    copies = []
    for a in range(len(ins)):
        copies.append(pltpu.make_async_copy(ins[a].at[mine], outs[a].at[mine], local_sems.at[a]))
        for r in range(1, 4):
            qx = 1 - x if r & 2 else x
            qy = 1 - y if r & 1 else y
            copies.append(pltpu.make_async_remote_copy(
                src_ref=ins[a].at[2 * qx + qy], dst_ref=outs[a].at[mine], send_sem=send_sems.at[a, r - 1],
                recv_sem=recv_sems.at[a, r - 1], device_id=(qx, qy, c), device_id_type=MESH))

    def start():
        for cp in copies:
            cp.start()

    def wait():
        for cp in copies:
            cp.wait()

    return start, wait


def _chip_exchange_sems(n):
    return [pltpu.SemaphoreType.DMA((n, 3)), pltpu.SemaphoreType.DMA((n, 3)), pltpu.SemaphoreType.DMA((n,))]


def exchange_chips(parts, name):
    n = len(parts)

    def body(*refs):
        start, wait = _chip_exchange(refs[:n], refs[n:2 * n], *refs[2 * n:])
        start()
        wait()

    return pl.pallas_call(
        body, name=name, in_specs=[_HBM] * n, out_specs=[_HBM] * n,
        out_shape=[jax.ShapeDtypeStruct(s.shape, s.dtype) for s in parts],
        scratch_shapes=_chip_exchange_sems(n),
    )(*parts)


def add_own(core, parts, recv, out_dtype, name):
    _, _, r, c = parts.shape
    tr = _pick(r, max(16, (1 << 19) // c), 16)

    def body(core_ref, p_ref, r_ref, o_ref):
        o_ref[...] = (p_ref[...] + r_ref[...]).astype(out_dtype)

    grid_spec = pltpu.PrefetchScalarGridSpec(
        num_scalar_prefetch=1, grid=(4, r // tr),
        in_specs=[pl.BlockSpec((None, None, tr, c), lambda p, i, core_ref: (p, core_ref[0], i, 0)),
                  pl.BlockSpec((None, tr, c), lambda p, i, core_ref: (p, i, 0))],
        out_specs=pl.BlockSpec((None, tr, c), lambda p, i, core_ref: (p, i, 0)))
    return pl.pallas_call(
        body, name=name, grid_spec=grid_spec, out_shape=jax.ShapeDtypeStruct((4, r, c), out_dtype),
        compiler_params=_params("parallel", "parallel"),
    )(core, parts, recv)


PARAMS = {
    "norm_mix_g": ((4, 1024), None), "norm_ffn_g": ((4, 1024), None),
    "pool_w": ((1, 4, 256, 256), 2), "pool_b": ((1, 1024), None), "pool_scale": ((1, 1024), None),
    "s5_lam_re": ((1, 64, 64), None), "s5_lam_im": ((1, 64, 64), None), "s5_log_dt": ((1, 64), None),
    "s5_b_re": ((1, 64, 64, 16), None), "s5_b_im": ((1, 64, 64, 16), None),
    "s5_c_re": ((1, 64, 16, 64), None), "s5_c_im": ((1, 64, 16, 64), None),
    "s5_d": ((1, 1024), 1), "s5_w_out": ((1, 1024, 2048), 2), "s5_b_out": ((1, 2048), 1),
    "lru_w_in": ((1, 1024, 2048), 2), "lru_conv_w": ((1, 4, 1024), 2), "lru_conv_b": ((1, 1024), 1),
    "lru_w_a": ((1, 4, 256, 256), 2), "lru_b_a": ((1, 1024), 1), "lru_w_x": ((1, 4, 256, 256), 2),
    "lru_b_x": ((1, 1024), 1), "lru_lam": ((1, 1024), 1), "lru_w_out": ((1, 1024, 1024), 1),
    "sb_w_qkv": ((1, 1024, 3072), 2), "sb_q_g": ((1, 64), None), "sb_k_g": ((1, 64), None),
    "sb_w_o": ((1, 1024, 1024), 1),
    "ffn_w_in": ((4, 1024, 5632), 2), "ffn_conv_w": ((4, 3, 5632), 2), "ffn_conv_b": ((4, 5632), None),
    "ffn_w_out": ((4, 2816, 1024), 1),
}
NAMES = list(PARAMS)
BIG = ["s5_w_out", "lru_w_in", "lru_w_out", "sb_w_qkv", "sb_w_o", "ffn_w_in", "ffn_w_out"]
SMALL = [n for n in NAMES if PARAMS[n][1] is not None and n not in BIG]
REPL = [n for n in NAMES if PARAMS[n][1] is None]


def _local_shape(name):
    shape, ax = PARAMS[name]
    return tuple(s // N_DEV if i == ax else s for i, s in enumerate(shape))


def _to_natural(gathered, name):
    shape, ax = PARAMS[name]
    return jnp.moveaxis(gathered, 0, ax).reshape(shape)


def _to_shards(nat, name):
    shape, ax = PARAMS[name]
    split = shape[:ax] + (N_DEV, shape[ax] // N_DEV) + shape[ax + 1:]
    return jnp.moveaxis(nat.reshape(split), ax, 0)


def _rows2d(shape):
    return (math.prod(shape[:-1]), shape[-1])


def _pack(arrays, lead=()):
    flat = jnp.concatenate([a.reshape(*lead, -1) for a in arrays], axis=-1)
    size = flat.shape[-1]
    rows = -(-size // (8 * LANES)) * 8
    flat = jnp.pad(flat, [(0, 0)] * len(lead) + [(0, rows * LANES - size)])
    return flat.reshape(*lead, rows, LANES)


def _unpack(packed, shapes, lead=()):
    flat = packed.reshape(*lead, -1)
    out, off = [], 0
    for s in shapes:
        size = math.prod(s)
        out.append(flat[..., off:off + size].reshape(*lead, *s))
        off += size
    return out


def kernel(x, norm_mix_g, norm_ffn_g, pool_w, pool_b, pool_scale, s5_lam_re, s5_lam_im, s5_log_dt, s5_b_re, s5_b_im, s5_c_re, s5_c_im, s5_d, s5_w_out, s5_b_out, lru_w_in, lru_conv_w, lru_conv_b, lru_w_a, lru_b_a, lru_w_x, lru_b_x, lru_lam, lru_w_out, sb_w_qkv, sb_q_g, sb_k_g, sb_w_o, ffn_w_in, ffn_conv_w, ffn_conv_b, ffn_w_out, loss_target, m_norm_mix_g, m_norm_ffn_g, m_pool_w, m_pool_b, m_pool_scale, m_s5_lam_re, m_s5_lam_im, m_s5_log_dt, m_s5_b_re, m_s5_b_im, m_s5_c_re, m_s5_c_im, m_s5_d, m_s5_w_out, m_s5_b_out, m_lru_w_in, m_lru_conv_w, m_lru_conv_b, m_lru_w_a, m_lru_b_a, m_lru_w_x, m_lru_b_x, m_lru_lam, m_lru_w_out, m_sb_w_qkv, m_sb_q_g, m_sb_k_g, m_sb_w_o, m_ffn_w_in, m_ffn_conv_w, m_ffn_conv_b, m_ffn_w_out, v_norm_mix_g, v_norm_ffn_g, v_pool_w, v_pool_b, v_pool_scale, v_s5_lam_re, v_s5_lam_im, v_s5_log_dt, v_s5_b_re, v_s5_b_im, v_s5_c_re, v_s5_c_im, v_s5_d, v_s5_w_out, v_s5_b_out, v_lru_w_in, v_lru_conv_w, v_lru_conv_b, v_lru_w_a, v_lru_b_a, v_lru_w_x, v_lru_b_x, v_lru_lam, v_lru_w_out, v_sb_w_qkv, v_sb_q_g, v_sb_k_g, v_sb_w_o, v_ffn_w_in, v_ffn_conv_w, v_ffn_conv_b, v_ffn_w_out):
    w = dict(zip(NAMES, (norm_mix_g, norm_ffn_g, pool_w, pool_b, pool_scale, s5_lam_re, s5_lam_im, s5_log_dt, s5_b_re, s5_b_im, s5_c_re, s5_c_im, s5_d, s5_w_out, s5_b_out, lru_w_in, lru_conv_w, lru_conv_b, lru_w_a, lru_b_a, lru_w_x, lru_b_x, lru_lam, lru_w_out, sb_w_qkv, sb_q_g, sb_k_g, sb_w_o, ffn_w_in, ffn_conv_w, ffn_conv_b, ffn_w_out)))
    m = dict(zip(NAMES, (m_norm_mix_g, m_norm_ffn_g, m_pool_w, m_pool_b, m_pool_scale, m_s5_lam_re, m_s5_lam_im, m_s5_log_dt, m_s5_b_re, m_s5_b_im, m_s5_c_re, m_s5_c_im, m_s5_d, m_s5_w_out, m_s5_b_out, m_lru_w_in, m_lru_conv_w, m_lru_conv_b, m_lru_w_a, m_lru_b_a, m_lru_w_x, m_lru_b_x, m_lru_lam, m_lru_w_out, m_sb_w_qkv, m_sb_q_g, m_sb_k_g, m_sb_w_o, m_ffn_w_in, m_ffn_conv_w, m_ffn_conv_b, m_ffn_w_out)))
    v = dict(zip(NAMES, (v_norm_mix_g, v_norm_ffn_g, v_pool_w, v_pool_b, v_pool_scale, v_s5_lam_re, v_s5_lam_im, v_s5_log_dt, v_s5_b_re, v_s5_b_im, v_s5_c_re, v_s5_c_im, v_s5_d, v_s5_w_out, v_s5_b_out, v_lru_w_in, v_lru_conv_w, v_lru_conv_b, v_lru_w_a, v_lru_b_a, v_lru_w_x, v_lru_b_x, v_lru_lam, v_lru_w_out, v_sb_w_qkv, v_sb_q_g, v_sb_k_g, v_sb_w_o, v_ffn_w_in, v_ffn_conv_w, v_ffn_conv_b, v_ffn_w_out)))
    return train_step(x, loss_target, w, m, v)


def train_step(x, loss_target, w, m, v):
    bsz, seq, d = x.shape
    t = bsz * seq
    n_layers = PARAMS["norm_mix_g"][0][0]

    small_shapes = [_local_shape(n) for n in SMALL]
    gathered = all_gather([w[n].astype(BF16) for n in BIG] + [_pack([w[n] for n in SMALL])], "gather_weights")
    nat = {n: _to_natural(g, n) for n, g in zip(BIG, gathered[:-1])}
    for n, g in zip(SMALL, _unpack(gathered[-1], small_shapes, lead=(N_DEV,))):
        nat[n] = _to_natural(g, n)
    for n in REPL:
        nat[n] = w[n]

    hid = FFN_HIDDEN
    pool_p = (nat["pool_w"][0].astype(BF16), nat["pool_b"], nat["pool_scale"])
    s5_p = dict(lam_re=nat["s5_lam_re"][0], lam_im=nat["s5_lam_im"][0], log_dt=nat["s5_log_dt"][0],
                b_re=nat["s5_b_re"][0], b_im=nat["s5_b_im"][0], c_re=nat["s5_c_re"][0], c_im=nat["s5_c_im"][0],
                d=nat["s5_d"], w_val=nat["s5_w_out"][0, :, :d], w_gate=nat["s5_w_out"][0, :, d:], b_out=nat["s5_b_out"])
    lru_p = dict(w_gate=nat["lru_w_in"][0, :, :d], w_rec=nat["lru_w_in"][0, :, d:], conv_w=nat["lru_conv_w"][0],
                 conv_b=nat["lru_conv_b"], b_a=nat["lru_b_a"], b_x=nat["lru_b_x"], lam=nat["lru_lam"],
                 w_a=nat["lru_w_a"][0].astype(BF16), w_x=nat["lru_w_x"][0].astype(BF16), w_out=nat["lru_w_out"][0])
    sb_p = dict(w_q=nat["sb_w_qkv"][0, :, :d], w_k=nat["sb_w_qkv"][0, :, d:2 * d], w_v=nat["sb_w_qkv"][0, :, 2 * d:],
                w_o=nat["sb_w_o"][0], q_g=nat["sb_q_g"], k_g=nat["sb_k_g"])

    def ffn_p(li):
        return (nat["ffn_w_in"][li, :, :hid], nat["ffn_w_in"][li, :, hid:], nat["ffn_conv_w"][li],
                nat["ffn_conv_b"][li:li + 1], nat["ffn_w_out"][li])

    def gain(name, li):
        return nat[name][li:li + 1]

    h = x.reshape(t, d)
    h, pool_saved = pool_fwd(h, gain("norm_mix_g", 0), *pool_p, bsz)
    h, ffn0 = ffn_fwd(h, gain("norm_ffn_g", 0), *ffn_p(0), bsz, 0)
    h, s5_saved = s5_fwd(h, gain("norm_mix_g", 1), s5_p, bsz)
    h, ffn1 = ffn_fwd(h, gain("norm_ffn_g", 1), *ffn_p(1), bsz, 1)
    h, lru_saved = lru_fwd(h, gain("norm_mix_g", 2), lru_p, bsz)
    h, ffn2 = ffn_fwd(h, gain("norm_ffn_g", 2), *ffn_p(2), bsz, 2)
    h, sb_saved = sb_fwd(h, gain("norm_mix_g", 3), sb_p, bsz)
    h, ffn3 = ffn_fwd(h, gain("norm_ffn_g", 3), *ffn_p(3), bsz, 3)
    sq, dh = loss_head(h, loss_target.reshape(t, d))
    loss = lax.psum(0.5 * jnp.sum(sq) / d, ("x", "y", "c"))

    core = lax.axis_index("c").astype(jnp.int32).reshape(1)

    def by_dest(grad2d, name):
        ax = PARAMS[name][1]
        r, c = grad2d.shape
        if ax == 2:
            return grad2d.reshape(r, N_DEV, c // N_DEV).transpose(1, 0, 2).reshape(4, 2, r, c // N_DEV)
        return grad2d.reshape(4, 2, r // N_DEV, c)

    def ffn_group(li, g):
        return [(f"ffn_w_in{li}", by_dest(jnp.concatenate([g[1], g[2]], axis=1), "ffn_w_in"), BF16),
                (f"ffn_w_out{li}", by_dest(g[5], "ffn_w_out"), BF16)]

    def core_stage(group, tag):
        recv = exchange_cores([a for _, a, _ in group], f"reduce_cores_{tag}")
        return [add_own(core, a, r, dt, f"add_{k}") for (k, a, dt), r in zip(group, recv)]

    reduced = {}
    fg = [None] * n_layers
    dh, fg[3] = ffn_bwd(dh, ffn3, gain("norm_ffn_g", 3), *ffn_p(3), 3)
    group = ffn_group(3, fg[3])
    dh, sb_g, arrived = sb_bwd(dh, sb_saved, gain("norm_mix_g", 3), sb_p, core_stage(group, "ffn3"))
    reduced.update(zip([k for k, _, _ in group], arrived))
    dh, fg[2] = ffn_bwd(dh, ffn2, gain("norm_ffn_g", 2), *ffn_p(2), 2)
    group = [("sb_w_qkv", by_dest(jnp.concatenate([sb_g["w_q"], sb_g["w_k"], sb_g["w_v"]], axis=1), "sb_w_qkv"), BF16),
             ("sb_w_o", by_dest(sb_g["w_o"], "sb_w_o"), BF16)] + ffn_group(2, fg[2])
    dh, lru_g, arrived = lru_bwd(dh, lru_saved, gain("norm_mix_g", 2), lru_p, core_stage(group, "sb_ffn2"))
    reduced.update(zip([k for k, _, _ in group], arrived))
    dh, fg[1] = ffn_bwd(dh, ffn1, gain("norm_ffn_g", 1), *ffn_p(1), 1)
    group = [("lru_w_in", by_dest(jnp.concatenate([lru_g["w_gate"], lru_g["w_rec"]], axis=1), "lru_w_in"), BF16),
             ("lru_w_out", by_dest(lru_g["w_out"], "lru_w_out"), BF16)] + ffn_group(1, fg[1])
    dh, s5_g, arrived = s5_bwd(dh, s5_saved, gain("norm_mix_g", 1), s5_p, core_stage(group, "lru_ffn1"))
    reduced.update(zip([k for k, _, _ in group], arrived))
    dh, fg[0] = ffn_bwd(dh, ffn0, gain("norm_ffn_g", 0), *ffn_p(0), 0)
    dh, pool_g = pool_bwd(dh, pool_saved, gain("norm_mix_g", 0), *pool_p)
    grad_x = dh.reshape(bsz, seq, d)

    part = {
        "norm_mix_g": jnp.concatenate([pool_g[0], s5_g["g"], lru_g["g"], sb_g["g"]], axis=0),
        "norm_ffn_g": jnp.concatenate([g[0] for g in fg], axis=0),
        "pool_w": pool_g[1][None], "pool_b": pool_g[2], "pool_scale": pool_g[3],
        "s5_lam_re": s5_g["lam_re"][None], "s5_lam_im": s5_g["lam_im"][None], "s5_log_dt": s5_g["log_dt"][None],
        "s5_b_re": s5_g["b_re"][None], "s5_b_im": s5_g["b_im"][None], "s5_c_re": s5_g["c_re"][None],
        "s5_c_im": s5_g["c_im"][None], "s5_d": s5_g["d"], "s5_b_out": s5_g["b_out"],
        "lru_conv_w": lru_g["conv_w"][None], "lru_conv_b": lru_g["conv_b"], "lru_w_a": lru_g["w_a"][None],
        "lru_b_a": lru_g["b_a"], "lru_w_x": lru_g["w_x"][None], "lru_b_x": lru_g["b_x"], "lru_lam": lru_g["lam"],
        "sb_q_g": sb_g["q_g"], "sb_k_g": sb_g["k_g"],
        "ffn_conv_w": jnp.stack([g[3] for g in fg]), "ffn_conv_b": jnp.concatenate([g[4] for g in fg], axis=0),
    }
    small_part = _pack([_to_shards(part[n], n) for n in SMALL], lead=(N_DEV,))
    group = [("s5_w_out", by_dest(jnp.concatenate([s5_g["w_val"], s5_g["w_gate"]], axis=1), "s5_w_out"), BF16)]
    group += ffn_group(0, fg[0]) + [("small", small_part.reshape(4, 2, *small_part.shape[1:]), F32)]
    arrived = exchange_chips(core_stage(group, "s5_ffn0_small"), "reduce_chips")
    reduced.update(zip([k for k, _, _ in group], arrived))

    out = {}
    for n in BIG:
        layers = PARAMS[n][0][0]
        res = []
        for li in range(layers):
            key = f"{n}{li}" if layers > 1 else n
            res.append(adamw(reduced[key], w[n][li], m[n][li], v[n][li], f"adamw_{key}"))
        out[n] = [jnp.stack([r[i] for r in res]) for i in range(4)]
    small_res = adamw(reduced["small"], *[_pack([tree[n] for n in SMALL]) for tree in (w, m, v)], "adamw_small")
    for i, res in enumerate(small_res):
        for n, r in zip(SMALL, _unpack(res, small_shapes)):
            out.setdefault(n, [None] * 4)[i] = r

    repl_shapes = [PARAMS[n][0] for n in REPL]
    repl_parts = all_gather([_pack([part[n] for n in REPL])], "gather_grads")[0]
    repl_res = adamw(repl_parts, *[_pack([tree[n] for n in REPL]) for tree in (w, m, v)], "adamw_replicated")
    for i, res in enumerate(repl_res):
        for n, r in zip(REPL, _unpack(res, repl_shapes)):
            out.setdefault(n, [None] * 4)[i] = r

    return (loss, grad_x, *[out[n][0] for n in NAMES], *[out[n][1] for n in NAMES], *[out[n][2] for n in NAMES],
            *[out[n][3] for n in NAMES])
```

```python
import functools
import math

import jax
import jax.numpy as jnp
from jax import lax
from jax.experimental import pallas as pl
from jax.experimental.pallas import tpu as pltpu

F32 = jnp.float32
BF16 = jnp.bfloat16
MESH = pl.DeviceIdType.MESH

N_DEV = 8
D_MODEL = 1024
EPS = 1e-6
POOL_GROUP = 256
S5_GROUPS, S5_GROUP, S5_STATE = 64, 16, 64
LRU_BLOCK = 256
LRU_C = 8.0
SB_HEAD_DIM = 64
ATT_BLOCK = 256
FFN_HIDDEN = 2816
ADAM_LR, ADAM_B1, ADAM_B2, ADAM_EPS, ADAM_WD, ADAM_STEP = 0.001, 0.9, 0.999, 1e-08, 0.01, 10
LANES = 128
SUBLANES = 8
VMEM_LIMIT = 56 * 1024 * 1024


def _pick(n, target, mult=LANES):
    best = None
    for d in range(mult, min(n, target) + 1, mult):
        if n % d == 0:
            best = d
    return best or n


def _params(*sem):
    return pltpu.CompilerParams(dimension_semantics=sem, vmem_limit_bytes=VMEM_LIMIT)


def mm(a, b, *, ta=False, tb=False, acc=None, out_dtype=F32, name):
    if ta:
        kdim, m = a.shape
    else:
        m, kdim = a.shape
    if tb:
        n, k2 = b.shape
    else:
        k2, n = b.shape
    assert kdim == k2, (a.shape, b.shape)
    tm = _pick(m, 1408 if ta else 512)
    tn = _pick(n, 1408)
    tk = _pick(kdim, 1024) if ta else kdim
    nk = kdim // tk
    a_spec = pl.BlockSpec((tk, tm), lambda j, i, k: (k, i)) if ta else pl.BlockSpec((tm, tk), lambda j, i, k: (i, k))
    b_spec = pl.BlockSpec((tn, tk), lambda j, i, k: (j, k)) if tb else pl.BlockSpec((tk, tn), lambda j, i, k: (k, j))
    o_spec = pl.BlockSpec((tm, tn), lambda j, i, k: (i, j))
    dims = (((0,) if ta else (1,), (1,) if tb else (0,)), ((), ()))
    has_acc = acc is not None

    def body(*refs):
        if has_acc:
            a_ref, b_ref, c_ref, o_ref, acc_ref = refs
        else:
            a_ref, b_ref, o_ref, acc_ref = refs
        k = pl.program_id(2)

        @pl.when(k == 0)
        def _():
            acc_ref[...] = c_ref[...].astype(F32) if has_acc else jnp.zeros_like(acc_ref)

        acc_ref[...] += lax.dot_general(a_ref[...].astype(BF16), b_ref[...].astype(BF16), dims,
                                        preferred_element_type=F32)

        @pl.when(k == nk - 1)
        def _():
            o_ref[...] = acc_ref[...].astype(out_dtype)

    ins = [a, b] + ([acc] if has_acc else [])
    specs = [a_spec, b_spec] + ([o_spec] if has_acc else [])
    return pl.pallas_call(
        body, name=name, grid=(n // tn, m // tm, nk), in_specs=specs, out_specs=o_spec,
        out_shape=jax.ShapeDtypeStruct((m, n), out_dtype), scratch_shapes=[pltpu.VMEM((tm, tn), F32)],
        compiler_params=_params("parallel", "parallel", "arbitrary"),
    )(*ins)


def rowwise(fn, tiled, bcast, outs, reds=(), *, tm=256, name):
    t = tiled[0].shape[0]
    tm = min(tm, t)
    assert t % tm == 0
    n_t, n_b, n_o, n_r = len(tiled), len(bcast), len(outs), len(reds)

    def body(*refs):
        vals = [r[...] for r in refs[:n_t + n_b]]
        res = fn(*vals)
        res = res if isinstance(res, tuple) else (res,)
        o_refs = refs[n_t + n_b:n_t + n_b + n_o]
        r_refs = refs[n_t + n_b + n_o:]
        for o_ref, v in zip(o_refs, res[:n_o]):
            o_ref[...] = v.astype(o_ref.dtype)
        first = pl.program_id(0) == 0
        for r_ref, v in zip(r_refs, res[n_o:]):
            @pl.when(first)
            def _(r_ref=r_ref, v=v):
                r_ref[...] = v.astype(F32)

            @pl.when(jnp.logical_not(first))
            def _(r_ref=r_ref, v=v):
                r_ref[...] += v.astype(F32)

    in_specs = [pl.BlockSpec((tm, a.shape[1]), lambda i: (i, 0)) for a in tiled]
    in_specs += [pl.BlockSpec(a.shape, lambda i, nd=a.ndim: (0,) * nd) for a in bcast]
    out_specs = [pl.BlockSpec((tm, c), lambda i: (i, 0)) for c, _ in outs]
    out_specs += [pl.BlockSpec(s, lambda i: (0, 0)) for s in reds]
    out_shape = [jax.ShapeDtypeStruct((t, c), dt) for c, dt in outs]
    out_shape += [jax.ShapeDtypeStruct(s, F32) for s in reds]
    res = pl.pallas_call(
        body, name=name, grid=(t // tm,), in_specs=in_specs, out_specs=out_specs, out_shape=out_shape,
        compiler_params=_params("arbitrary"),
    )(*tiled, *bcast)
    return res


def seqwise(fn, seqs, chans, outs, reds=(), *, ct, name, fulls=()):
    def split(x):
        return x if isinstance(x, tuple) else (x, 0)

    seqs = [split(s) for s in seqs]
    chans = [split(s) for s in chans]
    bsz, seq = seqs[0][0].shape[:2]
    n_c = outs[0][0] // ct if outs else reds[0][1] // ct
    n_s, n_ch, n_f, n_o, n_r = len(seqs), len(chans), len(fulls), len(outs), len(reds)

    def body(*refs):
        c = pl.program_id(0)
        vals = [r[...] for r in refs[:n_s + n_ch + n_f]]
        res = fn(c, *vals)
        res = res if isinstance(res, tuple) else (res,)
        o_refs = refs[n_s + n_ch + n_f:n_s + n_ch + n_f + n_o]
        r_refs = refs[n_s + n_ch + n_f + n_o:]
        for o_ref, v in zip(o_refs, res[:n_o]):
            o_ref[...] = v.astype(o_ref.dtype)
        first = pl.program_id(1) == 0
        for r_ref, v in zip(r_refs, res[n_o:]):
            @pl.when(first)
            def _(r_ref=r_ref, v=v):
                r_ref[...] = v.astype(F32)

            @pl.when(jnp.logical_not(first))
            def _(r_ref=r_ref, v=v):
                r_ref[...] += v.astype(F32)

    in_specs = [pl.BlockSpec((None, seq, ct), lambda c, b, off=off: (b, 0, c + off)) for _, off in seqs]
    in_specs += [pl.BlockSpec((a.shape[0], ct), lambda c, b, off=off: (0, c + off)) for a, off in chans]
    in_specs += [pl.BlockSpec(a.shape, lambda c, b, nd=a.ndim: (0,) * nd) for a in fulls]
    out_specs = [pl.BlockSpec((None, seq, ct), lambda c, b: (b, 0, c)) for _ in outs]
    out_specs += [pl.BlockSpec((r, ct), lambda c, b: (0, c)) for r, _ in reds]
    out_shape = [jax.ShapeDtypeStruct((bsz, seq, cc), dt) for cc, dt in outs]
    out_shape += [jax.ShapeDtypeStruct(s, F32) for s in reds]
    return pl.pallas_call(
        body, name=name, grid=(n_c, bsz), in_specs=in_specs, out_specs=out_specs, out_shape=out_shape,
        compiler_params=_params("arbitrary", "arbitrary"),
    )(*[a for a, _ in seqs], *[a for a, _ in chans], *fulls)


def _rows(x):
    return lax.broadcasted_iota(jnp.int32, x.shape, 0)


def shift_down(x, d, fill=0.0):
    if d == 0:
        return x
    s = x.shape[0]
    if d % SUBLANES == 0:
        return jnp.concatenate([jnp.full((d,) + x.shape[1:], fill, x.dtype), x[:s - d]], axis=0)
    rolled = pltpu.roll(x, d, 0)
    if d > SUBLANES or s <= SUBLANES:
        return jnp.where(_rows(x) >= d, rolled, fill)
    head = rolled[:SUBLANES]
    return jnp.concatenate([jnp.where(_rows(head) >= d, head, fill), rolled[SUBLANES:]], axis=0)


def shift_up(x, d, fill=0.0):
    if d == 0:
        return x
    s = x.shape[0]
    if d % SUBLANES == 0:
        return jnp.concatenate([x[d:], jnp.full((d,) + x.shape[1:], fill, x.dtype)], axis=0)
    rolled = pltpu.roll(x, s - d, 0)
    if d > SUBLANES or s <= SUBLANES:
        return jnp.where(_rows(x) < s - d, rolled, fill)
    tail = rolled[s - SUBLANES:]
    return jnp.concatenate([rolled[:s - SUBLANES], jnp.where(_rows(tail) < SUBLANES - d, tail, fill)], axis=0)


def conv_fwd(x, w, b):
    kw = w.shape[0]
    y = b + w[kw - 1:kw] * x
    for d in range(1, kw):
        y = y + w[kw - 1 - d:kw - d] * shift_down(x, d)
    return y


def conv_bwd(x, w, dy):
    kw = w.shape[0]
    dx = w[kw - 1:kw] * dy
    dws = [jnp.sum(dy * x, axis=0, keepdims=True)]
    for d in range(1, kw):
        dx = dx + w[kw - 1 - d:kw - d] * shift_up(dy, d)
        dws.append(jnp.sum(dy * shift_down(x, d), axis=0, keepdims=True))
    dw = jnp.concatenate(dws[::-1], axis=0)
    return dx, dw, jnp.sum(dy, axis=0, keepdims=True)


def sigmoid(x):
    return 0.5 * jnp.tanh(0.5 * x) + 0.5


def rms_fwd(x, g, out_dtype, name):
    def fn(x, g):
        return x * lax.rsqrt(jnp.mean(x * x, axis=-1, keepdims=True) + EPS) * g

    return rowwise(fn, [x], [g], [(x.shape[1], out_dtype)], name=name)[0]


def rms_bwd(x, g, dh_parts, dres, name):
    n_p = len(dh_parts)

    def fn(x, dres, *rest):
        dh = rest[0].astype(F32)
        for p in rest[1:n_p]:
            dh = dh + p.astype(F32)
        g = rest[n_p]
        r = lax.rsqrt(jnp.mean(x * x, axis=-1, keepdims=True) + EPS)
        xh = x * r
        dxh = dh * g
        dx = r * (dxh - xh * jnp.mean(dxh * xh, axis=-1, keepdims=True))
        return dres + dx, jnp.sum(dh * xh, axis=0, keepdims=True)

    return rowwise(fn, [x, dres, *dh_parts], [g], [(x.shape[1], F32)], [(1, x.shape[1])], name=name)


def ffn_fwd(x, g, w_val, w_gate, cw, cb, w_out, bsz, li):
    t, d = x.shape
    hid = w_val.shape[1]
    xn = rms_fwd(x, g, BF16, f"ffn{li}_rms")
    hv = mm(xn, w_val, name=f"ffn{li}_hv").reshape(bsz, t // bsz, hid)
    hg = mm(xn, w_gate, name=f"ffn{li}_hg").reshape(bsz, t // bsz, hid)
    ct = _pick(hid, 256)
    off = hid // ct

    def fn(c, hv, hg, wv, wg, bv, bg):
        val = conv_fwd(hv, wv, bv)
        gate = conv_fwd(hg, wg, bg)
        return gate * sigmoid(gate) * val

    act = seqwise(fn, [hv, hg], [cw, (cw, off), cb, (cb, off)], [(hid, BF16)], ct=ct, name=f"ffn{li}_gate")[0]
    act = act.reshape(t, hid)
    y = mm(act, w_out, acc=x, name=f"ffn{li}_out")
    return y, (x, xn, hv, hg, act)


def ffn_bwd(dy, saved, g, w_val, w_gate, cw, cb, w_out, li):
    x, xn, hv, hg, act = saved
    bsz, seq, hid = hv.shape
    t = bsz * seq
    d_wout = mm(act, dy, ta=True, name=f"ffn{li}_dwout")
    dact = mm(dy, w_out, tb=True, name=f"ffn{li}_dact").reshape(bsz, seq, hid)
    ct = _pick(hid, 256)
    off = hid // ct

    def fn(c, hv, hg, dact, wv, wg, bv, bg):
        val = conv_fwd(hv, wv, bv)
        gate = conv_fwd(hg, wg, bg)
        sg = sigmoid(gate)
        silu = gate * sg
        dval = dact * silu
        dgate = dact * val * (sg + silu * (1.0 - sg))
        dhv, dwv, dbv = conv_bwd(hv, wv, dval)
        dhg, dwg, dbg = conv_bwd(hg, wg, dgate)
        return dhv, dhg, dwv, dwg, dbv, dbg

    kw = cw.shape[0]
    dhv, dhg, dwv, dwg, dbv, dbg = seqwise(
        fn, [hv, hg, dact], [cw, (cw, off), cb, (cb, off)], [(hid, BF16), (hid, BF16)],
        [(kw, hid), (kw, hid), (1, hid), (1, hid)], ct=ct, name=f"ffn{li}_dgate")
    dhv = dhv.reshape(t, hid)
    dhg = dhg.reshape(t, hid)
    d_wv = mm(xn, dhv, ta=True, name=f"ffn{li}_dwv")
    d_wg = mm(xn, dhg, ta=True, name=f"ffn{li}_dwg")
    dxn = mm(dhv, w_val, tb=True, name=f"ffn{li}_dxn1")
    dxn = mm(dhg, w_gate, tb=True, acc=dxn, name=f"ffn{li}_dxn2")
    dx, dg = rms_bwd(x, g, [dxn], dy, f"ffn{li}_drms")
    d_cw = jnp.concatenate([dwv, dwg], axis=1)
    d_cb = jnp.concatenate([dbv, dbg], axis=1)
    return dx, (dg, d_wv, d_wg, d_cw, d_cb, d_wout)


def _window_sum(x, c, shift):
    s2 = x + shift(x, 1)
    s4 = s2 + shift(s2, 2)
    s8 = s4 + shift(s4, 4)
    s16 = s8 + shift(s8, 8)
    return jnp.where(c == 0, s2, jnp.where(c == 1, s4, jnp.where(c == 2, s8, s16)))


def _pool_inv_count(x, c):
    w = jnp.left_shift(2, c)
    return 1.0 / jnp.minimum(_rows(x) + 1, w).astype(F32)


def pool_fwd(x, g, w, b, scale, bsz):
    t, d = x.shape
    h = rms_fwd(x, g, F32, "pool_rms").reshape(bsz, t // bsz, d)

    def fn(c, h, x, b, scale, w):
        dd = _window_sum(h, c, shift_down) * _pool_inv_count(h, c) - h
        y = jnp.dot(dd.astype(BF16), w[0], preferred_element_type=F32) + b
        return x + scale * y

    wspec = pl.BlockSpec((1, POOL_GROUP, POOL_GROUP), lambda c, bb: (c, 0, 0))
    y = _seqwise_w(fn, [h, x.reshape(bsz, t // bsz, d)], [b, scale], [(w, wspec)], [(d, F32)], [], ct=POOL_GROUP,
                   name="pool_fwd")[0]
    return y.reshape(t, d), (x, h)


def pool_bwd(dy, saved, g, w, b, scale):
    x, h = saved
    bsz, seq, d = h.shape
    t = bsz * seq

    def fn(c, h, dy, b, scale, w):
        inv = _pool_inv_count(h, c)
        dd = _window_sum(h, c, shift_down) * inv - h
        ddb = dd.astype(BF16)
        y = jnp.dot(ddb, w[0], preferred_element_type=F32) + b
        dscale = jnp.sum(dy * y, axis=0, keepdims=True)
        dyy = dy * scale
        db = jnp.sum(dyy, axis=0, keepdims=True)
        dyb = dyy.astype(BF16)
        dw = lax.dot_general(ddb, dyb, (((0,), (0,)), ((), ())), preferred_element_type=F32)
        ddd = lax.dot_general(dyb, w[0], (((1,), (1,)), ((), ())), preferred_element_type=F32)
        dh = _window_sum(ddd * inv, c, shift_up) - ddd
        return dh, db, dscale, dw[None]

    wspec = pl.BlockSpec((1, POOL_GROUP, POOL_GROUP), lambda c, bb: (c, 0, 0))
    dh, db, dscale, dw = _seqwise_w(
        fn, [h, dy.reshape(bsz, seq, d)], [b, scale], [(w, wspec)], [(d, F32)], [(1, d), (1, d)], ct=POOL_GROUP,
        name="pool_bwd", wreds=[((4, POOL_GROUP, POOL_GROUP), wspec)])
    dx, dg = rms_bwd(x, g, [dh.reshape(t, d)], dy, "pool_drms")
    return dx, (dg, dw, db, dscale)


def _seqwise_w(fn, seqs, chans, blocked, outs, reds, *, ct, name, wreds=(), phased=False, side=()):
    bsz, seq = seqs[0].shape[:2]
    n_c = seqs[0].shape[2] // ct
    n_s = len(seqs)
    n_in = n_s + len(chans) + len(blocked)
    side, exchange, side_sems, side_shapes = _side_plan(side)
    n_o, n_r, n_x = len(outs), len(reds) + len(wreds), len(side)

    def body(*refs):
        c = pl.program_id(0)
        if side:
            ends = n_in + n_x + n_o + n_r
            start, wait = exchange(refs[n_in:n_in + n_x], refs[ends:ends + n_x], *refs[ends + n_x:])
            pl.when(jnp.logical_and(c == 0, pl.program_id(1) == 0))(start)
        seq_vals = [_load_phases(r) if phased else r[...] for r in refs[:n_s]]
        res = fn(c, *seq_vals, *[r[...] for r in refs[n_s:n_in]])
        res = res if isinstance(res, tuple) else (res,)
        own = refs[n_in + n_x:n_in + n_x + n_o + n_r]
        for o_ref, v in zip(own[:n_o], res[:n_o]):
            if phased:
                _store_phases(o_ref, v)
            else:
                o_ref[...] = v.astype(o_ref.dtype)
        first = pl.program_id(1) == 0
        for r_ref, v in zip(own[n_o:], res[n_o:]):
            @pl.when(first)
            def _(r_ref=r_ref, v=v):
                r_ref[...] = v.astype(F32)

            @pl.when(jnp.logical_not(first))
            def _(r_ref=r_ref, v=v):
                r_ref[...] += v.astype(F32)
        if side:
            pl.when(jnp.logical_and(c == n_c - 1, pl.program_id(1) == bsz - 1))(wait)

    in_specs = [pl.BlockSpec((None, seq, ct), lambda c, b: (b, 0, c)) for _ in seqs]
    in_specs += [pl.BlockSpec((a.shape[0], ct), lambda c, b: (0, c)) for a in chans]
    in_specs += [spec for _, spec in blocked]
    in_specs += [_HBM] * n_x
    out_specs = [pl.BlockSpec((None, seq, ct), lambda c, b: (b, 0, c)) for _ in outs]
    out_specs += [pl.BlockSpec((r, ct), lambda c, b: (0, c)) for r, _ in reds]
    out_specs += [spec for _, spec in wreds]
    out_specs += [_HBM] * n_x
    out_shape = [jax.ShapeDtypeStruct((bsz, seq, cc), dt) for cc, dt in outs]
    out_shape += [jax.ShapeDtypeStruct(s, F32) for s in reds]
    out_shape += [jax.ShapeDtypeStruct(s, F32) for s, _ in wreds]
    out_shape += side_shapes
    return pl.pallas_call(
        body, name=name, grid=(n_c, bsz), in_specs=in_specs, out_specs=out_specs, out_shape=out_shape,
        scratch_shapes=side_sems,
        compiler_params=_params("arbitrary", "arbitrary"),
    )(*seqs, *chans, *[a for a, _ in blocked], *side)


S5_TILE_GROUPS = LANES // S5_GROUP
S5_TILES = S5_GROUPS // S5_TILE_GROUPS
S5_TILE_STATE = S5_TILE_GROUPS * S5_STATE


def _s5_discretize(lam_re, lam_im, log_dt, b_re, b_im):
    lr = jnp.minimum(lam_re, -1e-4)
    dt = jnp.exp(log_dt)
    er = jnp.exp(lr * dt)
    ar = er * jnp.cos(lam_im * dt)
    ai = er * jnp.sin(lam_im * dt)
    den = lr * lr + lam_im * lam_im
    cr = ((ar - 1.0) * lr + ai * lam_im) / den
    ci = (ai * lr - (ar - 1.0) * lam_im) / den
    return ar, ai, cr * b_re - ci * b_im, cr * b_im + ci * b_re


def _whole(a):
    return pl.BlockSpec(a.shape, lambda *_: (0,) * a.ndim)


def s5_prep(lam_re, lam_im, log_dt, b_re, b_im):
    def body(lr, li, ld, br, bi, ar_o, ai_o, bbr_o, bbi_o):
        ar, ai, bbr, bbi = _s5_discretize(lr[...], li[...], ld[...], br[...], bi[...])
        ar_o[...] = ar
        ai_o[...] = ai
        bbr_o[...] = bbr
        bbi_o[...] = bbi

    ins = [lam_re, lam_im, log_dt, b_re, b_im]
    outs = [lam_re, lam_im, b_re, b_im]
    return pl.pallas_call(
        body, name="s5_prep", in_specs=[_whole(a) for a in ins], out_specs=[_whole(a) for a in outs],
        out_shape=[jax.ShapeDtypeStruct(a.shape, F32) for a in outs],
        compiler_params=pltpu.CompilerParams(vmem_limit_bytes=VMEM_LIMIT),
    )(*ins)


def s5_prep_bwd(lam_re, lam_im, log_dt, b_re, b_im, d_ar, d_ai, d_bbr, d_bbi):
    def body(lr, li, ld, br, bi, dar, dai, dbbr, dbbi, *outs):
        _, vjp = jax.vjp(_s5_discretize, lr[...], li[...], ld[...], br[...], bi[...])
        for o, v in zip(outs, vjp((dar[...], dai[...], dbbr[...], dbbi[...]))):
            o[...] = v

    ins = [lam_re, lam_im, log_dt, b_re, b_im, d_ar, d_ai, d_bbr, d_bbi]
    return pl.pallas_call(
        body, name="s5_prep_bwd", in_specs=[_whole(a) for a in ins], out_specs=[_whole(a) for a in ins[:5]],
        out_shape=[jax.ShapeDtypeStruct(a.shape, F32) for a in ins[:5]],
        compiler_params=pltpu.CompilerParams(vmem_limit_bytes=VMEM_LIMIT),
    )(*ins)


def scan_lti(br, bi, ar, ai, reverse=False):
    shift = shift_up if reverse else shift_down
    seq = br.shape[0]
    d = 1
    while d < seq:
        sr, si = shift(br, d), shift(bi, d)
        br, bi = br + ar * sr - ai * si, bi + ar * si + ai * sr
        ar, ai = ar * ar - ai * ai, 2.0 * ar * ai
        d *= 2
    return br, bi


PHASES = SUBLANES


def _load_phases(ref):
    groups = ref.shape[0] // PHASES
    return jnp.concatenate([ref[pl.ds(r, groups, stride=PHASES), :] for r in range(PHASES)], axis=0)


def _store_phases(ref, x):
    groups = ref.shape[0] // PHASES
    for r in range(PHASES):
        ref[pl.ds(r, groups, stride=PHASES), :] = x[r * groups:(r + 1) * groups].astype(ref.dtype)


def _split_phases(x):
    groups = x.shape[0] // PHASES
    return [x[r * groups:(r + 1) * groups] for r in range(PHASES)]


def _prev_phased(x):
    ph = _split_phases(x)
    return jnp.concatenate([shift_down(ph[PHASES - 1], 1)] + ph[:PHASES - 1], axis=0)


def _cmul_add(xr, xi, ar, ai, yr, yi):
    return xr + ar * yr - ai * yi, xi + ar * yi + ai * yr


def scan_phases(br, bi, ar, ai, reverse=False):
    rs, im = _split_phases(br), _split_phases(bi)
    order = range(PHASES - 2, -1, -1) if reverse else range(1, PHASES)
    step = 1 if reverse else -1
    for r in order:
        rs[r], im[r] = _cmul_add(rs[r], im[r], ar, ai, rs[r + step], im[r + step])
    powers = [(ar, ai)]
    for _ in range(PHASES - 1):
        pr, pi = powers[-1]
        powers.append((pr * ar - pi * ai, pr * ai + pi * ar))
    end = 0 if reverse else PHASES - 1
    cr, ci = scan_lti(rs[end], im[end], *powers[PHASES - 1], reverse)
    shift = shift_up if reverse else shift_down
    inr, ini = shift(cr, 1), shift(ci, 1)
    for r in range(PHASES):
        if r == end:
            rs[r], im[r] = cr, ci
        else:
            pr, pi = powers[PHASES - 1 - r if reverse else r]
            rs[r], im[r] = _cmul_add(rs[r], im[r], pr, pi, inr, ini)
    return jnp.concatenate(rs, axis=0), jnp.concatenate(im, axis=0)


def _s5_block_diag(w, rows_first):
    g, a, b = w.shape
    w = w.reshape(S5_TILES, S5_TILE_GROUPS, a, b)
    eye = jnp.eye(S5_TILE_GROUPS, dtype=w.dtype)
    if rows_first:
        return jnp.einsum("tgab,gk->tgakb", w, eye).reshape(S5_TILES, S5_TILE_GROUPS * a, S5_TILE_GROUPS * b)
    return jnp.einsum("tgab,gk->tgbka", w, eye).reshape(S5_TILES, S5_TILE_GROUPS * b, S5_TILE_GROUPS * a)


def _s5_diag_blocks(w, a, b):
    w = w.reshape(S5_TILES, S5_TILE_GROUPS, a, S5_TILE_GROUPS, b)
    eye = jnp.eye(S5_TILE_GROUPS, dtype=w.dtype)
    return jnp.einsum("tgakb,gk->tgab", w, eye).reshape(S5_GROUPS, a, b)


def _gelu(x):
    return jax.nn.gelu(x)


def _s5_tile_specs():
    b_spec = pl.BlockSpec((1, LANES, S5_TILE_STATE), lambda c, b: (c, 0, 0))
    c_spec = pl.BlockSpec((1, S5_TILE_STATE, LANES), lambda c, b: (c, 0, 0))
    a_spec = pl.BlockSpec((1, 1, S5_TILE_STATE), lambda c, b: (c, 0, 0))
    return b_spec, c_spec, a_spec


def s5_fwd(x, g, p, bsz, side=()):
    t, d = x.shape
    seq = t // bsz
    h = rms_fwd(x, g, F32, "s5_rms").reshape(bsz, seq, d)
    ar, ai, bbr, bbi = s5_prep(p["lam_re"][..., None], p["lam_im"][..., None], p["log_dt"][:, None, None],
                               p["b_re"], p["b_im"])
    bdr = _s5_block_diag(bbr, False).astype(BF16)
    bdi = _s5_block_diag(bbi, False).astype(BF16)
    cdr = _s5_block_diag(p["c_re"], False).astype(BF16)
    cdi = _s5_block_diag(p["c_im"], False).astype(BF16)
    a_r = ar.reshape(S5_TILES, 1, S5_TILE_STATE)
    a_i = ai.reshape(S5_TILES, 1, S5_TILE_STATE)
    b_spec, c_spec, a_spec = _s5_tile_specs()

    def fn(c, u, dskip, bdr, bdi, cdr, cdi, a_r, a_i):
        ub = u.astype(BF16)
        y = dskip * u
        for k in range(S5_TILE_STATE // LANES):
            sl = slice(k * LANES, (k + 1) * LANES)
            br = jnp.dot(ub, bdr[0][:, sl], preferred_element_type=F32)
            bi = jnp.dot(ub, bdi[0][:, sl], preferred_element_type=F32)
            sr, si = scan_phases(br, bi, a_r[0][:, sl], a_i[0][:, sl])
            y = y + jnp.dot(sr.astype(BF16), cdr[0][sl, :], preferred_element_type=F32)
            y = y - jnp.dot(si.astype(BF16), cdi[0][sl, :], preferred_element_type=F32)
        return y, _gelu(y)

    y, yg, *side_out = _seqwise_w(
        fn, [h], [p["d"]], [(bdr, b_spec), (bdi, b_spec), (cdr, c_spec), (cdi, c_spec), (a_r, a_spec), (a_i, a_spec)],
        [(d, F32), (d, F32)], [], ct=LANES, name="s5_core", phased=True, side=side)
    yg = yg.reshape(t, d)
    zv = mm(yg, p["w_val"], name="s5_zv")
    zg = mm(yg, p["w_gate"], name="s5_zg")

    def gate(x, zv, zg, bv, bg):
        return x + (zv + bv) * sigmoid(zg + bg)

    out = rowwise(gate, [x, zv, zg], [p["b_out"][:, :d], p["b_out"][:, d:]], [(d, F32)], name="s5_gate")[0]
    return out, (x, h, y, yg, zv, zg, (ar, ai, bdr, bdi, cdr, cdi, a_r, a_i)), side_out


def s5_bwd(dy, saved, g, p, side=()):
    x, h, y, yg, zv, zg, (ar, ai, bdr, bdi, cdr, cdi, a_r, a_i) = saved
    bsz, seq, d = h.shape
    t = bsz * seq

    def dgate(dy, zv, zg, bv, bg):
        val = zv + bv
        sg = sigmoid(zg + bg)
        dzv = dy * sg
        dzg = dy * val * sg * (1.0 - sg)
        return dzv, dzg, jnp.sum(dzv, axis=0, keepdims=True), jnp.sum(dzg, axis=0, keepdims=True)

    dzv, dzg, dbv, dbg = rowwise(dgate, [dy, zv, zg], [p["b_out"][:, :d], p["b_out"][:, d:]],
                                 [(d, BF16), (d, BF16)], [(1, d), (1, d)], name="s5_dgate")
    d_wv = mm(yg, dzv, ta=True, name="s5_dwv")
    d_wg = mm(yg, dzg, ta=True, name="s5_dwg")
    dyg = mm(dzv, p["w_val"], tb=True, name="s5_dyg1")
    dyg = mm(dzg, p["w_gate"], tb=True, acc=dyg, name="s5_dyg2").reshape(bsz, seq, d)
    b_spec, c_spec, a_spec = _s5_tile_specs()
    tr = (((0,), (0,)), ((), ()))
    nt = (((1,), (1,)), ((), ()))

    def fn(c, u, y, dyg, dskip, bdr, bdi, cdr, cdi, a_r, a_i):
        _, gelu_vjp = jax.vjp(_gelu, y)
        dyy = gelu_vjp(dyg)[0]
        ddskip = jnp.sum(dyy * u, axis=0, keepdims=True)
        du = dyy * dskip
        dyb = dyy.astype(BF16)
        ub = u.astype(BF16)
        dcr, dci, dbr, dbi, dar, dai = [], [], [], [], [], []
        for k in range(S5_TILE_STATE // LANES):
            sl = slice(k * LANES, (k + 1) * LANES)
            akr, aki = a_r[0][:, sl], a_i[0][:, sl]
            br = jnp.dot(ub, bdr[0][:, sl], preferred_element_type=F32)
            bi = jnp.dot(ub, bdi[0][:, sl], preferred_element_type=F32)
            sr, si = scan_phases(br, bi, akr, aki)
            dcr.append(lax.dot_general(sr.astype(BF16), dyb, tr, preferred_element_type=F32))
            dci.append(-lax.dot_general(si.astype(BF16), dyb, tr, preferred_element_type=F32))
            gr = lax.dot_general(dyb, cdr[0][sl, :], nt, preferred_element_type=F32)
            gi = -lax.dot_general(dyb, cdi[0][sl, :], nt, preferred_element_type=F32)
            gr, gi = scan_phases(gr, gi, akr, -aki, reverse=True)
            spr, spi = _prev_phased(sr), _prev_phased(si)
            dar.append(jnp.sum(gr * spr + gi * spi, axis=0, keepdims=True))
            dai.append(jnp.sum(gi * spr - gr * spi, axis=0, keepdims=True))
            grb, gib = gr.astype(BF16), gi.astype(BF16)
            dbr.append(lax.dot_general(ub, grb, tr, preferred_element_type=F32))
            dbi.append(lax.dot_general(ub, gib, tr, preferred_element_type=F32))
            du = du + lax.dot_general(grb, bdr[0][:, sl], nt, preferred_element_type=F32)
            du = du + lax.dot_general(gib, bdi[0][:, sl], nt, preferred_element_type=F32)
        return (du, ddskip, jnp.concatenate(dbr, axis=1)[None], jnp.concatenate(dbi, axis=1)[None],
                jnp.concatenate(dcr, axis=0)[None], jnp.concatenate(dci, axis=0)[None],
                jnp.concatenate(dar, axis=1)[None], jnp.concatenate(dai, axis=1)[None])

    du, ddskip, dbdr, dbdi, dcdr, dcdi, dar, dai, *side_out = _seqwise_w(
        fn, [h, y, dyg], [p["d"]], [(bdr, b_spec), (bdi, b_spec), (cdr, c_spec), (cdi, c_spec), (a_r, a_spec), (a_i, a_spec)],
        [(d, F32)], [(1, d)], ct=LANES, name="s5_dcore", phased=True,
        wreds=[((S5_TILES, LANES, S5_TILE_STATE), b_spec), ((S5_TILES, LANES, S5_TILE_STATE), b_spec),
               ((S5_TILES, S5_TILE_STATE, LANES), c_spec), ((S5_TILES, S5_TILE_STATE, LANES), c_spec),
               ((S5_TILES, 1, S5_TILE_STATE), a_spec), ((S5_TILES, 1, S5_TILE_STATE), a_spec)], side=side)
    d_bbr = _s5_diag_blocks(dbdr, S5_GROUP, S5_STATE).transpose(0, 2, 1)
    d_bbi = _s5_diag_blocks(dbdi, S5_GROUP, S5_STATE).transpose(0, 2, 1)
    d_cre = _s5_diag_blocks(dcdr, S5_STATE, S5_GROUP).transpose(0, 2, 1)
    d_cim = _s5_diag_blocks(dcdi, S5_STATE, S5_GROUP).transpose(0, 2, 1)
    d_lre, d_lim, d_ldt, d_bre, d_bim = s5_prep_bwd(
        p["lam_re"][..., None], p["lam_im"][..., None], p["log_dt"][:, None, None], p["b_re"], p["b_im"],
        dar.reshape(S5_GROUPS, S5_STATE, 1), dai.reshape(S5_GROUPS, S5_STATE, 1), d_bbr, d_bbi)
    dx, dg = rms_bwd(x, g, [du.reshape(t, d)], dy, "s5_drms")
    grads = dict(g=dg, lam_re=d_lre[..., 0], lam_im=d_lim[..., 0], log_dt=d_ldt[:, 0, 0], b_re=d_bre, b_im=d_bim,
                 c_re=d_cre, c_im=d_cim, d=ddskip, w_val=d_wv, w_gate=d_wg, b_out=jnp.concatenate([dbv, dbg], axis=1))
    return dx, grads, side_out


def _expm1_nonpos(x):
    u = jnp.exp(x)
    safe = (u - 1.0) * x / jnp.log(u)
    return jnp.where(u == 1.0, x, jnp.where(x < -20.0, -1.0, safe))


def _softplus(x):
    return jnp.maximum(x, 0.0) + jnp.log(1.0 + jnp.exp(-jnp.abs(x)))


def scan_ltv(a, b, reverse=False):
    shift = shift_up if reverse else shift_down
    seq = a.shape[0]
    d = 1
    while d < seq:
        b = b + a * shift(b, d)
        if 2 * d < seq:
            a = a * shift(a, d, 1.0)
        d *= 2
    return b


def _lru_gates(rec0, cw, cb, ba, bx, lam, wa, wx):
    rec = conv_fwd(rec0, cw, cb)
    recb = rec.astype(BF16)
    r = sigmoid(jnp.dot(recb, wa, preferred_element_type=F32) + ba)
    i = sigmoid(jnp.dot(recb, wx, preferred_element_type=F32) + bx)
    sp = _softplus(-lam)
    log_a = -LRU_C * r * sp
    a = jnp.exp(log_a)
    mult = jnp.sqrt(-_expm1_nonpos(2.0 * log_a))
    return rec, recb, r, i, sp, a, mult


def lru_fwd(x, g, p, bsz, side=()):
    t, d = x.shape
    seq = t // bsz
    xn = rms_fwd(x, g, BF16, "lru_rms")
    gb = mm(xn, p["w_gate"], name="lru_gb").reshape(bsz, seq, d)
    rec0 = mm(xn, p["w_rec"], name="lru_rec").reshape(bsz, seq, d)
    wspec = pl.BlockSpec((1, LRU_BLOCK, LRU_BLOCK), lambda c, b: (c, 0, 0))

    def fn(c, rec0, gb, cw, cb, ba, bx, lam, wa, wx):
        rec, _, _, i, _, a, mult = _lru_gates(rec0, cw, cb, ba, bx, lam, wa[0], wx[0])
        h = scan_ltv(a, mult * (i * rec))
        return _gelu(gb) * h

    y, *side_out = _seqwise_w(fn, [rec0, gb], [p["conv_w"], p["conv_b"], p["b_a"], p["b_x"], p["lam"]],
                              [(p["w_a"], wspec), (p["w_x"], wspec)], [(d, BF16)], [], ct=LRU_BLOCK, name="lru_core",
                              side=side)
    y = y.reshape(t, d)
    out = mm(y, p["w_out"], acc=x, name="lru_out")
    return out, (x, xn, gb, rec0, y), side_out


def lru_bwd(dy, saved, g, p, side=()):
    x, xn, gb, rec0, y = saved
    bsz, seq, d = gb.shape
    t = bsz * seq
    d_wout = mm(y, dy, ta=True, name="lru_dwout")
    dyy = mm(dy, p["w_out"], tb=True, name="lru_dy").reshape(bsz, seq, d)
    wspec = pl.BlockSpec((1, LRU_BLOCK, LRU_BLOCK), lambda c, b: (c, 0, 0))
    tr = (((0,), (0,)), ((), ()))
    nt = (((1,), (1,)), ((), ()))

    def fn(c, rec0, gb, dyy, cw, cb, ba, bx, lam, wa, wx):
        rec, recb, r, i, sp, a, mult = _lru_gates(rec0, cw, cb, ba, bx, lam, wa[0], wx[0])
        h = scan_ltv(a, mult * (i * rec))
        gg, gelu_vjp = jax.vjp(_gelu, gb)
        dgb = gelu_vjp(dyy * h)[0]
        gr = scan_ltv(shift_up(a, 1), dyy * gg, reverse=True)
        da = gr * shift_down(h, 1)
        dmult = gr * i * rec
        di = gr * mult * rec
        drec = gr * mult * i
        dla = da * a - dmult * (a * a) / mult
        dr = dla * (-LRU_C * sp)
        dlam = jnp.sum(dla * (-LRU_C * r), axis=0, keepdims=True) * (-sigmoid(-lam))
        dpa = dr * r * (1.0 - r)
        dpx = di * i * (1.0 - i)
        dpab, dpxb = dpa.astype(BF16), dpx.astype(BF16)
        dwa = lax.dot_general(recb, dpab, tr, preferred_element_type=F32)
        dwx = lax.dot_general(recb, dpxb, tr, preferred_element_type=F32)
        drec = drec + lax.dot_general(dpab, wa[0], nt, preferred_element_type=F32)
        drec = drec + lax.dot_general(dpxb, wx[0], nt, preferred_element_type=F32)
        drec0, dcw, dcb = conv_bwd(rec0, cw, drec)
        return (dgb, drec0, dcw, dcb, jnp.sum(dpa, axis=0, keepdims=True), jnp.sum(dpx, axis=0, keepdims=True), dlam,
                dwa[None], dwx[None])

    kw = p["conv_w"].shape[0]
    dgb, drec0, dcw, dcb, dba, dbx, dlam, dwa, dwx, *side_out = _seqwise_w(
        fn, [rec0, gb, dyy], [p["conv_w"], p["conv_b"], p["b_a"], p["b_x"], p["lam"]],
        [(p["w_a"], wspec), (p["w_x"], wspec)], [(d, BF16), (d, BF16)],
        [(kw, d), (1, d), (1, d), (1, d), (1, d)], ct=LRU_BLOCK, name="lru_dcore",
        wreds=[(p["w_a"].shape, wspec), (p["w_x"].shape, wspec)], side=side)
    dgb = dgb.reshape(t, d)
    drec0 = drec0.reshape(t, d)
    d_wgate = mm(xn, dgb, ta=True, name="lru_dwgate")
    d_wrec = mm(xn, drec0, ta=True, name="lru_dwrec")
    dxn = mm(dgb, p["w_gate"], tb=True, name="lru_dxn1")
    dxn = mm(drec0, p["w_rec"], tb=True, acc=dxn, name="lru_dxn2")
    dx, dg = rms_bwd(x, g, [dxn], dy, "lru_drms")
    grads = dict(g=dg, w_gate=d_wgate, w_rec=d_wrec, conv_w=dcw, conv_b=dcb, w_a=dwa, b_a=dba, w_x=dwx, b_x=dbx,
                 lam=dlam, w_out=d_wout)
    return dx, grads, side_out


_NT = (((1,), (1,)), ((), ()))
_TN = (((0,), (0,)), ((), ()))


def _head_norm(x, g):
    lo = lax.broadcasted_iota(jnp.int32, x.shape, 1) < SB_HEAD_DIM
    x2 = x * x
    s_lo = jnp.sum(jnp.where(lo, x2, 0.0), axis=-1, keepdims=True)
    s_hi = jnp.sum(jnp.where(lo, 0.0, x2), axis=-1, keepdims=True)
    ms = jnp.where(lo, s_lo, s_hi) * (1.0 / SB_HEAD_DIM)
    return x * lax.rsqrt(ms + EPS) * g


def _log_sigmoid(z):
    return jnp.minimum(z, 0.0) - jnp.log(1.0 + jnp.exp(-jnp.abs(z)))


def _dot_split(x, m):
    hi = x.astype(BF16)
    lo = (x - hi.astype(F32)).astype(BF16)
    return jnp.dot(hi, m, preferred_element_type=F32) + jnp.dot(lo, m, preferred_element_type=F32)


def _tri(cmp, n):
    r = lax.broadcasted_iota(jnp.int32, (n, n), 0)
    c = lax.broadcasted_iota(jnp.int32, (n, n), 1)
    return cmp(r, c)


def sb_attn_fwd(q, k, v, qg, kg, side=()):
    bsz, seq, d = q.shape
    blk = min(ATT_BLOCK, seq)
    nq = seq // blk
    scale = 1.0 / math.sqrt(SB_HEAD_DIM)
    side, exchange, side_sems, side_shapes = _side_plan(side)
    n_x = len(side)

    def body(*refs):
        q_ref, k_ref, v_ref, qg_ref, kg_ref = refs[:5]
        o_ref, tot_ref = refs[5 + n_x:7 + n_x]
        qn, kn, vb = refs[7 + 2 * n_x:10 + 2 * n_x]
        if side:
            start, wait = exchange(refs[5:5 + n_x], refs[7 + n_x:7 + 2 * n_x], *refs[10 + 2 * n_x:])
            pl.when(jnp.logical_and(pl.program_id(0) == 0, pl.program_id(1) == 0))(start)
        qn[...] = _head_norm(q_ref[...], qg_ref[...]).astype(BF16)
        kn[...] = _head_norm(k_ref[...], kg_ref[...]).astype(BF16)
        vb[...] = v_ref[...].astype(BF16)
        lane_lo = lax.broadcasted_iota(jnp.int32, (blk, LANES), 1) < SB_HEAD_DIM
        causal = _tri(lambda r, c: c < r, blk)
        upper = _tri(lambda r, c: r > c, blk).astype(BF16)

        def q_block(qi, _):
            rows = pl.ds(pl.multiple_of(qi * blk, blk), blk)
            q_all = qn[rows, :]
            zero = jnp.zeros((), BF16)
            qbs = (jnp.where(lane_lo, q_all, zero), jnp.where(lane_lo, zero, q_all))

            def block(j, state, masked):
                cols = pl.ds(pl.multiple_of(j * blk, blk), blk)
                kb, vv = kn[cols, :], vb[cols, :]
                zs = [lax.dot_general(qb, kb, _NT, preferred_element_type=F32) * scale for qb in qbs]
                lss = [_log_sigmoid(z) for z in zs]
                lgs = [ls - z for ls, z in zip(lss, zs)]
                if masked:
                    lgs = [jnp.where(causal, lg, 0.0) for lg in lgs]
                later = [_dot_split(lg, upper) for lg in lgs]
                atts = [jnp.exp(ls + carry + cs) for ls, (carry, _), cs in zip(lss, state, later)]
                if masked:
                    atts = [jnp.where(causal, att, 0.0) for att in atts]
                outs = [jnp.dot(att.astype(BF16), vv, preferred_element_type=F32) for att in atts]
                return tuple((carry + jnp.sum(lg, axis=1, keepdims=True), acc + out)
                             for (carry, acc), lg, out in zip(state, lgs, outs))

            init = (jnp.zeros((blk, 1), F32), jnp.zeros((blk, LANES), F32))
            state = block(qi, (init, init), True)
            state = lax.fori_loop(0, qi, lambda jj, s: block(qi - 1 - jj, s, False), state)
            (carry0, acc0), (carry1, acc1) = state
            o_ref[rows, :] = jnp.where(lane_lo, acc0, acc1).astype(o_ref.dtype)
            tot_ref[rows, :] = jnp.where(lane_lo, carry0, carry1)
            return 0

        lax.fori_loop(0, nq, q_block, 0)
        if side:
            pl.when(jnp.logical_and(pl.program_id(0) == bsz - 1, pl.program_id(1) == d // LANES - 1))(wait)

    spec = pl.BlockSpec((None, seq, LANES), lambda b, c: (b, 0, c))
    gspec = pl.BlockSpec((1, LANES), lambda b, c: (0, 0))
    return pl.pallas_call(
        body, name="sb_attn_fwd", grid=(bsz, d // LANES), in_specs=[spec, spec, spec, gspec, gspec] + [_HBM] * n_x,
        out_specs=[spec, spec] + [_HBM] * n_x,
        out_shape=[jax.ShapeDtypeStruct((bsz, seq, d), BF16), jax.ShapeDtypeStruct((bsz, seq, d), F32)] + side_shapes,
        scratch_shapes=[pltpu.VMEM((seq, LANES), BF16)] * 3 + side_sems,
        compiler_params=_params("arbitrary", "arbitrary"),
    )(q, k, v, qg, kg, *side)


def sb_attn_bwd(q, k, v, qg, kg, tot, do, side=()):
    bsz, seq, d = q.shape
    blk = min(ATT_BLOCK, seq)
    nq = seq // blk
    scale = 1.0 / math.sqrt(SB_HEAD_DIM)
    side, exchange, side_sems, side_shapes = _side_plan(side)
    n_x = len(side)
    n_in, n_out, n_scr = 7, 5, 6

    def body(*refs):
        q_ref, k_ref, v_ref, qg_ref, kg_ref, tot_ref, do_ref = refs[:n_in]
        dq_ref, dk_ref, dv_ref, dqg_ref, dkg_ref = refs[n_in + n_x:n_in + n_x + n_out]
        qn, kn, vb, dqn, dkn, dvv = refs[n_in + 2 * n_x + n_out:n_in + 2 * n_x + n_out + n_scr]
        if side:
            start, wait = exchange(refs[n_in:n_in + n_x], refs[n_in + n_x + n_out:n_in + 2 * n_x + n_out],
                                         *refs[n_in + 2 * n_x + n_out + n_scr:])
            pl.when(jnp.logical_and(pl.program_id(0) == 0, pl.program_id(1) == 0))(start)
        qn[...] = _head_norm(q_ref[...], qg_ref[...]).astype(BF16)
        kn[...] = _head_norm(k_ref[...], kg_ref[...]).astype(BF16)
        vb[...] = v_ref[...].astype(BF16)
        dkn[...] = jnp.zeros_like(dkn)
        dvv[...] = jnp.zeros_like(dvv)
        lane_lo = lax.broadcasted_iota(jnp.int32, (blk, LANES), 1) < SB_HEAD_DIM
        causal = _tri(lambda r, c: c < r, blk)
        upto = _tri(lambda r, c: r <= c, blk).astype(BF16)
        before = _tri(lambda r, c: r < c, blk).astype(BF16)

        def q_block(qi, _):
            rows = pl.ds(pl.multiple_of(qi * blk, blk), blk)
            zero = jnp.zeros((), BF16)
            q_all, do_all, tot_all = qn[rows, :], do_ref[rows, :].astype(BF16), tot_ref[rows, :]
            heads = []
            for hm in (lane_lo, jnp.logical_not(lane_lo)):
                heads.append((jnp.where(hm, q_all, zero), jnp.where(hm, do_all, zero),
                              jnp.max(jnp.where(hm, tot_all, -jnp.inf), axis=1, keepdims=True)))

            def block(j, state, masked):
                cols = pl.ds(pl.multiple_of(j * blk, blk), blk)
                kb, vv = kn[cols, :], vb[cols, :]
                two = range(2)
                zs = [lax.dot_general(heads[h][0], kb, _NT, preferred_element_type=F32) * scale for h in two]
                datts = [lax.dot_general(heads[h][1], vv, _NT, preferred_element_type=F32) for h in two]
                lss = [_log_sigmoid(z) for z in zs]
                lgs = [ls - z for ls, z in zip(lss, zs)]
                if masked:
                    lgs = [jnp.where(causal, lg, 0.0) for lg in lgs]
                sofar = [_dot_split(lg, upto) for lg in lgs]
                atts = [jnp.exp(lss[h] + heads[h][2] - state[h][0] - sofar[h]) for h in two]
                if masked:
                    atts = [jnp.where(causal, att, 0.0) for att in atts]
                es = [att * datt for att, datt in zip(atts, datts)]
                earlier = [_dot_split(e, before) for e in es]
                dlgs = [state[h][1] + earlier[h] for h in two]
                if masked:
                    dlgs = [jnp.where(causal, dlg, 0.0) for dlg in dlgs]
                betas = [jnp.exp(ls) for ls in lss]
                dzbs = [((es[h] * (1.0 - betas[h]) - dlgs[h] * betas[h]) * scale).astype(BF16) for h in two]
                dvv[cols, :] += sum(lax.dot_general(atts[h].astype(BF16), heads[h][1], _TN, preferred_element_type=F32)
                                    for h in two)
                dkn[cols, :] += sum(lax.dot_general(dzbs[h], heads[h][0], _TN, preferred_element_type=F32) for h in two)
                dqs = [jnp.dot(dzb, kb, preferred_element_type=F32) for dzb in dzbs]
                return tuple((state[h][0] + jnp.sum(lgs[h], axis=1, keepdims=True),
                              state[h][1] + jnp.sum(es[h], axis=1, keepdims=True), state[h][2] + dqs[h]) for h in two)

            col0 = jnp.zeros((blk, 1), F32)
            init = (col0, col0, jnp.zeros((blk, LANES), F32))
            state = lax.fori_loop(0, qi, lambda j, s: block(j, s, False), (init, init))
            state = block(qi, state, True)
            dqn[rows, :] = jnp.where(lane_lo, state[0][2], state[1][2])
            return 0

        lax.fori_loop(0, nq, q_block, 0)

        def fold(x):
            return x + pltpu.roll(x, SB_HEAD_DIM, 1)

        _, q_vjp = jax.vjp(_head_norm, q_ref[...], qg_ref[...])
        dq, dqg = q_vjp(dqn[...])
        _, k_vjp = jax.vjp(_head_norm, k_ref[...], kg_ref[...])
        dk, dkg = k_vjp(dkn[...])
        dq_ref[...] = dq.astype(dq_ref.dtype)
        dk_ref[...] = dk.astype(dk_ref.dtype)
        dv_ref[...] = dvv[...].astype(dv_ref.dtype)
        first = jnp.logical_and(pl.program_id(0) == 0, pl.program_id(1) == 0)

        @pl.when(first)
        def _():
            dqg_ref[...] = fold(dqg)
            dkg_ref[...] = fold(dkg)

        @pl.when(jnp.logical_not(first))
        def _():
            dqg_ref[...] += fold(dqg)
            dkg_ref[...] += fold(dkg)

        if side:
            pl.when(jnp.logical_and(pl.program_id(0) == bsz - 1, pl.program_id(1) == d // LANES - 1))(wait)

    spec = pl.BlockSpec((None, seq, LANES), lambda b, c: (b, 0, c))
    gspec = pl.BlockSpec((1, LANES), lambda b, c: (0, 0))
    act = jax.ShapeDtypeStruct((bsz, seq, d), BF16)
    gain = jax.ShapeDtypeStruct((1, LANES), F32)
    return pl.pallas_call(
        body, name="sb_attn_bwd", grid=(bsz, d // LANES),
        in_specs=[spec, spec, spec, gspec, gspec, spec, spec] + [_HBM] * n_x,
        out_specs=[spec, spec, spec, gspec, gspec] + [_HBM] * n_x,
        out_shape=[act, act, act, gain, gain] + side_shapes,
        scratch_shapes=[pltpu.VMEM((seq, LANES), BF16)] * 3 + [pltpu.VMEM((seq, LANES), F32)] * 3 + side_sems,
        compiler_params=_params("arbitrary", "arbitrary"),
    )(q, k, v, qg, kg, tot, do, *side)


def sb_fwd(x, g, p, bsz, side=()):
    t, d = x.shape
    seq = t // bsz
    xn = rms_fwd(x, g, BF16, "sb_rms")
    q = mm(xn, p["w_q"], name="sb_q").reshape(bsz, seq, d)
    k = mm(xn, p["w_k"], name="sb_k").reshape(bsz, seq, d)
    v = mm(xn, p["w_v"], name="sb_v").reshape(bsz, seq, d)
    qg = jnp.tile(p["q_g"], (1, 2))
    kg = jnp.tile(p["k_g"], (1, 2))
    o, tot, *side_out = sb_attn_fwd(q, k, v, qg, kg, side)
    o = o.reshape(t, d)
    out = mm(o, p["w_o"], acc=x, name="sb_out")
    return out, (x, xn, q, k, v, o, tot), side_out


def sb_bwd(dy, saved, g, p, side=()):
    x, xn, q, k, v, o, tot = saved
    bsz, seq, d = q.shape
    t = bsz * seq
    d_wo = mm(o, dy, ta=True, name="sb_dwo")
    do = mm(dy, p["w_o"], tb=True, out_dtype=BF16, name="sb_do").reshape(bsz, seq, d)
    qg = jnp.tile(p["q_g"], (1, 2))
    kg = jnp.tile(p["k_g"], (1, 2))
    dq, dk, dv, dqg, dkg, *side_out = sb_attn_bwd(q, k, v, qg, kg, tot, do, side)
    dq, dk, dv = dq.reshape(t, d), dk.reshape(t, d), dv.reshape(t, d)
    d_wq = mm(xn, dq, ta=True, name="sb_dwq")
    d_wk = mm(xn, dk, ta=True, name="sb_dwk")
    d_wv = mm(xn, dv, ta=True, name="sb_dwv")
    dxn = mm(dq, p["w_q"], tb=True, name="sb_dxn1")
    dxn = mm(dk, p["w_k"], tb=True, acc=dxn, name="sb_dxn2")
    dxn = mm(dv, p["w_v"], tb=True, acc=dxn, name="sb_dxn3")
    dx, dg = rms_bwd(x, g, [dxn], dy, "sb_drms")
    grads = dict(g=dg, w_q=d_wq, w_k=d_wk, w_v=d_wv, w_o=d_wo, q_g=dqg[:, :SB_HEAD_DIM], k_g=dkg[:, :SB_HEAD_DIM])
    return dx, grads, side_out


def loss_head(y, target):
    d = y.shape[1]

    def fn(y, tgt):
        err = y - tgt
        return err * (1.0 / d), jnp.sum(err * err, axis=0, keepdims=True)

    dy, sq = rowwise(fn, [y, target], [], [(d, F32)], [(1, d)], name="loss_head")
    return sq, dy


def adamw(parts, w, m, v, name):
    n, r, c = parts.shape
    tr = _pick(r, max(16, (1 << 20) // (c * n)), 16 if parts.dtype == BF16 else 8)

    def body(p_ref, w_ref, m_ref, v_ref, g_ref, d_ref, nm_ref, nv_ref):
        g = p_ref[0].astype(F32)
        for i in range(1, n):
            g = g + p_ref[i].astype(F32)
        nm = ADAM_B1 * m_ref[...] + (1.0 - ADAM_B1) * g
        nv = ADAM_B2 * v_ref[...] + (1.0 - ADAM_B2) * (g * g)
        m_hat = nm / (1.0 - ADAM_B1 ** ADAM_STEP)
        v_hat = nv / (1.0 - ADAM_B2 ** ADAM_STEP)
        g_ref[...] = g
        d_ref[...] = -ADAM_LR * (m_hat / (jnp.sqrt(v_hat) + ADAM_EPS) + ADAM_WD * w_ref[...])
        nm_ref[...] = nm
        nv_ref[...] = nv

    spec = pl.BlockSpec((tr, c), lambda i: (i, 0))
    return pl.pallas_call(
        body, name=name, grid=(r // tr,), in_specs=[pl.BlockSpec((n, tr, c), lambda i: (0, i, 0)), spec, spec, spec],
        out_specs=[spec] * 4, out_shape=[jax.ShapeDtypeStruct((r, c), F32)] * 4,
        compiler_params=_params("parallel"),
    )(parts, w, m, v)


_HBM = pl.BlockSpec(memory_space=pltpu.HBM)


def _mesh_pos():
    return lax.axis_index("x"), lax.axis_index("y"), lax.axis_index("c")


def all_gather(shards, name):
    n = len(shards)

    def body(*refs):
        start, finish = _gather_exchange(refs[:n], refs[n:2 * n], *refs[2 * n:])
        start()
        finish()

    return pl.pallas_call(
        body, name=name, in_specs=[_HBM] * n, out_specs=[_HBM] * n,
        out_shape=[jax.ShapeDtypeStruct((N_DEV, *s.shape), s.dtype) for s in shards],
        scratch_shapes=_gather_exchange_sems(n),
    )(*shards)


def _gather_exchange(ins, outs, send_sems, recv_sems, local_sems):
    n = len(ins)
    x, y, c = _mesh_pos()
    me, sibling = (x, y, c), (x, y, 1 - c)
    chips = [(1 - x, y), (x, 1 - y), (1 - x, 1 - y)]

    def copy(a, k, block, to, src=None):
        px, py, pc = block
        dst = outs[a].at[4 * px + 2 * py + pc]
        return pltpu.make_async_remote_copy(
            src_ref=dst if src is None else src, dst_ref=dst, send_sem=send_sems.at[a, k],
            recv_sem=recv_sems.at[a, k], device_id=to, device_id_type=MESH)

    mine = [pltpu.make_async_copy(ins[a], outs[a].at[4 * x + 2 * y + c], local_sems.at[a]) for a in range(n)]
    first = []
    for a in range(n):
        first.append(copy(a, 0, me, sibling, src=ins[a]))
        first += [copy(a, 1 + j, me, (*chip, c), src=ins[a]) for j, chip in enumerate(chips)]

    def start():
        for cp in mine + first:
            cp.start()

    def finish():
        passed = []
        for a in range(n):
            for j, chip in enumerate(chips):
                copy(a, 1 + j, (*chip, c), me).wait_recv()
                cp = copy(a, 4 + j, (*chip, c), sibling)
                cp.start()
                passed.append(cp)
        for a in range(n):
            copy(a, 0, sibling, me).wait_recv()
            for j, chip in enumerate(chips):
                copy(a, 4 + j, (*chip, 1 - c), me).wait_recv()
        for cp in first + passed:
            cp.wait_send()
        for cp in mine:
            cp.wait()

    return start, finish


def _gather_exchange_sems(n):
    return [pltpu.SemaphoreType.DMA((n, 7)), pltpu.SemaphoreType.DMA((n, 7)), pltpu.SemaphoreType.DMA((n,))]


def exchange_cores(parts, name):
    n = len(parts)

    def body(*refs):
        ins, outs = refs[:n], refs[n:2 * n]
        send_sems, recv_sems = refs[2 * n:]
        x, y, c = _mesh_pos()
        copies = []
        for a in range(n):
            for p in range(4):
                copies.append(pltpu.make_async_remote_copy(
                    src_ref=ins[a].at[p, 1 - c], dst_ref=outs[a].at[p], send_sem=send_sems.at[a, p],
                    recv_sem=recv_sems.at[a, p], device_id=(x, y, 1 - c), device_id_type=MESH))
        for cp in copies:
            cp.start()
        for cp in copies:
            cp.wait()

    return pl.pallas_call(
        body, name=name, in_specs=[_HBM] * n, out_specs=[_HBM] * n,
        out_shape=[jax.ShapeDtypeStruct((4, *s.shape[2:]), s.dtype) for s in parts],
        scratch_shapes=[pltpu.SemaphoreType.DMA((n, 4)), pltpu.SemaphoreType.DMA((n, 4))],
    )(*parts)


def _chip_exchange(ins, outs, send_sems, recv_sems, local_sems):
    x, y, c = _mesh_pos()
    mine = 2 * x + y
    copies = []
    for a in range(len(ins)):
        copies.append(pltpu.make_async_copy(ins[a].at[mine], outs[a].at[mine], local_sems.at[a]))
        for r in range(1, 4):
            qx = 1 - x if r & 2 else x
            qy = 1 - y if r & 1 else y
            copies.append(pltpu.make_async_remote_copy(
                src_ref=ins[a].at[2 * qx + qy], dst_ref=outs[a].at[mine], send_sem=send_sems.at[a, r - 1],
                recv_sem=recv_sems.at[a, r - 1], device_id=(qx, qy, c), device_id_type=MESH))

    def start():
        for cp in copies:
            cp.start()

    def wait():
        for cp in copies:
            cp.wait()

    return start, wait


def _chip_exchange_sems(n):
    return [pltpu.SemaphoreType.DMA((n, 3)), pltpu.SemaphoreType.DMA((n, 3)), pltpu.SemaphoreType.DMA((n,))]


def _side_plan(side):
    if not side:
        return (), None, [], []
    kind, arrays = side
    n = len(arrays)
    if kind == "chips":
        return arrays, _chip_exchange, _chip_exchange_sems(n), [jax.ShapeDtypeStruct(a.shape, a.dtype) for a in arrays]
    assert kind == "gather", kind
    shapes = [jax.ShapeDtypeStruct((N_DEV, *a.shape), a.dtype) for a in arrays]
    return arrays, _gather_exchange, _gather_exchange_sems(n), shapes


def exchange_chips(parts, name):
    n = len(parts)

    def body(*refs):
        start, wait = _chip_exchange(refs[:n], refs[n:2 * n], *refs[2 * n:])
        start()
        wait()

    return pl.pallas_call(
        body, name=name, in_specs=[_HBM] * n, out_specs=[_HBM] * n,
        out_shape=[jax.ShapeDtypeStruct(s.shape, s.dtype) for s in parts],
        scratch_shapes=_chip_exchange_sems(n),
    )(*parts)


def add_own(core, parts, recv, out_dtype, name):
    _, _, r, c = parts.shape
    tr = _pick(r, max(16, (1 << 19) // c), 16)

    def body(core_ref, p_ref, r_ref, o_ref):
        o_ref[...] = (p_ref[...] + r_ref[...]).astype(out_dtype)

    grid_spec = pltpu.PrefetchScalarGridSpec(
        num_scalar_prefetch=1, grid=(4, r // tr),
        in_specs=[pl.BlockSpec((None, None, tr, c), lambda p, i, core_ref: (p, core_ref[0], i, 0)),
                  pl.BlockSpec((None, tr, c), lambda p, i, core_ref: (p, i, 0))],
        out_specs=pl.BlockSpec((None, tr, c), lambda p, i, core_ref: (p, i, 0)))
    return pl.pallas_call(
        body, name=name, grid_spec=grid_spec, out_shape=jax.ShapeDtypeStruct((4, r, c), out_dtype),
        compiler_params=_params("parallel", "parallel"),
    )(core, parts, recv)


PARAMS = {
    "norm_mix_g": ((4, 1024), None), "norm_ffn_g": ((4, 1024), None),
    "pool_w": ((1, 4, 256, 256), 2), "pool_b": ((1, 1024), None), "pool_scale": ((1, 1024), None),
    "s5_lam_re": ((1, 64, 64), None), "s5_lam_im": ((1, 64, 64), None), "s5_log_dt": ((1, 64), None),
    "s5_b_re": ((1, 64, 64, 16), None), "s5_b_im": ((1, 64, 64, 16), None),
    "s5_c_re": ((1, 64, 16, 64), None), "s5_c_im": ((1, 64, 16, 64), None),
    "s5_d": ((1, 1024), 1), "s5_w_out": ((1, 1024, 2048), 2), "s5_b_out": ((1, 2048), 1),
    "lru_w_in": ((1, 1024, 2048), 2), "lru_conv_w": ((1, 4, 1024), 2), "lru_conv_b": ((1, 1024), 1),
    "lru_w_a": ((1, 4, 256, 256), 2), "lru_b_a": ((1, 1024), 1), "lru_w_x": ((1, 4, 256, 256), 2),
    "lru_b_x": ((1, 1024), 1), "lru_lam": ((1, 1024), 1), "lru_w_out": ((1, 1024, 1024), 1),
    "sb_w_qkv": ((1, 1024, 3072), 2), "sb_q_g": ((1, 64), None), "sb_k_g": ((1, 64), None),
    "sb_w_o": ((1, 1024, 1024), 1),
    "ffn_w_in": ((4, 1024, 5632), 2), "ffn_conv_w": ((4, 3, 5632), 2), "ffn_conv_b": ((4, 5632), None),
    "ffn_w_out": ((4, 2816, 1024), 1),
}
NAMES = list(PARAMS)
BIG = ["s5_w_out", "lru_w_in", "lru_w_out", "sb_w_qkv", "sb_w_o", "ffn_w_in", "ffn_w_out"]
SMALL = [n for n in NAMES if PARAMS[n][1] is not None and n not in BIG]
REPL = [n for n in NAMES if PARAMS[n][1] is None]


def _local_shape(name):
    shape, ax = PARAMS[name]
    return tuple(s // N_DEV if i == ax else s for i, s in enumerate(shape))


def _to_natural(gathered, name):
    shape, ax = PARAMS[name]
    return jnp.moveaxis(gathered, 0, ax).reshape(shape)


def _to_shards(nat, name):
    shape, ax = PARAMS[name]
    split = shape[:ax] + (N_DEV, shape[ax] // N_DEV) + shape[ax + 1:]
    return jnp.moveaxis(nat.reshape(split), ax, 0)


def _pack(arrays, lead=()):
    flat = jnp.concatenate([a.reshape(*lead, -1) for a in arrays], axis=-1)
    size = flat.shape[-1]
    rows = -(-size // (8 * LANES)) * 8
    flat = jnp.pad(flat, [(0, 0)] * len(lead) + [(0, rows * LANES - size)])
    return flat.reshape(*lead, rows, LANES)


def _unpack(packed, shapes, lead=()):
    flat = packed.reshape(*lead, -1)
    out, off = [], 0
    for s in shapes:
        size = math.prod(s)
        out.append(flat[..., off:off + size].reshape(*lead, *s))
        off += size
    return out


def kernel(x, norm_mix_g, norm_ffn_g, pool_w, pool_b, pool_scale, s5_lam_re, s5_lam_im, s5_log_dt, s5_b_re, s5_b_im, s5_c_re, s5_c_im, s5_d, s5_w_out, s5_b_out, lru_w_in, lru_conv_w, lru_conv_b, lru_w_a, lru_b_a, lru_w_x, lru_b_x, lru_lam, lru_w_out, sb_w_qkv, sb_q_g, sb_k_g, sb_w_o, ffn_w_in, ffn_conv_w, ffn_conv_b, ffn_w_out, loss_target, m_norm_mix_g, m_norm_ffn_g, m_pool_w, m_pool_b, m_pool_scale, m_s5_lam_re, m_s5_lam_im, m_s5_log_dt, m_s5_b_re, m_s5_b_im, m_s5_c_re, m_s5_c_im, m_s5_d, m_s5_w_out, m_s5_b_out, m_lru_w_in, m_lru_conv_w, m_lru_conv_b, m_lru_w_a, m_lru_b_a, m_lru_w_x, m_lru_b_x, m_lru_lam, m_lru_w_out, m_sb_w_qkv, m_sb_q_g, m_sb_k_g, m_sb_w_o, m_ffn_w_in, m_ffn_conv_w, m_ffn_conv_b, m_ffn_w_out, v_norm_mix_g, v_norm_ffn_g, v_pool_w, v_pool_b, v_pool_scale, v_s5_lam_re, v_s5_lam_im, v_s5_log_dt, v_s5_b_re, v_s5_b_im, v_s5_c_re, v_s5_c_im, v_s5_d, v_s5_w_out, v_s5_b_out, v_lru_w_in, v_lru_conv_w, v_lru_conv_b, v_lru_w_a, v_lru_b_a, v_lru_w_x, v_lru_b_x, v_lru_lam, v_lru_w_out, v_sb_w_qkv, v_sb_q_g, v_sb_k_g, v_sb_w_o, v_ffn_w_in, v_ffn_conv_w, v_ffn_conv_b, v_ffn_w_out):
    w = dict(zip(NAMES, (norm_mix_g, norm_ffn_g, pool_w, pool_b, pool_scale, s5_lam_re, s5_lam_im, s5_log_dt, s5_b_re, s5_b_im, s5_c_re, s5_c_im, s5_d, s5_w_out, s5_b_out, lru_w_in, lru_conv_w, lru_conv_b, lru_w_a, lru_b_a, lru_w_x, lru_b_x, lru_lam, lru_w_out, sb_w_qkv, sb_q_g, sb_k_g, sb_w_o, ffn_w_in, ffn_conv_w, ffn_conv_b, ffn_w_out)))
    m = dict(zip(NAMES, (m_norm_mix_g, m_norm_ffn_g, m_pool_w, m_pool_b, m_pool_scale, m_s5_lam_re, m_s5_lam_im, m_s5_log_dt, m_s5_b_re, m_s5_b_im, m_s5_c_re, m_s5_c_im, m_s5_d, m_s5_w_out, m_s5_b_out, m_lru_w_in, m_lru_conv_w, m_lru_conv_b, m_lru_w_a, m_lru_b_a, m_lru_w_x, m_lru_b_x, m_lru_lam, m_lru_w_out, m_sb_w_qkv, m_sb_q_g, m_sb_k_g, m_sb_w_o, m_ffn_w_in, m_ffn_conv_w, m_ffn_conv_b, m_ffn_w_out)))
    v = dict(zip(NAMES, (v_norm_mix_g, v_norm_ffn_g, v_pool_w, v_pool_b, v_pool_scale, v_s5_lam_re, v_s5_lam_im, v_s5_log_dt, v_s5_b_re, v_s5_b_im, v_s5_c_re, v_s5_c_im, v_s5_d, v_s5_w_out, v_s5_b_out, v_lru_w_in, v_lru_conv_w, v_lru_conv_b, v_lru_w_a, v_lru_b_a, v_lru_w_x, v_lru_b_x, v_lru_lam, v_lru_w_out, v_sb_w_qkv, v_sb_q_g, v_sb_k_g, v_sb_w_o, v_ffn_w_in, v_ffn_conv_w, v_ffn_conv_b, v_ffn_w_out)))
    return train_step(x, loss_target, w, m, v)


def train_step(x, loss_target, w, m, v):
    bsz, seq, d = x.shape
    t = bsz * seq
    n_layers = PARAMS["norm_mix_g"][0][0]

    small_shapes = [_local_shape(n) for n in SMALL]
    mixers = ["s5_w_out", "lru_w_in", "lru_w_out", "sb_w_qkv", "sb_w_o"]

    def ffn_shards(li):
        return [w["ffn_w_in"][li].astype(BF16), w["ffn_w_out"][li].astype(BF16)]

    def whole2d(gathered, name):
        _, r, c = gathered.shape
        if PARAMS[name][1] == 2:
            return gathered.transpose(1, 0, 2).reshape(r, N_DEV * c)
        return gathered.reshape(N_DEV * r, c)

    def ffn_whole(gathered):
        return whole2d(gathered[0], "ffn_w_in"), whole2d(gathered[1], "ffn_w_out")

    gathered = all_gather([w[n][0].astype(BF16) for n in mixers] + ffn_shards(0) + [_pack([w[n] for n in SMALL])],
                          "gather_weights")
    nat = {n: whole2d(g, n) for n, g in zip(mixers, gathered)}
    ffn_w = {0: ffn_whole(gathered[len(mixers):len(mixers) + 2])}
    for n, g in zip(SMALL, _unpack(gathered[-1], small_shapes, lead=(N_DEV,))):
        nat[n] = _to_natural(g, n)
    for n in REPL:
        nat[n] = w[n]

    hid = FFN_HIDDEN
    pool_p = (nat["pool_w"][0].astype(BF16), nat["pool_b"], nat["pool_scale"])
    s5_p = dict(lam_re=nat["s5_lam_re"][0], lam_im=nat["s5_lam_im"][0], log_dt=nat["s5_log_dt"][0],
                b_re=nat["s5_b_re"][0], b_im=nat["s5_b_im"][0], c_re=nat["s5_c_re"][0], c_im=nat["s5_c_im"][0],
                d=nat["s5_d"], w_val=nat["s5_w_out"][:, :d], w_gate=nat["s5_w_out"][:, d:], b_out=nat["s5_b_out"])
    lru_p = dict(w_gate=nat["lru_w_in"][:, :d], w_rec=nat["lru_w_in"][:, d:], conv_w=nat["lru_conv_w"][0],
                 conv_b=nat["lru_conv_b"], b_a=nat["lru_b_a"], b_x=nat["lru_b_x"], lam=nat["lru_lam"],
                 w_a=nat["lru_w_a"][0].astype(BF16), w_x=nat["lru_w_x"][0].astype(BF16), w_out=nat["lru_w_out"])
    sb_p = dict(w_q=nat["sb_w_qkv"][:, :d], w_k=nat["sb_w_qkv"][:, d:2 * d], w_v=nat["sb_w_qkv"][:, 2 * d:],
                w_o=nat["sb_w_o"], q_g=nat["sb_q_g"], k_g=nat["sb_k_g"])

    def ffn_p(li):
        w_in, w_out = ffn_w[li]
        return (w_in[:, :hid], w_in[:, hid:], nat["ffn_conv_w"][li], nat["ffn_conv_b"][li:li + 1], w_out)

    def gain(name, li):
        return nat[name][li:li + 1]

    h = x.reshape(t, d)
    h, pool_saved = pool_fwd(h, gain("norm_mix_g", 0), *pool_p, bsz)
    h, ffn0 = ffn_fwd(h, gain("norm_ffn_g", 0), *ffn_p(0), bsz, 0)
    h, s5_saved, arrived = s5_fwd(h, gain("norm_mix_g", 1), s5_p, bsz, ("gather", ffn_shards(1)))
    ffn_w[1] = ffn_whole(arrived)
    h, ffn1 = ffn_fwd(h, gain("norm_ffn_g", 1), *ffn_p(1), bsz, 1)
    h, lru_saved, arrived = lru_fwd(h, gain("norm_mix_g", 2), lru_p, bsz, ("gather", ffn_shards(2)))
    ffn_w[2] = ffn_whole(arrived)
    h, ffn2 = ffn_fwd(h, gain("norm_ffn_g", 2), *ffn_p(2), bsz, 2)
    h, sb_saved, arrived = sb_fwd(h, gain("norm_mix_g", 3), sb_p, bsz, ("gather", ffn_shards(3)))
    ffn_w[3] = ffn_whole(arrived)
    h, ffn3 = ffn_fwd(h, gain("norm_ffn_g", 3), *ffn_p(3), bsz, 3)
    sq, dh = loss_head(h, loss_target.reshape(t, d))
    loss = lax.psum(0.5 * jnp.sum(sq) / d, ("x", "y", "c"))

    core = lax.axis_index("c").astype(jnp.int32).reshape(1)

    def by_dest(grad2d, name):
        ax = PARAMS[name][1]
        r, c = grad2d.shape
        if ax == 2:
            return grad2d.reshape(r, N_DEV, c // N_DEV).transpose(1, 0, 2).reshape(4, 2, r, c // N_DEV)
        return grad2d.reshape(4, 2, r // N_DEV, c)

    def ffn_group(li, g):
        return [(f"ffn_w_in{li}", by_dest(jnp.concatenate([g[1], g[2]], axis=1), "ffn_w_in"), BF16),
                (f"ffn_w_out{li}", by_dest(g[5], "ffn_w_out"), BF16)]

    def core_stage(group, tag):
        recv = exchange_cores([a for _, a, _ in group], f"reduce_cores_{tag}")
        return "chips", [add_own(core, a, r, dt, f"add_{k}") for (k, a, dt), r in zip(group, recv)]

    reduced = {}
    fg = [None] * n_layers
    dh, fg[3] = ffn_bwd(dh, ffn3, gain("norm_ffn_g", 3), *ffn_p(3), 3)
    group = ffn_group(3, fg[3])
    dh, sb_g, arrived = sb_bwd(dh, sb_saved, gain("norm_mix_g", 3), sb_p, core_stage(group, "ffn3"))
    reduced.update(zip([k for k, _, _ in group], arrived))
    dh, fg[2] = ffn_bwd(dh, ffn2, gain("norm_ffn_g", 2), *ffn_p(2), 2)
    group = [("sb_w_qkv", by_dest(jnp.concatenate([sb_g["w_q"], sb_g["w_k"], sb_g["w_v"]], axis=1), "sb_w_qkv"), BF16),
             ("sb_w_o", by_dest(sb_g["w_o"], "sb_w_o"), BF16)] + ffn_group(2, fg[2])
    dh, lru_g, arrived = lru_bwd(dh, lru_saved, gain("norm_mix_g", 2), lru_p, core_stage(group, "sb_ffn2"))
    reduced.update(zip([k for k, _, _ in group], arrived))
    dh, fg[1] = ffn_bwd(dh, ffn1, gain("norm_ffn_g", 1), *ffn_p(1), 1)
    group = [("lru_w_in", by_dest(jnp.concatenate([lru_g["w_gate"], lru_g["w_rec"]], axis=1), "lru_w_in"), BF16),
             ("lru_w_out", by_dest(lru_g["w_out"], "lru_w_out"), BF16)] + ffn_group(1, fg[1])
    dh, s5_g, arrived = s5_bwd(dh, s5_saved, gain("norm_mix_g", 1), s5_p, core_stage(group, "lru_ffn1"))
    reduced.update(zip([k for k, _, _ in group], arrived))
    dh, fg[0] = ffn_bwd(dh, ffn0, gain("norm_ffn_g", 0), *ffn_p(0), 0)
    dh, pool_g = pool_bwd(dh, pool_saved, gain("norm_mix_g", 0), *pool_p)
    grad_x = dh.reshape(bsz, seq, d)

    part = {
        "norm_mix_g": jnp.concatenate([pool_g[0], s5_g["g"], lru_g["g"], sb_g["g"]], axis=0),
        "norm_ffn_g": jnp.concatenate([g[0] for g in fg], axis=0),
        "pool_w": pool_g[1][None], "pool_b": pool_g[2], "pool_scale": pool_g[3],
        "s5_lam_re": s5_g["lam_re"][None], "s5_lam_im": s5_g["lam_im"][None], "s5_log_dt": s5_g["log_dt"][None],
        "s5_b_re": s5_g["b_re"][None], "s5_b_im": s5_g["b_im"][None], "s5_c_re": s5_g["c_re"][None],
        "s5_c_im": s5_g["c_im"][None], "s5_d": s5_g["d"], "s5_b_out": s5_g["b_out"],
        "lru_conv_w": lru_g["conv_w"][None], "lru_conv_b": lru_g["conv_b"], "lru_w_a": lru_g["w_a"][None],
        "lru_b_a": lru_g["b_a"], "lru_w_x": lru_g["w_x"][None], "lru_b_x": lru_g["b_x"], "lru_lam": lru_g["lam"],
        "sb_q_g": sb_g["q_g"], "sb_k_g": sb_g["k_g"],
        "ffn_conv_w": jnp.stack([g[3] for g in fg]), "ffn_conv_b": jnp.concatenate([g[4] for g in fg], axis=0),
    }
    small_part = _pack([_to_shards(part[n], n) for n in SMALL], lead=(N_DEV,))
    group = [("s5_w_out", by_dest(jnp.concatenate([s5_g["w_val"], s5_g["w_gate"]], axis=1), "s5_w_out"), BF16)]
    group += ffn_group(0, fg[0]) + [("small", small_part.reshape(4, 2, *small_part.shape[1:]), F32)]
    arrived = exchange_chips(core_stage(group, "s5_ffn0_small")[1], "reduce_chips")
    reduced.update(zip([k for k, _, _ in group], arrived))

    out = {}
    for n in BIG:
        layers = PARAMS[n][0][0]
        res = []
        for li in range(layers):
            key = f"{n}{li}" if layers > 1 else n
            res.append(adamw(reduced[key], w[n][li], m[n][li], v[n][li], f"adamw_{key}"))
        out[n] = [jnp.stack([r[i] for r in res]) for i in range(4)]
    small_res = adamw(reduced["small"], *[_pack([tree[n] for n in SMALL]) for tree in (w, m, v)], "adamw_small")
    for i, res in enumerate(small_res):
        for n, r in zip(SMALL, _unpack(res, small_shapes)):
            out.setdefault(n, [None] * 4)[i] = r

    repl_shapes = [PARAMS[n][0] for n in REPL]
    repl_parts = all_gather([_pack([part[n] for n in REPL])], "gather_grads")[0]
    repl_res = adamw(repl_parts, *[_pack([tree[n] for n in REPL]) for tree in (w, m, v)], "adamw_replicated")
    for i, res in enumerate(repl_res):
        for n, r in zip(REPL, _unpack(res, repl_shapes)):
            out.setdefault(n, [None] * 4)[i] = r

    return (loss, grad_x, *[out[n][0] for n in NAMES], *[out[n][1] for n in NAMES], *[out[n][2] for n in NAMES],
            *[out[n][3] for n in NAMES])
```

```python
import functools
import math

import jax
import jax.numpy as jnp
from jax import lax
from jax.experimental import pallas as pl
from jax.experimental.pallas import tpu as pltpu

F32 = jnp.float32
BF16 = jnp.bfloat16
MESH = pl.DeviceIdType.MESH

N_DEV = 8
D_MODEL = 1024
EPS = 1e-6
POOL_GROUP = 256
S5_GROUPS, S5_GROUP, S5_STATE = 64, 16, 64
LRU_BLOCK = 256
LRU_C = 8.0
SB_HEAD_DIM = 64
ATT_BLOCK = 256
FFN_HIDDEN = 2816
ADAM_LR, ADAM_B1, ADAM_B2, ADAM_EPS, ADAM_WD, ADAM_STEP = 0.001, 0.9, 0.999, 1e-08, 0.01, 10
LANES = 128
SUBLANES = 8
VMEM_LIMIT = 56 * 1024 * 1024


def _pick(n, target, mult=LANES):
    best = None
    for d in range(mult, min(n, target) + 1, mult):
        if n % d == 0:
            best = d
    return best or n


def _params(*sem):
    return pltpu.CompilerParams(dimension_semantics=sem, vmem_limit_bytes=VMEM_LIMIT)


def mm(a, b, *, ta=False, tb=False, acc=None, out_dtype=F32, name):
    if ta:
        kdim, m = a.shape
    else:
        m, kdim = a.shape
    if tb:
        n, k2 = b.shape
    else:
        k2, n = b.shape
    assert kdim == k2, (a.shape, b.shape)
    tm = _pick(m, 1408 if ta else 512)
    tn = _pick(n, 1408)
    tk = _pick(kdim, 1024) if ta else kdim
    nk = kdim // tk
    a_spec = pl.BlockSpec((tk, tm), lambda j, i, k: (k, i)) if ta else pl.BlockSpec((tm, tk), lambda j, i, k: (i, k))
    b_spec = pl.BlockSpec((tn, tk), lambda j, i, k: (j, k)) if tb else pl.BlockSpec((tk, tn), lambda j, i, k: (k, j))
    o_spec = pl.BlockSpec((tm, tn), lambda j, i, k: (i, j))
    dims = (((0,) if ta else (1,), (1,) if tb else (0,)), ((), ()))
    has_acc = acc is not None

    def body(*refs):
        if has_acc:
            a_ref, b_ref, c_ref, o_ref, acc_ref = refs
        else:
            a_ref, b_ref, o_ref, acc_ref = refs
        k = pl.program_id(2)

        @pl.when(k == 0)
        def _():
            acc_ref[...] = c_ref[...].astype(F32) if has_acc else jnp.zeros_like(acc_ref)

        acc_ref[...] += lax.dot_general(a_ref[...].astype(BF16), b_ref[...].astype(BF16), dims,
                                        preferred_element_type=F32)

        @pl.when(k == nk - 1)
        def _():
            o_ref[...] = acc_ref[...].astype(out_dtype)

    ins = [a, b] + ([acc] if has_acc else [])
    specs = [a_spec, b_spec] + ([o_spec] if has_acc else [])
    return pl.pallas_call(
        body, name=name, grid=(n // tn, m // tm, nk), in_specs=specs, out_specs=o_spec,
        out_shape=jax.ShapeDtypeStruct((m, n), out_dtype), scratch_shapes=[pltpu.VMEM((tm, tn), F32)],
        compiler_params=_params("parallel", "parallel", "arbitrary"),
    )(*ins)


def rowwise(fn, tiled, bcast, outs, reds=(), *, tm=256, name):
    t = tiled[0].shape[0]
    tm = min(tm, t)
    assert t % tm == 0
    n_t, n_b, n_o, n_r = len(tiled), len(bcast), len(outs), len(reds)

    def body(*refs):
        vals = [r[...] for r in refs[:n_t + n_b]]
        res = fn(*vals)
        res = res if isinstance(res, tuple) else (res,)
        o_refs = refs[n_t + n_b:n_t + n_b + n_o]
        r_refs = refs[n_t + n_b + n_o:]
        for o_ref, v in zip(o_refs, res[:n_o]):
            o_ref[...] = v.astype(o_ref.dtype)
        first = pl.program_id(0) == 0
        for r_ref, v in zip(r_refs, res[n_o:]):
            @pl.when(first)
            def _(r_ref=r_ref, v=v):
                r_ref[...] = v.astype(F32)

            @pl.when(jnp.logical_not(first))
            def _(r_ref=r_ref, v=v):
                r_ref[...] += v.astype(F32)

    in_specs = [pl.BlockSpec((tm, a.shape[1]), lambda i: (i, 0)) for a in tiled]
    in_specs += [pl.BlockSpec(a.shape, lambda i, nd=a.ndim: (0,) * nd) for a in bcast]
    out_specs = [pl.BlockSpec((tm, c), lambda i: (i, 0)) for c, _ in outs]
    out_specs += [pl.BlockSpec(s, lambda i: (0, 0)) for s in reds]
    out_shape = [jax.ShapeDtypeStruct((t, c), dt) for c, dt in outs]
    out_shape += [jax.ShapeDtypeStruct(s, F32) for s in reds]
    res = pl.pallas_call(
        body, name=name, grid=(t // tm,), in_specs=in_specs, out_specs=out_specs, out_shape=out_shape,
        compiler_params=_params("arbitrary"),
    )(*tiled, *bcast)
    return res


def seqwise(fn, seqs, chans, outs, reds=(), *, ct, name, fulls=()):
    def split(x):
        return x if isinstance(x, tuple) else (x, 0)

    seqs = [split(s) for s in seqs]
    chans = [split(s) for s in chans]
    bsz, seq = seqs[0][0].shape[:2]
    n_c = outs[0][0] // ct if outs else reds[0][1] // ct
    n_s, n_ch, n_f, n_o, n_r = len(seqs), len(chans), len(fulls), len(outs), len(reds)

    def body(*refs):
        c = pl.program_id(0)
        vals = [r[...] for r in refs[:n_s + n_ch + n_f]]
        res = fn(c, *vals)
        res = res if isinstance(res, tuple) else (res,)
        o_refs = refs[n_s + n_ch + n_f:n_s + n_ch + n_f + n_o]
        r_refs = refs[n_s + n_ch + n_f + n_o:]
        for o_ref, v in zip(o_refs, res[:n_o]):
            o_ref[...] = v.astype(o_ref.dtype)
        first = pl.program_id(1) == 0
        for r_ref, v in zip(r_refs, res[n_o:]):
            @pl.when(first)
            def _(r_ref=r_ref, v=v):
                r_ref[...] = v.astype(F32)

            @pl.when(jnp.logical_not(first))
            def _(r_ref=r_ref, v=v):
                r_ref[...] += v.astype(F32)

    in_specs = [pl.BlockSpec((None, seq, ct), lambda c, b, off=off: (b, 0, c + off)) for _, off in seqs]
    in_specs += [pl.BlockSpec((a.shape[0], ct), lambda c, b, off=off: (0, c + off)) for a, off in chans]
    in_specs += [pl.BlockSpec(a.shape, lambda c, b, nd=a.ndim: (0,) * nd) for a in fulls]
    out_specs = [pl.BlockSpec((None, seq, ct), lambda c, b: (b, 0, c)) for _ in outs]
    out_specs += [pl.BlockSpec((r, ct), lambda c, b: (0, c)) for r, _ in reds]
    out_shape = [jax.ShapeDtypeStruct((bsz, seq, cc), dt) for cc, dt in outs]
    out_shape += [jax.ShapeDtypeStruct(s, F32) for s in reds]
    return pl.pallas_call(
        body, name=name, grid=(n_c, bsz), in_specs=in_specs, out_specs=out_specs, out_shape=out_shape,
        compiler_params=_params("arbitrary", "arbitrary"),
    )(*[a for a, _ in seqs], *[a for a, _ in chans], *fulls)


def _rows(x):
    return lax.broadcasted_iota(jnp.int32, x.shape, 0)


def shift_down(x, d, fill=0.0):
    if d == 0:
        return x
    s = x.shape[0]
    if d % SUBLANES == 0:
        return jnp.concatenate([jnp.full((d,) + x.shape[1:], fill, x.dtype), x[:s - d]], axis=0)
    rolled = pltpu.roll(x, d, 0)
    if d > SUBLANES or s <= SUBLANES:
        return jnp.where(_rows(x) >= d, rolled, fill)
    head = rolled[:SUBLANES]
    return jnp.concatenate([jnp.where(_rows(head) >= d, head, fill), rolled[SUBLANES:]], axis=0)


def shift_up(x, d, fill=0.0):
    if d == 0:
        return x
    s = x.shape[0]
    if d % SUBLANES == 0:
        return jnp.concatenate([x[d:], jnp.full((d,) + x.shape[1:], fill, x.dtype)], axis=0)
    rolled = pltpu.roll(x, s - d, 0)
    if d > SUBLANES or s <= SUBLANES:
        return jnp.where(_rows(x) < s - d, rolled, fill)
    tail = rolled[s - SUBLANES:]
    return jnp.concatenate([rolled[:s - SUBLANES], jnp.where(_rows(tail) < SUBLANES - d, tail, fill)], axis=0)


def conv_fwd(x, w, b):
    kw = w.shape[0]
    y = b + w[kw - 1:kw] * x
    for d in range(1, kw):
        y = y + w[kw - 1 - d:kw - d] * shift_down(x, d)
    return y


def conv_bwd(x, w, dy):
    kw = w.shape[0]
    dx = w[kw - 1:kw] * dy
    dws = [jnp.sum(dy * x, axis=0, keepdims=True)]
    for d in range(1, kw):
        dx = dx + w[kw - 1 - d:kw - d] * shift_up(dy, d)
        dws.append(jnp.sum(dy * shift_down(x, d), axis=0, keepdims=True))
    dw = jnp.concatenate(dws[::-1], axis=0)
    return dx, dw, jnp.sum(dy, axis=0, keepdims=True)


def sigmoid(x):
    return 0.5 * jnp.tanh(0.5 * x) + 0.5


def rms_fwd(x, g, out_dtype, name):
    def fn(x, g):
        return x * lax.rsqrt(jnp.mean(x * x, axis=-1, keepdims=True) + EPS) * g

    return rowwise(fn, [x], [g], [(x.shape[1], out_dtype)], name=name)[0]


def rms_bwd(x, g, dh_parts, dres, name):
    n_p = len(dh_parts)

    def fn(x, dres, *rest):
        dh = rest[0].astype(F32)
        for p in rest[1:n_p]:
            dh = dh + p.astype(F32)
        g = rest[n_p]
        r = lax.rsqrt(jnp.mean(x * x, axis=-1, keepdims=True) + EPS)
        xh = x * r
        dxh = dh * g
        dx = r * (dxh - xh * jnp.mean(dxh * xh, axis=-1, keepdims=True))
        return dres + dx, jnp.sum(dh * xh, axis=0, keepdims=True)

    return rowwise(fn, [x, dres, *dh_parts], [g], [(x.shape[1], F32)], [(1, x.shape[1])], name=name)


def ffn_fwd(x, g, w_val, w_gate, cw, cb, w_out, bsz, li):
    t, d = x.shape
    hid = w_val.shape[1]
    xn = rms_fwd(x, g, BF16, f"ffn{li}_rms")
    hv = mm(xn, w_val, name=f"ffn{li}_hv").reshape(bsz, t // bsz, hid)
    hg = mm(xn, w_gate, name=f"ffn{li}_hg").reshape(bsz, t // bsz, hid)
    ct = _pick(hid, 256)
    off = hid // ct

    def fn(c, hv, hg, wv, wg, bv, bg):
        val = conv_fwd(hv, wv, bv)
        gate = conv_fwd(hg, wg, bg)
        return gate * sigmoid(gate) * val

    act = seqwise(fn, [hv, hg], [cw, (cw, off), cb, (cb, off)], [(hid, BF16)], ct=ct, name=f"ffn{li}_gate")[0]
    act = act.reshape(t, hid)
    y = mm(act, w_out, acc=x, name=f"ffn{li}_out")
    return y, (x, xn, hv, hg, act)


def ffn_bwd(dy, saved, g, w_val, w_gate, cw, cb, w_out, li):
    x, xn, hv, hg, act = saved
    bsz, seq, hid = hv.shape
    t = bsz * seq
    d_wout = mm(act, dy, ta=True, name=f"ffn{li}_dwout")
    dact = mm(dy, w_out, tb=True, name=f"ffn{li}_dact").reshape(bsz, seq, hid)
    ct = _pick(hid, 256)
    off = hid // ct

    def fn(c, hv, hg, dact, wv, wg, bv, bg):
        val = conv_fwd(hv, wv, bv)
        gate = conv_fwd(hg, wg, bg)
        sg = sigmoid(gate)
        silu = gate * sg
        dval = dact * silu
        dgate = dact * val * (sg + silu * (1.0 - sg))
        dhv, dwv, dbv = conv_bwd(hv, wv, dval)
        dhg, dwg, dbg = conv_bwd(hg, wg, dgate)
        return dhv, dhg, dwv, dwg, dbv, dbg

    kw = cw.shape[0]
    dhv, dhg, dwv, dwg, dbv, dbg = seqwise(
        fn, [hv, hg, dact], [cw, (cw, off), cb, (cb, off)], [(hid, BF16), (hid, BF16)],
        [(kw, hid), (kw, hid), (1, hid), (1, hid)], ct=ct, name=f"ffn{li}_dgate")
    dhv = dhv.reshape(t, hid)
    dhg = dhg.reshape(t, hid)
    d_wv = mm(xn, dhv, ta=True, name=f"ffn{li}_dwv")
    d_wg = mm(xn, dhg, ta=True, name=f"ffn{li}_dwg")
    dxn = mm(dhv, w_val, tb=True, name=f"ffn{li}_dxn1")
    dxn = mm(dhg, w_gate, tb=True, acc=dxn, name=f"ffn{li}_dxn2")
    dx, dg = rms_bwd(x, g, [dxn], dy, f"ffn{li}_drms")
    d_cw = jnp.concatenate([dwv, dwg], axis=1)
    d_cb = jnp.concatenate([dbv, dbg], axis=1)
    return dx, (dg, d_wv, d_wg, d_cw, d_cb, d_wout)


def _window_sum(x, c, shift):
    s2 = x + shift(x, 1)
    s4 = s2 + shift(s2, 2)
    s8 = s4 + shift(s4, 4)
    s16 = s8 + shift(s8, 8)
    return jnp.where(c == 0, s2, jnp.where(c == 1, s4, jnp.where(c == 2, s8, s16)))


def _pool_inv_count(x, c):
    w = jnp.left_shift(2, c)
    return 1.0 / jnp.minimum(_rows(x) + 1, w).astype(F32)


def pool_fwd(x, g, w, b, scale, bsz):
    t, d = x.shape
    h = rms_fwd(x, g, F32, "pool_rms").reshape(bsz, t // bsz, d)

    def fn(c, h, x, b, scale, w):
        dd = _window_sum(h, c, shift_down) * _pool_inv_count(h, c) - h
        y = jnp.dot(dd.astype(BF16), w[0], preferred_element_type=F32) + b
        return x + scale * y

    wspec = pl.BlockSpec((1, POOL_GROUP, POOL_GROUP), lambda c, bb: (c, 0, 0))
    y = _seqwise_w(fn, [h, x.reshape(bsz, t // bsz, d)], [b, scale], [(w, wspec)], [(d, F32)], [], ct=POOL_GROUP,
                   name="pool_fwd")[0]
    return y.reshape(t, d), (x, h)


def pool_bwd(dy, saved, g, w, b, scale):
    x, h = saved
    bsz, seq, d = h.shape
    t = bsz * seq

    def fn(c, h, dy, b, scale, w):
        inv = _pool_inv_count(h, c)
        dd = _window_sum(h, c, shift_down) * inv - h
        ddb = dd.astype(BF16)
        y = jnp.dot(ddb, w[0], preferred_element_type=F32) + b
        dscale = jnp.sum(dy * y, axis=0, keepdims=True)
        dyy = dy * scale
        db = jnp.sum(dyy, axis=0, keepdims=True)
        dyb = dyy.astype(BF16)
        dw = lax.dot_general(ddb, dyb, (((0,), (0,)), ((), ())), preferred_element_type=F32)
        ddd = lax.dot_general(dyb, w[0], (((1,), (1,)), ((), ())), preferred_element_type=F32)
        dh = _window_sum(ddd * inv, c, shift_up) - ddd
        return dh, db, dscale, dw[None]

    wspec = pl.BlockSpec((1, POOL_GROUP, POOL_GROUP), lambda c, bb: (c, 0, 0))
    dh, db, dscale, dw = _seqwise_w(
        fn, [h, dy.reshape(bsz, seq, d)], [b, scale], [(w, wspec)], [(d, F32)], [(1, d), (1, d)], ct=POOL_GROUP,
        name="pool_bwd", wreds=[((4, POOL_GROUP, POOL_GROUP), wspec)])
    dx, dg = rms_bwd(x, g, [dh.reshape(t, d)], dy, "pool_drms")
    return dx, (dg, dw, db, dscale)


def _seqwise_w(fn, seqs, chans, blocked, outs, reds, *, ct, name, wreds=(), phased=False, side=()):
    bsz, seq = seqs[0].shape[:2]
    n_c = seqs[0].shape[2] // ct
    n_s = len(seqs)
    n_in = n_s + len(chans) + len(blocked)
    side, exchange, side_sems, side_shapes = _side_plan(side)
    n_o, n_r, n_x = len(outs), len(reds) + len(wreds), len(side)

    def body(*refs):
        c = pl.program_id(0)
        if side:
            ends = n_in + n_x + n_o + n_r
            start, wait = exchange(refs[n_in:n_in + n_x], refs[ends:ends + n_x], *refs[ends + n_x:])
            pl.when(jnp.logical_and(c == 0, pl.program_id(1) == 0))(start)
        seq_vals = [_load_phases(r) if phased else r[...] for r in refs[:n_s]]
        res = fn(c, *seq_vals, *[r[...] for r in refs[n_s:n_in]])
        res = res if isinstance(res, tuple) else (res,)
        own = refs[n_in + n_x:n_in + n_x + n_o + n_r]
        for o_ref, v in zip(own[:n_o], res[:n_o]):
            if phased:
                _store_phases(o_ref, v)
            else:
                o_ref[...] = v.astype(o_ref.dtype)
        first = pl.program_id(1) == 0
        for r_ref, v in zip(own[n_o:], res[n_o:]):
            @pl.when(first)
            def _(r_ref=r_ref, v=v):
                r_ref[...] = v.astype(F32)

            @pl.when(jnp.logical_not(first))
            def _(r_ref=r_ref, v=v):
                r_ref[...] += v.astype(F32)
        if side:
            pl.when(jnp.logical_and(c == n_c - 1, pl.program_id(1) == bsz - 1))(wait)

    in_specs = [pl.BlockSpec((None, seq, ct), lambda c, b: (b, 0, c)) for _ in seqs]
    in_specs += [pl.BlockSpec((a.shape[0], ct), lambda c, b: (0, c)) for a in chans]
    in_specs += [spec for _, spec in blocked]
    in_specs += [_HBM] * n_x
    out_specs = [pl.BlockSpec((None, seq, ct), lambda c, b: (b, 0, c)) for _ in outs]
    out_specs += [pl.BlockSpec((r, ct), lambda c, b: (0, c)) for r, _ in reds]
    out_specs += [spec for _, spec in wreds]
    out_specs += [_HBM] * n_x
    out_shape = [jax.ShapeDtypeStruct((bsz, seq, cc), dt) for cc, dt in outs]
    out_shape += [jax.ShapeDtypeStruct(s, F32) for s in reds]
    out_shape += [jax.ShapeDtypeStruct(s, F32) for s, _ in wreds]
    out_shape += side_shapes
    return pl.pallas_call(
        body, name=name, grid=(n_c, bsz), in_specs=in_specs, out_specs=out_specs, out_shape=out_shape,
        scratch_shapes=side_sems,
        compiler_params=_params("arbitrary", "arbitrary"),
    )(*seqs, *chans, *[a for a, _ in blocked], *side)


S5_TILE_GROUPS = LANES // S5_GROUP
S5_TILES = S5_GROUPS // S5_TILE_GROUPS
S5_TILE_STATE = S5_TILE_GROUPS * S5_STATE


def _s5_discretize(lam_re, lam_im, log_dt, b_re, b_im):
    lr = jnp.minimum(lam_re, -1e-4)
    dt = jnp.exp(log_dt)
    er = jnp.exp(lr * dt)
    ar = er * jnp.cos(lam_im * dt)
    ai = er * jnp.sin(lam_im * dt)
    den = lr * lr + lam_im * lam_im
    cr = ((ar - 1.0) * lr + ai * lam_im) / den
    ci = (ai * lr - (ar - 1.0) * lam_im) / den
    return ar, ai, cr * b_re - ci * b_im, cr * b_im + ci * b_re


def _whole(a):
    return pl.BlockSpec(a.shape, lambda *_: (0,) * a.ndim)


def s5_prep(lam_re, lam_im, log_dt, b_re, b_im):
    def body(lr, li, ld, br, bi, ar_o, ai_o, bbr_o, bbi_o):
        ar, ai, bbr, bbi = _s5_discretize(lr[...], li[...], ld[...], br[...], bi[...])
        ar_o[...] = ar
        ai_o[...] = ai
        bbr_o[...] = bbr
        bbi_o[...] = bbi

    ins = [lam_re, lam_im, log_dt, b_re, b_im]
    outs = [lam_re, lam_im, b_re, b_im]
    return pl.pallas_call(
        body, name="s5_prep", in_specs=[_whole(a) for a in ins], out_specs=[_whole(a) for a in outs],
        out_shape=[jax.ShapeDtypeStruct(a.shape, F32) for a in outs],
        compiler_params=pltpu.CompilerParams(vmem_limit_bytes=VMEM_LIMIT),
    )(*ins)


def s5_prep_bwd(lam_re, lam_im, log_dt, b_re, b_im, d_ar, d_ai, d_bbr, d_bbi):
    def body(lr, li, ld, br, bi, dar, dai, dbbr, dbbi, *outs):
        _, vjp = jax.vjp(_s5_discretize, lr[...], li[...], ld[...], br[...], bi[...])
        for o, v in zip(outs, vjp((dar[...], dai[...], dbbr[...], dbbi[...]))):
            o[...] = v

    ins = [lam_re, lam_im, log_dt, b_re, b_im, d_ar, d_ai, d_bbr, d_bbi]
    return pl.pallas_call(
        body, name="s5_prep_bwd", in_specs=[_whole(a) for a in ins], out_specs=[_whole(a) for a in ins[:5]],
        out_shape=[jax.ShapeDtypeStruct(a.shape, F32) for a in ins[:5]],
        compiler_params=pltpu.CompilerParams(vmem_limit_bytes=VMEM_LIMIT),
    )(*ins)


def scan_lti(br, bi, ar, ai, reverse=False):
    shift = shift_up if reverse else shift_down
    seq = br.shape[0]
    d = 1
    while d < seq:
        sr, si = shift(br, d), shift(bi, d)
        br, bi = br + ar * sr - ai * si, bi + ar * si + ai * sr
        ar, ai = ar * ar - ai * ai, 2.0 * ar * ai
        d *= 2
    return br, bi


PHASES = SUBLANES


def _load_phases(ref):
    groups = ref.shape[0] // PHASES
    return jnp.concatenate([ref[pl.ds(r, groups, stride=PHASES), :] for r in range(PHASES)], axis=0)


def _store_phases(ref, x):
    groups = ref.shape[0] // PHASES
    for r in range(PHASES):
        ref[pl.ds(r, groups, stride=PHASES), :] = x[r * groups:(r + 1) * groups].astype(ref.dtype)


def _split_phases(x):
    groups = x.shape[0] // PHASES
    return [x[r * groups:(r + 1) * groups] for r in range(PHASES)]


def _prev_phased(x):
    ph = _split_phases(x)
    return jnp.concatenate([shift_down(ph[PHASES - 1], 1)] + ph[:PHASES - 1], axis=0)


def _cmul_add(xr, xi, ar, ai, yr, yi):
    return xr + ar * yr - ai * yi, xi + ar * yi + ai * yr


def scan_phases(br, bi, ar, ai, reverse=False):
    rs, im = _split_phases(br), _split_phases(bi)
    order = range(PHASES - 2, -1, -1) if reverse else range(1, PHASES)
    step = 1 if reverse else -1
    for r in order:
        rs[r], im[r] = _cmul_add(rs[r], im[r], ar, ai, rs[r + step], im[r + step])
    powers = [(ar, ai)]
    for _ in range(PHASES - 1):
        pr, pi = powers[-1]
        powers.append((pr * ar - pi * ai, pr * ai + pi * ar))
    end = 0 if reverse else PHASES - 1
    cr, ci = scan_lti(rs[end], im[end], *powers[PHASES - 1], reverse)
    shift = shift_up if reverse else shift_down
    inr, ini = shift(cr, 1), shift(ci, 1)
    for r in range(PHASES):
        if r == end:
            rs[r], im[r] = cr, ci
        else:
            pr, pi = powers[PHASES - 1 - r if reverse else r]
            rs[r], im[r] = _cmul_add(rs[r], im[r], pr, pi, inr, ini)
    return jnp.concatenate(rs, axis=0), jnp.concatenate(im, axis=0)


def _s5_block_diag(w, rows_first):
    g, a, b = w.shape
    w = w.reshape(S5_TILES, S5_TILE_GROUPS, a, b)
    eye = jnp.eye(S5_TILE_GROUPS, dtype=w.dtype)
    if rows_first:
        return jnp.einsum("tgab,gk->tgakb", w, eye).reshape(S5_TILES, S5_TILE_GROUPS * a, S5_TILE_GROUPS * b)
    return jnp.einsum("tgab,gk->tgbka", w, eye).reshape(S5_TILES, S5_TILE_GROUPS * b, S5_TILE_GROUPS * a)


def _s5_diag_blocks(w, a, b):
    w = w.reshape(S5_TILES, S5_TILE_GROUPS, a, S5_TILE_GROUPS, b)
    eye = jnp.eye(S5_TILE_GROUPS, dtype=w.dtype)
    return jnp.einsum("tgakb,gk->tgab", w, eye).reshape(S5_GROUPS, a, b)


def _gelu(x):
    return jax.nn.gelu(x)


def _s5_tile_specs():
    b_spec = pl.BlockSpec((1, LANES, S5_TILE_STATE), lambda c, b: (c, 0, 0))
    c_spec = pl.BlockSpec((1, S5_TILE_STATE, LANES), lambda c, b: (c, 0, 0))
    a_spec = pl.BlockSpec((1, 1, S5_TILE_STATE), lambda c, b: (c, 0, 0))
    return b_spec, c_spec, a_spec


def s5_fwd(x, g, p, bsz, side=()):
    t, d = x.shape
    seq = t // bsz
    h = rms_fwd(x, g, F32, "s5_rms").reshape(bsz, seq, d)
    ar, ai, bbr, bbi = s5_prep(p["lam_re"][..., None], p["lam_im"][..., None], p["log_dt"][:, None, None],
                               p["b_re"], p["b_im"])
    bdr = _s5_block_diag(bbr, False).astype(BF16)
    bdi = _s5_block_diag(bbi, False).astype(BF16)
    cdr = _s5_block_diag(p["c_re"], False).astype(BF16)
    cdi = _s5_block_diag(p["c_im"], False).astype(BF16)
    a_r = ar.reshape(S5_TILES, 1, S5_TILE_STATE)
    a_i = ai.reshape(S5_TILES, 1, S5_TILE_STATE)
    b_spec, c_spec, a_spec = _s5_tile_specs()

    def fn(c, u, dskip, bdr, bdi, cdr, cdi, a_r, a_i):
        ub = u.astype(BF16)
        y = dskip * u
        for k in range(S5_TILE_STATE // LANES):
            sl = slice(k * LANES, (k + 1) * LANES)
            br = jnp.dot(ub, bdr[0][:, sl], preferred_element_type=F32)
            bi = jnp.dot(ub, bdi[0][:, sl], preferred_element_type=F32)
            sr, si = scan_phases(br, bi, a_r[0][:, sl], a_i[0][:, sl])
            y = y + jnp.dot(sr.astype(BF16), cdr[0][sl, :], preferred_element_type=F32)
            y = y - jnp.dot(si.astype(BF16), cdi[0][sl, :], preferred_element_type=F32)
        return y, _gelu(y)

    y, yg, *side_out = _seqwise_w(
        fn, [h], [p["d"]], [(bdr, b_spec), (bdi, b_spec), (cdr, c_spec), (cdi, c_spec), (a_r, a_spec), (a_i, a_spec)],
        [(d, F32), (d, F32)], [], ct=LANES, name="s5_core", phased=True, side=side)
    yg = yg.reshape(t, d)
    zv = mm(yg, p["w_val"], name="s5_zv")
    zg = mm(yg, p["w_gate"], name="s5_zg")

    def gate(x, zv, zg, bv, bg):
        return x + (zv + bv) * sigmoid(zg + bg)

    out = rowwise(gate, [x, zv, zg], [p["b_out"][:, :d], p["b_out"][:, d:]], [(d, F32)], name="s5_gate")[0]
    return out, (x, h, y, yg, zv, zg, (ar, ai, bdr, bdi, cdr, cdi, a_r, a_i)), side_out


def s5_bwd(dy, saved, g, p, side=()):
    x, h, y, yg, zv, zg, (ar, ai, bdr, bdi, cdr, cdi, a_r, a_i) = saved
    bsz, seq, d = h.shape
    t = bsz * seq

    def dgate(dy, zv, zg, bv, bg):
        val = zv + bv
        sg = sigmoid(zg + bg)
        dzv = dy * sg
        dzg = dy * val * sg * (1.0 - sg)
        return dzv, dzg, jnp.sum(dzv, axis=0, keepdims=True), jnp.sum(dzg, axis=0, keepdims=True)

    dzv, dzg, dbv, dbg = rowwise(dgate, [dy, zv, zg], [p["b_out"][:, :d], p["b_out"][:, d:]],
                                 [(d, BF16), (d, BF16)], [(1, d), (1, d)], name="s5_dgate")
    d_wv = mm(yg, dzv, ta=True, name="s5_dwv")
    d_wg = mm(yg, dzg, ta=True, name="s5_dwg")
    dyg = mm(dzv, p["w_val"], tb=True, name="s5_dyg1")
    dyg = mm(dzg, p["w_gate"], tb=True, acc=dyg, name="s5_dyg2").reshape(bsz, seq, d)
    b_spec, c_spec, a_spec = _s5_tile_specs()
    tr = (((0,), (0,)), ((), ()))
    nt = (((1,), (1,)), ((), ()))

    def fn(c, u, y, dyg, dskip, bdr, bdi, cdr, cdi, a_r, a_i):
        _, gelu_vjp = jax.vjp(_gelu, y)
        dyy = gelu_vjp(dyg)[0]
        ddskip = jnp.sum(dyy * u, axis=0, keepdims=True)
        du = dyy * dskip
        dyb = dyy.astype(BF16)
        ub = u.astype(BF16)
        dcr, dci, dbr, dbi, dar, dai = [], [], [], [], [], []
        for k in range(S5_TILE_STATE // LANES):
            sl = slice(k * LANES, (k + 1) * LANES)
            akr, aki = a_r[0][:, sl], a_i[0][:, sl]
            br = jnp.dot(ub, bdr[0][:, sl], preferred_element_type=F32)
            bi = jnp.dot(ub, bdi[0][:, sl], preferred_element_type=F32)
            sr, si = scan_phases(br, bi, akr, aki)
            dcr.append(lax.dot_general(sr.astype(BF16), dyb, tr, preferred_element_type=F32))
            dci.append(-lax.dot_general(si.astype(BF16), dyb, tr, preferred_element_type=F32))
            gr = lax.dot_general(dyb, cdr[0][sl, :], nt, preferred_element_type=F32)
            gi = -lax.dot_general(dyb, cdi[0][sl, :], nt, preferred_element_type=F32)
            gr, gi = scan_phases(gr, gi, akr, -aki, reverse=True)
            spr, spi = _prev_phased(sr), _prev_phased(si)
            dar.append(jnp.sum(gr * spr + gi * spi, axis=0, keepdims=True))
            dai.append(jnp.sum(gi * spr - gr * spi, axis=0, keepdims=True))
            grb, gib = gr.astype(BF16), gi.astype(BF16)
            dbr.append(lax.dot_general(ub, grb, tr, preferred_element_type=F32))
            dbi.append(lax.dot_general(ub, gib, tr, preferred_element_type=F32))
            du = du + lax.dot_general(grb, bdr[0][:, sl], nt, preferred_element_type=F32)
            du = du + lax.dot_general(gib, bdi[0][:, sl], nt, preferred_element_type=F32)
        return (du, ddskip, jnp.concatenate(dbr, axis=1)[None], jnp.concatenate(dbi, axis=1)[None],
                jnp.concatenate(dcr, axis=0)[None], jnp.concatenate(dci, axis=0)[None],
                jnp.concatenate(dar, axis=1)[None], jnp.concatenate(dai, axis=1)[None])

    du, ddskip, dbdr, dbdi, dcdr, dcdi, dar, dai, *side_out = _seqwise_w(
        fn, [h, y, dyg], [p["d"]], [(bdr, b_spec), (bdi, b_spec), (cdr, c_spec), (cdi, c_spec), (a_r, a_spec), (a_i, a_spec)],
        [(d, F32)], [(1, d)], ct=LANES, name="s5_dcore", phased=True,
        wreds=[((S5_TILES, LANES, S5_TILE_STATE), b_spec), ((S5_TILES, LANES, S5_TILE_STATE), b_spec),
               ((S5_TILES, S5_TILE_STATE, LANES), c_spec), ((S5_TILES, S5_TILE_STATE, LANES), c_spec),
               ((S5_TILES, 1, S5_TILE_STATE), a_spec), ((S5_TILES, 1, S5_TILE_STATE), a_spec)], side=side)
    d_bbr = _s5_diag_blocks(dbdr, S5_GROUP, S5_STATE).transpose(0, 2, 1)
    d_bbi = _s5_diag_blocks(dbdi, S5_GROUP, S5_STATE).transpose(0, 2, 1)
    d_cre = _s5_diag_blocks(dcdr, S5_STATE, S5_GROUP).transpose(0, 2, 1)
    d_cim = _s5_diag_blocks(dcdi, S5_STATE, S5_GROUP).transpose(0, 2, 1)
    d_lre, d_lim, d_ldt, d_bre, d_bim = s5_prep_bwd(
        p["lam_re"][..., None], p["lam_im"][..., None], p["log_dt"][:, None, None], p["b_re"], p["b_im"],
        dar.reshape(S5_GROUPS, S5_STATE, 1), dai.reshape(S5_GROUPS, S5_STATE, 1), d_bbr, d_bbi)
    dx, dg = rms_bwd(x, g, [du.reshape(t, d)], dy, "s5_drms")
    grads = dict(g=dg, lam_re=d_lre[..., 0], lam_im=d_lim[..., 0], log_dt=d_ldt[:, 0, 0], b_re=d_bre, b_im=d_bim,
                 c_re=d_cre, c_im=d_cim, d=ddskip, w_val=d_wv, w_gate=d_wg, b_out=jnp.concatenate([dbv, dbg], axis=1))
    return dx, grads, side_out


def _expm1_nonpos(x):
    u = jnp.exp(x)
    safe = (u - 1.0) * x / jnp.log(u)
    return jnp.where(u == 1.0, x, jnp.where(x < -20.0, -1.0, safe))


def _softplus(x):
    return jnp.maximum(x, 0.0) + jnp.log(1.0 + jnp.exp(-jnp.abs(x)))


def scan_ltv(a, b, reverse=False):
    shift = shift_up if reverse else shift_down
    seq = a.shape[0]
    d = 1
    while d < seq:
        b = b + a * shift(b, d)
        if 2 * d < seq:
            a = a * shift(a, d, 1.0)
        d *= 2
    return b


def _lru_gates(rec0, cw, cb, ba, bx, lam, wa, wx):
    rec = conv_fwd(rec0, cw, cb)
    recb = rec.astype(BF16)
    r = sigmoid(jnp.dot(recb, wa, preferred_element_type=F32) + ba)
    i = sigmoid(jnp.dot(recb, wx, preferred_element_type=F32) + bx)
    sp = _softplus(-lam)
    log_a = -LRU_C * r * sp
    a = jnp.exp(log_a)
    mult = jnp.sqrt(-_expm1_nonpos(2.0 * log_a))
    return rec, recb, r, i, sp, a, mult


def lru_fwd(x, g, p, bsz, side=()):
    t, d = x.shape
    seq = t // bsz
    xn = rms_fwd(x, g, BF16, "lru_rms")
    gb = mm(xn, p["w_gate"], name="lru_gb").reshape(bsz, seq, d)
    rec0 = mm(xn, p["w_rec"], name="lru_rec").reshape(bsz, seq, d)
    wspec = pl.BlockSpec((1, LRU_BLOCK, LRU_BLOCK), lambda c, b: (c, 0, 0))

    def fn(c, rec0, gb, cw, cb, ba, bx, lam, wa, wx):
        rec, _, _, i, _, a, mult = _lru_gates(rec0, cw, cb, ba, bx, lam, wa[0], wx[0])
        h = scan_ltv(a, mult * (i * rec))
        return _gelu(gb) * h

    y, *side_out = _seqwise_w(fn, [rec0, gb], [p["conv_w"], p["conv_b"], p["b_a"], p["b_x"], p["lam"]],
                              [(p["w_a"], wspec), (p["w_x"], wspec)], [(d, BF16)], [], ct=LRU_BLOCK, name="lru_core",
                              side=side)
    y = y.reshape(t, d)
    out = mm(y, p["w_out"], acc=x, name="lru_out")
    return out, (x, xn, gb, rec0, y), side_out


def lru_bwd(dy, saved, g, p, side=()):
    x, xn, gb, rec0, y = saved
    bsz, seq, d = gb.shape
    t = bsz * seq
    d_wout = mm(y, dy, ta=True, name="lru_dwout")
    dyy = mm(dy, p["w_out"], tb=True, name="lru_dy").reshape(bsz, seq, d)
    wspec = pl.BlockSpec((1, LRU_BLOCK, LRU_BLOCK), lambda c, b: (c, 0, 0))
    tr = (((0,), (0,)), ((), ()))
    nt = (((1,), (1,)), ((), ()))

    def fn(c, rec0, gb, dyy, cw, cb, ba, bx, lam, wa, wx):
        rec, recb, r, i, sp, a, mult = _lru_gates(rec0, cw, cb, ba, bx, lam, wa[0], wx[0])
        h = scan_ltv(a, mult * (i * rec))
        gg, gelu_vjp = jax.vjp(_gelu, gb)
        dgb = gelu_vjp(dyy * h)[0]
        gr = scan_ltv(shift_up(a, 1), dyy * gg, reverse=True)
        da = gr * shift_down(h, 1)
        dmult = gr * i * rec
        di = gr * mult * rec
        drec = gr * mult * i
        dla = da * a - dmult * (a * a) / mult
        dr = dla * (-LRU_C * sp)
        dlam = jnp.sum(dla * (-LRU_C * r), axis=0, keepdims=True) * (-sigmoid(-lam))
        dpa = dr * r * (1.0 - r)
        dpx = di * i * (1.0 - i)
        dpab, dpxb = dpa.astype(BF16), dpx.astype(BF16)
        dwa = lax.dot_general(recb, dpab, tr, preferred_element_type=F32)
        dwx = lax.dot_general(recb, dpxb, tr, preferred_element_type=F32)
        drec = drec + lax.dot_general(dpab, wa[0], nt, preferred_element_type=F32)
        drec = drec + lax.dot_general(dpxb, wx[0], nt, preferred_element_type=F32)
        drec0, dcw, dcb = conv_bwd(rec0, cw, drec)
        return (dgb, drec0, dcw, dcb, jnp.sum(dpa, axis=0, keepdims=True), jnp.sum(dpx, axis=0, keepdims=True), dlam,
                dwa[None], dwx[None])

    kw = p["conv_w"].shape[0]
    dgb, drec0, dcw, dcb, dba, dbx, dlam, dwa, dwx, *side_out = _seqwise_w(
        fn, [rec0, gb, dyy], [p["conv_w"], p["conv_b"], p["b_a"], p["b_x"], p["lam"]],
        [(p["w_a"], wspec), (p["w_x"], wspec)], [(d, BF16), (d, BF16)],
        [(kw, d), (1, d), (1, d), (1, d), (1, d)], ct=LRU_BLOCK, name="lru_dcore",
        wreds=[(p["w_a"].shape, wspec), (p["w_x"].shape, wspec)], side=side)
    dgb = dgb.reshape(t, d)
    drec0 = drec0.reshape(t, d)
    d_wgate = mm(xn, dgb, ta=True, name="lru_dwgate")
    d_wrec = mm(xn, drec0, ta=True, name="lru_dwrec")
    dxn = mm(dgb, p["w_gate"], tb=True, name="lru_dxn1")
    dxn = mm(drec0, p["w_rec"], tb=True, acc=dxn, name="lru_dxn2")
    dx, dg = rms_bwd(x, g, [dxn], dy, "lru_drms")
    grads = dict(g=dg, w_gate=d_wgate, w_rec=d_wrec, conv_w=dcw, conv_b=dcb, w_a=dwa, b_a=dba, w_x=dwx, b_x=dbx,
                 lam=dlam, w_out=d_wout)
    return dx, grads, side_out


_NT = (((1,), (1,)), ((), ()))
_TN = (((0,), (0,)), ((), ()))


def _head_norm(x, g):
    lo = lax.broadcasted_iota(jnp.int32, x.shape, 1) < SB_HEAD_DIM
    x2 = x * x
    s_lo = jnp.sum(jnp.where(lo, x2, 0.0), axis=-1, keepdims=True)
    s_hi = jnp.sum(jnp.where(lo, 0.0, x2), axis=-1, keepdims=True)
    ms = jnp.where(lo, s_lo, s_hi) * (1.0 / SB_HEAD_DIM)
    return x * lax.rsqrt(ms + EPS) * g


def _log_sigmoid(z):
    return jnp.minimum(z, 0.0) - jnp.log(1.0 + jnp.exp(-jnp.abs(z)))


def _dot_split(x, m):
    hi = x.astype(BF16)
    lo = (x - hi.astype(F32)).astype(BF16)
    return jnp.dot(hi, m, preferred_element_type=F32) + jnp.dot(lo, m, preferred_element_type=F32)


def _tri(cmp, n):
    r = lax.broadcasted_iota(jnp.int32, (n, n), 0)
    c = lax.broadcasted_iota(jnp.int32, (n, n), 1)
    return cmp(r, c)


def sb_attn_fwd(q, k, v, qg, kg, side=()):
    bsz, seq, d = q.shape
    blk = min(ATT_BLOCK, seq)
    nq = seq // blk
    scale = 1.0 / math.sqrt(SB_HEAD_DIM)
    side, exchange, side_sems, side_shapes = _side_plan(side)
    n_x = len(side)

    def body(*refs):
        q_ref, k_ref, v_ref, qg_ref, kg_ref = refs[:5]
        o_ref, tot_ref = refs[5 + n_x:7 + n_x]
        qn, kn, vb = refs[7 + 2 * n_x:10 + 2 * n_x]
        if side:
            start, wait = exchange(refs[5:5 + n_x], refs[7 + n_x:7 + 2 * n_x], *refs[10 + 2 * n_x:])
            pl.when(jnp.logical_and(pl.program_id(0) == 0, pl.program_id(1) == 0))(start)
        qn[...] = _head_norm(q_ref[...], qg_ref[...]).astype(BF16)
        kn[...] = _head_norm(k_ref[...], kg_ref[...]).astype(BF16)
        vb[...] = v_ref[...].astype(BF16)
        lane_lo = lax.broadcasted_iota(jnp.int32, (blk, LANES), 1) < SB_HEAD_DIM
        causal = _tri(lambda r, c: c < r, blk)
        upper = _tri(lambda r, c: r > c, blk).astype(BF16)

        def q_block(qi, _):
            rows = pl.ds(pl.multiple_of(qi * blk, blk), blk)
            q_all = qn[rows, :]
            zero = jnp.zeros((), BF16)
            qbs = (jnp.where(lane_lo, q_all, zero), jnp.where(lane_lo, zero, q_all))

            def block(j, state, masked):
                cols = pl.ds(pl.multiple_of(j * blk, blk), blk)
                kb, vv = kn[cols, :], vb[cols, :]
                zs = [lax.dot_general(qb, kb, _NT, preferred_element_type=F32) * scale for qb in qbs]
                lss = [_log_sigmoid(z) for z in zs]
                lgs = [ls - z for ls, z in zip(lss, zs)]
                if masked:
                    lgs = [jnp.where(causal, lg, 0.0) for lg in lgs]
                later = [_dot_split(lg, upper) for lg in lgs]
                atts = [jnp.exp(ls + carry + cs) for ls, (carry, _), cs in zip(lss, state, later)]
                if masked:
                    atts = [jnp.where(causal, att, 0.0) for att in atts]
                outs = [jnp.dot(att.astype(BF16), vv, preferred_element_type=F32) for att in atts]
                return tuple((carry + jnp.sum(lg, axis=1, keepdims=True), acc + out)
                             for (carry, acc), lg, out in zip(state, lgs, outs))

            init = (jnp.zeros((blk, 1), F32), jnp.zeros((blk, LANES), F32))
            state = block(qi, (init, init), True)
            state = lax.fori_loop(0, qi, lambda jj, s: block(qi - 1 - jj, s, False), state)
            (carry0, acc0), (carry1, acc1) = state
            o_ref[rows, :] = jnp.where(lane_lo, acc0, acc1).astype(o_ref.dtype)
            tot_ref[rows, :] = jnp.where(lane_lo, carry0, carry1)
            return 0

        lax.fori_loop(0, nq, q_block, 0)
        if side:
            pl.when(jnp.logical_and(pl.program_id(0) == bsz - 1, pl.program_id(1) == d // LANES - 1))(wait)

    spec = pl.BlockSpec((None, seq, LANES), lambda b, c: (b, 0, c))
    gspec = pl.BlockSpec((1, LANES), lambda b, c: (0, 0))
    return pl.pallas_call(
        body, name="sb_attn_fwd", grid=(bsz, d // LANES), in_specs=[spec, spec, spec, gspec, gspec] + [_HBM] * n_x,
        out_specs=[spec, spec] + [_HBM] * n_x,
        out_shape=[jax.ShapeDtypeStruct((bsz, seq, d), BF16), jax.ShapeDtypeStruct((bsz, seq, d), F32)] + side_shapes,
        scratch_shapes=[pltpu.VMEM((seq, LANES), BF16)] * 3 + side_sems,
        compiler_params=_params("arbitrary", "arbitrary"),
    )(q, k, v, qg, kg, *side)


def sb_attn_bwd(q, k, v, qg, kg, tot, do, side=()):
    bsz, seq, d = q.shape
    blk = min(ATT_BLOCK, seq)
    nq = seq // blk
    scale = 1.0 / math.sqrt(SB_HEAD_DIM)
    side, exchange, side_sems, side_shapes = _side_plan(side)
    n_x = len(side)
    n_in, n_out, n_scr = 7, 5, 6

    def body(*refs):
        q_ref, k_ref, v_ref, qg_ref, kg_ref, tot_ref, do_ref = refs[:n_in]
        dq_ref, dk_ref, dv_ref, dqg_ref, dkg_ref = refs[n_in + n_x:n_in + n_x + n_out]
        qn, kn, vb, dqn, dkn, dvv = refs[n_in + 2 * n_x + n_out:n_in + 2 * n_x + n_out + n_scr]
        if side:
            start, wait = exchange(refs[n_in:n_in + n_x], refs[n_in + n_x + n_out:n_in + 2 * n_x + n_out],
                                         *refs[n_in + 2 * n_x + n_out + n_scr:])
            pl.when(jnp.logical_and(pl.program_id(0) == 0, pl.program_id(1) == 0))(start)
        qn[...] = _head_norm(q_ref[...], qg_ref[...]).astype(BF16)
        kn[...] = _head_norm(k_ref[...], kg_ref[...]).astype(BF16)
        vb[...] = v_ref[...].astype(BF16)
        dkn[...] = jnp.zeros_like(dkn)
        dvv[...] = jnp.zeros_like(dvv)
        lane_lo = lax.broadcasted_iota(jnp.int32, (blk, LANES), 1) < SB_HEAD_DIM
        causal = _tri(lambda r, c: c < r, blk)
        upto = _tri(lambda r, c: r <= c, blk).astype(BF16)
        before = _tri(lambda r, c: r < c, blk).astype(BF16)

        def q_block(qi, _):
            rows = pl.ds(pl.multiple_of(qi * blk, blk), blk)
            zero = jnp.zeros((), BF16)
            q_all, do_all, tot_all = qn[rows, :], do_ref[rows, :].astype(BF16), tot_ref[rows, :]
            heads = []
            for hm in (lane_lo, jnp.logical_not(lane_lo)):
                heads.append((jnp.where(hm, q_all, zero), jnp.where(hm, do_all, zero),
                              jnp.max(jnp.where(hm, tot_all, -jnp.inf), axis=1, keepdims=True)))

            def block(j, state, masked):
                cols = pl.ds(pl.multiple_of(j * blk, blk), blk)
                kb, vv = kn[cols, :], vb[cols, :]
                two = range(2)
                zs = [lax.dot_general(heads[h][0], kb, _NT, preferred_element_type=F32) * scale for h in two]
                datts = [lax.dot_general(heads[h][1], vv, _NT, preferred_element_type=F32) for h in two]
                lss = [_log_sigmoid(z) for z in zs]
                lgs = [ls - z for ls, z in zip(lss, zs)]
                if masked:
                    lgs = [jnp.where(causal, lg, 0.0) for lg in lgs]
                sofar = [_dot_split(lg, upto) for lg in lgs]
                atts = [jnp.exp(lss[h] + heads[h][2] - state[h][0] - sofar[h]) for h in two]
                if masked:
                    atts = [jnp.where(causal, att, 0.0) for att in atts]
                es = [att * datt for att, datt in zip(atts, datts)]
                earlier = [_dot_split(e, before) for e in es]
                dlgs = [state[h][1] + earlier[h] for h in two]
                if masked:
                    dlgs = [jnp.where(causal, dlg, 0.0) for dlg in dlgs]
                betas = [jnp.exp(ls) for ls in lss]
                dzbs = [((es[h] * (1.0 - betas[h]) - dlgs[h] * betas[h]) * scale).astype(BF16) for h in two]
                dvv[cols, :] += sum(lax.dot_general(atts[h].astype(BF16), heads[h][1], _TN, preferred_element_type=F32)
                                    for h in two)
                dkn[cols, :] += sum(lax.dot_general(dzbs[h], heads[h][0], _TN, preferred_element_type=F32) for h in two)
                dqs = [jnp.dot(dzb, kb, preferred_element_type=F32) for dzb in dzbs]
                return tuple((state[h][0] + jnp.sum(lgs[h], axis=1, keepdims=True),
                              state[h][1] + jnp.sum(es[h], axis=1, keepdims=True), state[h][2] + dqs[h]) for h in two)

            col0 = jnp.zeros((blk, 1), F32)
            init = (col0, col0, jnp.zeros((blk, LANES), F32))
            state = lax.fori_loop(0, qi, lambda j, s: block(j, s, False), (init, init))
            state = block(qi, state, True)
            dqn[rows, :] = jnp.where(lane_lo, state[0][2], state[1][2])
            return 0

        lax.fori_loop(0, nq, q_block, 0)

        def fold(x):
            return x + pltpu.roll(x, SB_HEAD_DIM, 1)

        _, q_vjp = jax.vjp(_head_norm, q_ref[...], qg_ref[...])
        dq, dqg = q_vjp(dqn[...])
        _, k_vjp = jax.vjp(_head_norm, k_ref[...], kg_ref[...])
        dk, dkg = k_vjp(dkn[...])
        dq_ref[...] = dq.astype(dq_ref.dtype)
        dk_ref[...] = dk.astype(dk_ref.dtype)
        dv_ref[...] = dvv[...].astype(dv_ref.dtype)
        first = jnp.logical_and(pl.program_id(0) == 0, pl.program_id(1) == 0)

        @pl.when(first)
        def _():
            dqg_ref[...] = fold(dqg)
            dkg_ref[...] = fold(dkg)

        @pl.when(jnp.logical_not(first))
        def _():
            dqg_ref[...] += fold(dqg)
            dkg_ref[...] += fold(dkg)

        if side:
            pl.when(jnp.logical_and(pl.program_id(0) == bsz - 1, pl.program_id(1) == d // LANES - 1))(wait)

    spec = pl.BlockSpec((None, seq, LANES), lambda b, c: (b, 0, c))
    gspec = pl.BlockSpec((1, LANES), lambda b, c: (0, 0))
    act = jax.ShapeDtypeStruct((bsz, seq, d), BF16)
    gain = jax.ShapeDtypeStruct((1, LANES), F32)
    return pl.pallas_call(
        body, name="sb_attn_bwd", grid=(bsz, d // LANES),
        in_specs=[spec, spec, spec, gspec, gspec, spec, spec] + [_HBM] * n_x,
        out_specs=[spec, spec, spec, gspec, gspec] + [_HBM] * n_x,
        out_shape=[act, act, act, gain, gain] + side_shapes,
        scratch_shapes=[pltpu.VMEM((seq, LANES), BF16)] * 3 + [pltpu.VMEM((seq, LANES), F32)] * 3 + side_sems,
        compiler_params=_params("arbitrary", "arbitrary"),
    )(q, k, v, qg, kg, tot, do, *side)


def sb_fwd(x, g, p, bsz, side=()):
    t, d = x.shape
    seq = t // bsz
    xn = rms_fwd(x, g, BF16, "sb_rms")
    q = mm(xn, p["w_q"], name="sb_q").reshape(bsz, seq, d)
    k = mm(xn, p["w_k"], name="sb_k").reshape(bsz, seq, d)
    v = mm(xn, p["w_v"], name="sb_v").reshape(bsz, seq, d)
    qg = jnp.tile(p["q_g"], (1, 2))
    kg = jnp.tile(p["k_g"], (1, 2))
    o, tot, *side_out = sb_attn_fwd(q, k, v, qg, kg, side)
    o = o.reshape(t, d)
    out = mm(o, p["w_o"], acc=x, name="sb_out")
    return out, (x, xn, q, k, v, o, tot), side_out


def sb_bwd(dy, saved, g, p, side=()):
    x, xn, q, k, v, o, tot = saved
    bsz, seq, d = q.shape
    t = bsz * seq
    d_wo = mm(o, dy, ta=True, name="sb_dwo")
    do = mm(dy, p["w_o"], tb=True, out_dtype=BF16, name="sb_do").reshape(bsz, seq, d)
    qg = jnp.tile(p["q_g"], (1, 2))
    kg = jnp.tile(p["k_g"], (1, 2))
    dq, dk, dv, dqg, dkg, *side_out = sb_attn_bwd(q, k, v, qg, kg, tot, do, side)
    dq, dk, dv = dq.reshape(t, d), dk.reshape(t, d), dv.reshape(t, d)
    d_wq = mm(xn, dq, ta=True, name="sb_dwq")
    d_wk = mm(xn, dk, ta=True, name="sb_dwk")
    d_wv = mm(xn, dv, ta=True, name="sb_dwv")
    dxn = mm(dq, p["w_q"], tb=True, name="sb_dxn1")
    dxn = mm(dk, p["w_k"], tb=True, acc=dxn, name="sb_dxn2")
    dxn = mm(dv, p["w_v"], tb=True, acc=dxn, name="sb_dxn3")
    dx, dg = rms_bwd(x, g, [dxn], dy, "sb_drms")
    grads = dict(g=dg, w_q=d_wq, w_k=d_wk, w_v=d_wv, w_o=d_wo, q_g=dqg[:, :SB_HEAD_DIM], k_g=dkg[:, :SB_HEAD_DIM])
    return dx, grads, side_out


def loss_head(y, target):
    d = y.shape[1]

    def fn(y, tgt):
        err = y - tgt
        return err * (1.0 / d), jnp.sum(err * err, axis=0, keepdims=True)

    dy, sq = rowwise(fn, [y, target], [], [(d, F32)], [(1, d)], name="loss_head")
    return sq, dy


def adamw(parts, w, m, v, name):
    n, r, c = parts.shape
    tr = _pick(r, max(16, (1 << 20) // (c * n)), 16 if parts.dtype == BF16 else 8)

    def body(p_ref, w_ref, m_ref, v_ref, g_ref, d_ref, nm_ref, nv_ref):
        g = p_ref[0].astype(F32)
        for i in range(1, n):
            g = g + p_ref[i].astype(F32)
        nm = ADAM_B1 * m_ref[...] + (1.0 - ADAM_B1) * g
        nv = ADAM_B2 * v_ref[...] + (1.0 - ADAM_B2) * (g * g)
        m_hat = nm / (1.0 - ADAM_B1 ** ADAM_STEP)
        v_hat = nv / (1.0 - ADAM_B2 ** ADAM_STEP)
        g_ref[...] = g
        d_ref[...] = -ADAM_LR * (m_hat / (jnp.sqrt(v_hat) + ADAM_EPS) + ADAM_WD * w_ref[...])
        nm_ref[...] = nm
        nv_ref[...] = nv

    spec = pl.BlockSpec((tr, c), lambda i: (i, 0))
    return pl.pallas_call(
        body, name=name, grid=(r // tr,), in_specs=[pl.BlockSpec((n, tr, c), lambda i: (0, i, 0)), spec, spec, spec],
        out_specs=[spec] * 4, out_shape=[jax.ShapeDtypeStruct((r, c), F32)] * 4,
        compiler_params=_params("parallel"),
    )(parts, w, m, v)


_HBM = pl.BlockSpec(memory_space=pltpu.HBM)


def _mesh_pos():
    return lax.axis_index("x"), lax.axis_index("y"), lax.axis_index("c")


def all_gather(shards, name):
    n = len(shards)

    def body(*refs):
        start, finish = _gather_exchange(refs[:n], refs[n:2 * n], *refs[2 * n:])
        start()
        finish()

    return pl.pallas_call(
        body, name=name, in_specs=[_HBM] * n, out_specs=[_HBM] * n,
        out_shape=[jax.ShapeDtypeStruct((N_DEV, *s.shape), s.dtype) for s in shards],
        scratch_shapes=_gather_exchange_sems(n),
    )(*shards)


def _gather_exchange(ins, outs, send_sems, recv_sems, local_sems):
    n = len(ins)
    x, y, c = _mesh_pos()
    me, sibling = (x, y, c), (x, y, 1 - c)
    chips = [(1 - x, y), (x, 1 - y), (1 - x, 1 - y)]

    def copy(a, k, block, to, src=None):
        px, py, pc = block
        dst = outs[a].at[4 * px + 2 * py + pc]
        return pltpu.make_async_remote_copy(
            src_ref=dst if src is None else src, dst_ref=dst, send_sem=send_sems.at[a, k],
            recv_sem=recv_sems.at[a, k], device_id=to, device_id_type=MESH)

    mine = [pltpu.make_async_copy(ins[a], outs[a].at[4 * x + 2 * y + c], local_sems.at[a]) for a in range(n)]
    first = []
    for a in range(n):
        first.append(copy(a, 0, me, sibling, src=ins[a]))
        first += [copy(a, 1 + j, me, (*chip, c), src=ins[a]) for j, chip in enumerate(chips)]

    def start():
        for cp in mine + first:
            cp.start()

    def finish():
        passed = []
        for a in range(n):
            for j, chip in enumerate(chips):
                copy(a, 1 + j, (*chip, c), me).wait_recv()
                cp = copy(a, 4 + j, (*chip, c), sibling)
                cp.start()
                passed.append(cp)
        for a in range(n):
            copy(a, 0, sibling, me).wait_recv()
            for j, chip in enumerate(chips):
                copy(a, 4 + j, (*chip, 1 - c), me).wait_recv()
        for cp in first + passed:
            cp.wait_send()
        for cp in mine:
            cp.wait()

    return start, finish


def _gather_exchange_sems(n):
    return [pltpu.SemaphoreType.DMA((n, 7)), pltpu.SemaphoreType.DMA((n, 7)), pltpu.SemaphoreType.DMA((n,))]


def exchange_cores(parts, name):
    n = len(parts)

    def body(*refs):
        ins, outs = refs[:n], refs[n:2 * n]
        send_sems, recv_sems = refs[2 * n:]
        x, y, c = _mesh_pos()
        copies = []
        for a in range(n):
            for p in range(4):
                copies.append(pltpu.make_async_remote_copy(
                    src_ref=ins[a].at[p, 1 - c], dst_ref=outs[a].at[p], send_sem=send_sems.at[a, p],
                    recv_sem=recv_sems.at[a, p], device_id=(x, y, 1 - c), device_id_type=MESH))
        for cp in copies:
            cp.start()
        for cp in copies:
            cp.wait()

    return pl.pallas_call(
        body, name=name, in_specs=[_HBM] * n, out_specs=[_HBM] * n,
        out_shape=[jax.ShapeDtypeStruct((4, *s.shape[2:]), s.dtype) for s in parts],
        scratch_shapes=[pltpu.SemaphoreType.DMA((n, 4)), pltpu.SemaphoreType.DMA((n, 4))],
    )(*parts)


def _chip_exchange(ins, outs, send_sems, recv_sems, local_sems):
    x, y, c = _mesh_pos()
    mine = 2 * x + y
    copies = []
    for a in range(len(ins)):
        copies.append(pltpu.make_async_copy(ins[a].at[mine], outs[a].at[mine], local_sems.at[a]))
        for r in range(1, 4):
            qx = 1 - x if r & 2 else x
            qy = 1 - y if r & 1 else y
            copies.append(pltpu.make_async_remote_copy(
                src_ref=ins[a].at[2 * qx + qy], dst_ref=outs[a].at[mine], send_sem=send_sems.at[a, r - 1],
                recv_sem=recv_sems.at[a, r - 1], device_id=(qx, qy, c), device_id_type=MESH))

    def start():
        for cp in copies:
            cp.start()

    def wait():
        for cp in copies:
            cp.wait()

    return start, wait


def _chip_exchange_sems(n):
    return [pltpu.SemaphoreType.DMA((n, 3)), pltpu.SemaphoreType.DMA((n, 3)), pltpu.SemaphoreType.DMA((n,))]


def _side_plan(side):
    if not side:
        return (), None, [], []
    kind, arrays = side
    n = len(arrays)
    if kind == "chips":
        return arrays, _chip_exchange, _chip_exchange_sems(n), [jax.ShapeDtypeStruct(a.shape, a.dtype) for a in arrays]
    assert kind == "gather", kind
    shapes = [jax.ShapeDtypeStruct((N_DEV, *a.shape), a.dtype) for a in arrays]
    return arrays, _gather_exchange, _gather_exchange_sems(n), shapes


def exchange_chips_and_gather(parts, shards, name):
    n, k = len(parts), len(shards)

    def body(*refs):
        sems = refs[2 * (n + k):]
        start_chips, wait_chips = _chip_exchange(refs[:n], refs[n + k:2 * n + k], *sems[:3])
        start_gather, finish_gather = _gather_exchange(refs[n:n + k], refs[2 * n + k:2 * (n + k)], *sems[3:])
        start_chips()
        start_gather()
        finish_gather()
        wait_chips()

    res = pl.pallas_call(
        body, name=name, in_specs=[_HBM] * (n + k), out_specs=[_HBM] * (n + k),
        out_shape=([jax.ShapeDtypeStruct(s.shape, s.dtype) for s in parts]
                   + [jax.ShapeDtypeStruct((N_DEV, *s.shape), s.dtype) for s in shards]),
        scratch_shapes=_chip_exchange_sems(n) + _gather_exchange_sems(k),
    )(*parts, *shards)
    return res[:n], res[n:]


def add_own(core, parts, recv, out_dtype, name):
    _, _, r, c = parts.shape
    tr = _pick(r, max(16, (1 << 19) // c), 16)

    def body(core_ref, p_ref, r_ref, o_ref):
        o_ref[...] = (p_ref[...] + r_ref[...]).astype(out_dtype)

    grid_spec = pltpu.PrefetchScalarGridSpec(
        num_scalar_prefetch=1, grid=(4, r // tr),
        in_specs=[pl.BlockSpec((None, None, tr, c), lambda p, i, core_ref: (p, core_ref[0], i, 0)),
                  pl.BlockSpec((None, tr, c), lambda p, i, core_ref: (p, i, 0))],
        out_specs=pl.BlockSpec((None, tr, c), lambda p, i, core_ref: (p, i, 0)))
    return pl.pallas_call(
        body, name=name, grid_spec=grid_spec, out_shape=jax.ShapeDtypeStruct((4, r, c), out_dtype),
        compiler_params=_params("parallel", "parallel"),
    )(core, parts, recv)


PARAMS = {
    "norm_mix_g": ((4, 1024), None), "norm_ffn_g": ((4, 1024), None),
    "pool_w": ((1, 4, 256, 256), 2), "pool_b": ((1, 1024), None), "pool_scale": ((1, 1024), None),
    "s5_lam_re": ((1, 64, 64), None), "s5_lam_im": ((1, 64, 64), None), "s5_log_dt": ((1, 64), None),
    "s5_b_re": ((1, 64, 64, 16), None), "s5_b_im": ((1, 64, 64, 16), None),
    "s5_c_re": ((1, 64, 16, 64), None), "s5_c_im": ((1, 64, 16, 64), None),
    "s5_d": ((1, 1024), 1), "s5_w_out": ((1, 1024, 2048), 2), "s5_b_out": ((1, 2048), 1),
    "lru_w_in": ((1, 1024, 2048), 2), "lru_conv_w": ((1, 4, 1024), 2), "lru_conv_b": ((1, 1024), 1),
    "lru_w_a": ((1, 4, 256, 256), 2), "lru_b_a": ((1, 1024), 1), "lru_w_x": ((1, 4, 256, 256), 2),
    "lru_b_x": ((1, 1024), 1), "lru_lam": ((1, 1024), 1), "lru_w_out": ((1, 1024, 1024), 1),
    "sb_w_qkv": ((1, 1024, 3072), 2), "sb_q_g": ((1, 64), None), "sb_k_g": ((1, 64), None),
    "sb_w_o": ((1, 1024, 1024), 1),
    "ffn_w_in": ((4, 1024, 5632), 2), "ffn_conv_w": ((4, 3, 5632), 2), "ffn_conv_b": ((4, 5632), None),
    "ffn_w_out": ((4, 2816, 1024), 1),
}
NAMES = list(PARAMS)
BIG = ["s5_w_out", "lru_w_in", "lru_w_out", "sb_w_qkv", "sb_w_o", "ffn_w_in", "ffn_w_out"]
SMALL = [n for n in NAMES if PARAMS[n][1] is not None and n not in BIG]
REPL = [n for n in NAMES if PARAMS[n][1] is None]


def _local_shape(name):
    shape, ax = PARAMS[name]
    return tuple(s // N_DEV if i == ax else s for i, s in enumerate(shape))


def _to_natural(gathered, name):
    shape, ax = PARAMS[name]
    return jnp.moveaxis(gathered, 0, ax).reshape(shape)


def _to_shards(nat, name):
    shape, ax = PARAMS[name]
    split = shape[:ax] + (N_DEV, shape[ax] // N_DEV) + shape[ax + 1:]
    return jnp.moveaxis(nat.reshape(split), ax, 0)


def _pack(arrays, lead=()):
    flat = jnp.concatenate([a.reshape(*lead, -1) for a in arrays], axis=-1)
    size = flat.shape[-1]
    rows = -(-size // (8 * LANES)) * 8
    flat = jnp.pad(flat, [(0, 0)] * len(lead) + [(0, rows * LANES - size)])
    return flat.reshape(*lead, rows, LANES)


def _unpack(packed, shapes, lead=()):
    flat = packed.reshape(*lead, -1)
    out, off = [], 0
    for s in shapes:
        size = math.prod(s)
        out.append(flat[..., off:off + size].reshape(*lead, *s))
        off += size
    return out


def kernel(x, norm_mix_g, norm_ffn_g, pool_w, pool_b, pool_scale, s5_lam_re, s5_lam_im, s5_log_dt, s5_b_re, s5_b_im, s5_c_re, s5_c_im, s5_d, s5_w_out, s5_b_out, lru_w_in, lru_conv_w, lru_conv_b, lru_w_a, lru_b_a, lru_w_x, lru_b_x, lru_lam, lru_w_out, sb_w_qkv, sb_q_g, sb_k_g, sb_w_o, ffn_w_in, ffn_conv_w, ffn_conv_b, ffn_w_out, loss_target, m_norm_mix_g, m_norm_ffn_g, m_pool_w, m_pool_b, m_pool_scale, m_s5_lam_re, m_s5_lam_im, m_s5_log_dt, m_s5_b_re, m_s5_b_im, m_s5_c_re, m_s5_c_im, m_s5_d, m_s5_w_out, m_s5_b_out, m_lru_w_in, m_lru_conv_w, m_lru_conv_b, m_lru_w_a, m_lru_b_a, m_lru_w_x, m_lru_b_x, m_lru_lam, m_lru_w_out, m_sb_w_qkv, m_sb_q_g, m_sb_k_g, m_sb_w_o, m_ffn_w_in, m_ffn_conv_w, m_ffn_conv_b, m_ffn_w_out, v_norm_mix_g, v_norm_ffn_g, v_pool_w, v_pool_b, v_pool_scale, v_s5_lam_re, v_s5_lam_im, v_s5_log_dt, v_s5_b_re, v_s5_b_im, v_s5_c_re, v_s5_c_im, v_s5_d, v_s5_w_out, v_s5_b_out, v_lru_w_in, v_lru_conv_w, v_lru_conv_b, v_lru_w_a, v_lru_b_a, v_lru_w_x, v_lru_b_x, v_lru_lam, v_lru_w_out, v_sb_w_qkv, v_sb_q_g, v_sb_k_g, v_sb_w_o, v_ffn_w_in, v_ffn_conv_w, v_ffn_conv_b, v_ffn_w_out):
    w = dict(zip(NAMES, (norm_mix_g, norm_ffn_g, pool_w, pool_b, pool_scale, s5_lam_re, s5_lam_im, s5_log_dt, s5_b_re, s5_b_im, s5_c_re, s5_c_im, s5_d, s5_w_out, s5_b_out, lru_w_in, lru_conv_w, lru_conv_b, lru_w_a, lru_b_a, lru_w_x, lru_b_x, lru_lam, lru_w_out, sb_w_qkv, sb_q_g, sb_k_g, sb_w_o, ffn_w_in, ffn_conv_w, ffn_conv_b, ffn_w_out)))
    m = dict(zip(NAMES, (m_norm_mix_g, m_norm_ffn_g, m_pool_w, m_pool_b, m_pool_scale, m_s5_lam_re, m_s5_lam_im, m_s5_log_dt, m_s5_b_re, m_s5_b_im, m_s5_c_re, m_s5_c_im, m_s5_d, m_s5_w_out, m_s5_b_out, m_lru_w_in, m_lru_conv_w, m_lru_conv_b, m_lru_w_a, m_lru_b_a, m_lru_w_x, m_lru_b_x, m_lru_lam, m_lru_w_out, m_sb_w_qkv, m_sb_q_g, m_sb_k_g, m_sb_w_o, m_ffn_w_in, m_ffn_conv_w, m_ffn_conv_b, m_ffn_w_out)))
    v = dict(zip(NAMES, (v_norm_mix_g, v_norm_ffn_g, v_pool_w, v_pool_b, v_pool_scale, v_s5_lam_re, v_s5_lam_im, v_s5_log_dt, v_s5_b_re, v_s5_b_im, v_s5_c_re, v_s5_c_im, v_s5_d, v_s5_w_out, v_s5_b_out, v_lru_w_in, v_lru_conv_w, v_lru_conv_b, v_lru_w_a, v_lru_b_a, v_lru_w_x, v_lru_b_x, v_lru_lam, v_lru_w_out, v_sb_w_qkv, v_sb_q_g, v_sb_k_g, v_sb_w_o, v_ffn_w_in, v_ffn_conv_w, v_ffn_conv_b, v_ffn_w_out)))
    return train_step(x, loss_target, w, m, v)


def train_step(x, loss_target, w, m, v):
    bsz, seq, d = x.shape
    t = bsz * seq
    n_layers = PARAMS["norm_mix_g"][0][0]

    small_shapes = [_local_shape(n) for n in SMALL]
    mixers = ["s5_w_out", "lru_w_in", "lru_w_out", "sb_w_qkv", "sb_w_o"]

    def ffn_shards(li):
        return [w["ffn_w_in"][li].astype(BF16), w["ffn_w_out"][li].astype(BF16)]

    def whole2d(gathered, name):
        _, r, c = gathered.shape
        if PARAMS[name][1] == 2:
            return gathered.transpose(1, 0, 2).reshape(r, N_DEV * c)
        return gathered.reshape(N_DEV * r, c)

    def ffn_whole(gathered):
        return whole2d(gathered[0], "ffn_w_in"), whole2d(gathered[1], "ffn_w_out")

    gathered = all_gather([w[n][0].astype(BF16) for n in mixers] + ffn_shards(0) + [_pack([w[n] for n in SMALL])],
                          "gather_weights")
    nat = {n: whole2d(g, n) for n, g in zip(mixers, gathered)}
    ffn_w = {0: ffn_whole(gathered[len(mixers):len(mixers) + 2])}
    for n, g in zip(SMALL, _unpack(gathered[-1], small_shapes, lead=(N_DEV,))):
        nat[n] = _to_natural(g, n)
    for n in REPL:
        nat[n] = w[n]

    hid = FFN_HIDDEN
    pool_p = (nat["pool_w"][0].astype(BF16), nat["pool_b"], nat["pool_scale"])
    s5_p = dict(lam_re=nat["s5_lam_re"][0], lam_im=nat["s5_lam_im"][0], log_dt=nat["s5_log_dt"][0],
                b_re=nat["s5_b_re"][0], b_im=nat["s5_b_im"][0], c_re=nat["s5_c_re"][0], c_im=nat["s5_c_im"][0],
                d=nat["s5_d"], w_val=nat["s5_w_out"][:, :d], w_gate=nat["s5_w_out"][:, d:], b_out=nat["s5_b_out"])
    lru_p = dict(w_gate=nat["lru_w_in"][:, :d], w_rec=nat["lru_w_in"][:, d:], conv_w=nat["lru_conv_w"][0],
                 conv_b=nat["lru_conv_b"], b_a=nat["lru_b_a"], b_x=nat["lru_b_x"], lam=nat["lru_lam"],
                 w_a=nat["lru_w_a"][0].astype(BF16), w_x=nat["lru_w_x"][0].astype(BF16), w_out=nat["lru_w_out"])
    sb_p = dict(w_q=nat["sb_w_qkv"][:, :d], w_k=nat["sb_w_qkv"][:, d:2 * d], w_v=nat["sb_w_qkv"][:, 2 * d:],
                w_o=nat["sb_w_o"], q_g=nat["sb_q_g"], k_g=nat["sb_k_g"])

    def ffn_p(li):
        w_in, w_out = ffn_w[li]
        return (w_in[:, :hid], w_in[:, hid:], nat["ffn_conv_w"][li], nat["ffn_conv_b"][li:li + 1], w_out)

    def gain(name, li):
        return nat[name][li:li + 1]

    h = x.reshape(t, d)
    h, pool_saved = pool_fwd(h, gain("norm_mix_g", 0), *pool_p, bsz)
    h, ffn0 = ffn_fwd(h, gain("norm_ffn_g", 0), *ffn_p(0), bsz, 0)
    h, s5_saved, arrived = s5_fwd(h, gain("norm_mix_g", 1), s5_p, bsz, ("gather", ffn_shards(1)))
    ffn_w[1] = ffn_whole(arrived)
    h, ffn1 = ffn_fwd(h, gain("norm_ffn_g", 1), *ffn_p(1), bsz, 1)
    h, lru_saved, arrived = lru_fwd(h, gain("norm_mix_g", 2), lru_p, bsz, ("gather", ffn_shards(2)))
    ffn_w[2] = ffn_whole(arrived)
    h, ffn2 = ffn_fwd(h, gain("norm_ffn_g", 2), *ffn_p(2), bsz, 2)
    h, sb_saved, arrived = sb_fwd(h, gain("norm_mix_g", 3), sb_p, bsz, ("gather", ffn_shards(3)))
    ffn_w[3] = ffn_whole(arrived)
    h, ffn3 = ffn_fwd(h, gain("norm_ffn_g", 3), *ffn_p(3), bsz, 3)
    sq, dh = loss_head(h, loss_target.reshape(t, d))
    loss = lax.psum(0.5 * jnp.sum(sq) / d, ("x", "y", "c"))

    core = lax.axis_index("c").astype(jnp.int32).reshape(1)

    def by_dest(grad2d, name):
        ax = PARAMS[name][1]
        r, c = grad2d.shape
        if ax == 2:
            return grad2d.reshape(r, N_DEV, c // N_DEV).transpose(1, 0, 2).reshape(4, 2, r, c // N_DEV)
        return grad2d.reshape(4, 2, r // N_DEV, c)

    def by_dest_cols(col_groups, name):
        shard = PARAMS[name][0][2] // N_DEV
        rows = col_groups[0].shape[0]
        blocks = [g.reshape(rows, g.shape[1] // shard, shard).transpose(1, 0, 2) for g in col_groups]
        return jnp.concatenate(blocks, axis=0).reshape(4, 2, rows, shard)

    def ffn_group(li, g):
        return [(f"ffn_w_in{li}", by_dest_cols([g[1], g[2]], "ffn_w_in"), BF16),
                (f"ffn_w_out{li}", by_dest(g[5], "ffn_w_out"), BF16)]

    def core_stage(group, tag):
        recv = exchange_cores([a for _, a, _ in group], f"reduce_cores_{tag}")
        return "chips", [add_own(core, a, r, dt, f"add_{k}") for (k, a, dt), r in zip(group, recv)]

    reduced = {}
    fg = [None] * n_layers
    dh, fg[3] = ffn_bwd(dh, ffn3, gain("norm_ffn_g", 3), *ffn_p(3), 3)
    group = ffn_group(3, fg[3])
    dh, sb_g, arrived = sb_bwd(dh, sb_saved, gain("norm_mix_g", 3), sb_p, core_stage(group, "ffn3"))
    reduced.update(zip([k for k, _, _ in group], arrived))
    dh, fg[2] = ffn_bwd(dh, ffn2, gain("norm_ffn_g", 2), *ffn_p(2), 2)
    group = [("sb_w_qkv", by_dest(jnp.concatenate([sb_g["w_q"], sb_g["w_k"], sb_g["w_v"]], axis=1), "sb_w_qkv"), BF16),
             ("sb_w_o", by_dest(sb_g["w_o"], "sb_w_o"), BF16)] + ffn_group(2, fg[2])
    dh, lru_g, arrived = lru_bwd(dh, lru_saved, gain("norm_mix_g", 2), lru_p, core_stage(group, "sb_ffn2"))
    reduced.update(zip([k for k, _, _ in group], arrived))
    dh, fg[1] = ffn_bwd(dh, ffn1, gain("norm_ffn_g", 1), *ffn_p(1), 1)
    group = [("lru_w_in", by_dest_cols([lru_g["w_gate"], lru_g["w_rec"]], "lru_w_in"), BF16),
             ("lru_w_out", by_dest(lru_g["w_out"], "lru_w_out"), BF16)] + ffn_group(1, fg[1])
    dh, s5_g, arrived = s5_bwd(dh, s5_saved, gain("norm_mix_g", 1), s5_p, core_stage(group, "lru_ffn1"))
    reduced.update(zip([k for k, _, _ in group], arrived))
    dh, fg[0] = ffn_bwd(dh, ffn0, gain("norm_ffn_g", 0), *ffn_p(0), 0)
    dh, pool_g = pool_bwd(dh, pool_saved, gain("norm_mix_g", 0), *pool_p)
    grad_x = dh.reshape(bsz, seq, d)

    part = {
        "norm_mix_g": jnp.concatenate([pool_g[0], s5_g["g"], lru_g["g"], sb_g["g"]], axis=0),
        "norm_ffn_g": jnp.concatenate([g[0] for g in fg], axis=0),
        "pool_w": pool_g[1][None], "pool_b": pool_g[2], "pool_scale": pool_g[3],
        "s5_lam_re": s5_g["lam_re"][None], "s5_lam_im": s5_g["lam_im"][None], "s5_log_dt": s5_g["log_dt"][None],
        "s5_b_re": s5_g["b_re"][None], "s5_b_im": s5_g["b_im"][None], "s5_c_re": s5_g["c_re"][None],
        "s5_c_im": s5_g["c_im"][None], "s5_d": s5_g["d"], "s5_b_out": s5_g["b_out"],
        "lru_conv_w": lru_g["conv_w"][None], "lru_conv_b": lru_g["conv_b"], "lru_w_a": lru_g["w_a"][None],
        "lru_b_a": lru_g["b_a"], "lru_w_x": lru_g["w_x"][None], "lru_b_x": lru_g["b_x"], "lru_lam": lru_g["lam"],
        "sb_q_g": sb_g["q_g"], "sb_k_g": sb_g["k_g"],
        "ffn_conv_w": jnp.stack([g[3] for g in fg]), "ffn_conv_b": jnp.concatenate([g[4] for g in fg], axis=0),
    }
    small_part = _pack([_to_shards(part[n], n) for n in SMALL], lead=(N_DEV,))
    group = [("s5_w_out", by_dest_cols([s5_g["w_val"], s5_g["w_gate"]], "s5_w_out"), BF16)]
    group += ffn_group(0, fg[0]) + [("small", small_part.reshape(4, 2, *small_part.shape[1:]), F32)]
    arrived, (repl_parts,) = exchange_chips_and_gather(
        core_stage(group, "s5_ffn0_small")[1], [_pack([part[n] for n in REPL])], "reduce_chips_gather_grads")
    reduced.update(zip([k for k, _, _ in group], arrived))

    out = {}
    for n in BIG:
        layers = PARAMS[n][0][0]
        res = []
        for li in range(layers):
            key = f"{n}{li}" if layers > 1 else n
            res.append(adamw(reduced[key], w[n][li], m[n][li], v[n][li], f"adamw_{key}"))
        out[n] = [jnp.stack([r[i] for r in res]) for i in range(4)]
    small_res = adamw(reduced["small"], *[_pack([tree[n] for n in SMALL]) for tree in (w, m, v)], "adamw_small")
    for i, res in enumerate(small_res):
        for n, r in zip(SMALL, _unpack(res, small_shapes)):
            out.setdefault(n, [None] * 4)[i] = r

    repl_shapes = [PARAMS[n][0] for n in REPL]
    repl_res = adamw(repl_parts, *[_pack([tree[n] for n in REPL]) for tree in (w, m, v)], "adamw_replicated")
    for i, res in enumerate(repl_res):
        for n, r in zip(REPL, _unpack(res, repl_shapes)):
            out.setdefault(n, [None] * 4)[i] = r

    return (loss, grad_x, *[out[n][0] for n in NAMES], *[out[n][1] for n in NAMES], *[out[n][2] for n in NAMES],
            *[out[n][3] for n in NAMES])
```

```python
import functools
import math

import jax
import jax.numpy as jnp
from jax import lax
from jax.experimental import pallas as pl
from jax.experimental.pallas import tpu as pltpu

F32 = jnp.float32
BF16 = jnp.bfloat16
MESH = pl.DeviceIdType.MESH

N_DEV = 8
D_MODEL = 1024
EPS = 1e-6
POOL_GROUP = 256
S5_GROUPS, S5_GROUP, S5_STATE = 64, 16, 64
LRU_BLOCK = 256
LRU_C = 8.0
SB_HEAD_DIM = 64
ATT_BLOCK = 256
ATT_Q_ROWS = 512
FFN_HIDDEN = 2816
ADAM_LR, ADAM_B1, ADAM_B2, ADAM_EPS, ADAM_WD, ADAM_STEP = 0.001, 0.9, 0.999, 1e-08, 0.01, 10
LANES = 128
SUBLANES = 8
VMEM_LIMIT = 56 * 1024 * 1024


def _pick(n, target, mult=LANES):
    best = None
    for d in range(mult, min(n, target) + 1, mult):
        if n % d == 0:
            best = d
    return best or n


def _params(*sem):
    return pltpu.CompilerParams(dimension_semantics=sem, vmem_limit_bytes=VMEM_LIMIT)


def mm(a, b, *, ta=False, tb=False, acc=None, out_dtype=F32, name):
    if ta:
        kdim, m = a.shape
    else:
        m, kdim = a.shape
    if tb:
        n, k2 = b.shape
    else:
        k2, n = b.shape
    assert kdim == k2, (a.shape, b.shape)
    tm = _pick(m, 1408 if ta else 512)
    tn = _pick(n, 1408)
    tk = _pick(kdim, 1024) if ta else kdim
    nk = kdim // tk
    a_spec = pl.BlockSpec((tk, tm), lambda j, i, k: (k, i)) if ta else pl.BlockSpec((tm, tk), lambda j, i, k: (i, k))
    b_spec = pl.BlockSpec((tn, tk), lambda j, i, k: (j, k)) if tb else pl.BlockSpec((tk, tn), lambda j, i, k: (k, j))
    o_spec = pl.BlockSpec((tm, tn), lambda j, i, k: (i, j))
    dims = (((0,) if ta else (1,), (1,) if tb else (0,)), ((), ()))
    has_acc = acc is not None

    def body(*refs):
        if has_acc:
            a_ref, b_ref, c_ref, o_ref, acc_ref = refs
        else:
            a_ref, b_ref, o_ref, acc_ref = refs
        k = pl.program_id(2)

        @pl.when(k == 0)
        def _():
            acc_ref[...] = c_ref[...].astype(F32) if has_acc else jnp.zeros_like(acc_ref)

        acc_ref[...] += lax.dot_general(a_ref[...].astype(BF16), b_ref[...].astype(BF16), dims,
                                        preferred_element_type=F32)

        @pl.when(k == nk - 1)
        def _():
            o_ref[...] = acc_ref[...].astype(out_dtype)

    ins = [a, b] + ([acc] if has_acc else [])
    specs = [a_spec, b_spec] + ([o_spec] if has_acc else [])
    return pl.pallas_call(
        body, name=name, grid=(n // tn, m // tm, nk), in_specs=specs, out_specs=o_spec,
        out_shape=jax.ShapeDtypeStruct((m, n), out_dtype), scratch_shapes=[pltpu.VMEM((tm, tn), F32)],
        compiler_params=_params("parallel", "parallel", "arbitrary"),
    )(*ins)


def rowwise(fn, tiled, bcast, outs, reds=(), *, tm=256, name):
    t = tiled[0].shape[0]
    tm = min(tm, t)
    assert t % tm == 0
    n_t, n_b, n_o, n_r = len(tiled), len(bcast), len(outs), len(reds)

    def body(*refs):
        vals = [r[...] for r in refs[:n_t + n_b]]
        res = fn(*vals)
        res = res if isinstance(res, tuple) else (res,)
        o_refs = refs[n_t + n_b:n_t + n_b + n_o]
        r_refs = refs[n_t + n_b + n_o:]
        for o_ref, v in zip(o_refs, res[:n_o]):
            o_ref[...] = v.astype(o_ref.dtype)
        first = pl.program_id(0) == 0
        for r_ref, v in zip(r_refs, res[n_o:]):
            @pl.when(first)
            def _(r_ref=r_ref, v=v):
                r_ref[...] = v.astype(F32)

            @pl.when(jnp.logical_not(first))
            def _(r_ref=r_ref, v=v):
                r_ref[...] += v.astype(F32)

    in_specs = [pl.BlockSpec((tm, a.shape[1]), lambda i: (i, 0)) for a in tiled]
    in_specs += [pl.BlockSpec(a.shape, lambda i, nd=a.ndim: (0,) * nd) for a in bcast]
    out_specs = [pl.BlockSpec((tm, c), lambda i: (i, 0)) for c, _ in outs]
    out_specs += [pl.BlockSpec(s, lambda i: (0, 0)) for s in reds]
    out_shape = [jax.ShapeDtypeStruct((t, c), dt) for c, dt in outs]
    out_shape += [jax.ShapeDtypeStruct(s, F32) for s in reds]
    res = pl.pallas_call(
        body, name=name, grid=(t // tm,), in_specs=in_specs, out_specs=out_specs, out_shape=out_shape,
        compiler_params=_params("arbitrary"),
    )(*tiled, *bcast)
    return res


def seqwise(fn, seqs, chans, outs, reds=(), *, ct, name, fulls=()):
    def split(x):
        return x if isinstance(x, tuple) else (x, 0)

    seqs = [split(s) for s in seqs]
    chans = [split(s) for s in chans]
    bsz, seq = seqs[0][0].shape[:2]
    n_c = outs[0][0] // ct if outs else reds[0][1] // ct
    n_s, n_ch, n_f, n_o, n_r = len(seqs), len(chans), len(fulls), len(outs), len(reds)

    def body(*refs):
        c = pl.program_id(0)
        vals = [r[...] for r in refs[:n_s + n_ch + n_f]]
        res = fn(c, *vals)
        res = res if isinstance(res, tuple) else (res,)
        o_refs = refs[n_s + n_ch + n_f:n_s + n_ch + n_f + n_o]
        r_refs = refs[n_s + n_ch + n_f + n_o:]
        for o_ref, v in zip(o_refs, res[:n_o]):
            o_ref[...] = v.astype(o_ref.dtype)
        first = pl.program_id(1) == 0
        for r_ref, v in zip(r_refs, res[n_o:]):
            @pl.when(first)
            def _(r_ref=r_ref, v=v):
                r_ref[...] = v.astype(F32)

            @pl.when(jnp.logical_not(first))
            def _(r_ref=r_ref, v=v):
                r_ref[...] += v.astype(F32)

    in_specs = [pl.BlockSpec((None, seq, ct), lambda c, b, off=off: (b, 0, c + off)) for _, off in seqs]
    in_specs += [pl.BlockSpec((a.shape[0], ct), lambda c, b, off=off: (0, c + off)) for a, off in chans]
    in_specs += [pl.BlockSpec(a.shape, lambda c, b, nd=a.ndim: (0,) * nd) for a in fulls]
    out_specs = [pl.BlockSpec((None, seq, ct), lambda c, b: (b, 0, c)) for _ in outs]
    out_specs += [pl.BlockSpec((r, ct), lambda c, b: (0, c)) for r, _ in reds]
    out_shape = [jax.ShapeDtypeStruct((bsz, seq, cc), dt) for cc, dt in outs]
    out_shape += [jax.ShapeDtypeStruct(s, F32) for s in reds]
    return pl.pallas_call(
        body, name=name, grid=(n_c, bsz), in_specs=in_specs, out_specs=out_specs, out_shape=out_shape,
        compiler_params=_params("arbitrary", "arbitrary"),
    )(*[a for a, _ in seqs], *[a for a, _ in chans], *fulls)


def _rows(x):
    return lax.broadcasted_iota(jnp.int32, x.shape, 0)


def shift_down(x, d, fill=0.0):
    if d == 0:
        return x
    s = x.shape[0]
    if d % SUBLANES == 0:
        return jnp.concatenate([jnp.full((d,) + x.shape[1:], fill, x.dtype), x[:s - d]], axis=0)
    rolled = pltpu.roll(x, d, 0)
    if d > SUBLANES or s <= SUBLANES:
        return jnp.where(_rows(x) >= d, rolled, fill)
    head = rolled[:SUBLANES]
    return jnp.concatenate([jnp.where(_rows(head) >= d, head, fill), rolled[SUBLANES:]], axis=0)


def shift_up(x, d, fill=0.0):
    if d == 0:
        return x
    s = x.shape[0]
    if d % SUBLANES == 0:
        return jnp.concatenate([x[d:], jnp.full((d,) + x.shape[1:], fill, x.dtype)], axis=0)
    rolled = pltpu.roll(x, s - d, 0)
    if d > SUBLANES or s <= SUBLANES:
        return jnp.where(_rows(x) < s - d, rolled, fill)
    tail = rolled[s - SUBLANES:]
    return jnp.concatenate([rolled[:s - SUBLANES], jnp.where(_rows(tail) < SUBLANES - d, tail, fill)], axis=0)


def conv_fwd(x, w, b):
    kw = w.shape[0]
    y = b + w[kw - 1:kw] * x
    for d in range(1, kw):
        y = y + w[kw - 1 - d:kw - d] * shift_down(x, d)
    return y


def conv_bwd(x, w, dy):
    kw = w.shape[0]
    dx = w[kw - 1:kw] * dy
    dws = [jnp.sum(dy * x, axis=0, keepdims=True)]
    for d in range(1, kw):
        dx = dx + w[kw - 1 - d:kw - d] * shift_up(dy, d)
        dws.append(jnp.sum(dy * shift_down(x, d), axis=0, keepdims=True))
    dw = jnp.concatenate(dws[::-1], axis=0)
    return dx, dw, jnp.sum(dy, axis=0, keepdims=True)


def sigmoid(x):
    return 0.5 * jnp.tanh(0.5 * x) + 0.5


def rms_fwd(x, g, out_dtype, name):
    def fn(x, g):
        return x * lax.rsqrt(jnp.mean(x * x, axis=-1, keepdims=True) + EPS) * g

    return rowwise(fn, [x], [g], [(x.shape[1], out_dtype)], name=name)[0]


def rms_bwd(x, g, dh_parts, dres, name):
    n_p = len(dh_parts)

    def fn(x, dres, *rest):
        dh = rest[0].astype(F32)
        for p in rest[1:n_p]:
            dh = dh + p.astype(F32)
        g = rest[n_p]
        r = lax.rsqrt(jnp.mean(x * x, axis=-1, keepdims=True) + EPS)
        xh = x * r
        dxh = dh * g
        dx = r * (dxh - xh * jnp.mean(dxh * xh, axis=-1, keepdims=True))
        return dres + dx, jnp.sum(dh * xh, axis=0, keepdims=True)

    return rowwise(fn, [x, dres, *dh_parts], [g], [(x.shape[1], F32)], [(1, x.shape[1])], name=name)


def ffn_fwd(x, g, w_val, w_gate, cw, cb, w_out, bsz, li):
    t, d = x.shape
    hid = w_val.shape[1]
    xn = rms_fwd(x, g, BF16, f"ffn{li}_rms")
    hv = mm(xn, w_val, name=f"ffn{li}_hv").reshape(bsz, t // bsz, hid)
    hg = mm(xn, w_gate, name=f"ffn{li}_hg").reshape(bsz, t // bsz, hid)
    ct = _pick(hid, 256)
    off = hid // ct

    def fn(c, hv, hg, wv, wg, bv, bg):
        val = conv_fwd(hv, wv, bv)
        gate = conv_fwd(hg, wg, bg)
        return gate * sigmoid(gate) * val

    act = seqwise(fn, [hv, hg], [cw, (cw, off), cb, (cb, off)], [(hid, BF16)], ct=ct, name=f"ffn{li}_gate")[0]
    act = act.reshape(t, hid)
    y = mm(act, w_out, acc=x, name=f"ffn{li}_out")
    return y, (x, xn, hv, hg, act)


def ffn_bwd(dy, saved, g, w_val, w_gate, cw, cb, w_out, li):
    x, xn, hv, hg, act = saved
    bsz, seq, hid = hv.shape
    t = bsz * seq
    d_wout = mm(act, dy, ta=True, name=f"ffn{li}_dwout")
    dact = mm(dy, w_out, tb=True, name=f"ffn{li}_dact").reshape(bsz, seq, hid)
    ct = _pick(hid, 256)
    off = hid // ct

    def fn(c, hv, hg, dact, wv, wg, bv, bg):
        val = conv_fwd(hv, wv, bv)
        gate = conv_fwd(hg, wg, bg)
        sg = sigmoid(gate)
        silu = gate * sg
        dval = dact * silu
        dgate = dact * val * (sg + silu * (1.0 - sg))
        dhv, dwv, dbv = conv_bwd(hv, wv, dval)
        dhg, dwg, dbg = conv_bwd(hg, wg, dgate)
        return dhv, dhg, dwv, dwg, dbv, dbg

    kw = cw.shape[0]
    dhv, dhg, dwv, dwg, dbv, dbg = seqwise(
        fn, [hv, hg, dact], [cw, (cw, off), cb, (cb, off)], [(hid, BF16), (hid, BF16)],
        [(kw, hid), (kw, hid), (1, hid), (1, hid)], ct=ct, name=f"ffn{li}_dgate")
    dhv = dhv.reshape(t, hid)
    dhg = dhg.reshape(t, hid)
    d_wv = mm(xn, dhv, ta=True, name=f"ffn{li}_dwv")
    d_wg = mm(xn, dhg, ta=True, name=f"ffn{li}_dwg")
    dxn = mm(dhv, w_val, tb=True, name=f"ffn{li}_dxn1")
    dxn = mm(dhg, w_gate, tb=True, acc=dxn, name=f"ffn{li}_dxn2")
    dx, dg = rms_bwd(x, g, [dxn], dy, f"ffn{li}_drms")
    d_cw = jnp.concatenate([dwv, dwg], axis=1)
    d_cb = jnp.concatenate([dbv, dbg], axis=1)
    return dx, (dg, d_wv, d_wg, d_cw, d_cb, d_wout)


def _window_sum(x, c, shift):
    s2 = x + shift(x, 1)
    s4 = s2 + shift(s2, 2)
    s8 = s4 + shift(s4, 4)
    s16 = s8 + shift(s8, 8)
    return jnp.where(c == 0, s2, jnp.where(c == 1, s4, jnp.where(c == 2, s8, s16)))


def _pool_inv_count(x, c):
    w = jnp.left_shift(2, c)
    return 1.0 / jnp.minimum(_rows(x) + 1, w).astype(F32)


def pool_fwd(x, g, w, b, scale, bsz):
    t, d = x.shape
    h = rms_fwd(x, g, F32, "pool_rms").reshape(bsz, t // bsz, d)

    def fn(c, h, x, b, scale, w):
        dd = _window_sum(h, c, shift_down) * _pool_inv_count(h, c) - h
        y = jnp.dot(dd.astype(BF16), w[0], preferred_element_type=F32) + b
        return x + scale * y

    wspec = pl.BlockSpec((1, POOL_GROUP, POOL_GROUP), lambda c, bb: (c, 0, 0))
    y = _seqwise_w(fn, [h, x.reshape(bsz, t // bsz, d)], [b, scale], [(w, wspec)], [(d, F32)], [], ct=POOL_GROUP,
                   name="pool_fwd")[0]
    return y.reshape(t, d), (x, h)


def pool_bwd(dy, saved, g, w, b, scale):
    x, h = saved
    bsz, seq, d = h.shape
    t = bsz * seq

    def fn(c, h, dy, b, scale, w):
        inv = _pool_inv_count(h, c)
        dd = _window_sum(h, c, shift_down) * inv - h
        ddb = dd.astype(BF16)
        y = jnp.dot(ddb, w[0], preferred_element_type=F32) + b
        dscale = jnp.sum(dy * y, axis=0, keepdims=True)
        dyy = dy * scale
        db = jnp.sum(dyy, axis=0, keepdims=True)
        dyb = dyy.astype(BF16)
        dw = lax.dot_general(ddb, dyb, (((0,), (0,)), ((), ())), preferred_element_type=F32)
        ddd = lax.dot_general(dyb, w[0], (((1,), (1,)), ((), ())), preferred_element_type=F32)
        dh = _window_sum(ddd * inv, c, shift_up) - ddd
        return dh, db, dscale, dw[None]

    wspec = pl.BlockSpec((1, POOL_GROUP, POOL_GROUP), lambda c, bb: (c, 0, 0))
    dh, db, dscale, dw = _seqwise_w(
        fn, [h, dy.reshape(bsz, seq, d)], [b, scale], [(w, wspec)], [(d, F32)], [(1, d), (1, d)], ct=POOL_GROUP,
        name="pool_bwd", wreds=[((4, POOL_GROUP, POOL_GROUP), wspec)])
    dx, dg = rms_bwd(x, g, [dh.reshape(t, d)], dy, "pool_drms")
    return dx, (dg, dw, db, dscale)


def _seqwise_w(fn, seqs, chans, blocked, outs, reds, *, ct, name, wreds=(), phased=False, side=()):
    bsz, seq = seqs[0].shape[:2]
    n_c = seqs[0].shape[2] // ct
    n_s = len(seqs)
    n_in = n_s + len(chans) + len(blocked)
    side, exchange, side_sems, side_shapes = _side_plan(side)
    n_o, n_r, n_x = len(outs), len(reds) + len(wreds), len(side)

    def body(*refs):
        c = pl.program_id(0)
        if side:
            ends = n_in + n_x + n_o + n_r
            start, wait = exchange(refs[n_in:n_in + n_x], refs[ends:ends + n_x], *refs[ends + n_x:])
            pl.when(jnp.logical_and(c == 0, pl.program_id(1) == 0))(start)
        seq_vals = [_load_phases(r) if phased else r[...] for r in refs[:n_s]]
        res = fn(c, *seq_vals, *[r[...] for r in refs[n_s:n_in]])
        res = res if isinstance(res, tuple) else (res,)
        own = refs[n_in + n_x:n_in + n_x + n_o + n_r]
        for o_ref, v in zip(own[:n_o], res[:n_o]):
            if phased:
                _store_phases(o_ref, v)
            else:
                o_ref[...] = v.astype(o_ref.dtype)
        first = pl.program_id(1) == 0
        for r_ref, v in zip(own[n_o:], res[n_o:]):
            @pl.when(first)
            def _(r_ref=r_ref, v=v):
                r_ref[...] = v.astype(F32)

            @pl.when(jnp.logical_not(first))
            def _(r_ref=r_ref, v=v):
                r_ref[...] += v.astype(F32)
        if side:
            pl.when(jnp.logical_and(c == n_c - 1, pl.program_id(1) == bsz - 1))(wait)

    in_specs = [pl.BlockSpec((None, seq, ct), lambda c, b: (b, 0, c)) for _ in seqs]
    in_specs += [pl.BlockSpec((a.shape[0], ct), lambda c, b: (0, c)) for a in chans]
    in_specs += [spec for _, spec in blocked]
    in_specs += [_HBM] * n_x
    out_specs = [pl.BlockSpec((None, seq, ct), lambda c, b: (b, 0, c)) for _ in outs]
    out_specs += [pl.BlockSpec((r, ct), lambda c, b: (0, c)) for r, _ in reds]
    out_specs += [spec for _, spec in wreds]
    out_specs += [_HBM] * n_x
    out_shape = [jax.ShapeDtypeStruct((bsz, seq, cc), dt) for cc, dt in outs]
    out_shape += [jax.ShapeDtypeStruct(s, F32) for s in reds]
    out_shape += [jax.ShapeDtypeStruct(s, F32) for s, _ in wreds]
    out_shape += side_shapes
    return pl.pallas_call(
        body, name=name, grid=(n_c, bsz), in_specs=in_specs, out_specs=out_specs, out_shape=out_shape,
        scratch_shapes=side_sems,
        compiler_params=_params("arbitrary", "arbitrary"),
    )(*seqs, *chans, *[a for a, _ in blocked], *side)


S5_TILE_GROUPS = LANES // S5_GROUP
S5_TILES = S5_GROUPS // S5_TILE_GROUPS
S5_TILE_STATE = S5_TILE_GROUPS * S5_STATE


def _s5_discretize(lam_re, lam_im, log_dt, b_re, b_im):
    lr = jnp.minimum(lam_re, -1e-4)
    dt = jnp.exp(log_dt)
    er = jnp.exp(lr * dt)
    ar = er * jnp.cos(lam_im * dt)
    ai = er * jnp.sin(lam_im * dt)
    den = lr * lr + lam_im * lam_im
    cr = ((ar - 1.0) * lr + ai * lam_im) / den
    ci = (ai * lr - (ar - 1.0) * lam_im) / den
    return ar, ai, cr * b_re - ci * b_im, cr * b_im + ci * b_re


def _whole(a):
    return pl.BlockSpec(a.shape, lambda *_: (0,) * a.ndim)


def s5_prep(lam_re, lam_im, log_dt, b_re, b_im):
    def body(lr, li, ld, br, bi, ar_o, ai_o, bbr_o, bbi_o):
        ar, ai, bbr, bbi = _s5_discretize(lr[...], li[...], ld[...], br[...], bi[...])
        ar_o[...] = ar
        ai_o[...] = ai
        bbr_o[...] = bbr
        bbi_o[...] = bbi

    ins = [lam_re, lam_im, log_dt, b_re, b_im]
    outs = [lam_re, lam_im, b_re, b_im]
    return pl.pallas_call(
        body, name="s5_prep", in_specs=[_whole(a) for a in ins], out_specs=[_whole(a) for a in outs],
        out_shape=[jax.ShapeDtypeStruct(a.shape, F32) for a in outs],
        compiler_params=pltpu.CompilerParams(vmem_limit_bytes=VMEM_LIMIT),
    )(*ins)


def s5_prep_bwd(lam_re, lam_im, log_dt, b_re, b_im, d_ar, d_ai, d_bbr, d_bbi):
    def body(lr, li, ld, br, bi, dar, dai, dbbr, dbbi, *outs):
        _, vjp = jax.vjp(_s5_discretize, lr[...], li[...], ld[...], br[...], bi[...])
        for o, v in zip(outs, vjp((dar[...], dai[...], dbbr[...], dbbi[...]))):
            o[...] = v

    ins = [lam_re, lam_im, log_dt, b_re, b_im, d_ar, d_ai, d_bbr, d_bbi]
    return pl.pallas_call(
        body, name="s5_prep_bwd", in_specs=[_whole(a) for a in ins], out_specs=[_whole(a) for a in ins[:5]],
        out_shape=[jax.ShapeDtypeStruct(a.shape, F32) for a in ins[:5]],
        compiler_params=pltpu.CompilerParams(vmem_limit_bytes=VMEM_LIMIT),
    )(*ins)


def scan_lti(br, bi, ar, ai, reverse=False):
    shift = shift_up if reverse else shift_down
    seq = br.shape[0]
    d = 1
    while d < seq:
        sr, si = shift(br, d), shift(bi, d)
        br, bi = br + ar * sr - ai * si, bi + ar * si + ai * sr
        ar, ai = ar * ar - ai * ai, 2.0 * ar * ai
        d *= 2
    return br, bi


PHASES = SUBLANES


def _load_phases(ref):
    groups = ref.shape[0] // PHASES
    return jnp.concatenate([ref[pl.ds(r, groups, stride=PHASES), :] for r in range(PHASES)], axis=0)


def _store_phases(ref, x):
    groups = ref.shape[0] // PHASES
    for r in range(PHASES):
        ref[pl.ds(r, groups, stride=PHASES), :] = x[r * groups:(r + 1) * groups].astype(ref.dtype)


def _split_phases(x):
    groups = x.shape[0] // PHASES
    return [x[r * groups:(r + 1) * groups] for r in range(PHASES)]


def _prev_phased(x):
    ph = _split_phases(x)
    return jnp.concatenate([shift_down(ph[PHASES - 1], 1)] + ph[:PHASES - 1], axis=0)


def _cmul_add(xr, xi, ar, ai, yr, yi):
    return xr + ar * yr - ai * yi, xi + ar * yi + ai * yr


def scan_phases(br, bi, ar, ai, reverse=False):
    rs, im = _split_phases(br), _split_phases(bi)
    order = range(PHASES - 2, -1, -1) if reverse else range(1, PHASES)
    step = 1 if reverse else -1
    for r in order:
        rs[r], im[r] = _cmul_add(rs[r], im[r], ar, ai, rs[r + step], im[r + step])
    powers = [(ar, ai)]
    for _ in range(PHASES - 1):
        pr, pi = powers[-1]
        powers.append((pr * ar - pi * ai, pr * ai + pi * ar))
    end = 0 if reverse else PHASES - 1
    cr, ci = scan_lti(rs[end], im[end], *powers[PHASES - 1], reverse)
    shift = shift_up if reverse else shift_down
    inr, ini = shift(cr, 1), shift(ci, 1)
    for r in range(PHASES):
        if r == end:
            rs[r], im[r] = cr, ci
        else:
            pr, pi = powers[PHASES - 1 - r if reverse else r]
            rs[r], im[r] = _cmul_add(rs[r], im[r], pr, pi, inr, ini)
    return jnp.concatenate(rs, axis=0), jnp.concatenate(im, axis=0)


def _s5_block_diag(w, rows_first):
    g, a, b = w.shape
    w = w.reshape(S5_TILES, S5_TILE_GROUPS, a, b)
    eye = jnp.eye(S5_TILE_GROUPS, dtype=w.dtype)
    if rows_first:
        return jnp.einsum("tgab,gk->tgakb", w, eye).reshape(S5_TILES, S5_TILE_GROUPS * a, S5_TILE_GROUPS * b)
    return jnp.einsum("tgab,gk->tgbka", w, eye).reshape(S5_TILES, S5_TILE_GROUPS * b, S5_TILE_GROUPS * a)


def _s5_diag_blocks(w, a, b):
    w = w.reshape(S5_TILES, S5_TILE_GROUPS, a, S5_TILE_GROUPS, b)
    eye = jnp.eye(S5_TILE_GROUPS, dtype=w.dtype)
    return jnp.einsum("tgakb,gk->tgab", w, eye).reshape(S5_GROUPS, a, b)


def _gelu(x):
    return jax.nn.gelu(x)


def _s5_tile_specs():
    b_spec = pl.BlockSpec((1, LANES, S5_TILE_STATE), lambda c, b: (c, 0, 0))
    c_spec = pl.BlockSpec((1, S5_TILE_STATE, LANES), lambda c, b: (c, 0, 0))
    a_spec = pl.BlockSpec((1, 1, S5_TILE_STATE), lambda c, b: (c, 0, 0))
    return b_spec, c_spec, a_spec


def s5_fwd(x, g, p, bsz, side=()):
    t, d = x.shape
    seq = t // bsz
    h = rms_fwd(x, g, F32, "s5_rms").reshape(bsz, seq, d)
    ar, ai, bbr, bbi = s5_prep(p["lam_re"][..., None], p["lam_im"][..., None], p["log_dt"][:, None, None],
                               p["b_re"], p["b_im"])
    bdr = _s5_block_diag(bbr, False).astype(BF16)
    bdi = _s5_block_diag(bbi, False).astype(BF16)
    cdr = _s5_block_diag(p["c_re"], False).astype(BF16)
    cdi = _s5_block_diag(p["c_im"], False).astype(BF16)
    a_r = ar.reshape(S5_TILES, 1, S5_TILE_STATE)
    a_i = ai.reshape(S5_TILES, 1, S5_TILE_STATE)
    b_spec, c_spec, a_spec = _s5_tile_specs()

    def fn(c, u, dskip, bdr, bdi, cdr, cdi, a_r, a_i):
        ub = u.astype(BF16)
        y = dskip * u
        for k in range(S5_TILE_STATE // LANES):
            sl = slice(k * LANES, (k + 1) * LANES)
            br = jnp.dot(ub, bdr[0][:, sl], preferred_element_type=F32)
            bi = jnp.dot(ub, bdi[0][:, sl], preferred_element_type=F32)
            sr, si = scan_phases(br, bi, a_r[0][:, sl], a_i[0][:, sl])
            y = y + jnp.dot(sr.astype(BF16), cdr[0][sl, :], preferred_element_type=F32)
            y = y - jnp.dot(si.astype(BF16), cdi[0][sl, :], preferred_element_type=F32)
        return y, _gelu(y)

    y, yg, *side_out = _seqwise_w(
        fn, [h], [p["d"]], [(bdr, b_spec), (bdi, b_spec), (cdr, c_spec), (cdi, c_spec), (a_r, a_spec), (a_i, a_spec)],
        [(d, F32), (d, F32)], [], ct=LANES, name="s5_core", phased=True, side=side)
    yg = yg.reshape(t, d)
    zv = mm(yg, p["w_val"], name="s5_zv")
    zg = mm(yg, p["w_gate"], name="s5_zg")

    def gate(x, zv, zg, bv, bg):
        return x + (zv + bv) * sigmoid(zg + bg)

    out = rowwise(gate, [x, zv, zg], [p["b_out"][:, :d], p["b_out"][:, d:]], [(d, F32)], name="s5_gate")[0]
    return out, (x, h, y, yg, zv, zg, (ar, ai, bdr, bdi, cdr, cdi, a_r, a_i)), side_out


def s5_bwd(dy, saved, g, p, side=()):
    x, h, y, yg, zv, zg, (ar, ai, bdr, bdi, cdr, cdi, a_r, a_i) = saved
    bsz, seq, d = h.shape
    t = bsz * seq

    def dgate(dy, zv, zg, bv, bg):
        val = zv + bv
        sg = sigmoid(zg + bg)
        dzv = dy * sg
        dzg = dy * val * sg * (1.0 - sg)
        return dzv, dzg, jnp.sum(dzv, axis=0, keepdims=True), jnp.sum(dzg, axis=0, keepdims=True)

    dzv, dzg, dbv, dbg = rowwise(dgate, [dy, zv, zg], [p["b_out"][:, :d], p["b_out"][:, d:]],
                                 [(d, BF16), (d, BF16)], [(1, d), (1, d)], name="s5_dgate")
    d_wv = mm(yg, dzv, ta=True, name="s5_dwv")
    d_wg = mm(yg, dzg, ta=True, name="s5_dwg")
    dyg = mm(dzv, p["w_val"], tb=True, name="s5_dyg1")
    dyg = mm(dzg, p["w_gate"], tb=True, acc=dyg, name="s5_dyg2").reshape(bsz, seq, d)
    b_spec, c_spec, a_spec = _s5_tile_specs()
    tr = (((0,), (0,)), ((), ()))
    nt = (((1,), (1,)), ((), ()))

    def fn(c, u, y, dyg, dskip, bdr, bdi, cdr, cdi, a_r, a_i):
        _, gelu_vjp = jax.vjp(_gelu, y)
        dyy = gelu_vjp(dyg)[0]
        ddskip = jnp.sum(dyy * u, axis=0, keepdims=True)
        du = dyy * dskip
        dyb = dyy.astype(BF16)
        ub = u.astype(BF16)
        dcr, dci, dbr, dbi, dar, dai = [], [], [], [], [], []
        for k in range(S5_TILE_STATE // LANES):
            sl = slice(k * LANES, (k + 1) * LANES)
            akr, aki = a_r[0][:, sl], a_i[0][:, sl]
            br = jnp.dot(ub, bdr[0][:, sl], preferred_element_type=F32)
            bi = jnp.dot(ub, bdi[0][:, sl], preferred_element_type=F32)
            sr, si = scan_phases(br, bi, akr, aki)
            dcr.append(lax.dot_general(sr.astype(BF16), dyb, tr, preferred_element_type=F32))
            dci.append(-lax.dot_general(si.astype(BF16), dyb, tr, preferred_element_type=F32))
            gr = lax.dot_general(dyb, cdr[0][sl, :], nt, preferred_element_type=F32)
            gi = -lax.dot_general(dyb, cdi[0][sl, :], nt, preferred_element_type=F32)
            gr, gi = scan_phases(gr, gi, akr, -aki, reverse=True)
            spr, spi = _prev_phased(sr), _prev_phased(si)
            dar.append(jnp.sum(gr * spr + gi * spi, axis=0, keepdims=True))
            dai.append(jnp.sum(gi * spr - gr * spi, axis=0, keepdims=True))
            grb, gib = gr.astype(BF16), gi.astype(BF16)
            dbr.append(lax.dot_general(ub, grb, tr, preferred_element_type=F32))
            dbi.append(lax.dot_general(ub, gib, tr, preferred_element_type=F32))
            du = du + lax.dot_general(grb, bdr[0][:, sl], nt, preferred_element_type=F32)
            du = du + lax.dot_general(gib, bdi[0][:, sl], nt, preferred_element_type=F32)
        return (du, ddskip, jnp.concatenate(dbr, axis=1)[None], jnp.concatenate(dbi, axis=1)[None],
                jnp.concatenate(dcr, axis=0)[None], jnp.concatenate(dci, axis=0)[None],
                jnp.concatenate(dar, axis=1)[None], jnp.concatenate(dai, axis=1)[None])

    du, ddskip, dbdr, dbdi, dcdr, dcdi, dar, dai, *side_out = _seqwise_w(
        fn, [h, y, dyg], [p["d"]], [(bdr, b_spec), (bdi, b_spec), (cdr, c_spec), (cdi, c_spec), (a_r, a_spec), (a_i, a_spec)],
        [(d, F32)], [(1, d)], ct=LANES, name="s5_dcore", phased=True,
        wreds=[((S5_TILES, LANES, S5_TILE_STATE), b_spec), ((S5_TILES, LANES, S5_TILE_STATE), b_spec),
               ((S5_TILES, S5_TILE_STATE, LANES), c_spec), ((S5_TILES, S5_TILE_STATE, LANES), c_spec),
               ((S5_TILES, 1, S5_TILE_STATE), a_spec), ((S5_TILES, 1, S5_TILE_STATE), a_spec)], side=side)
    d_bbr = _s5_diag_blocks(dbdr, S5_GROUP, S5_STATE).transpose(0, 2, 1)
    d_bbi = _s5_diag_blocks(dbdi, S5_GROUP, S5_STATE).transpose(0, 2, 1)
    d_cre = _s5_diag_blocks(dcdr, S5_STATE, S5_GROUP).transpose(0, 2, 1)
    d_cim = _s5_diag_blocks(dcdi, S5_STATE, S5_GROUP).transpose(0, 2, 1)
    d_lre, d_lim, d_ldt, d_bre, d_bim = s5_prep_bwd(
        p["lam_re"][..., None], p["lam_im"][..., None], p["log_dt"][:, None, None], p["b_re"], p["b_im"],
        dar.reshape(S5_GROUPS, S5_STATE, 1), dai.reshape(S5_GROUPS, S5_STATE, 1), d_bbr, d_bbi)
    dx, dg = rms_bwd(x, g, [du.reshape(t, d)], dy, "s5_drms")
    grads = dict(g=dg, lam_re=d_lre[..., 0], lam_im=d_lim[..., 0], log_dt=d_ldt[:, 0, 0], b_re=d_bre, b_im=d_bim,
                 c_re=d_cre, c_im=d_cim, d=ddskip, w_val=d_wv, w_gate=d_wg, b_out=jnp.concatenate([dbv, dbg], axis=1))
    return dx, grads, side_out


def _expm1_nonpos(x):
    u = jnp.exp(x)
    safe = (u - 1.0) * x / jnp.log(u)
    return jnp.where(u == 1.0, x, jnp.where(x < -20.0, -1.0, safe))


def _softplus(x):
    return jnp.maximum(x, 0.0) + jnp.log(1.0 + jnp.exp(-jnp.abs(x)))


def scan_ltv(a, b, reverse=False):
    shift = shift_up if reverse else shift_down
    seq = a.shape[0]
    d = 1
    while d < seq:
        b = b + a * shift(b, d)
        if 2 * d < seq:
            a = a * shift(a, d, 1.0)
        d *= 2
    return b


def _lru_gates(rec0, cw, cb, ba, bx, lam, wa, wx):
    rec = conv_fwd(rec0, cw, cb)
    recb = rec.astype(BF16)
    r = sigmoid(jnp.dot(recb, wa, preferred_element_type=F32) + ba)
    i = sigmoid(jnp.dot(recb, wx, preferred_element_type=F32) + bx)
    sp = _softplus(-lam)
    log_a = -LRU_C * r * sp
    a = jnp.exp(log_a)
    mult = jnp.sqrt(-_expm1_nonpos(2.0 * log_a))
    return rec, recb, r, i, sp, a, mult


def lru_fwd(x, g, p, bsz, side=()):
    t, d = x.shape
    seq = t // bsz
    xn = rms_fwd(x, g, BF16, "lru_rms")
    gb = mm(xn, p["w_gate"], name="lru_gb").reshape(bsz, seq, d)
    rec0 = mm(xn, p["w_rec"], name="lru_rec").reshape(bsz, seq, d)
    wspec = pl.BlockSpec((1, LRU_BLOCK, LRU_BLOCK), lambda c, b: (c, 0, 0))

    def fn(c, rec0, gb, cw, cb, ba, bx, lam, wa, wx):
        rec, _, _, i, _, a, mult = _lru_gates(rec0, cw, cb, ba, bx, lam, wa[0], wx[0])
        h = scan_ltv(a, mult * (i * rec))
        return _gelu(gb) * h

    y, *side_out = _seqwise_w(fn, [rec0, gb], [p["conv_w"], p["conv_b"], p["b_a"], p["b_x"], p["lam"]],
                              [(p["w_a"], wspec), (p["w_x"], wspec)], [(d, BF16)], [], ct=LRU_BLOCK, name="lru_core",
                              side=side)
    y = y.reshape(t, d)
    out = mm(y, p["w_out"], acc=x, name="lru_out")
    return out, (x, xn, gb, rec0, y), side_out


def lru_bwd(dy, saved, g, p, side=()):
    x, xn, gb, rec0, y = saved
    bsz, seq, d = gb.shape
    t = bsz * seq
    d_wout = mm(y, dy, ta=True, name="lru_dwout")
    dyy = mm(dy, p["w_out"], tb=True, name="lru_dy").reshape(bsz, seq, d)
    wspec = pl.BlockSpec((1, LRU_BLOCK, LRU_BLOCK), lambda c, b: (c, 0, 0))
    tr = (((0,), (0,)), ((), ()))
    nt = (((1,), (1,)), ((), ()))

    def fn(c, rec0, gb, dyy, cw, cb, ba, bx, lam, wa, wx):
        rec, recb, r, i, sp, a, mult = _lru_gates(rec0, cw, cb, ba, bx, lam, wa[0], wx[0])
        h = scan_ltv(a, mult * (i * rec))
        gg, gelu_vjp = jax.vjp(_gelu, gb)
        dgb = gelu_vjp(dyy * h)[0]
        gr = scan_ltv(shift_up(a, 1), dyy * gg, reverse=True)
        da = gr * shift_down(h, 1)
        dmult = gr * i * rec
        di = gr * mult * rec
        drec = gr * mult * i
        dla = da * a - dmult * (a * a) / mult
        dr = dla * (-LRU_C * sp)
        dlam = jnp.sum(dla * (-LRU_C * r), axis=0, keepdims=True) * (-sigmoid(-lam))
        dpa = dr * r * (1.0 - r)
        dpx = di * i * (1.0 - i)
        dpab, dpxb = dpa.astype(BF16), dpx.astype(BF16)
        dwa = lax.dot_general(recb, dpab, tr, preferred_element_type=F32)
        dwx = lax.dot_general(recb, dpxb, tr, preferred_element_type=F32)
        drec = drec + lax.dot_general(dpab, wa[0], nt, preferred_element_type=F32)
        drec = drec + lax.dot_general(dpxb, wx[0], nt, preferred_element_type=F32)
        drec0, dcw, dcb = conv_bwd(rec0, cw, drec)
        return (dgb, drec0, dcw, dcb, jnp.sum(dpa, axis=0, keepdims=True), jnp.sum(dpx, axis=0, keepdims=True), dlam,
                dwa[None], dwx[None])

    kw = p["conv_w"].shape[0]
    dgb, drec0, dcw, dcb, dba, dbx, dlam, dwa, dwx, *side_out = _seqwise_w(
        fn, [rec0, gb, dyy], [p["conv_w"], p["conv_b"], p["b_a"], p["b_x"], p["lam"]],
        [(p["w_a"], wspec), (p["w_x"], wspec)], [(d, BF16), (d, BF16)],
        [(kw, d), (1, d), (1, d), (1, d), (1, d)], ct=LRU_BLOCK, name="lru_dcore",
        wreds=[(p["w_a"].shape, wspec), (p["w_x"].shape, wspec)], side=side)
    dgb = dgb.reshape(t, d)
    drec0 = drec0.reshape(t, d)
    d_wgate = mm(xn, dgb, ta=True, name="lru_dwgate")
    d_wrec = mm(xn, drec0, ta=True, name="lru_dwrec")
    dxn = mm(dgb, p["w_gate"], tb=True, name="lru_dxn1")
    dxn = mm(drec0, p["w_rec"], tb=True, acc=dxn, name="lru_dxn2")
    dx, dg = rms_bwd(x, g, [dxn], dy, "lru_drms")
    grads = dict(g=dg, w_gate=d_wgate, w_rec=d_wrec, conv_w=dcw, conv_b=dcb, w_a=dwa, b_a=dba, w_x=dwx, b_x=dbx,
                 lam=dlam, w_out=d_wout)
    return dx, grads, side_out


_NT = (((1,), (1,)), ((), ()))
_TN = (((0,), (0,)), ((), ()))


def _head_norm(x, g):
    lo = lax.broadcasted_iota(jnp.int32, x.shape, 1) < SB_HEAD_DIM
    x2 = x * x
    s_lo = jnp.sum(jnp.where(lo, x2, 0.0), axis=-1, keepdims=True)
    s_hi = jnp.sum(jnp.where(lo, 0.0, x2), axis=-1, keepdims=True)
    ms = jnp.where(lo, s_lo, s_hi) * (1.0 / SB_HEAD_DIM)
    return x * lax.rsqrt(ms + EPS) * g


def _log_sigmoid(z):
    return jnp.minimum(z, 0.0) - jnp.log(1.0 + jnp.exp(-jnp.abs(z)))


def _dot_split(x, m):
    hi = x.astype(BF16)
    lo = (x - hi.astype(F32)).astype(BF16)
    return jnp.dot(hi, m, preferred_element_type=F32) + jnp.dot(lo, m, preferred_element_type=F32)


def _tri(cmp, n):
    r = lax.broadcasted_iota(jnp.int32, (n, n), 0)
    c = lax.broadcasted_iota(jnp.int32, (n, n), 1)
    return cmp(r, c)


def sb_attn_fwd(q, k, v, qg, kg, side=()):
    bsz, seq, d = q.shape
    blk = min(ATT_BLOCK, seq)
    rows_q = ATT_Q_ROWS if seq % ATT_Q_ROWS == 0 else blk
    per_q = rows_q // blk
    nq = seq // rows_q
    scale = 1.0 / math.sqrt(SB_HEAD_DIM)
    side, exchange, side_sems, side_shapes = _side_plan(side)
    n_x = len(side)

    def body(*refs):
        q_ref, k_ref, v_ref, qg_ref, kg_ref = refs[:5]
        o_ref, tot_ref = refs[5 + n_x:7 + n_x]
        qn, kn, vb = refs[7 + 2 * n_x:10 + 2 * n_x]
        if side:
            start, wait = exchange(refs[5:5 + n_x], refs[7 + n_x:7 + 2 * n_x], *refs[10 + 2 * n_x:])
            pl.when(jnp.logical_and(pl.program_id(0) == 0, pl.program_id(1) == 0))(start)
        qn[...] = _head_norm(q_ref[...], qg_ref[...]).astype(BF16)
        kn[...] = _head_norm(k_ref[...], kg_ref[...]).astype(BF16)
        vb[...] = v_ref[...].astype(BF16)
        lane_lo = lax.broadcasted_iota(jnp.int32, (rows_q, LANES), 1) < SB_HEAD_DIM
        q_row = lax.broadcasted_iota(jnp.int32, (rows_q, blk), 0)
        k_col = lax.broadcasted_iota(jnp.int32, (rows_q, blk), 1)
        causal = [k_col + dd * blk < q_row for dd in range(per_q)]
        upper = _tri(lambda r, c: r > c, blk).astype(BF16)

        def q_block(qi, _):
            rows = pl.ds(pl.multiple_of(qi * rows_q, rows_q), rows_q)
            q_all = qn[rows, :]
            zero = jnp.zeros((), BF16)
            qbs = (jnp.where(lane_lo, q_all, zero), jnp.where(lane_lo, zero, q_all))

            def block(j, state, mask=None):
                cols = pl.ds(pl.multiple_of(j * blk, blk), blk)
                kb, vv = kn[cols, :], vb[cols, :]
                zs = [lax.dot_general(qb, kb, _NT, preferred_element_type=F32) * scale for qb in qbs]
                lss = [_log_sigmoid(z) for z in zs]
                lgs = [ls - z for ls, z in zip(lss, zs)]
                if mask is not None:
                    lgs = [jnp.where(mask, lg, 0.0) for lg in lgs]
                later = [_dot_split(lg, upper) for lg in lgs]
                atts = [jnp.exp(ls + carry + cs) for ls, (carry, _), cs in zip(lss, state, later)]
                if mask is not None:
                    atts = [jnp.where(mask, att, 0.0) for att in atts]
                outs = [jnp.dot(att.astype(BF16), vv, preferred_element_type=F32) for att in atts]
                return tuple((carry + jnp.sum(lg, axis=1, keepdims=True), acc + out)
                             for (carry, acc), lg, out in zip(state, lgs, outs))

            init = (jnp.zeros((rows_q, 1), F32), jnp.zeros((rows_q, LANES), F32))
            state = (init, init)
            first = qi * per_q
            for dd in reversed(range(per_q)):
                state = block(first + dd, state, causal[dd])
            state = lax.fori_loop(0, first, lambda jj, s: block(first - 1 - jj, s), state)
            (carry0, acc0), (carry1, acc1) = state
            o_ref[rows, :] = jnp.where(lane_lo, acc0, acc1).astype(o_ref.dtype)
            tot_ref[rows, :] = jnp.where(lane_lo, carry0, carry1)
            return 0

        lax.fori_loop(0, nq, q_block, 0)
        if side:
            pl.when(jnp.logical_and(pl.program_id(0) == bsz - 1, pl.program_id(1) == d // LANES - 1))(wait)

    spec = pl.BlockSpec((None, seq, LANES), lambda b, c: (b, 0, c))
    gspec = pl.BlockSpec((1, LANES), lambda b, c: (0, 0))
    return pl.pallas_call(
        body, name="sb_attn_fwd", grid=(bsz, d // LANES), in_specs=[spec, spec, spec, gspec, gspec] + [_HBM] * n_x,
        out_specs=[spec, spec] + [_HBM] * n_x,
        out_shape=[jax.ShapeDtypeStruct((bsz, seq, d), BF16), jax.ShapeDtypeStruct((bsz, seq, d), F32)] + side_shapes,
        scratch_shapes=[pltpu.VMEM((seq, LANES), BF16)] * 3 + side_sems,
        compiler_params=_params("arbitrary", "arbitrary"),
    )(q, k, v, qg, kg, *side)


def sb_attn_bwd(q, k, v, qg, kg, tot, do, side=()):
    bsz, seq, d = q.shape
    blk = min(ATT_BLOCK, seq)
    rows_q = ATT_Q_ROWS if seq % ATT_Q_ROWS == 0 else blk
    per_q = rows_q // blk
    nq = seq // rows_q
    scale = 1.0 / math.sqrt(SB_HEAD_DIM)
    side, exchange, side_sems, side_shapes = _side_plan(side)
    n_x = len(side)
    n_in, n_out, n_scr = 7, 5, 6

    def body(*refs):
        q_ref, k_ref, v_ref, qg_ref, kg_ref, tot_ref, do_ref = refs[:n_in]
        dq_ref, dk_ref, dv_ref, dqg_ref, dkg_ref = refs[n_in + n_x:n_in + n_x + n_out]
        qn, kn, vb, dqn, dkn, dvv = refs[n_in + 2 * n_x + n_out:n_in + 2 * n_x + n_out + n_scr]
        if side:
            start, wait = exchange(refs[n_in:n_in + n_x], refs[n_in + n_x + n_out:n_in + 2 * n_x + n_out],
                                         *refs[n_in + 2 * n_x + n_out + n_scr:])
            pl.when(jnp.logical_and(pl.program_id(0) == 0, pl.program_id(1) == 0))(start)
        qn[...] = _head_norm(q_ref[...], qg_ref[...]).astype(BF16)
        kn[...] = _head_norm(k_ref[...], kg_ref[...]).astype(BF16)
        vb[...] = v_ref[...].astype(BF16)
        dkn[...] = jnp.zeros_like(dkn)
        dvv[...] = jnp.zeros_like(dvv)
        lane_lo = lax.broadcasted_iota(jnp.int32, (rows_q, LANES), 1) < SB_HEAD_DIM
        q_row = lax.broadcasted_iota(jnp.int32, (rows_q, blk), 0)
        k_col = lax.broadcasted_iota(jnp.int32, (rows_q, blk), 1)
        causal = [k_col + dd * blk < q_row for dd in range(per_q)]
        upto = _tri(lambda r, c: r <= c, blk).astype(BF16)
        before = _tri(lambda r, c: r < c, blk).astype(BF16)

        def q_block(qi, _):
            rows = pl.ds(pl.multiple_of(qi * rows_q, rows_q), rows_q)
            zero = jnp.zeros((), BF16)
            q_all, do_all, tot_all = qn[rows, :], do_ref[rows, :].astype(BF16), tot_ref[rows, :]
            heads = []
            for hm in (lane_lo, jnp.logical_not(lane_lo)):
                heads.append((jnp.where(hm, q_all, zero), jnp.where(hm, do_all, zero),
                              jnp.max(jnp.where(hm, tot_all, -jnp.inf), axis=1, keepdims=True)))

            def block(j, state, mask=None):
                cols = pl.ds(pl.multiple_of(j * blk, blk), blk)
                kb, vv = kn[cols, :], vb[cols, :]
                two = range(2)
                zs = [lax.dot_general(heads[h][0], kb, _NT, preferred_element_type=F32) * scale for h in two]
                datts = [lax.dot_general(heads[h][1], vv, _NT, preferred_element_type=F32) for h in two]
                lss = [_log_sigmoid(z) for z in zs]
                lgs = [ls - z for ls, z in zip(lss, zs)]
                if mask is not None:
                    lgs = [jnp.where(mask, lg, 0.0) for lg in lgs]
                sofar = [_dot_split(lg, upto) for lg in lgs]
                atts = [jnp.exp(lss[h] + heads[h][2] - state[h][0] - sofar[h]) for h in two]
                if mask is not None:
                    atts = [jnp.where(mask, att, 0.0) for att in atts]
                es = [att * datt for att, datt in zip(atts, datts)]
                earlier = [_dot_split(e, before) for e in es]
                dlgs = [state[h][1] + earlier[h] for h in two]
                if mask is not None:
                    dlgs = [jnp.where(mask, dlg, 0.0) for dlg in dlgs]
                betas = [jnp.exp(ls) for ls in lss]
                dzbs = [((es[h] * (1.0 - betas[h]) - dlgs[h] * betas[h]) * scale).astype(BF16) for h in two]
                dvv[cols, :] += sum(lax.dot_general(atts[h].astype(BF16), heads[h][1], _TN, preferred_element_type=F32)
                                    for h in two)
                dkn[cols, :] += sum(lax.dot_general(dzbs[h], heads[h][0], _TN, preferred_element_type=F32) for h in two)
                dqs = [jnp.dot(dzb, kb, preferred_element_type=F32) for dzb in dzbs]
                return tuple((state[h][0] + jnp.sum(lgs[h], axis=1, keepdims=True),
                              state[h][1] + jnp.sum(es[h], axis=1, keepdims=True), state[h][2] + dqs[h]) for h in two)

            col0 = jnp.zeros((rows_q, 1), F32)
            init = (col0, col0, jnp.zeros((rows_q, LANES), F32))
            first = qi * per_q
            state = lax.fori_loop(0, first, lambda j, s: block(j, s), (init, init))
            for dd in range(per_q):
                state = block(first + dd, state, causal[dd])
            dqn[rows, :] = jnp.where(lane_lo, state[0][2], state[1][2])
            return 0

        lax.fori_loop(0, nq, q_block, 0)

        def fold(x):
            return x + pltpu.roll(x, SB_HEAD_DIM, 1)

        _, q_vjp = jax.vjp(_head_norm, q_ref[...], qg_ref[...])
        dq, dqg = q_vjp(dqn[...])
        _, k_vjp = jax.vjp(_head_norm, k_ref[...], kg_ref[...])
        dk, dkg = k_vjp(dkn[...])
        dq_ref[...] = dq.astype(dq_ref.dtype)
        dk_ref[...] = dk.astype(dk_ref.dtype)
        dv_ref[...] = dvv[...].astype(dv_ref.dtype)
        first = jnp.logical_and(pl.program_id(0) == 0, pl.program_id(1) == 0)

        @pl.when(first)
        def _():
            dqg_ref[...] = fold(dqg)
            dkg_ref[...] = fold(dkg)

        @pl.when(jnp.logical_not(first))
        def _():
            dqg_ref[...] += fold(dqg)
            dkg_ref[...] += fold(dkg)

        if side:
            pl.when(jnp.logical_and(pl.program_id(0) == bsz - 1, pl.program_id(1) == d // LANES - 1))(wait)

    spec = pl.BlockSpec((None, seq, LANES), lambda b, c: (b, 0, c))
    gspec = pl.BlockSpec((1, LANES), lambda b, c: (0, 0))
    act = jax.ShapeDtypeStruct((bsz, seq, d), BF16)
    gain = jax.ShapeDtypeStruct((1, LANES), F32)
    return pl.pallas_call(
        body, name="sb_attn_bwd", grid=(bsz, d // LANES),
        in_specs=[spec, spec, spec, gspec, gspec, spec, spec] + [_HBM] * n_x,
        out_specs=[spec, spec, spec, gspec, gspec] + [_HBM] * n_x,
        out_shape=[act, act, act, gain, gain] + side_shapes,
        scratch_shapes=[pltpu.VMEM((seq, LANES), BF16)] * 3 + [pltpu.VMEM((seq, LANES), F32)] * 3 + side_sems,
        compiler_params=_params("arbitrary", "arbitrary"),
    )(q, k, v, qg, kg, tot, do, *side)


def sb_fwd(x, g, p, bsz, side=()):
    t, d = x.shape
    seq = t // bsz
    xn = rms_fwd(x, g, BF16, "sb_rms")
    q = mm(xn, p["w_q"], name="sb_q").reshape(bsz, seq, d)
    k = mm(xn, p["w_k"], name="sb_k").reshape(bsz, seq, d)
    v = mm(xn, p["w_v"], name="sb_v").reshape(bsz, seq, d)
    qg = jnp.tile(p["q_g"], (1, 2))
    kg = jnp.tile(p["k_g"], (1, 2))
    o, tot, *side_out = sb_attn_fwd(q, k, v, qg, kg, side)
    o = o.reshape(t, d)
    out = mm(o, p["w_o"], acc=x, name="sb_out")
    return out, (x, xn, q, k, v, o, tot), side_out


def sb_bwd(dy, saved, g, p, side=()):
    x, xn, q, k, v, o, tot = saved
    bsz, seq, d = q.shape
    t = bsz * seq
    d_wo = mm(o, dy, ta=True, name="sb_dwo")
    do = mm(dy, p["w_o"], tb=True, out_dtype=BF16, name="sb_do").reshape(bsz, seq, d)
    qg = jnp.tile(p["q_g"], (1, 2))
    kg = jnp.tile(p["k_g"], (1, 2))
    dq, dk, dv, dqg, dkg, *side_out = sb_attn_bwd(q, k, v, qg, kg, tot, do, side)
    dq, dk, dv = dq.reshape(t, d), dk.reshape(t, d), dv.reshape(t, d)
    d_wq = mm(xn, dq, ta=True, name="sb_dwq")
    d_wk = mm(xn, dk, ta=True, name="sb_dwk")
    d_wv = mm(xn, dv, ta=True, name="sb_dwv")
    dxn = mm(dq, p["w_q"], tb=True, name="sb_dxn1")
    dxn = mm(dk, p["w_k"], tb=True, acc=dxn, name="sb_dxn2")
    dxn = mm(dv, p["w_v"], tb=True, acc=dxn, name="sb_dxn3")
    dx, dg = rms_bwd(x, g, [dxn], dy, "sb_drms")
    grads = dict(g=dg, w_q=d_wq, w_k=d_wk, w_v=d_wv, w_o=d_wo, q_g=dqg[:, :SB_HEAD_DIM], k_g=dkg[:, :SB_HEAD_DIM])
    return dx, grads, side_out


def loss_head(y, target):
    d = y.shape[1]

    def fn(y, tgt):
        err = y - tgt
        return err * (1.0 / d), jnp.sum(err * err, axis=0, keepdims=True)

    dy, sq = rowwise(fn, [y, target], [], [(d, F32)], [(1, d)], name="loss_head")
    return sq, dy


def adamw(parts, w, m, v, name):
    n, r, c = parts.shape
    tr = _pick(r, max(16, (1 << 20) // (c * n)), 16 if parts.dtype == BF16 else 8)

    def body(p_ref, w_ref, m_ref, v_ref, g_ref, d_ref, nm_ref, nv_ref):
        g = p_ref[0].astype(F32)
        for i in range(1, n):
            g = g + p_ref[i].astype(F32)
        nm = ADAM_B1 * m_ref[...] + (1.0 - ADAM_B1) * g
        nv = ADAM_B2 * v_ref[...] + (1.0 - ADAM_B2) * (g * g)
        m_hat = nm / (1.0 - ADAM_B1 ** ADAM_STEP)
        v_hat = nv / (1.0 - ADAM_B2 ** ADAM_STEP)
        g_ref[...] = g
        d_ref[...] = -ADAM_LR * (m_hat / (jnp.sqrt(v_hat) + ADAM_EPS) + ADAM_WD * w_ref[...])
        nm_ref[...] = nm
        nv_ref[...] = nv

    spec = pl.BlockSpec((tr, c), lambda i: (i, 0))
    return pl.pallas_call(
        body, name=name, grid=(r // tr,), in_specs=[pl.BlockSpec((n, tr, c), lambda i: (0, i, 0)), spec, spec, spec],
        out_specs=[spec] * 4, out_shape=[jax.ShapeDtypeStruct((r, c), F32)] * 4,
        compiler_params=_params("parallel"),
    )(parts, w, m, v)


_HBM = pl.BlockSpec(memory_space=pltpu.HBM)


def _mesh_pos():
    return lax.axis_index("x"), lax.axis_index("y"), lax.axis_index("c")


def all_gather(shards, name):
    n = len(shards)

    def body(*refs):
        start, finish = _gather_exchange(refs[:n], refs[n:2 * n], *refs[2 * n:])
        start()
        finish()

    return pl.pallas_call(
        body, name=name, in_specs=[_HBM] * n, out_specs=[_HBM] * n,
        out_shape=[jax.ShapeDtypeStruct((N_DEV, *s.shape), s.dtype) for s in shards],
        scratch_shapes=_gather_exchange_sems(n),
    )(*shards)


def _gather_exchange(ins, outs, send_sems, recv_sems, local_sems):
    n = len(ins)
    x, y, c = _mesh_pos()
    me, sibling = (x, y, c), (x, y, 1 - c)
    chips = [(1 - x, y), (x, 1 - y), (1 - x, 1 - y)]

    def copy(a, k, block, to, src=None):
        px, py, pc = block
        dst = outs[a].at[4 * px + 2 * py + pc]
        return pltpu.make_async_remote_copy(
            src_ref=dst if src is None else src, dst_ref=dst, send_sem=send_sems.at[a, k],
            recv_sem=recv_sems.at[a, k], device_id=to, device_id_type=MESH)

    mine = [pltpu.make_async_copy(ins[a], outs[a].at[4 * x + 2 * y + c], local_sems.at[a]) for a in range(n)]
    first = []
    for a in range(n):
        first.append(copy(a, 0, me, sibling, src=ins[a]))
        first += [copy(a, 1 + j, me, (*chip, c), src=ins[a]) for j, chip in enumerate(chips)]

    def start():
        for cp in mine + first:
            cp.start()

    def finish():
        passed = []
        for a in range(n):
            for j, chip in enumerate(chips):
                copy(a, 1 + j, (*chip, c), me).wait_recv()
                cp = copy(a, 4 + j, (*chip, c), sibling)
                cp.start()
                passed.append(cp)
        for a in range(n):
            copy(a, 0, sibling, me).wait_recv()
            for j, chip in enumerate(chips):
                copy(a, 4 + j, (*chip, 1 - c), me).wait_recv()
        for cp in first + passed:
            cp.wait_send()
        for cp in mine:
            cp.wait()

    return start, finish


def _gather_exchange_sems(n):
    return [pltpu.SemaphoreType.DMA((n, 7)), pltpu.SemaphoreType.DMA((n, 7)), pltpu.SemaphoreType.DMA((n,))]


def exchange_cores(parts, name):
    n = len(parts)

    def body(*refs):
        ins, outs = refs[:n], refs[n:2 * n]
        send_sems, recv_sems = refs[2 * n:]
        x, y, c = _mesh_pos()
        copies = []
        for a in range(n):
            for p in range(4):
                copies.append(pltpu.make_async_remote_copy(
                    src_ref=ins[a].at[p, 1 - c], dst_ref=outs[a].at[p], send_sem=send_sems.at[a, p],
                    recv_sem=recv_sems.at[a, p], device_id=(x, y, 1 - c), device_id_type=MESH))
        for cp in copies:
            cp.start()
        for cp in copies:
            cp.wait()

    return pl.pallas_call(
        body, name=name, in_specs=[_HBM] * n, out_specs=[_HBM] * n,
        out_shape=[jax.ShapeDtypeStruct((4, *s.shape[2:]), s.dtype) for s in parts],
        scratch_shapes=[pltpu.SemaphoreType.DMA((n, 4)), pltpu.SemaphoreType.DMA((n, 4))],
    )(*parts)


def _chip_exchange(ins, outs, send_sems, recv_sems, local_sems):
    x, y, c = _mesh_pos()
    mine = 2 * x + y
    copies = []
    for a in range(len(ins)):
        copies.append(pltpu.make_async_copy(ins[a].at[mine], outs[a].at[mine], local_sems.at[a]))
        for r in range(1, 4):
            qx = 1 - x if r & 2 else x
            qy = 1 - y if r & 1 else y
            copies.append(pltpu.make_async_remote_copy(
                src_ref=ins[a].at[2 * qx + qy], dst_ref=outs[a].at[mine], send_sem=send_sems.at[a, r - 1],
                recv_sem=recv_sems.at[a, r - 1], device_id=(qx, qy, c), device_id_type=MESH))

    def start():
        for cp in copies:
            cp.start()

    def wait():
        for cp in copies:
            cp.wait()

    return start, wait


def _chip_exchange_sems(n):
    return [pltpu.SemaphoreType.DMA((n, 3)), pltpu.SemaphoreType.DMA((n, 3)), pltpu.SemaphoreType.DMA((n,))]


def _side_plan(side):
    if not side:
        return (), None, [], []
    kind, arrays = side
    n = len(arrays)
    if kind == "chips":
        return arrays, _chip_exchange, _chip_exchange_sems(n), [jax.ShapeDtypeStruct(a.shape, a.dtype) for a in arrays]
    assert kind == "gather", kind
    shapes = [jax.ShapeDtypeStruct((N_DEV, *a.shape), a.dtype) for a in arrays]
    return arrays, _gather_exchange, _gather_exchange_sems(n), shapes


def exchange_chips(parts, name):
    n = len(parts)

    def body(*refs):
        start, wait = _chip_exchange(refs[:n], refs[n:2 * n], *refs[2 * n:])
        start()
        wait()

    return pl.pallas_call(
        body, name=name, in_specs=[_HBM] * n, out_specs=[_HBM] * n,
        out_shape=[jax.ShapeDtypeStruct(s.shape, s.dtype) for s in parts],
        scratch_shapes=_chip_exchange_sems(n),
    )(*parts)


def add_own(core, parts, recv, out_dtype, name):
    _, _, r, c = parts.shape
    tr = _pick(r, max(16, (1 << 19) // c), 16)

    def body(core_ref, p_ref, r_ref, o_ref):
        o_ref[...] = (p_ref[...] + r_ref[...]).astype(out_dtype)

    grid_spec = pltpu.PrefetchScalarGridSpec(
        num_scalar_prefetch=1, grid=(4, r // tr),
        in_specs=[pl.BlockSpec((None, None, tr, c), lambda p, i, core_ref: (p, core_ref[0], i, 0)),
                  pl.BlockSpec((None, tr, c), lambda p, i, core_ref: (p, i, 0))],
        out_specs=pl.BlockSpec((None, tr, c), lambda p, i, core_ref: (p, i, 0)))
    return pl.pallas_call(
        body, name=name, grid_spec=grid_spec, out_shape=jax.ShapeDtypeStruct((4, r, c), out_dtype),
        compiler_params=_params("parallel", "parallel"),
    )(core, parts, recv)


PARAMS = {
    "norm_mix_g": ((4, 1024), None), "norm_ffn_g": ((4, 1024), None),
    "pool_w": ((1, 4, 256, 256), 2), "pool_b": ((1, 1024), None), "pool_scale": ((1, 1024), None),
    "s5_lam_re": ((1, 64, 64), None), "s5_lam_im": ((1, 64, 64), None), "s5_log_dt": ((1, 64), None),
    "s5_b_re": ((1, 64, 64, 16), None), "s5_b_im": ((1, 64, 64, 16), None),
    "s5_c_re": ((1, 64, 16, 64), None), "s5_c_im": ((1, 64, 16, 64), None),
    "s5_d": ((1, 1024), 1), "s5_w_out": ((1, 1024, 2048), 2), "s5_b_out": ((1, 2048), 1),
    "lru_w_in": ((1, 1024, 2048), 2), "lru_conv_w": ((1, 4, 1024), 2), "lru_conv_b": ((1, 1024), 1),
    "lru_w_a": ((1, 4, 256, 256), 2), "lru_b_a": ((1, 1024), 1), "lru_w_x": ((1, 4, 256, 256), 2),
    "lru_b_x": ((1, 1024), 1), "lru_lam": ((1, 1024), 1), "lru_w_out": ((1, 1024, 1024), 1),
    "sb_w_qkv": ((1, 1024, 3072), 2), "sb_q_g": ((1, 64), None), "sb_k_g": ((1, 64), None),
    "sb_w_o": ((1, 1024, 1024), 1),
    "ffn_w_in": ((4, 1024, 5632), 2), "ffn_conv_w": ((4, 3, 5632), 2), "ffn_conv_b": ((4, 5632), None),
    "ffn_w_out": ((4, 2816, 1024), 1),
}
NAMES = list(PARAMS)
BIG = ["s5_w_out", "lru_w_in", "lru_w_out", "sb_w_qkv", "sb_w_o", "ffn_w_in", "ffn_w_out"]
SMALL = [n for n in NAMES if PARAMS[n][1] is not None and n not in BIG]
REPL = [n for n in NAMES if PARAMS[n][1] is None]


def _local_shape(name):
    shape, ax = PARAMS[name]
    return tuple(s // N_DEV if i == ax else s for i, s in enumerate(shape))


def _to_natural(gathered, name):
    shape, ax = PARAMS[name]
    return jnp.moveaxis(gathered, 0, ax).reshape(shape)


def _to_shards(nat, name):
    shape, ax = PARAMS[name]
    split = shape[:ax] + (N_DEV, shape[ax] // N_DEV) + shape[ax + 1:]
    return jnp.moveaxis(nat.reshape(split), ax, 0)


def _pack(arrays, lead=()):
    flat = jnp.concatenate([a.reshape(*lead, -1) for a in arrays], axis=-1)
    size = flat.shape[-1]
    rows = -(-size // (8 * LANES)) * 8
    flat = jnp.pad(flat, [(0, 0)] * len(lead) + [(0, rows * LANES - size)])
    return flat.reshape(*lead, rows, LANES)


def _unpack(packed, shapes, lead=()):
    flat = packed.reshape(*lead, -1)
    out, off = [], 0
    for s in shapes:
        size = math.prod(s)
        out.append(flat[..., off:off + size].reshape(*lead, *s))
        off += size
    return out


def kernel(x, norm_mix_g, norm_ffn_g, pool_w, pool_b, pool_scale, s5_lam_re, s5_lam_im, s5_log_dt, s5_b_re, s5_b_im, s5_c_re, s5_c_im, s5_d, s5_w_out, s5_b_out, lru_w_in, lru_conv_w, lru_conv_b, lru_w_a, lru_b_a, lru_w_x, lru_b_x, lru_lam, lru_w_out, sb_w_qkv, sb_q_g, sb_k_g, sb_w_o, ffn_w_in, ffn_conv_w, ffn_conv_b, ffn_w_out, loss_target, m_norm_mix_g, m_norm_ffn_g, m_pool_w, m_pool_b, m_pool_scale, m_s5_lam_re, m_s5_lam_im, m_s5_log_dt, m_s5_b_re, m_s5_b_im, m_s5_c_re, m_s5_c_im, m_s5_d, m_s5_w_out, m_s5_b_out, m_lru_w_in, m_lru_conv_w, m_lru_conv_b, m_lru_w_a, m_lru_b_a, m_lru_w_x, m_lru_b_x, m_lru_lam, m_lru_w_out, m_sb_w_qkv, m_sb_q_g, m_sb_k_g, m_sb_w_o, m_ffn_w_in, m_ffn_conv_w, m_ffn_conv_b, m_ffn_w_out, v_norm_mix_g, v_norm_ffn_g, v_pool_w, v_pool_b, v_pool_scale, v_s5_lam_re, v_s5_lam_im, v_s5_log_dt, v_s5_b_re, v_s5_b_im, v_s5_c_re, v_s5_c_im, v_s5_d, v_s5_w_out, v_s5_b_out, v_lru_w_in, v_lru_conv_w, v_lru_conv_b, v_lru_w_a, v_lru_b_a, v_lru_w_x, v_lru_b_x, v_lru_lam, v_lru_w_out, v_sb_w_qkv, v_sb_q_g, v_sb_k_g, v_sb_w_o, v_ffn_w_in, v_ffn_conv_w, v_ffn_conv_b, v_ffn_w_out):
    w = dict(zip(NAMES, (norm_mix_g, norm_ffn_g, pool_w, pool_b, pool_scale, s5_lam_re, s5_lam_im, s5_log_dt, s5_b_re, s5_b_im, s5_c_re, s5_c_im, s5_d, s5_w_out, s5_b_out, lru_w_in, lru_conv_w, lru_conv_b, lru_w_a, lru_b_a, lru_w_x, lru_b_x, lru_lam, lru_w_out, sb_w_qkv, sb_q_g, sb_k_g, sb_w_o, ffn_w_in, ffn_conv_w, ffn_conv_b, ffn_w_out)))
    m = dict(zip(NAMES, (m_norm_mix_g, m_norm_ffn_g, m_pool_w, m_pool_b, m_pool_scale, m_s5_lam_re, m_s5_lam_im, m_s5_log_dt, m_s5_b_re, m_s5_b_im, m_s5_c_re, m_s5_c_im, m_s5_d, m_s5_w_out, m_s5_b_out, m_lru_w_in, m_lru_conv_w, m_lru_conv_b, m_lru_w_a, m_lru_b_a, m_lru_w_x, m_lru_b_x, m_lru_lam, m_lru_w_out, m_sb_w_qkv, m_sb_q_g, m_sb_k_g, m_sb_w_o, m_ffn_w_in, m_ffn_conv_w, m_ffn_conv_b, m_ffn_w_out)))
    v = dict(zip(NAMES, (v_norm_mix_g, v_norm_ffn_g, v_pool_w, v_pool_b, v_pool_scale, v_s5_lam_re, v_s5_lam_im, v_s5_log_dt, v_s5_b_re, v_s5_b_im, v_s5_c_re, v_s5_c_im, v_s5_d, v_s5_w_out, v_s5_b_out, v_lru_w_in, v_lru_conv_w, v_lru_conv_b, v_lru_w_a, v_lru_b_a, v_lru_w_x, v_lru_b_x, v_lru_lam, v_lru_w_out, v_sb_w_qkv, v_sb_q_g, v_sb_k_g, v_sb_w_o, v_ffn_w_in, v_ffn_conv_w, v_ffn_conv_b, v_ffn_w_out)))
    return train_step(x, loss_target, w, m, v)


def train_step(x, loss_target, w, m, v):
    bsz, seq, d = x.shape
    t = bsz * seq
    n_layers = PARAMS["norm_mix_g"][0][0]

    small_shapes = [_local_shape(n) for n in SMALL]
    mixers = ["s5_w_out", "lru_w_in", "lru_w_out", "sb_w_qkv", "sb_w_o"]

    def ffn_shards(li):
        return [w["ffn_w_in"][li].astype(BF16), w["ffn_w_out"][li].astype(BF16)]

    def whole2d(gathered, name):
        _, r, c = gathered.shape
        if PARAMS[name][1] == 2:
            return gathered.transpose(1, 0, 2).reshape(r, N_DEV * c)
        return gathered.reshape(N_DEV * r, c)

    def ffn_whole(gathered):
        return whole2d(gathered[0], "ffn_w_in"), whole2d(gathered[1], "ffn_w_out")

    gathered = all_gather([w[n][0].astype(BF16) for n in mixers] + ffn_shards(0) + [_pack([w[n] for n in SMALL])],
                          "gather_weights")
    nat = {n: whole2d(g, n) for n, g in zip(mixers, gathered)}
    ffn_w = {0: ffn_whole(gathered[len(mixers):len(mixers) + 2])}
    for n, g in zip(SMALL, _unpack(gathered[-1], small_shapes, lead=(N_DEV,))):
        nat[n] = _to_natural(g, n)
    for n in REPL:
        nat[n] = w[n]

    hid = FFN_HIDDEN
    pool_p = (nat["pool_w"][0].astype(BF16), nat["pool_b"], nat["pool_scale"])
    s5_p = dict(lam_re=nat["s5_lam_re"][0], lam_im=nat["s5_lam_im"][0], log_dt=nat["s5_log_dt"][0],
                b_re=nat["s5_b_re"][0], b_im=nat["s5_b_im"][0], c_re=nat["s5_c_re"][0], c_im=nat["s5_c_im"][0],
                d=nat["s5_d"], w_val=nat["s5_w_out"][:, :d], w_gate=nat["s5_w_out"][:, d:], b_out=nat["s5_b_out"])
    lru_p = dict(w_gate=nat["lru_w_in"][:, :d], w_rec=nat["lru_w_in"][:, d:], conv_w=nat["lru_conv_w"][0],
                 conv_b=nat["lru_conv_b"], b_a=nat["lru_b_a"], b_x=nat["lru_b_x"], lam=nat["lru_lam"],
                 w_a=nat["lru_w_a"][0].astype(BF16), w_x=nat["lru_w_x"][0].astype(BF16), w_out=nat["lru_w_out"])
    sb_p = dict(w_q=nat["sb_w_qkv"][:, :d], w_k=nat["sb_w_qkv"][:, d:2 * d], w_v=nat["sb_w_qkv"][:, 2 * d:],
                w_o=nat["sb_w_o"], q_g=nat["sb_q_g"], k_g=nat["sb_k_g"])

    def ffn_p(li):
        w_in, w_out = ffn_w[li]
        return (w_in[:, :hid], w_in[:, hid:], nat["ffn_conv_w"][li], nat["ffn_conv_b"][li:li + 1], w_out)

    def gain(name, li):
        return nat[name][li:li + 1]

    h = x.reshape(t, d)
    h, pool_saved = pool_fwd(h, gain("norm_mix_g", 0), *pool_p, bsz)
    h, ffn0 = ffn_fwd(h, gain("norm_ffn_g", 0), *ffn_p(0), bsz, 0)
    h, s5_saved, arrived = s5_fwd(h, gain("norm_mix_g", 1), s5_p, bsz, ("gather", ffn_shards(1)))
    ffn_w[1] = ffn_whole(arrived)
    h, ffn1 = ffn_fwd(h, gain("norm_ffn_g", 1), *ffn_p(1), bsz, 1)
    h, lru_saved, arrived = lru_fwd(h, gain("norm_mix_g", 2), lru_p, bsz, ("gather", ffn_shards(2)))
    ffn_w[2] = ffn_whole(arrived)
    h, ffn2 = ffn_fwd(h, gain("norm_ffn_g", 2), *ffn_p(2), bsz, 2)
    h, sb_saved, arrived = sb_fwd(h, gain("norm_mix_g", 3), sb_p, bsz, ("gather", ffn_shards(3)))
    ffn_w[3] = ffn_whole(arrived)
    h, ffn3 = ffn_fwd(h, gain("norm_ffn_g", 3), *ffn_p(3), bsz, 3)
    sq, dh = loss_head(h, loss_target.reshape(t, d))
    loss = lax.psum(0.5 * jnp.sum(sq) / d, ("x", "y", "c"))

    core = lax.axis_index("c").astype(jnp.int32).reshape(1)

    def by_dest(grad2d, name):
        ax = PARAMS[name][1]
        r, c = grad2d.shape
        if ax == 2:
            return grad2d.reshape(r, N_DEV, c // N_DEV).transpose(1, 0, 2).reshape(4, 2, r, c // N_DEV)
        return grad2d.reshape(4, 2, r // N_DEV, c)

    def ffn_group(li, g):
        return [(f"ffn_w_in{li}", by_dest(jnp.concatenate([g[1], g[2]], axis=1), "ffn_w_in"), BF16),
                (f"ffn_w_out{li}", by_dest(g[5], "ffn_w_out"), BF16)]

    def core_stage(group, tag):
        recv = exchange_cores([a for _, a, _ in group], f"reduce_cores_{tag}")
        return "chips", [add_own(core, a, r, dt, f"add_{k}") for (k, a, dt), r in zip(group, recv)]

    reduced = {}
    fg = [None] * n_layers
    dh, fg[3] = ffn_bwd(dh, ffn3, gain("norm_ffn_g", 3), *ffn_p(3), 3)
    group = ffn_group(3, fg[3])
    dh, sb_g, arrived = sb_bwd(dh, sb_saved, gain("norm_mix_g", 3), sb_p, core_stage(group, "ffn3"))
    reduced.update(zip([k for k, _, _ in group], arrived))
    dh, fg[2] = ffn_bwd(dh, ffn2, gain("norm_ffn_g", 2), *ffn_p(2), 2)
    group = [("sb_w_qkv", by_dest(jnp.concatenate([sb_g["w_q"], sb_g["w_k"], sb_g["w_v"]], axis=1), "sb_w_qkv"), BF16),
             ("sb_w_o", by_dest(sb_g["w_o"], "sb_w_o"), BF16)] + ffn_group(2, fg[2])
    dh, lru_g, arrived = lru_bwd(dh, lru_saved, gain("norm_mix_g", 2), lru_p, core_stage(group, "sb_ffn2"))
    reduced.update(zip([k for k, _, _ in group], arrived))
    dh, fg[1] = ffn_bwd(dh, ffn1, gain("norm_ffn_g", 1), *ffn_p(1), 1)
    group = [("lru_w_in", by_dest(jnp.concatenate([lru_g["w_gate"], lru_g["w_rec"]], axis=1), "lru_w_in"), BF16),
             ("lru_w_out", by_dest(lru_g["w_out"], "lru_w_out"), BF16)] + ffn_group(1, fg[1])
    dh, s5_g, arrived = s5_bwd(dh, s5_saved, gain("norm_mix_g", 1), s5_p, core_stage(group, "lru_ffn1"))
    reduced.update(zip([k for k, _, _ in group], arrived))
    dh, fg[0] = ffn_bwd(dh, ffn0, gain("norm_ffn_g", 0), *ffn_p(0), 0)
    dh, pool_g = pool_bwd(dh, pool_saved, gain("norm_mix_g", 0), *pool_p)
    grad_x = dh.reshape(bsz, seq, d)

    part = {
        "norm_mix_g": jnp.concatenate([pool_g[0], s5_g["g"], lru_g["g"], sb_g["g"]], axis=0),
        "norm_ffn_g": jnp.concatenate([g[0] for g in fg], axis=0),
        "pool_w": pool_g[1][None], "pool_b": pool_g[2], "pool_scale": pool_g[3],
        "s5_lam_re": s5_g["lam_re"][None], "s5_lam_im": s5_g["lam_im"][None], "s5_log_dt": s5_g["log_dt"][None],
        "s5_b_re": s5_g["b_re"][None], "s5_b_im": s5_g["b_im"][None], "s5_c_re": s5_g["c_re"][None],
        "s5_c_im": s5_g["c_im"][None], "s5_d": s5_g["d"], "s5_b_out": s5_g["b_out"],
        "lru_conv_w": lru_g["conv_w"][None], "lru_conv_b": lru_g["conv_b"], "lru_w_a": lru_g["w_a"][None],
        "lru_b_a": lru_g["b_a"], "lru_w_x": lru_g["w_x"][None], "lru_b_x": lru_g["b_x"], "lru_lam": lru_g["lam"],
        "sb_q_g": sb_g["q_g"], "sb_k_g": sb_g["k_g"],
        "ffn_conv_w": jnp.stack([g[3] for g in fg]), "ffn_conv_b": jnp.concatenate([g[4] for g in fg], axis=0),
    }
    small_part = _pack([_to_shards(part[n], n) for n in SMALL], lead=(N_DEV,))
    group = [("s5_w_out", by_dest(jnp.concatenate([s5_g["w_val"], s5_g["w_gate"]], axis=1), "s5_w_out"), BF16)]
    group += ffn_group(0, fg[0]) + [("small", small_part.reshape(4, 2, *small_part.shape[1:]), F32)]
    arrived = exchange_chips(core_stage(group, "s5_ffn0_small")[1], "reduce_chips")
    reduced.update(zip([k for k, _, _ in group], arrived))

    out = {}
    for n in BIG:
        layers = PARAMS[n][0][0]
        res = []
        for li in range(layers):
            key = f"{n}{li}" if layers > 1 else n
            res.append(adamw(reduced[key], w[n][li], m[n][li], v[n][li], f"adamw_{key}"))
        out[n] = [jnp.stack([r[i] for r in res]) for i in range(4)]
    small_res = adamw(reduced["small"], *[_pack([tree[n] for n in SMALL]) for tree in (w, m, v)], "adamw_small")
    for i, res in enumerate(small_res):
        for n, r in zip(SMALL, _unpack(res, small_shapes)):
            out.setdefault(n, [None] * 4)[i] = r

    repl_shapes = [PARAMS[n][0] for n in REPL]
    repl_parts = all_gather([_pack([part[n] for n in REPL])], "gather_grads")[0]
    repl_res = adamw(repl_parts, *[_pack([tree[n] for n in REPL]) for tree in (w, m, v)], "adamw_replicated")
    for i, res in enumerate(repl_res):
        for n, r in zip(REPL, _unpack(res, repl_shapes)):
            out.setdefault(n, [None] * 4)[i] = r

    return (loss, grad_x, *[out[n][0] for n in NAMES], *[out[n][1] for n in NAMES], *[out[n][2] for n in NAMES],
            *[out[n][3] for n in NAMES])
```
